```python
import math
import jax, jax.numpy as jnp
from jax import lax
import numpy as np

D_MODEL = 1024
BATCH = 2
SEQ = 8192
DEPTH = 2

N_MEM = 256
EPS = 1e-6
MOBA_HEADS = 8
MOBA_HEAD_DIM = 64
MOBA_BLOCK = 256
MOBA_TOPK = 3
MOBA_Q_CHUNK = 64
A_WIDTH = MOBA_HEADS * MOBA_HEAD_DIM
POOL_WINDOWS = (2, 4, 8, 16)
POOL_GROUP = 128
POOL_WIDTH = POOL_GROUP * len(POOL_WINDOWS)
EV_IN = 3 * A_WIDTH + POOL_WIDTH
EV_MIX = A_WIDTH + POOL_WIDTH
SGU_GROUPS = 4
SGU_GROUP = 128
SGU_CHUNK = 128
SGU_WIDTH = SGU_GROUPS * SGU_GROUP
DN_HEADS = 4
DN_HEAD_DIM = 128
DN_WIDTH = DN_HEADS * DN_HEAD_DIM
DN_CONV = 4
DN_CHUNK = 64
OD_SPLITS = [2 * SGU_WIDTH, 2 * SGU_WIDTH + 3 * DN_WIDTH,
             2 * SGU_WIDTH + 4 * DN_WIDTH, 2 * SGU_WIDTH + 4 * DN_WIDTH + DN_HEADS]
OD_IN = 2 * SGU_WIDTH + 4 * DN_WIDTH + 2 * DN_HEADS
OD_MIX = SGU_WIDTH + DN_WIDTH
XATTN_HEADS = 4
XATTN_HEAD_DIM = D_MODEL // XATTN_HEADS
D_FF = 256 * ((8 * D_MODEL // 3 + 255) // 256)
FFN_CONV = 3
N_EVEN = (DEPTH + 1) // 2
N_ODD = DEPTH // 2

kernel_name = 'hybrid_moba_pool_sgu_gdeltanet'


def rmsnorm(x, g):
    xf = x.astype(jnp.float32)
    y = xf * lax.rsqrt(jnp.mean(xf * xf, axis=-1, keepdims=True) + EPS)
    return (y * g.astype(jnp.float32)).astype(x.dtype)


def causal_dwconv(x, w):
    K = w.shape[0]
    S = x.shape[1]
    xp = jnp.pad(x, ((0, 0), (K - 1, 0), (0, 0)))
    y = xp[:, 0:S] * w[0]
    for j in range(1, K):
        y = y + xp[:, j:j + S] * w[j]
    return y


def split_heads(t, n_heads, head_dim):
    B, S, _ = t.shape
    return t.reshape(B, S, n_heads, head_dim).transpose(0, 2, 1, 3)


def moba_attention(q, k, v):
    B, H, S, Dh = q.shape
    BS, QC = MOBA_BLOCK, MOBA_Q_CHUNK
    nb = -(-S // BS)
    pad = nb * BS - S
    k_blk = jnp.pad(k, ((0, 0), (0, 0), (0, pad), (0, 0))).reshape(B, H, nb, BS, Dh)
    v_blk = jnp.pad(v, ((0, 0), (0, 0), (0, pad), (0, 0))).reshape(B, H, nb, BS, Dh)
    k_mean = jnp.mean(k_blk.astype(jnp.float32), axis=3)
    q_blk_id = jnp.arange(S) // BS
    gate = jnp.einsum('bhsd,bhnd->bhsn', q.astype(jnp.float32), k_mean)
    fully_past = jnp.arange(nb)[None, :] < q_blk_id[:, None]
    gate = jnp.where(fully_past, gate, -jnp.inf)
    n_sel = min(MOBA_TOPK, nb)
    _, sel = lax.top_k(gate, n_sel)
    sel_valid = sel < q_blk_id[:, None]
    scale = Dh ** -0.5
    gather = jax.vmap(jax.vmap(lambda blk, ix: blk[ix]))

    def chunk(c):
        t0 = c * QC
        qc = lax.dynamic_slice_in_dim(q, t0, QC, axis=2)
        sc = lax.dynamic_slice_in_dim(sel, t0, QC, axis=2)
        vc = lax.dynamic_slice_in_dim(sel_valid, t0, QC, axis=2)
        own = t0 // BS
        k_own = lax.dynamic_index_in_dim(k_blk, own, axis=2, keepdims=False)
        v_own = lax.dynamic_index_in_dim(v_blk, own, axis=2, keepdims=False)
        k_sel = gather(k_blk, sc)
        v_sel = gather(v_blk, sc)
        s_sel = jnp.einsum('bhqd,bhqnkd->bhqnk', qc, k_sel).astype(jnp.float32) * scale
        s_sel = jnp.where(vc[..., None], s_sel, -jnp.inf).reshape(B, H, QC, n_sel * BS)
        tq = t0 + jnp.arange(QC)
        tk = own * BS + jnp.arange(BS)
        s_own = jnp.einsum('bhqd,bhkd->bhqk', qc, k_own).astype(jnp.float32) * scale
        s_own = jnp.where(tk[None, :] <= tq[:, None], s_own, -jnp.inf)
        p = jax.nn.softmax(jnp.concatenate([s_sel, s_own], axis=-1), axis=-1).astype(v.dtype)
        p_sel = p[..., :n_sel * BS].reshape(B, H, QC, n_sel, BS)
        p_own = p[..., n_sel * BS:]
        return (jnp.einsum('bhqnk,bhqnkd->bhqd', p_sel, v_sel)
                + jnp.einsum('bhqk,bhkd->bhqd', p_own, v_own))

    out = lax.map(chunk, jnp.arange(S // QC))
    return out.transpose(1, 2, 0, 3, 4).reshape(B, H, S, Dh)


def multiscale_pool(p, pool_w, pool_scale):
    B, S, _ = p.shape
    G = len(POOL_WINDOWS)
    pf = p.astype(jnp.float32).reshape(B, S, G, POOL_GROUP)
    cs = jnp.cumsum(pf, axis=1)
    t1 = jnp.arange(1, S + 1, dtype=jnp.float32)
    outs = []
    for g, w in enumerate(POOL_WINDOWS):
        c = cs[:, :, g]
        c_prev = jnp.pad(c, ((0, 0), (w, 0), (0, 0)))[:, :S]
        cnt = jnp.minimum(t1, float(w))[None, :, None]
        outs.append((c - c_prev) / cnt - pf[:, :, g])
    pooled = jnp.stack(outs, axis=2).astype(p.dtype)
    y = jnp.einsum('bsgc,gcd->bsgd', pooled, pool_w).reshape(B, S, POOL_WIDTH)
    return y * pool_scale


def spatial_gating(z, ln_g, ln_b, w_s, b_s):
    B, S, _ = z.shape
    u, v = jnp.split(z, 2, axis=-1)
    vf = v.astype(jnp.float32)
    mu = jnp.mean(vf, axis=-1, keepdims=True)
    var = jnp.mean(jnp.square(vf - mu), axis=-1, keepdims=True)
    vn = ((vf - mu) * lax.rsqrt(var + EPS) * ln_g + ln_b).astype(z.dtype)
    vn = vn.reshape(B, S // SGU_CHUNK, SGU_CHUNK, SGU_GROUPS, SGU_GROUP)
    causal = jnp.tril(jnp.ones((SGU_CHUNK, SGU_CHUNK), dtype=bool))
    w = jnp.where(causal[None], w_s, 0)
    s = jnp.einsum('gts,bnsgc->bntgc', w, vn) + b_s.T[None, None, :, :, None]
    return u * s.reshape(B, S, SGU_WIDTH)


def l2norm(t):
    return t * lax.rsqrt(jnp.sum(t * t, axis=-1, keepdims=True) + EPS)


def chunked_gated_delta_rule(q, k, v, g, beta):
    B, H, S, DK = q.shape
    DV = v.shape[-1]
    C = DN_CHUNK
    N = S // C
    q = q.reshape(B, H, N, C, DK)
    k = k.reshape(B, H, N, C, DK)
    v = v.reshape(B, H, N, C, DV)
    beta = beta.reshape(B, H, N, C)
    gc = jnp.cumsum(g.reshape(B, H, N, C), axis=-1)
    idx = jnp.arange(C)
    causal = idx[:, None] >= idx[None, :]
    strict = idx[:, None] > idx[None, :]
    decay = jnp.exp(jnp.where(causal, gc[..., :, None] - gc[..., None, :], -jnp.inf))
    kb = k * beta[..., None]
    a = jnp.where(strict, jnp.einsum('bhnid,bhnjd->bhnij', kb, k) * decay, 0.0)
    m = a + jnp.eye(C, dtype=a.dtype)
    rhs = jnp.concatenate([v * beta[..., None], kb * jnp.exp(gc)[..., None]], axis=-1)
    sol = lax.linalg.triangular_solve(m, rhs, left_side=True, lower=True, unit_diagonal=True)
    u, w = sol[..., :DV], sol[..., DV:]
    qk = jnp.einsum('bhnid,bhnjd->bhnij', q, k) * decay
    q_dec = q * jnp.exp(gc)[..., None]
    k_dec = k * jnp.exp(gc[..., -1:] - gc)[..., None]
    g_last = jnp.exp(gc[..., -1])

    def step(state, inp):
        qk_n, qd_n, kd_n, u_n, w_n, gl_n = inp
        v_new = u_n - jnp.einsum('bhck,bhkv->bhcv', w_n, state)
        o = (jnp.einsum('bhck,bhkv->bhcv', qd_n, state)
             + jnp.einsum('bhij,bhjv->bhiv', qk_n, v_new))
        state = state * gl_n[..., None, None] + jnp.einsum('bhck,bhcv->bhkv', kd_n, v_new)
        return state, o

    xs = (jnp.moveaxis(qk, 2, 0), jnp.moveaxis(q_dec, 2, 0), jnp.moveaxis(k_dec, 2, 0),
          jnp.moveaxis(u, 2, 0), jnp.moveaxis(w, 2, 0), jnp.moveaxis(g_last, 2, 0))
    state0 = jnp.zeros((B, H, DK, DV), q.dtype)
    _, o = lax.scan(step, state0, xs)
    return jnp.moveaxis(o, 0, 2).reshape(B, H, S, DV)


def gated_deltanet(qkv, gate, b_raw, a_raw, conv_w, a_log, dt_bias, norm_g):
    B, S, _ = qkv.shape
    dt = qkv.dtype
    qkv = jax.nn.silu(causal_dwconv(qkv, conv_w))
    q, k, v = jnp.split(qkv, 3, axis=-1)
    q = l2norm(split_heads(q, DN_HEADS, DN_HEAD_DIM).astype(jnp.float32)) * DN_HEAD_DIM ** -0.5
    k = l2norm(split_heads(k, DN_HEADS, DN_HEAD_DIM).astype(jnp.float32))
    v = split_heads(v, DN_HEADS, DN_HEAD_DIM).astype(jnp.float32)
    beta = jax.nn.sigmoid(b_raw.astype(jnp.float32)).transpose(0, 2, 1)
    g = (-jnp.exp(a_log.astype(jnp.float32))
         * jax.nn.softplus(a_raw.astype(jnp.float32) + dt_bias.astype(jnp.float32))).transpose(0, 2, 1)
    o = chunked_gated_delta_rule(q, k, v, g, beta)
    o = o * lax.rsqrt(jnp.mean(o * o, axis=-1, keepdims=True) + EPS) * norm_g.astype(jnp.float32)
    o = o.transpose(0, 2, 1, 3) * jax.nn.silu(gate.astype(jnp.float32).reshape(B, S, DN_HEADS, DN_HEAD_DIM))
    return o.reshape(B, S, DN_WIDTH).astype(dt)


def memory_cross_attention(xn, mem_n, wq, wkv, wo):
    B, S, D = xn.shape
    M = mem_n.shape[1]
    q = (xn @ wq).reshape(B, S, XATTN_HEADS, XATTN_HEAD_DIM)
    k, v = jnp.split(mem_n @ wkv, 2, axis=-1)
    k = k.reshape(B, M, XATTN_HEADS, XATTN_HEAD_DIM)
    v = v.reshape(B, M, XATTN_HEADS, XATTN_HEAD_DIM)
    s = jnp.einsum('bshd,bmhd->bhsm', q, k).astype(jnp.float32) * XATTN_HEAD_DIM ** -0.5
    p = jax.nn.softmax(s, axis=-1).astype(v.dtype)
    o = jnp.einsum('bhsm,bmhd->bshd', p, v).reshape(B, S, D)
    return o @ wo


def conv_ffn(xn, w_up, conv_w, w_down):
    h = causal_dwconv(xn @ w_up, conv_w)
    g, u = jnp.split(h, 2, axis=-1)
    return (jax.nn.silu(g) * u) @ w_down


def setup_inputs(seed: int = 0) -> dict:
    key = jax.random.key(seed)
    ks = iter(jax.random.split(key, 32))
    D = D_MODEL

    def nrm(shape, scale):
        return jax.random.normal(next(ks), shape, jnp.float32) * scale

    def gain(shape):
        return 1.0 + nrm(shape, 0.02)

    x = nrm((BATCH, SEQ, D), 1.0)
    mem = nrm((BATCH, N_MEM, D), 1.0)
    mem_norm = gain((D,))
    norm_mix = gain((DEPTH, D))
    norm_xattn = gain((DEPTH, D))
    norm_ffn = gain((DEPTH, D))
    ev_w_in = nrm((N_EVEN, D, EV_IN), D ** -0.5)
    pool_w = nrm((N_EVEN, len(POOL_WINDOWS), POOL_GROUP, POOL_GROUP), POOL_GROUP ** -0.5)
    pool_scale = 1.0 + nrm((N_EVEN, POOL_WIDTH), 0.1)
    ev_w_out = nrm((N_EVEN, EV_MIX, D), EV_MIX ** -0.5)
    od_w_in = nrm((N_ODD, D, OD_IN), D ** -0.5)
    sgu_ln_g = gain((N_ODD, SGU_WIDTH))
    sgu_ln_b = nrm((N_ODD, SGU_WIDTH), 0.02)
    sgu_w = nrm((N_ODD, SGU_GROUPS, SGU_CHUNK, SGU_CHUNK), SGU_CHUNK ** -0.5)
    sgu_b = 1.0 + nrm((N_ODD, SGU_GROUPS, SGU_CHUNK), 0.02)
    dn_conv = nrm((N_ODD, DN_CONV, 3 * DN_WIDTH), DN_CONV ** -0.5)
    dn_a_log = jnp.log(jax.random.uniform(next(ks), (N_ODD, DN_HEADS), jnp.float32, 1.0, 16.0))
    dt = jnp.exp(jax.random.uniform(next(ks), (N_ODD, DN_HEADS), jnp.float32,
                                    math.log(1e-3), math.log(1e-1)))
    dn_dt_bias = dt + jnp.log(-jnp.expm1(-dt))
    dn_norm_g = gain((N_ODD, DN_HEAD_DIM))
    od_w_out = nrm((N_ODD, OD_MIX, D), OD_MIX ** -0.5)
    xattn_wq = nrm((DEPTH, D, D), D ** -0.5)
    xattn_wkv = nrm((DEPTH, D, 2 * D), D ** -0.5)
    xattn_wo = nrm((DEPTH, D, D), D ** -0.5)
    ffn_w_up = nrm((DEPTH, D, 2 * D_FF), D ** -0.5)
    ffn_conv = nrm((DEPTH, FFN_CONV, 2 * D_FF), FFN_CONV ** -0.5)
    ffn_w_down = nrm((DEPTH, D_FF, D), D_FF ** -0.5)
    final_norm = gain((D,))
    return {'x': x, 'mem': mem, 'mem_norm': mem_norm, 'norm_mix': norm_mix,
            'norm_xattn': norm_xattn, 'norm_ffn': norm_ffn,
            'ev_w_in': ev_w_in, 'pool_w': pool_w, 'pool_scale': pool_scale, 'ev_w_out': ev_w_out,
            'od_w_in': od_w_in, 'sgu_ln_g': sgu_ln_g, 'sgu_ln_b': sgu_ln_b, 'sgu_w': sgu_w,
            'sgu_b': sgu_b, 'dn_conv': dn_conv, 'dn_a_log': dn_a_log, 'dn_dt_bias': dn_dt_bias,
            'dn_norm_g': dn_norm_g, 'od_w_out': od_w_out,
            'xattn_wq': xattn_wq, 'xattn_wkv': xattn_wkv, 'xattn_wo': xattn_wo,
            'ffn_w_up': ffn_w_up, 'ffn_conv': ffn_conv, 'ffn_w_down': ffn_w_down,
            'final_norm': final_norm}


def reference(x, mem, mem_norm, norm_mix, norm_xattn, norm_ffn,
              ev_w_in, pool_w, pool_scale, ev_w_out,
              od_w_in, sgu_ln_g, sgu_ln_b, sgu_w, sgu_b,
              dn_conv, dn_a_log, dn_dt_bias, dn_norm_g, od_w_out,
              xattn_wq, xattn_wkv, xattn_wo,
              ffn_w_up, ffn_conv, ffn_w_down, final_norm):
    B, S, _ = x.shape
    mem_n = rmsnorm(mem, mem_norm)
    h = x
    for layer in range(DEPTH):
        xn = rmsnorm(h, norm_mix[layer])
        i = layer // 2
        if layer % 2 == 0:
            proj = xn @ ev_w_in[i]
            q, k, v, p = jnp.split(proj, [A_WIDTH, 2 * A_WIDTH, 3 * A_WIDTH], axis=-1)
            a_out = moba_attention(split_heads(q, MOBA_HEADS, MOBA_HEAD_DIM),
                                   split_heads(k, MOBA_HEADS, MOBA_HEAD_DIM),
                                   split_heads(v, MOBA_HEADS, MOBA_HEAD_DIM))
            a_out = a_out.transpose(0, 2, 1, 3).reshape(B, S, A_WIDTH)
            b_out = multiscale_pool(p, pool_w[i], pool_scale[i])
            mix = jnp.concatenate([a_out, b_out], axis=-1) @ ev_w_out[i]
        else:
            proj = xn @ od_w_in[i]
            z, qkv, gate, b_raw, a_raw = jnp.split(proj, OD_SPLITS, axis=-1)
            c_out = spatial_gating(jax.nn.gelu(z), sgu_ln_g[i], sgu_ln_b[i], sgu_w[i], sgu_b[i])
            d_out = gated_deltanet(qkv, gate, b_raw, a_raw, dn_conv[i], dn_a_log[i],
                                   dn_dt_bias[i], dn_norm_g[i])
            mix = jnp.concatenate([c_out, d_out], axis=-1) @ od_w_out[i]
        h = h + mix
        h = h + memory_cross_attention(rmsnorm(h, norm_xattn[layer]), mem_n,
                                       xattn_wq[layer], xattn_wkv[layer], xattn_wo[layer])
        h = h + conv_ffn(rmsnorm(h, norm_ffn[layer]), ffn_w_up[layer], ffn_conv[layer], ffn_w_down[layer])
    return rmsnorm(h, final_norm)
```

```python
import functools
import math

import jax
import jax.numpy as jnp
from jax import lax
from jax.experimental import pallas as pl
from jax.experimental.pallas import tpu as pltpu

F32 = jnp.float32
BF16 = jnp.bfloat16
EPS = 1e-6
NEG_BIG = -1e30

VMEM_LIMIT_BYTES = 48 * 1024 * 1024
BF16_SUBLANES = 16
LANES = 128

MOBA_HEADS, MOBA_HEAD_DIM, MOBA_BLOCK, MOBA_TOPK = 8, 64, 256, 3
A_WIDTH = MOBA_HEADS * MOBA_HEAD_DIM
POOL_WINDOWS = (2, 4, 8, 16)
POOL_GROUP = 128
POOL_WIDTH = POOL_GROUP * len(POOL_WINDOWS)
SGU_GROUPS, SGU_GROUP, SGU_CHUNK = 4, 128, 128
SGU_WIDTH = SGU_GROUPS * SGU_GROUP
DN_HEADS, DN_HEAD_DIM, DN_CONV, DN_CHUNK = 4, 128, 4, 64
DN_WIDTH = DN_HEADS * DN_HEAD_DIM
XATTN_HEADS = 4
FFN_CONV = 3


def _cparams(*sem):
    return pltpu.CompilerParams(dimension_semantics=sem, vmem_limit_bytes=VMEM_LIMIT_BYTES)


def _rmsnorm(x, g):
    return x * lax.rsqrt(jnp.mean(x * x, axis=-1, keepdims=True) + EPS) * g


def _silu(x):
    return x / (1.0 + jnp.exp(-x))


def _dot(a, b):
    return jnp.dot(a, b, preferred_element_type=F32)


def _dot_nt(a, b, precision=None):
    return lax.dot_general(a, b, (((1,), (1,)), ((), ())), preferred_element_type=F32,
                           precision=precision)


def _dot_tn(a, b):
    return lax.dot_general(a, b, (((0,), (0,)), ((), ())), preferred_element_type=F32)


def _norm_mm_kernel(x_ref, g_ref, w_ref, o_ref, xn_ref):
    @pl.when(pl.program_id(1) == 0)
    def _():
        xn_ref[...] = _rmsnorm(x_ref[...], g_ref[...]).astype(BF16)

    o_ref[...] = _dot(xn_ref[...], w_ref[...]).astype(o_ref.dtype)


def norm_matmul(x, g, w, out_dtype, bm, bn):
    M, D = x.shape
    N = w.shape[1]
    return pl.pallas_call(
        _norm_mm_kernel,
        grid=(M // bm, N // bn),
        in_specs=[pl.BlockSpec((bm, D), lambda i, j: (i, 0)),
                  pl.BlockSpec((1, D), lambda i, j: (0, 0)),
                  pl.BlockSpec((D, bn), lambda i, j: (0, j))],
        out_specs=pl.BlockSpec((bm, bn), lambda i, j: (i, j)),
        out_shape=jax.ShapeDtypeStruct((M, N), out_dtype),
        scratch_shapes=[pltpu.VMEM((bm, D), BF16)],
        compiler_params=_cparams("parallel", "arbitrary"),
    )(x, g.reshape(1, D), w)


def _moba_kernel(q_ref, k_ref, v_ref, o_ref, kme_ref, *, nb):
    BS = MOBA_BLOCK
    HD = MOBA_HEAD_DIM
    i = pl.program_id(2)
    lane = lax.broadcasted_iota(jnp.int32, (1, LANES), 1)
    head_lanes = (lane < HD, lane >= HD)
    bias_off = (HD, 0)

    @pl.when(i == 0)
    def _():
        kme_ref[...] = jnp.zeros_like(kme_ref)
        for n in range(nb):
            mean = jnp.sum(k_ref[0, n * BS:(n + 1) * BS, :].astype(F32), axis=0, keepdims=True) / BS
            kme_ref[0, HD + n:HD + n + 1, :] = jnp.where(head_lanes[0], mean, 0.0)
            kme_ref[1, n:n + 1, :] = jnp.where(head_lanes[1], mean, 0.0)

    qf = q_ref[0].astype(F32)
    q_aug = []
    for hh in range(2):
        qm = jnp.where(head_lanes[hh], qf, 0.0)
        gate = _dot_nt(qm, kme_ref[hh], precision=lax.Precision.HIGHEST)
        blk = (lane - bias_off[hh]).astype(F32)
        valid = (blk >= 0.0) & (blk < i.astype(F32))
        g = jnp.where(valid, gate, -jnp.inf)
        sel = jnp.zeros(g.shape, jnp.bool_)
        for _ in range(MOBA_TOPK):
            m = jnp.max(g, axis=1, keepdims=True)
            idx = jnp.min(jnp.where(g == m, blk, float(1 << 20)), axis=1, keepdims=True)
            pick = blk == idx
            sel = sel | pick
            g = jnp.where(pick, -jnp.inf, g)
        sel = sel & valid
        scale = HD ** -0.5
        q_aug.append(jnp.where(head_lanes[hh], qf * scale, jnp.where(sel, 0.0, NEG_BIG)).astype(BF16))

    def tile(j, onehot_on):
        start = pl.multiple_of(j * BS, BS)
        kj = k_ref[0, pl.ds(start, BS), :]
        vj = v_ref[0, pl.ds(start, BS), :]
        res = []
        for hh in range(2):
            hit = (lane == bias_off[hh] + j) if onehot_on else (lane < 0)
            onehot = jnp.where(hit, 1.0, 0.0).astype(BF16)
            k_aug = jnp.where(head_lanes[hh], kj, onehot)
            v_aug = jnp.where(head_lanes[hh], vj, jnp.ones((), BF16))
            res.append((_dot_nt(q_aug[hh], k_aug), v_aug))
        return res

    def update(m, acc, s, v_aug):
        m_new = jnp.maximum(m, jnp.max(s, axis=1, keepdims=True))
        alpha = jnp.exp(m - m_new)
        p = jnp.exp(s - m_new)
        return m_new, acc * alpha + _dot(p.astype(BF16), v_aug)

    def body(j, carry):
        (s0, va0), (s1, va1) = tile(j, True)
        m0, a0 = update(carry[0], carry[1], s0, va0)
        m1, a1 = update(carry[2], carry[3], s1, va1)
        return m0, a0, m1, a1

    init_m = jnp.full((BS, 1), -jnp.inf, F32)
    init_a = jnp.zeros((BS, LANES), F32)
    m0, a0, m1, a1 = lax.fori_loop(0, i, body, (init_m, init_a, init_m, init_a))

    row = lax.broadcasted_iota(jnp.int32, (BS, BS), 0)
    col = lax.broadcasted_iota(jnp.int32, (BS, BS), 1)
    (s0, va0), (s1, va1) = tile(i, False)
    m0, a0 = update(m0, a0, jnp.where(col <= row, s0, -jnp.inf), va0)
    m1, a1 = update(m1, a1, jnp.where(col <= row, s1, -jnp.inf), va1)
    o0 = a0 / pltpu.roll(a0, HD, axis=1)
    o1 = a1 / pltpu.roll(a1, HD, axis=1)
    o_ref[0] = jnp.where(head_lanes[0], o0, o1).astype(o_ref.dtype)


def moba_attention(proj, B, S):
    nb = S // MOBA_BLOCK
    pairs = A_WIDTH // LANES
    return pl.pallas_call(
        functools.partial(_moba_kernel, nb=nb),
        grid=(B, pairs, nb),
        in_specs=[pl.BlockSpec((1, MOBA_BLOCK, LANES), lambda b, p, i: (b, i, p)),
                  pl.BlockSpec((1, S, LANES), lambda b, p, i: (b, 0, pairs + p)),
                  pl.BlockSpec((1, S, LANES), lambda b, p, i: (b, 0, 2 * pairs + p))],
        out_specs=pl.BlockSpec((1, MOBA_BLOCK, LANES), lambda b, p, i: (b, i, p)),
        out_shape=jax.ShapeDtypeStruct((B, S, A_WIDTH), BF16),
        scratch_shapes=[pltpu.VMEM((2, LANES, LANES), F32)],
        compiler_params=_cparams("parallel", "parallel", "arbitrary"),
    )(proj, proj, proj)


def _pool_kernel(p_ref, halo_ref, w_ref, sc_ref, o_ref, *, blocks_per_seq):
    bm = p_ref.shape[0]
    H = BF16_SUBLANES
    i = pl.program_id(0)
    first = (i % blocks_per_seq) == 0
    t1 = (lax.broadcasted_iota(jnp.int32, (bm, 1), 0) + (i % blocks_per_seq) * bm + 1).astype(F32)
    for g, w in enumerate(POOL_WINDOWS):
        cols = slice(g * POOL_GROUP, (g + 1) * POOL_GROUP)
        cur = p_ref[:, cols].astype(F32)
        halo = jnp.where(first, 0.0, halo_ref[:, cols].astype(F32))
        ext = jnp.concatenate([halo, cur], axis=0)
        acc = ext
        sh = 1
        while sh < w:
            acc = acc + pltpu.roll(acc, sh, axis=0)
            sh *= 2
        win = acc[H:, :]
        pooled = win / jnp.minimum(t1, float(w)) - cur
        y = _dot(pooled.astype(BF16), w_ref[g])
        o_ref[:, cols] = (y * sc_ref[:, cols]).astype(o_ref.dtype)


def multiscale_pool(proj, pool_w, pool_scale, M, S, bm):
    H = BF16_SUBLANES
    pcol = 3 * A_WIDTH // POOL_WIDTH
    return pl.pallas_call(
        functools.partial(_pool_kernel, blocks_per_seq=S // bm),
        grid=(M // bm,),
        in_specs=[pl.BlockSpec((bm, POOL_WIDTH), lambda i: (i, pcol)),
                  pl.BlockSpec((H, POOL_WIDTH), lambda i: (jnp.maximum(i * (bm // H) - 1, 0), pcol)),
                  pl.BlockSpec((len(POOL_WINDOWS), POOL_GROUP, POOL_GROUP), lambda i: (0, 0, 0)),
                  pl.BlockSpec((1, POOL_WIDTH), lambda i: (0, 0))],
        out_specs=pl.BlockSpec((bm, POOL_WIDTH), lambda i: (i, 0)),
        out_shape=jax.ShapeDtypeStruct((M, POOL_WIDTH), BF16),
        compiler_params=_cparams("parallel"),
    )(proj, proj, pool_w, pool_scale.reshape(1, POOL_WIDTH))


def _mix_kernel(h_ref, a_ref, b_ref, w_ref, o_ref):
    ka = a_ref.shape[1]
    o_ref[...] = h_ref[...] + _dot(a_ref[...], w_ref[:ka, :]) + _dot(b_ref[...], w_ref[ka:, :])


def mix_residual(h, a, b, w, bm):
    M, D = h.shape
    ka, kb = a.shape[1], b.shape[1]
    return pl.pallas_call(
        _mix_kernel,
        grid=(M // bm,),
        in_specs=[pl.BlockSpec((bm, D), lambda i: (i, 0)),
                  pl.BlockSpec((bm, ka), lambda i: (i, 0)),
                  pl.BlockSpec((bm, kb), lambda i: (i, 0)),
                  pl.BlockSpec((ka + kb, D), lambda i: (0, 0))],
        out_specs=pl.BlockSpec((bm, D), lambda i: (i, 0)),
        out_shape=jax.ShapeDtypeStruct((M, D), F32),
        compiler_params=_cparams("parallel"),
    )(h, a, b, w)


def _xattn_kernel(h_ref, g_ref, wq_ref, k_ref, v_ref, wo_ref, o_ref):
    h = h_ref[...]
    D = h.shape[1]
    hd = D // XATTN_HEADS
    xn = _rmsnorm(h, g_ref[...]).astype(BF16)
    q = (_dot(xn, wq_ref[...]) * hd ** -0.5).astype(BF16)
    outs = []
    for hh in range(XATTN_HEADS):
        cols = slice(hh * hd, (hh + 1) * hd)
        s = _dot_nt(q[:, cols], k_ref[:, cols])
        m = jnp.max(s, axis=1, keepdims=True)
        p = jnp.exp(s - m)
        l = jnp.sum(p, axis=1, keepdims=True)
        outs.append((_dot(p.astype(BF16), v_ref[:, cols]) / l).astype(BF16))
    o = jnp.concatenate(outs, axis=1)
    o_ref[...] = h + _dot(o, wo_ref[...])


def xattn_residual(h, g, wq, kv, wo, S, n_mem, bm):
    M, D = h.shape
    bps = S // bm
    return pl.pallas_call(
        _xattn_kernel,
        grid=(M // bm,),
        in_specs=[pl.BlockSpec((bm, D), lambda i: (i, 0)),
                  pl.BlockSpec((1, D), lambda i: (0, 0)),
                  pl.BlockSpec((D, D), lambda i: (0, 0)),
                  pl.BlockSpec((n_mem, D), lambda i: (i // bps, 0)),
                  pl.BlockSpec((n_mem, D), lambda i: (i // bps, 1)),
                  pl.BlockSpec((D, D), lambda i: (0, 0))],
        out_specs=pl.BlockSpec((bm, D), lambda i: (i, 0)),
        out_shape=jax.ShapeDtypeStruct((M, D), F32),
        compiler_params=_cparams("parallel"),
    )(h, g.reshape(1, D), wq, kv, kv, wo)


def _ffn_down_kernel(h_ref, up_ref, halo_ref, cw_ref, wd_ref, fg_ref, o_ref, act_ref, *,
                     blocks_per_seq, d_ff, cw, final_norm):
    bm = up_ref.shape[0]
    H = BF16_SUBLANES
    first = (pl.program_id(0) % blocks_per_seq) == 0

    def conv(c0):
        cols = slice(c0, c0 + cw)
        cur = up_ref[:, cols].astype(F32)
        halo = jnp.where(first, 0.0, halo_ref[:, cols].astype(F32))
        ext = jnp.concatenate([halo, cur], axis=0)
        w = cw_ref[:, cols]
        y = cur * w[FFN_CONV - 1:FFN_CONV, :]
        for k in range(1, FFN_CONV):
            y = y + pltpu.roll(ext, k, axis=0)[H:, :] * w[FFN_CONV - 1 - k:FFN_CONV - k, :]
        return y

    for c0 in range(0, d_ff, cw):
        act_ref[:, c0:c0 + cw] = (_silu(conv(c0)) * conv(d_ff + c0)).astype(BF16)
    y = h_ref[...] + _dot(act_ref[...], wd_ref[...])
    if final_norm:
        y = _rmsnorm(y, fg_ref[...])
    o_ref[...] = y


def ffn_down_residual(h, up, conv_w, w_down, final_g, S, bm, final_norm):
    M, D = h.shape
    d_ff = w_down.shape[0]
    H = BF16_SUBLANES
    return pl.pallas_call(
        functools.partial(_ffn_down_kernel, blocks_per_seq=S // bm, d_ff=d_ff, cw=256,
                          final_norm=final_norm),
        grid=(M // bm,),
        in_specs=[pl.BlockSpec((bm, D), lambda i: (i, 0)),
                  pl.BlockSpec((bm, 2 * d_ff), lambda i: (i, 0)),
                  pl.BlockSpec((H, 2 * d_ff), lambda i: (jnp.maximum(i * (bm // H) - 1, 0), 0)),
                  pl.BlockSpec((FFN_CONV, 2 * d_ff), lambda i: (0, 0)),
                  pl.BlockSpec((d_ff, D), lambda i: (0, 0)),
                  pl.BlockSpec((1, D), lambda i: (0, 0))],
        out_specs=pl.BlockSpec((bm, D), lambda i: (i, 0)),
        out_shape=jax.ShapeDtypeStruct((M, D), F32),
        scratch_shapes=[pltpu.VMEM((bm, d_ff), BF16)],
        compiler_params=_cparams("parallel"),
    )(h, up, up, conv_w, w_down, final_g.reshape(1, D))


def _gelu_tanh(x):
    return 0.5 * x * (1.0 + jnp.tanh(math.sqrt(2.0 / math.pi) * (x + 0.044715 * (x * x * x))))


def _sgu_kernel(u_ref, v_ref, lg_ref, lb_ref, w_ref, bt_ref, o_ref):
    rows = u_ref.shape[0]
    T = SGU_CHUNK
    v = _gelu_tanh(v_ref[...].astype(F32))
    mu = jnp.mean(v, axis=-1, keepdims=True)
    d = v - mu
    var = jnp.mean(d * d, axis=-1, keepdims=True)
    vn = (d * lax.rsqrt(var + EPS) * lg_ref[...] + lb_ref[...]).astype(BF16)
    causal = (lax.broadcasted_iota(jnp.int32, (T, T), 1) <= lax.broadcasted_iota(jnp.int32, (T, T), 0))
    for g in range(SGU_GROUPS):
        cols = slice(g * SGU_GROUP, (g + 1) * SGU_GROUP)
        wg = jnp.where(causal, w_ref[g], 0.0).astype(BF16)
        bias = bt_ref[:, g:g + 1]
        for c in range(rows // T):
            rs = slice(c * T, (c + 1) * T)
            s = _dot(wg, vn[rs, cols]) + bias
            o_ref[rs, cols] = (_gelu_tanh(u_ref[rs, cols].astype(F32)) * s).astype(o_ref.dtype)


def spatial_gating(proj, ln_g, ln_b, w_s, b_s, M, rows):
    return pl.pallas_call(
        _sgu_kernel,
        grid=(M // rows,),
        in_specs=[pl.BlockSpec((rows, SGU_WIDTH), lambda i: (i, 0)),
                  pl.BlockSpec((rows, SGU_WIDTH), lambda i: (i, 1)),
                  pl.BlockSpec((1, SGU_WIDTH), lambda i: (0, 0)),
                  pl.BlockSpec((1, SGU_WIDTH), lambda i: (0, 0)),
                  pl.BlockSpec((SGU_GROUPS, SGU_CHUNK, SGU_CHUNK), lambda i: (0, 0, 0)),
                  pl.BlockSpec((SGU_CHUNK, SGU_GROUPS), lambda i: (0, 0))],
        out_specs=pl.BlockSpec((rows, SGU_WIDTH), lambda i: (i, 0)),
        out_shape=jax.ShapeDtypeStruct((M, SGU_WIDTH), BF16),
        compiler_params=_cparams("parallel"),
    )(proj, proj, ln_g.reshape(1, -1), ln_b.reshape(1, -1), w_s, b_s.T)


def _gdn_intra_kernel(q_ref, k_ref, v_ref, qh_ref, kh_ref, vh_ref, cw_ref, tail_ref, tailt_ref, alog_ref,
                      dtb_ref, u_ref, w_ref, qd_ref, kd_ref, qk_ref, gl_ref, *, blocks_per_seq):
    rows = q_ref.shape[0]
    C = DN_CHUNK
    HD = DN_HEAD_DIM
    H = BF16_SUBLANES
    first = (pl.program_id(0) % blocks_per_seq) == 0

    def conv_silu(part, ref, halo_ref):
        cur = ref[...].astype(F32)
        halo = jnp.where(first, 0.0, halo_ref[...].astype(F32))
        ext = jnp.concatenate([halo, cur], axis=0)
        cw = cw_ref[:, part * DN_WIDTH:(part + 1) * DN_WIDTH]
        y = cur * cw[DN_CONV - 1:DN_CONV, :]
        for k in range(1, DN_CONV):
            y = y + pltpu.roll(ext, k, axis=0)[H:, :] * cw[DN_CONV - 1 - k:DN_CONV - k, :]
        return _silu(y)

    x = jnp.concatenate([conv_silu(0, q_ref, qh_ref), conv_silu(1, k_ref, kh_ref),
                         conv_silu(2, v_ref, vh_ref)], axis=1)

    ii = lax.broadcasted_iota(jnp.int32, (C, C), 0)
    jj = lax.broadcasted_iota(jnp.int32, (C, C), 1)
    lower = jj <= ii
    strict = jj < ii
    su = lax.broadcasted_iota(jnp.int32, (C, LANES), 0)
    ju = lax.broadcasted_iota(jnp.int32, (C, LANES), 1)
    upper_ext = jnp.where(((ju < C) & (su > ju)) | (ju == C), 1.0, 0.0)

    for hh in range(DN_HEADS):
        qh = x[:, hh * HD:(hh + 1) * HD]
        kh = x[:, DN_WIDTH + hh * HD:DN_WIDTH + (hh + 1) * HD]
        vh = x[:, 2 * DN_WIDTH + hh * HD:2 * DN_WIDTH + (hh + 1) * HD]
        qh = qh * lax.rsqrt(jnp.sum(qh * qh, axis=-1, keepdims=True) + EPS) * HD ** -0.5
        kh = kh * lax.rsqrt(jnp.sum(kh * kh, axis=-1, keepdims=True) + EPS)
        beta = 1.0 / (1.0 + jnp.exp(-tail_ref[:, hh:hh + 1]))
        a_raw = tailt_ref[DN_HEADS + hh:DN_HEADS + hh + 1, :]
        z = a_raw + dtb_ref[0:1, hh:hh + 1]
        softplus = jnp.maximum(z, 0.0) + jnp.log(1.0 + jnp.exp(-jnp.abs(z)))
        g_row = -jnp.exp(alog_ref[0:1, hh:hh + 1]) * softplus
        hcols = slice(hh * HD, (hh + 1) * HD)
        for c in range(rows // C):
            rs = slice(c * C, (c + 1) * C)
            q, k, v, b = qh[rs], kh[rs], vh[rs], beta[rs]
            gr = jnp.broadcast_to(g_row[:, rs], (C, C))
            stacked = jnp.concatenate([jnp.where(lower, gr, 0.0), jnp.where(lower, 0.0, gr)], axis=0)
            dext = jnp.dot(stacked, upper_ext, preferred_element_type=F32,
                           precision=lax.Precision.HIGHEST)
            diff = dext[:C, :C]
            gc = dext[:C, C:C + 1]
            gc_rev = dext[C:, C:C + 1]
            gc_last = dext[C - 1:C, C:C + 1]
            decay = jnp.exp(jnp.where(lower, diff, -jnp.inf))
            kb = k * b
            kb16, k16 = kb.astype(BF16), k.astype(BF16)
            a = jnp.where(strict, _dot_nt(kb16, k16) * decay, 0.0)
            rhs = jnp.concatenate([v * b, kb * jnp.exp(gc)], axis=1)
            pw = a.astype(BF16)
            sol = rhs - _dot(pw, rhs.astype(BF16))
            n_sq = int(math.log2(C)) - 1
            for t in range(n_sq):
                pw32 = _dot(pw, pw)
                pw = pw32.astype(BF16)
                sol = sol + _dot(pw, sol.astype(BF16))
            u_ref[rs, hcols] = sol[:, :HD].astype(u_ref.dtype)
            w_ref[rs, hcols] = sol[:, HD:].astype(w_ref.dtype)
            qk = jnp.where(lower, _dot_nt(q.astype(BF16), k16) * decay, 0.0)
            qk_ref[rs, hcols] = jnp.concatenate([qk, jnp.zeros_like(qk)], axis=1).astype(qk_ref.dtype)
            qd_ref[rs, hcols] = (q * jnp.exp(gc)).astype(qd_ref.dtype)
            kd_ref[rs, hcols] = (k * jnp.exp(gc_rev)).astype(kd_ref.dtype)
            gl_ref[c * 8:(c + 1) * 8, hcols] = jnp.broadcast_to(jnp.exp(gc_last), (8, HD))


def gdn_intra(proj, tail, dn_conv, a_log, dt_bias, M, S, rows):
    H = BF16_SUBLANES
    c0 = 2 * SGU_WIDTH // DN_WIDTH
    tail_t = tail[:, :2 * DN_HEADS].T
    pad = lambda p: jnp.pad(p.reshape(1, -1), ((0, 0), (0, LANES - p.shape[0])))
    W3 = 3 * DN_WIDTH
    seq = lambda dt: jax.ShapeDtypeStruct((M, DN_WIDTH), dt)
    row_spec = pl.BlockSpec((rows, DN_WIDTH), lambda i: (i, 0))
    cur_spec = lambda part: pl.BlockSpec((rows, DN_WIDTH), lambda i: (i, c0 + part))
    halo_spec = lambda part: pl.BlockSpec(
        (H, DN_WIDTH), lambda i: (jnp.maximum(i * (rows // H) - 1, 0), c0 + part))
    return pl.pallas_call(
        functools.partial(_gdn_intra_kernel, blocks_per_seq=S // rows),
        grid=(M // rows,),
        in_specs=[cur_spec(0), cur_spec(1), cur_spec(2), halo_spec(0), halo_spec(1), halo_spec(2),
                  pl.BlockSpec((DN_CONV, W3), lambda i: (0, 0)),
                  pl.BlockSpec((rows, LANES), lambda i: (i, 0)),
                  pl.BlockSpec((2 * DN_HEADS, rows), lambda i: (0, i)),
                  pl.BlockSpec((1, LANES), lambda i: (0, 0)),
                  pl.BlockSpec((1, LANES), lambda i: (0, 0))],
        out_specs=[row_spec, row_spec, row_spec, row_spec, row_spec,
                   pl.BlockSpec((rows // DN_CHUNK * 8, DN_WIDTH), lambda i: (i, 0))],
        out_shape=[seq(BF16), seq(BF16), seq(BF16), seq(BF16), seq(BF16),
                   jax.ShapeDtypeStruct((M // DN_CHUNK * 8, DN_WIDTH), F32)],
        compiler_params=_cparams("parallel"),
    )(proj, proj, proj, proj, proj, proj, dn_conv, tail, tail_t, pad(a_log), pad(dt_bias))


def _gdn_scan_kernel(u_ref, w_ref, qd_ref, kd_ref, qk_ref, gl_ref, gate_ref, ng_ref, o_ref, state_ref, *,
                     chunks):
    C = DN_CHUNK
    HD = DN_HEAD_DIM
    B = u_ref.shape[0]

    @pl.when(pl.program_id(0) == 0)
    def _():
        state_ref[...] = jnp.zeros_like(state_ref)

    ng = ng_ref[...]
    for c in range(chunks):
        rs = slice(c * C, (c + 1) * C)
        for b in range(B):
            for hh in range(DN_HEADS):
                cols = slice(hh * HD, (hh + 1) * HD)
                st = state_ref[b, hh]
                st16 = st.astype(BF16)
                v_new = u_ref[b, rs, cols].astype(F32) - _dot(w_ref[b, rs, cols], st16)
                vn16 = v_new.astype(BF16)
                o = _dot(qd_ref[b, rs, cols], st16) + _dot(qk_ref[b, rs, cols][:, :C], vn16)
                gl = gl_ref[b, c * 8:c * 8 + 1, cols]
                state_ref[b, hh] = st * gl + _dot_tn(kd_ref[b, rs, cols], vn16)
                o = o * lax.rsqrt(jnp.mean(o * o, axis=-1, keepdims=True) + EPS) * ng
                o_ref[b, rs, cols] = (o * _silu(gate_ref[b, rs, cols].astype(F32))).astype(o_ref.dtype)


def gdn_scan(u, w, qd, kd, qk, gl, proj3, norm_g, B, S, chunks):
    rows = chunks * DN_CHUNK
    r3 = lambda a: a.reshape(B, S, DN_WIDTH)
    gcol = (2 * SGU_WIDTH + 3 * DN_WIDTH) // DN_WIDTH
    seq_spec = pl.BlockSpec((B, rows, DN_WIDTH), lambda n: (0, n, 0))
    return pl.pallas_call(
        functools.partial(_gdn_scan_kernel, chunks=chunks),
        grid=(S // rows,),
        in_specs=[seq_spec, seq_spec, seq_spec, seq_spec, seq_spec,
                  pl.BlockSpec((B, chunks * 8, DN_WIDTH), lambda n: (0, n, 0)),
                  pl.BlockSpec((B, rows, DN_WIDTH), lambda n: (0, n, gcol)),
                  pl.BlockSpec((1, DN_HEAD_DIM), lambda n: (0, 0))],
        out_specs=seq_spec,
        out_shape=jax.ShapeDtypeStruct((B, S, DN_WIDTH), BF16),
        scratch_shapes=[pltpu.VMEM((B, DN_HEADS, DN_HEAD_DIM, DN_HEAD_DIM), F32)],
        compiler_params=_cparams("arbitrary"),
    )(r3(u), r3(w), r3(qd), r3(kd), r3(qk), gl.reshape(B, S // DN_CHUNK * 8, DN_WIDTH), proj3,
      norm_g.reshape(1, DN_HEAD_DIM))


def _forward(x, mem, mem_norm, norm_mix, norm_xattn, norm_ffn, ev_w_in, pool_w, pool_scale, ev_w_out,
             od_w_in, sgu_ln_g, sgu_ln_b, sgu_w, sgu_b, dn_conv, dn_a_log, dn_dt_bias, dn_norm_g,
             od_w_out, xattn_wq, xattn_wkv, xattn_wo, ffn_w_up, ffn_conv, ffn_w_down, final_norm):
    B, S, D = x.shape
    n_mem = mem.shape[1]
    M = B * S
    depth = norm_mix.shape[0]
    bf = lambda a: a.astype(BF16)
    bm = min(512, S)

    h = x.reshape(M, D)
    mem2 = mem.reshape(B * n_mem, D)
    for layer in range(depth):
        i = layer // 2
        if layer % 2 == 0:
            proj = norm_matmul(h, norm_mix[layer], bf(ev_w_in[i]), BF16, bm, 1024)
            a_out = moba_attention(proj.reshape(B, S, -1), B, S).reshape(M, A_WIDTH)
            b_out = multiscale_pool(proj, bf(pool_w[i]), pool_scale[i], M, S, bm)
            h = mix_residual(h, a_out, b_out, bf(ev_w_out[i]), bm)
        else:
            main_w = 2 * SGU_WIDTH + 4 * DN_WIDTH
            w_in = od_w_in[i]
            proj = norm_matmul(h, norm_mix[layer], bf(w_in[:, :main_w]), BF16, bm, 1024)
            w_tail = jnp.pad(w_in[:, main_w:], ((0, 0), (0, LANES - 2 * DN_HEADS)))
            tail = norm_matmul(h, norm_mix[layer], bf(w_tail), F32, bm, LANES)
            c_out = spatial_gating(proj, sgu_ln_g[i], sgu_ln_b[i], sgu_w[i], sgu_b[i], M, min(256, S))
            u, w, qd, kd, qk, gl = gdn_intra(proj, tail, dn_conv[i], dn_a_log[i], dn_dt_bias[i], M, S,
                                             min(128, S))
            d_out = gdn_scan(u, w, qd, kd, qk, gl, proj.reshape(B, S, -1), dn_norm_g[i], B, S, 1)
            h = mix_residual(h, c_out, d_out.reshape(M, DN_WIDTH), bf(od_w_out[i]), bm)
        kv = norm_matmul(mem2, mem_norm, bf(xattn_wkv[layer]), BF16, B * n_mem, 1024)
        h = xattn_residual(h, norm_xattn[layer], bf(xattn_wq[layer]), kv, bf(xattn_wo[layer]), S, n_mem, bm)
        up = norm_matmul(h, norm_ffn[layer], bf(ffn_w_up[layer]), BF16, bm, 512)
        h = ffn_down_residual(h, up, ffn_conv[layer], bf(ffn_w_down[layer]), final_norm, S, min(256, S),
                              final_norm=(layer == depth - 1))
    return h.reshape(B, S, D)


def kernel(x, mem, mem_norm, norm_mix, norm_xattn, norm_ffn, ev_w_in, pool_w, pool_scale, ev_w_out, od_w_in, sgu_ln_g, sgu_ln_b, sgu_w, sgu_b, dn_conv, dn_a_log, dn_dt_bias, dn_norm_g, od_w_out, xattn_wq, xattn_wkv, xattn_wo, ffn_w_up, ffn_conv, ffn_w_down, final_norm):
    return _forward(x, mem, mem_norm, norm_mix, norm_xattn, norm_ffn, ev_w_in, pool_w, pool_scale, ev_w_out,
                    od_w_in, sgu_ln_g, sgu_ln_b, sgu_w, sgu_b, dn_conv, dn_a_log, dn_dt_bias, dn_norm_g,
                    od_w_out, xattn_wq, xattn_wkv, xattn_wo, ffn_w_up, ffn_conv, ffn_w_down, final_norm)
```

```python
import functools
import math

import jax
import jax.numpy as jnp
from jax import lax
from jax.experimental import pallas as pl
from jax.experimental.pallas import tpu as pltpu

F32 = jnp.float32
BF16 = jnp.bfloat16
EPS = 1e-6
NEG_BIG = -1e30

VMEM_LIMIT_BYTES = 48 * 1024 * 1024
BF16_SUBLANES = 16
LANES = 128

MOBA_HEADS, MOBA_HEAD_DIM, MOBA_BLOCK, MOBA_TOPK = 8, 64, 256, 3
A_WIDTH = MOBA_HEADS * MOBA_HEAD_DIM
POOL_WINDOWS = (2, 4, 8, 16)
POOL_GROUP = 128
POOL_WIDTH = POOL_GROUP * len(POOL_WINDOWS)
SGU_GROUPS, SGU_GROUP, SGU_CHUNK = 4, 128, 128
SGU_WIDTH = SGU_GROUPS * SGU_GROUP
DN_HEADS, DN_HEAD_DIM, DN_CONV, DN_CHUNK = 4, 128, 4, 64
DN_WIDTH = DN_HEADS * DN_HEAD_DIM
XATTN_HEADS = 4
FFN_CONV = 3


def _cparams(*sem):
    return pltpu.CompilerParams(dimension_semantics=sem, vmem_limit_bytes=VMEM_LIMIT_BYTES)


def _rmsnorm(x, g):
    return x * lax.rsqrt(jnp.mean(x * x, axis=-1, keepdims=True) + EPS) * g


def _silu(x):
    return x / (1.0 + jnp.exp(-x))


def _dot(a, b):
    return jnp.dot(a, b, preferred_element_type=F32)


def _dot_nt(a, b, precision=None):
    return lax.dot_general(a, b, (((1,), (1,)), ((), ())), preferred_element_type=F32,
                           precision=precision)


def _dot_tn(a, b):
    return lax.dot_general(a, b, (((0,), (0,)), ((), ())), preferred_element_type=F32)


def _norm_mm_kernel(x_ref, g_ref, *refs, bn):
    n = len(refs) // 2
    xn = _rmsnorm(x_ref[...], g_ref[...]).astype(BF16)
    for w_ref, o_ref in zip(refs[:n], refs[n:]):
        N = w_ref.shape[1]
        for c0 in range(0, N, bn):
            c1 = min(c0 + bn, N)
            o_ref[:, c0:c1] = _dot(xn, w_ref[:, c0:c1]).astype(o_ref.dtype)


def norm_matmul(x, g, ws, out_dtypes, bm, bn=512):
    M, D = x.shape
    return pl.pallas_call(
        functools.partial(_norm_mm_kernel, bn=bn),
        grid=(M // bm,),
        in_specs=[pl.BlockSpec((bm, D), lambda i: (i, 0)),
                  pl.BlockSpec((1, D), lambda i: (0, 0))]
                 + [pl.BlockSpec(w.shape, lambda i: (0, 0)) for w in ws],
        out_specs=[pl.BlockSpec((bm, w.shape[1]), lambda i: (i, 0)) for w in ws],
        out_shape=[jax.ShapeDtypeStruct((M, w.shape[1]), dt) for w, dt in zip(ws, out_dtypes)],
        compiler_params=_cparams("parallel"),
    )(x, g.reshape(1, D), *ws)


def _moba_kernel(q_ref, k_ref, v_ref, o_ref, kme_ref, vt_ref, sel_ref, m_ref, acc_ref, s_ref, sd_ref, *,
                 nb, nbp, unroll):
    BS = MOBA_BLOCK
    HD = MOBA_HEAD_DIM
    i = pl.program_id(2)
    lane = lax.broadcasted_iota(jnp.int32, (1, LANES), 1)
    head_lanes = (lane < HD, lane >= HD)

    @pl.when(i == 0)
    def _():
        kme_ref[...] = jnp.zeros_like(kme_ref)
        for n in range(nb):
            rows = slice(n * BS, (n + 1) * BS)
            mean = jnp.sum(k_ref[0, rows, :].astype(F32), axis=0, keepdims=True) / BS
            kme_ref[n:n + 1, :] = jnp.where(head_lanes[0], mean, 0.0)
            kme_ref[nbp + n:nbp + n + 1, :] = jnp.where(head_lanes[1], mean, 0.0)
            vt_ref[:, rows] = v_ref[0, rows, :].astype(F32).T.astype(BF16)

    q = q_ref[0]
    scale = HD ** -0.5 * math.log2(math.e)
    q_aug = [jnp.where(head_lanes[hh], q.astype(F32) * scale, 0.0).astype(BF16) for hh in range(2)]
    k_own = k_ref[0, pl.ds(pl.multiple_of(i * BS, BS), BS), :]
    for hh in range(2):
        sd_ref[hh] = _dot_nt(k_own, q_aug[hh])

    gate = _dot_nt(kme_ref[...], q.astype(F32), precision=lax.Precision.HIGHEST)
    blk = lax.broadcasted_iota(jnp.int32, (nbp, 1), 0).astype(F32)
    valid = blk < i.astype(F32)
    for hh in range(2):
        g = jnp.where(valid, gate[hh * nbp:(hh + 1) * nbp], -jnp.inf)
        sel = jnp.zeros(g.shape, jnp.bool_)
        for _ in range(MOBA_TOPK):
            mx = jnp.max(g, axis=0, keepdims=True)
            idx = jnp.min(jnp.where(g == mx, blk, float(1 << 20)), axis=0, keepdims=True)
            pick = blk == idx
            sel = sel | pick
            g = jnp.where(pick, -jnp.inf, g)
        sel_ref[hh, :nbp, :] = jnp.where(sel & valid, 1.0, 0.0)
        sel_ref[hh, nbp:, :] = jnp.zeros((8, BS), F32)
        m_ref[hh] = jnp.full((1, BS), NEG_BIG, F32)
        acc_ref[hh] = jnp.zeros((LANES, BS), F32)

    krow = lax.broadcasted_iota(jnp.int32, (BS, BS), 0)
    qcol = lax.broadcasted_iota(jnp.int32, (BS, BS), 1)
    dim_row = lax.broadcasted_iota(jnp.int32, (LANES, 1), 0)
    head_rows = (dim_row < HD, dim_row >= HD)

    def block_start(j):
        return pl.multiple_of(jnp.minimum(j, nb - 1) * BS, BS)

    def produce(g, slot):
        for t in range(unroll):
            kj = k_ref[0, pl.ds(block_start(g * unroll + t), BS), :]
            for hh in range(2):
                s_ref[slot, hh, t * BS:(t + 1) * BS, :] = _dot_nt(kj, q_aug[hh])

    def softmax_update(hh, sts, sels, vts):
        cand = jnp.full((1, BS), NEG_BIG, F32)
        for st, sel in zip(sts, sels):
            mx = jnp.max(st, axis=0, keepdims=True)
            cand = jnp.maximum(cand, mx if sel is None else jnp.where(sel, mx, NEG_BIG))
        m_old = m_ref[hh]
        m_new = jnp.maximum(m_old, cand)
        alpha = jnp.exp2(m_old - m_new)
        ps = []
        for st, sel in zip(sts, sels):
            sub = m_new if sel is None else jnp.where(sel, m_new, -NEG_BIG)
            ps.append(jnp.exp2(st - sub).astype(BF16))
        p = ps[0] if len(ps) == 1 else jnp.concatenate(ps, axis=0)
        vts = [jnp.where(head_rows[hh], vt, jnp.ones((), BF16)) for vt in vts]
        vt = vts[0] if len(vts) == 1 else jnp.concatenate(vts, axis=1)
        acc_ref[hh] = acc_ref[hh] * alpha + _dot(vt, p)
        m_ref[hh] = m_new

    def consume(g, slot):
        for hh in range(2):
            sts, sels, vts = [], [], []
            for t in range(unroll):
                j = g * unroll + t
                sts.append(s_ref[slot, hh, t * BS:(t + 1) * BS, :])
                sels.append(sel_ref[hh, pl.ds(j, 1), :] > 0.5)
                vts.append(vt_ref[:, pl.ds(block_start(j), BS)])
            softmax_update(hh, sts, sels, vts)

    def body(gg, c):
        produce(2 * gg + 1, 1)
        consume(2 * gg, 0)
        produce(2 * gg + 2, 0)
        consume(2 * gg + 1, 1)
        return c

    n_groups = (i + unroll - 1) // unroll
    produce(0, 0)
    lax.fori_loop(0, (n_groups + 1) // 2, body, 0)
    for hh in range(2):
        st = jnp.where(krow <= qcol, sd_ref[hh], -jnp.inf)
        softmax_update(hh, [st], [None], [vt_ref[:, pl.ds(block_start(i), BS)]])

    a0, a1 = acc_ref[0], acc_ref[1]
    out_t = jnp.where(head_rows[0], a0 / a0[HD:HD + 1, :], a1 / a1[0:1, :])
    o_ref[0] = out_t.T.astype(o_ref.dtype)


def moba_attention(proj, B, S):
    nb = S // MOBA_BLOCK
    nbp = -(-nb // 8) * 8
    pairs = A_WIDTH // LANES
    unroll = 2
    return pl.pallas_call(
        functools.partial(_moba_kernel, nb=nb, nbp=nbp, unroll=unroll),
        grid=(B, pairs, nb),
        in_specs=[pl.BlockSpec((1, MOBA_BLOCK, LANES), lambda b, p, i: (b, i, p)),
                  pl.BlockSpec((1, S, LANES), lambda b, p, i: (b, 0, pairs + p)),
                  pl.BlockSpec((1, S, LANES), lambda b, p, i: (b, 0, 2 * pairs + p))],
        out_specs=pl.BlockSpec((1, MOBA_BLOCK, LANES), lambda b, p, i: (b, i, p)),
        out_shape=jax.ShapeDtypeStruct((B, S, A_WIDTH), BF16),
        scratch_shapes=[pltpu.VMEM((2 * nbp, LANES), F32),
                        pltpu.VMEM((LANES, S), BF16),
                        pltpu.VMEM((2, nbp + 8, MOBA_BLOCK), F32),
                        pltpu.VMEM((2, 1, MOBA_BLOCK), F32),
                        pltpu.VMEM((2, LANES, MOBA_BLOCK), F32),
                        pltpu.VMEM((2, 2, unroll * MOBA_BLOCK, MOBA_BLOCK), F32),
                        pltpu.VMEM((2, MOBA_BLOCK, MOBA_BLOCK), F32)],
        compiler_params=_cparams("parallel", "parallel", "arbitrary"),
    )(proj, proj, proj)


def _pool_kernel(p_ref, halo_ref, w_ref, sc_ref, o_ref, *, blocks_per_seq):
    bm = p_ref.shape[0]
    H = BF16_SUBLANES
    i = pl.program_id(0)
    first = (i % blocks_per_seq) == 0
    t1 = (lax.broadcasted_iota(jnp.int32, (bm, 1), 0) + (i % blocks_per_seq) * bm + 1).astype(F32)
    for g, w in enumerate(POOL_WINDOWS):
        cols = slice(g * POOL_GROUP, (g + 1) * POOL_GROUP)
        cur = p_ref[:, cols].astype(F32)
        halo = jnp.where(first, 0.0, halo_ref[:, cols].astype(F32))
        ext = jnp.concatenate([halo, cur], axis=0)
        acc = ext
        sh = 1
        while sh < w:
            acc = acc + pltpu.roll(acc, sh, axis=0)
            sh *= 2
        win = acc[H:, :]
        pooled = win / jnp.minimum(t1, float(w)) - cur
        y = _dot(pooled.astype(BF16), w_ref[g])
        o_ref[:, cols] = (y * sc_ref[:, cols]).astype(o_ref.dtype)


def multiscale_pool(proj, pool_w, pool_scale, M, S, bm):
    H = BF16_SUBLANES
    pcol = 3 * A_WIDTH // POOL_WIDTH
    return pl.pallas_call(
        functools.partial(_pool_kernel, blocks_per_seq=S // bm),
        grid=(M // bm,),
        in_specs=[pl.BlockSpec((bm, POOL_WIDTH), lambda i: (i, pcol)),
                  pl.BlockSpec((H, POOL_WIDTH), lambda i: (jnp.maximum(i * (bm // H) - 1, 0), pcol)),
                  pl.BlockSpec((len(POOL_WINDOWS), POOL_GROUP, POOL_GROUP), lambda i: (0, 0, 0)),
                  pl.BlockSpec((1, POOL_WIDTH), lambda i: (0, 0))],
        out_specs=pl.BlockSpec((bm, POOL_WIDTH), lambda i: (i, 0)),
        out_shape=jax.ShapeDtypeStruct((M, POOL_WIDTH), BF16),
        compiler_params=_cparams("parallel"),
    )(proj, proj, pool_w, pool_scale.reshape(1, POOL_WIDTH))


def _mix_kernel(h_ref, a_ref, b_ref, w_ref, o_ref):
    ka = a_ref.shape[1]
    o_ref[...] = h_ref[...] + _dot(a_ref[...], w_ref[:ka, :]) + _dot(b_ref[...], w_ref[ka:, :])


def mix_residual(h, a, b, w, bm):
    M, D = h.shape
    ka, kb = a.shape[1], b.shape[1]
    return pl.pallas_call(
        _mix_kernel,
        grid=(M // bm,),
        in_specs=[pl.BlockSpec((bm, D), lambda i: (i, 0)),
                  pl.BlockSpec((bm, ka), lambda i: (i, 0)),
                  pl.BlockSpec((bm, kb), lambda i: (i, 0)),
                  pl.BlockSpec((ka + kb, D), lambda i: (0, 0))],
        out_specs=pl.BlockSpec((bm, D), lambda i: (i, 0)),
        out_shape=jax.ShapeDtypeStruct((M, D), F32),
        compiler_params=_cparams("parallel"),
    )(h, a, b, w)


def _xattn_kernel(h_ref, g_ref, wq_ref, k_ref, v_ref, wo_ref, o_ref):
    h = h_ref[...]
    D = h.shape[1]
    hd = D // XATTN_HEADS
    xn = _rmsnorm(h, g_ref[...]).astype(BF16)
    q = (_dot(xn, wq_ref[...]) * hd ** -0.5).astype(BF16)
    outs = []
    for hh in range(XATTN_HEADS):
        cols = slice(hh * hd, (hh + 1) * hd)
        s = _dot_nt(q[:, cols], k_ref[:, cols])
        m = jnp.max(s, axis=1, keepdims=True)
        p = jnp.exp(s - m)
        l = jnp.sum(p, axis=1, keepdims=True)
        outs.append((_dot(p.astype(BF16), v_ref[:, cols]) / l).astype(BF16))
    o = jnp.concatenate(outs, axis=1)
    o_ref[...] = h + _dot(o, wo_ref[...])


def xattn_residual(h, g, wq, kv, wo, S, n_mem, bm):
    M, D = h.shape
    bps = S // bm
    return pl.pallas_call(
        _xattn_kernel,
        grid=(M // bm,),
        in_specs=[pl.BlockSpec((bm, D), lambda i: (i, 0)),
                  pl.BlockSpec((1, D), lambda i: (0, 0)),
                  pl.BlockSpec((D, D), lambda i: (0, 0)),
                  pl.BlockSpec((n_mem, D), lambda i: (i // bps, 0)),
                  pl.BlockSpec((n_mem, D), lambda i: (i // bps, 1)),
                  pl.BlockSpec((D, D), lambda i: (0, 0))],
        out_specs=pl.BlockSpec((bm, D), lambda i: (i, 0)),
        out_shape=jax.ShapeDtypeStruct((M, D), F32),
        compiler_params=_cparams("parallel"),
    )(h, g.reshape(1, D), wq, kv, kv, wo)


def _ffn_kernel(h_ref, halo_ref, g_ref, wg_ref, wu_ref, cwg_ref, cwu_ref, wd_ref, fg_ref, o_ref,
                xn_ref, acc_ref, *, blocks_per_seq, final_norm):
    H = BF16_SUBLANES
    c = pl.program_id(1)

    @pl.when(c == 0)
    def _():
        first = (pl.program_id(0) % blocks_per_seq) == 0
        xn_ref[:H, :] = jnp.where(first, 0.0, _rmsnorm(halo_ref[...], g_ref[...])).astype(BF16)
        xn_ref[H:, :] = _rmsnorm(h_ref[...], g_ref[...]).astype(BF16)
        acc_ref[...] = jnp.zeros_like(acc_ref)

    xn = xn_ref[...]

    def conv(w_ref, cw_ref):
        y = _dot(xn, w_ref[...])
        cw = cw_ref[...]
        out = y[H:, :] * cw[FFN_CONV - 1:FFN_CONV, :]
        for k in range(1, FFN_CONV):
            out = out + pltpu.roll(y, k, axis=0)[H:, :] * cw[FFN_CONV - 1 - k:FFN_CONV - k, :]
        return out

    act = _silu(conv(wg_ref, cwg_ref)) * conv(wu_ref, cwu_ref)
    acc_ref[...] += _dot(act.astype(BF16), wd_ref[...])

    @pl.when(c == pl.num_programs(1) - 1)
    def _():
        y = h_ref[...] + acc_ref[...]
        if final_norm:
            y = _rmsnorm(y, fg_ref[...])
        o_ref[...] = y


def ffn_residual(h, g, w_up, conv_w, w_down, final_g, S, bm, cf, final_norm):
    M, D = h.shape
    d_ff = w_down.shape[0]
    H = BF16_SUBLANES
    nc = d_ff // cf
    return pl.pallas_call(
        functools.partial(_ffn_kernel, blocks_per_seq=S // bm, final_norm=final_norm),
        grid=(M // bm, nc),
        in_specs=[pl.BlockSpec((bm, D), lambda i, c: (i, 0)),
                  pl.BlockSpec((H, D), lambda i, c: (jnp.maximum(i * (bm // H) - 1, 0), 0)),
                  pl.BlockSpec((1, D), lambda i, c: (0, 0)),
                  pl.BlockSpec((D, cf), lambda i, c: (0, c)),
                  pl.BlockSpec((D, cf), lambda i, c: (0, nc + c)),
                  pl.BlockSpec((FFN_CONV, cf), lambda i, c: (0, c)),
                  pl.BlockSpec((FFN_CONV, cf), lambda i, c: (0, nc + c)),
                  pl.BlockSpec((cf, D), lambda i, c: (c, 0)),
                  pl.BlockSpec((1, D), lambda i, c: (0, 0))],
        out_specs=pl.BlockSpec((bm, D), lambda i, c: (i, 0)),
        out_shape=jax.ShapeDtypeStruct((M, D), F32),
        scratch_shapes=[pltpu.VMEM((H + bm, D), BF16), pltpu.VMEM((bm, D), F32)],
        compiler_params=_cparams("parallel", "arbitrary"),
    )(h, h, g.reshape(1, D), w_up, w_up, conv_w, conv_w, w_down, final_g.reshape(1, D))


def _gelu_tanh(x):
    return 0.5 * x * (1.0 + jnp.tanh(math.sqrt(2.0 / math.pi) * (x + 0.044715 * (x * x * x))))


def _sgu_kernel(u_ref, v_ref, lg_ref, lb_ref, w_ref, bt_ref, o_ref):
    rows = u_ref.shape[0]
    T = SGU_CHUNK
    v = _gelu_tanh(v_ref[...].astype(F32))
    mu = jnp.mean(v, axis=-1, keepdims=True)
    d = v - mu
    var = jnp.mean(d * d, axis=-1, keepdims=True)
    vn = (d * lax.rsqrt(var + EPS) * lg_ref[...] + lb_ref[...]).astype(BF16)
    causal = (lax.broadcasted_iota(jnp.int32, (T, T), 1) <= lax.broadcasted_iota(jnp.int32, (T, T), 0))
    for g in range(SGU_GROUPS):
        cols = slice(g * SGU_GROUP, (g + 1) * SGU_GROUP)
        wg = jnp.where(causal, w_ref[g], 0.0).astype(BF16)
        bias = bt_ref[:, g:g + 1]
        for c in range(rows // T):
            rs = slice(c * T, (c + 1) * T)
            s = _dot(wg, vn[rs, cols]) + bias
            o_ref[rs, cols] = (_gelu_tanh(u_ref[rs, cols].astype(F32)) * s).astype(o_ref.dtype)


def spatial_gating(proj, ln_g, ln_b, w_s, b_s, M, rows):
    return pl.pallas_call(
        _sgu_kernel,
        grid=(M // rows,),
        in_specs=[pl.BlockSpec((rows, SGU_WIDTH), lambda i: (i, 0)),
                  pl.BlockSpec((rows, SGU_WIDTH), lambda i: (i, 1)),
                  pl.BlockSpec((1, SGU_WIDTH), lambda i: (0, 0)),
                  pl.BlockSpec((1, SGU_WIDTH), lambda i: (0, 0)),
                  pl.BlockSpec((SGU_GROUPS, SGU_CHUNK, SGU_CHUNK), lambda i: (0, 0, 0)),
                  pl.BlockSpec((SGU_CHUNK, SGU_GROUPS), lambda i: (0, 0))],
        out_specs=pl.BlockSpec((rows, SGU_WIDTH), lambda i: (i, 0)),
        out_shape=jax.ShapeDtypeStruct((M, SGU_WIDTH), BF16),
        compiler_params=_cparams("parallel"),
    )(proj, proj, ln_g.reshape(1, -1), ln_b.reshape(1, -1), w_s, b_s.T)


def _gdn_intra_kernel(q_ref, k_ref, v_ref, qh_ref, kh_ref, vh_ref, cw_ref, tail_ref, tailt_ref, alog_ref,
                      dtb_ref, u_ref, w_ref, qd_ref, kd_ref, qk_ref, gl_ref, *, blocks_per_seq):
    rows = q_ref.shape[0]
    C = DN_CHUNK
    HD = DN_HEAD_DIM
    H = BF16_SUBLANES
    first = (pl.program_id(0) % blocks_per_seq) == 0

    def conv_silu(part, ref, halo_ref):
        cur = ref[...].astype(F32)
        halo = jnp.where(first, 0.0, halo_ref[...].astype(F32))
        ext = jnp.concatenate([halo, cur], axis=0)
        cw = cw_ref[:, part * DN_WIDTH:(part + 1) * DN_WIDTH]
        y = cur * cw[DN_CONV - 1:DN_CONV, :]
        for k in range(1, DN_CONV):
            y = y + pltpu.roll(ext, k, axis=0)[H:, :] * cw[DN_CONV - 1 - k:DN_CONV - k, :]
        return _silu(y)

    x = jnp.concatenate([conv_silu(0, q_ref, qh_ref), conv_silu(1, k_ref, kh_ref),
                         conv_silu(2, v_ref, vh_ref)], axis=1)

    ii = lax.broadcasted_iota(jnp.int32, (C, C), 0)
    jj = lax.broadcasted_iota(jnp.int32, (C, C), 1)
    lower = jj <= ii
    strict = jj < ii
    su = lax.broadcasted_iota(jnp.int32, (C, LANES), 0)
    ju = lax.broadcasted_iota(jnp.int32, (C, LANES), 1)
    upper_ext = jnp.where(((ju < C) & (su > ju)) | (ju == C), 1.0, 0.0)

    inst = [(c, hh) for hh in range(DN_HEADS) for c in range(rows // C)]
    qs, ks, vs, bs, stacks = [], [], [], [], []
    for hh in range(DN_HEADS):
        qh = x[:, hh * HD:(hh + 1) * HD]
        kh = x[:, DN_WIDTH + hh * HD:DN_WIDTH + (hh + 1) * HD]
        vh = x[:, 2 * DN_WIDTH + hh * HD:2 * DN_WIDTH + (hh + 1) * HD]
        qh = qh * lax.rsqrt(jnp.sum(qh * qh, axis=-1, keepdims=True) + EPS) * HD ** -0.5
        kh = kh * lax.rsqrt(jnp.sum(kh * kh, axis=-1, keepdims=True) + EPS)
        beta = 1.0 / (1.0 + jnp.exp(-tail_ref[:, hh:hh + 1]))
        a_raw = tailt_ref[DN_HEADS + hh:DN_HEADS + hh + 1, :]
        z = a_raw + dtb_ref[0:1, hh:hh + 1]
        softplus = jnp.maximum(z, 0.0) + jnp.log(1.0 + jnp.exp(-jnp.abs(z)))
        g_row = -jnp.exp(alog_ref[0:1, hh:hh + 1]) * softplus
        for c in range(rows // C):
            rs = slice(c * C, (c + 1) * C)
            qs.append(qh[rs]); ks.append(kh[rs]); vs.append(vh[rs]); bs.append(beta[rs])
            gr = jnp.broadcast_to(g_row[:, rs], (C, C))
            stacks += [jnp.where(lower, gr, 0.0), jnp.where(lower, 0.0, gr)]

    stacked = jnp.concatenate(stacks, axis=0)
    s_hi = stacked.astype(BF16)
    r1 = stacked - s_hi.astype(F32)
    s_mid = r1.astype(BF16)
    s_lo = (r1 - s_mid.astype(F32)).astype(BF16)
    ue = upper_ext.astype(BF16)
    dall = _dot(s_hi, ue) + _dot(s_mid, ue) + _dot(s_lo, ue)

    decays, gcs, gc_revs, k16s, kbs = [], [], [], [], []
    for n, (c, hh) in enumerate(inst):
        dext = dall[n * 2 * C:(n + 1) * 2 * C]
        decays.append(jnp.exp(jnp.where(lower, dext[:C, :C], -jnp.inf)))
        gcs.append(dext[:C, C:C + 1])
        gc_revs.append(dext[C:, C:C + 1])
        kbs.append(ks[n] * bs[n])
        k16s.append(ks[n].astype(BF16))
    kk = [_dot_nt(kbs[n].astype(BF16), k16s[n]) for n in range(len(inst))]
    qk = [_dot_nt(qs[n].astype(BF16), k16s[n]) for n in range(len(inst))]
    pws = [jnp.where(strict, kk[n] * decays[n], 0.0).astype(BF16) for n in range(len(inst))]
    egc = [jnp.exp(g) for g in gcs]
    rhs = [jnp.concatenate([vs[n] * bs[n], kbs[n] * egc[n]], axis=1) for n in range(len(inst))]
    sols = [rhs[n] - _dot(pws[n], rhs[n].astype(BF16)) for n in range(len(inst))]
    for _ in range(int(math.log2(C)) - 1):
        pws = [_dot(p, p).astype(BF16) for p in pws]
        sols = [s + _dot(p, s.astype(BF16)) for p, s in zip(pws, sols)]
    for n, (c, hh) in enumerate(inst):
        rs = slice(c * C, (c + 1) * C)
        hcols = slice(hh * HD, (hh + 1) * HD)
        u_ref[rs, hcols] = sols[n][:, :HD].astype(u_ref.dtype)
        w_ref[rs, hcols] = sols[n][:, HD:].astype(w_ref.dtype)
        qkd = jnp.where(lower, qk[n] * decays[n], 0.0)
        qk_ref[rs, hcols] = jnp.concatenate([qkd, jnp.zeros_like(qkd)], axis=1).astype(qk_ref.dtype)
        qd_ref[rs, hcols] = (qs[n] * egc[n]).astype(qd_ref.dtype)
        kd_ref[rs, hcols] = (ks[n] * jnp.exp(gc_revs[n])).astype(kd_ref.dtype)
        gl_ref[c * 8:(c + 1) * 8, hcols] = jnp.broadcast_to(egc[n][C - 1:C, :], (8, HD))


def gdn_intra(proj, tail, dn_conv, a_log, dt_bias, M, S, rows):
    H = BF16_SUBLANES
    c0 = 2 * SGU_WIDTH // DN_WIDTH
    tail_t = tail[:, :2 * DN_HEADS].T
    pad = lambda p: jnp.pad(p.reshape(1, -1), ((0, 0), (0, LANES - p.shape[0])))
    W3 = 3 * DN_WIDTH
    seq = lambda dt: jax.ShapeDtypeStruct((M, DN_WIDTH), dt)
    row_spec = pl.BlockSpec((rows, DN_WIDTH), lambda i: (i, 0))
    cur_spec = lambda part: pl.BlockSpec((rows, DN_WIDTH), lambda i: (i, c0 + part))
    halo_spec = lambda part: pl.BlockSpec(
        (H, DN_WIDTH), lambda i: (jnp.maximum(i * (rows // H) - 1, 0), c0 + part))
    return pl.pallas_call(
        functools.partial(_gdn_intra_kernel, blocks_per_seq=S // rows),
        grid=(M // rows,),
        in_specs=[cur_spec(0), cur_spec(1), cur_spec(2), halo_spec(0), halo_spec(1), halo_spec(2),
                  pl.BlockSpec((DN_CONV, W3), lambda i: (0, 0)),
                  pl.BlockSpec((rows, LANES), lambda i: (i, 0)),
                  pl.BlockSpec((2 * DN_HEADS, rows), lambda i: (0, i)),
                  pl.BlockSpec((1, LANES), lambda i: (0, 0)),
                  pl.BlockSpec((1, LANES), lambda i: (0, 0))],
        out_specs=[row_spec, row_spec, row_spec, row_spec, row_spec,
                   pl.BlockSpec((rows // DN_CHUNK * 8, DN_WIDTH), lambda i: (i, 0))],
        out_shape=[seq(BF16), seq(BF16), seq(BF16), seq(BF16), seq(BF16),
                   jax.ShapeDtypeStruct((M // DN_CHUNK * 8, DN_WIDTH), F32)],
        compiler_params=_cparams("parallel"),
    )(proj, proj, proj, proj, proj, proj, dn_conv, tail, tail_t, pad(a_log), pad(dt_bias))


def _gdn_scan_kernel(u_ref, w_ref, qd_ref, kd_ref, qk_ref, gl_ref, gate_ref, ng_ref, o_ref, state_ref, *,
                     chunks):
    C = DN_CHUNK
    HD = DN_HEAD_DIM
    B = u_ref.shape[0]

    @pl.when(pl.program_id(0) == 0)
    def _():
        state_ref[...] = jnp.zeros_like(state_ref)

    ng = ng_ref[...]
    for c in range(chunks):
        rs = slice(c * C, (c + 1) * C)
        for b in range(B):
            for hh in range(DN_HEADS):
                cols = slice(hh * HD, (hh + 1) * HD)
                st = state_ref[b, hh]
                st16 = st.astype(BF16)
                v_new = u_ref[b, rs, cols].astype(F32) - _dot(w_ref[b, rs, cols], st16)
                vn16 = v_new.astype(BF16)
                o = _dot(qd_ref[b, rs, cols], st16) + _dot(qk_ref[b, rs, cols][:, :C], vn16)
                gl = gl_ref[b, c * 8:c * 8 + 1, cols]
                state_ref[b, hh] = st * gl + _dot_tn(kd_ref[b, rs, cols], vn16)
                o = o * lax.rsqrt(jnp.mean(o * o, axis=-1, keepdims=True) + EPS) * ng
                o_ref[b, rs, cols] = (o * _silu(gate_ref[b, rs, cols].astype(F32))).astype(o_ref.dtype)


def gdn_scan(u, w, qd, kd, qk, gl, proj3, norm_g, B, S, chunks):
    rows = chunks * DN_CHUNK
    r3 = lambda a: a.reshape(B, S, DN_WIDTH)
    gcol = (2 * SGU_WIDTH + 3 * DN_WIDTH) // DN_WIDTH
    seq_spec = pl.BlockSpec((B, rows, DN_WIDTH), lambda n: (0, n, 0))
    return pl.pallas_call(
        functools.partial(_gdn_scan_kernel, chunks=chunks),
        grid=(S // rows,),
        in_specs=[seq_spec, seq_spec, seq_spec, seq_spec, seq_spec,
                  pl.BlockSpec((B, chunks * 8, DN_WIDTH), lambda n: (0, n, 0)),
                  pl.BlockSpec((B, rows, DN_WIDTH), lambda n: (0, n, gcol)),
                  pl.BlockSpec((1, DN_HEAD_DIM), lambda n: (0, 0))],
        out_specs=seq_spec,
        out_shape=jax.ShapeDtypeStruct((B, S, DN_WIDTH), BF16),
        scratch_shapes=[pltpu.VMEM((B, DN_HEADS, DN_HEAD_DIM, DN_HEAD_DIM), F32)],
        compiler_params=_cparams("arbitrary"),
    )(r3(u), r3(w), r3(qd), r3(kd), r3(qk), gl.reshape(B, S // DN_CHUNK * 8, DN_WIDTH), proj3,
      norm_g.reshape(1, DN_HEAD_DIM))


def _forward(x, mem, mem_norm, norm_mix, norm_xattn, norm_ffn, ev_w_in, pool_w, pool_scale, ev_w_out,
             od_w_in, sgu_ln_g, sgu_ln_b, sgu_w, sgu_b, dn_conv, dn_a_log, dn_dt_bias, dn_norm_g,
             od_w_out, xattn_wq, xattn_wkv, xattn_wo, ffn_w_up, ffn_conv, ffn_w_down, final_norm):
    B, S, D = x.shape
    n_mem = mem.shape[1]
    M = B * S
    depth = norm_mix.shape[0]
    bf = lambda a: a.astype(BF16)
    bm = min(512, S)
    bm_big = min(1024, S)

    h = x.reshape(M, D)
    mem2 = mem.reshape(B * n_mem, D)
    for layer in range(depth):
        i = layer // 2
        if layer % 2 == 0:
            proj, = norm_matmul(h, norm_mix[layer], [bf(ev_w_in[i])], [BF16], bm_big)
            a_out = moba_attention(proj.reshape(B, S, -1), B, S).reshape(M, A_WIDTH)
            b_out = multiscale_pool(proj, bf(pool_w[i]), pool_scale[i], M, S, bm)
            h = mix_residual(h, a_out, b_out, bf(ev_w_out[i]), bm)
        else:
            main_w = 2 * SGU_WIDTH + 4 * DN_WIDTH
            w_in = od_w_in[i]
            w_tail = jnp.pad(w_in[:, main_w:], ((0, 0), (0, LANES - 2 * DN_HEADS)))
            proj, tail = norm_matmul(h, norm_mix[layer], [bf(w_in[:, :main_w]), bf(w_tail)], [BF16, F32],
                                     bm_big)
            c_out = spatial_gating(proj, sgu_ln_g[i], sgu_ln_b[i], sgu_w[i], sgu_b[i], M, min(256, S))
            u, w, qd, kd, qk, gl = gdn_intra(proj, tail, dn_conv[i], dn_a_log[i], dn_dt_bias[i], M, S,
                                             min(256, S))
            d_out = gdn_scan(u, w, qd, kd, qk, gl, proj.reshape(B, S, -1), dn_norm_g[i], B, S, 1)
            h = mix_residual(h, c_out, d_out.reshape(M, DN_WIDTH), bf(od_w_out[i]), bm)
        kv, = norm_matmul(mem2, mem_norm, [bf(xattn_wkv[layer])], [BF16], B * n_mem)
        h = xattn_residual(h, norm_xattn[layer], bf(xattn_wq[layer]), kv, bf(xattn_wo[layer]), S, n_mem, bm)
        h = ffn_residual(h, norm_ffn[layer], bf(ffn_w_up[layer]), ffn_conv[layer], bf(ffn_w_down[layer]),
                         final_norm, S, bm_big, 256, final_norm=(layer == depth - 1))
    return h.reshape(B, S, D)


def kernel(x, mem, mem_norm, norm_mix, norm_xattn, norm_ffn, ev_w_in, pool_w, pool_scale, ev_w_out, od_w_in, sgu_ln_g, sgu_ln_b, sgu_w, sgu_b, dn_conv, dn_a_log, dn_dt_bias, dn_norm_g, od_w_out, xattn_wq, xattn_wkv, xattn_wo, ffn_w_up, ffn_conv, ffn_w_down, final_norm):
    return _forward(x, mem, mem_norm, norm_mix, norm_xattn, norm_ffn, ev_w_in, pool_w, pool_scale, ev_w_out,
                    od_w_in, sgu_ln_g, sgu_ln_b, sgu_w, sgu_b, dn_conv, dn_a_log, dn_dt_bias, dn_norm_g,
                    od_w_out, xattn_wq, xattn_wkv, xattn_wo, ffn_w_up, ffn_conv, ffn_w_down, final_norm)
```

```python
import functools
import math

import jax
import jax.numpy as jnp
from jax import lax
from jax.experimental import pallas as pl
from jax.experimental.pallas import tpu as pltpu

F32 = jnp.float32
BF16 = jnp.bfloat16
EPS = 1e-6
NEG_BIG = -1e30

VMEM_LIMIT_BYTES = 48 * 1024 * 1024
BF16_SUBLANES = 16
LANES = 128

MOBA_HEADS, MOBA_HEAD_DIM, MOBA_BLOCK, MOBA_TOPK = 8, 64, 256, 3
A_WIDTH = MOBA_HEADS * MOBA_HEAD_DIM
POOL_WINDOWS = (2, 4, 8, 16)
POOL_GROUP = 128
POOL_WIDTH = POOL_GROUP * len(POOL_WINDOWS)
SGU_GROUPS, SGU_GROUP, SGU_CHUNK = 4, 128, 128
SGU_WIDTH = SGU_GROUPS * SGU_GROUP
DN_HEADS, DN_HEAD_DIM, DN_CONV, DN_CHUNK = 4, 128, 4, 64
DN_WIDTH = DN_HEADS * DN_HEAD_DIM
XATTN_HEADS = 4
FFN_CONV = 3


def _cparams(*sem):
    return pltpu.CompilerParams(dimension_semantics=sem, vmem_limit_bytes=VMEM_LIMIT_BYTES)


def _rmsnorm(x, g):
    return x * lax.rsqrt(jnp.mean(x * x, axis=-1, keepdims=True) + EPS) * g


def _silu(x):
    return x * (0.5 * jnp.tanh(0.5 * x) + 0.5)


def _dot(a, b):
    return jnp.dot(a, b, preferred_element_type=F32)


def _dot_nt(a, b, precision=None):
    return lax.dot_general(a, b, (((1,), (1,)), ((), ())), preferred_element_type=F32,
                           precision=precision)


def _dot_tn(a, b):
    return lax.dot_general(a, b, (((0,), (0,)), ((), ())), preferred_element_type=F32)


def _norm_mm_kernel(x_ref, g_ref, *refs, bn):
    n = len(refs) // 2
    xn = _rmsnorm(x_ref[...], g_ref[...]).astype(BF16)
    for w_ref, o_ref in zip(refs[:n], refs[n:]):
        N = w_ref.shape[1]
        for c0 in range(0, N, bn):
            c1 = min(c0 + bn, N)
            o_ref[:, c0:c1] = _dot(xn, w_ref[:, c0:c1]).astype(o_ref.dtype)


def norm_matmul(x, g, ws, out_dtypes, bm, bn=512):
    M, D = x.shape
    return pl.pallas_call(
        functools.partial(_norm_mm_kernel, bn=bn),
        grid=(M // bm,),
        in_specs=[pl.BlockSpec((bm, D), lambda i: (i, 0)),
                  pl.BlockSpec((1, D), lambda i: (0, 0))]
                 + [pl.BlockSpec(w.shape, lambda i: (0, 0)) for w in ws],
        out_specs=[pl.BlockSpec((bm, w.shape[1]), lambda i: (i, 0)) for w in ws],
        out_shape=[jax.ShapeDtypeStruct((M, w.shape[1]), dt) for w, dt in zip(ws, out_dtypes)],
        compiler_params=_cparams("parallel"),
    )(x, g.reshape(1, D), *ws)


def _moba_kernel(q_ref, k_ref, v_ref, o_ref, kme_ref, vt_ref, sel_ref, m_ref, acc_ref, s_ref, sd_ref, *,
                 nb, nbp, unroll):
    BS = MOBA_BLOCK
    HD = MOBA_HEAD_DIM
    i = pl.program_id(2)
    lane = lax.broadcasted_iota(jnp.int32, (1, LANES), 1)
    head_lanes = (lane < HD, lane >= HD)

    @pl.when(i == 0)
    def _():
        kme_ref[...] = jnp.zeros_like(kme_ref)
        for n in range(nb):
            rows = slice(n * BS, (n + 1) * BS)
            mean = jnp.sum(k_ref[0, rows, :].astype(F32), axis=0, keepdims=True) / BS
            kme_ref[n:n + 1, :] = jnp.where(head_lanes[0], mean, 0.0)
            kme_ref[nbp + n:nbp + n + 1, :] = jnp.where(head_lanes[1], mean, 0.0)
            vt_ref[:, rows] = v_ref[0, rows, :].astype(F32).T.astype(BF16)

    q = q_ref[0]
    scale = HD ** -0.5 * math.log2(math.e)
    q_aug = [jnp.where(head_lanes[hh], q.astype(F32) * scale, 0.0).astype(BF16) for hh in range(2)]
    k_own = k_ref[0, pl.ds(pl.multiple_of(i * BS, BS), BS), :]
    for hh in range(2):
        sd_ref[hh] = _dot_nt(k_own, q_aug[hh])

    gate = _dot_nt(kme_ref[...], q.astype(F32), precision=lax.Precision.HIGHEST)
    blk = lax.broadcasted_iota(jnp.int32, (nbp, 1), 0).astype(F32)
    valid = blk < i.astype(F32)
    for hh in range(2):
        g = jnp.where(valid, gate[hh * nbp:(hh + 1) * nbp], -jnp.inf)
        sel = jnp.zeros(g.shape, jnp.bool_)
        for _ in range(MOBA_TOPK):
            mx = jnp.max(g, axis=0, keepdims=True)
            idx = jnp.min(jnp.where(g == mx, blk, float(1 << 20)), axis=0, keepdims=True)
            pick = blk == idx
            sel = sel | pick
            g = jnp.where(pick, -jnp.inf, g)
        sel_ref[hh, :nbp, :] = jnp.where(sel & valid, 1.0, 0.0)
        sel_ref[hh, nbp:, :] = jnp.zeros((8, BS), F32)
        m_ref[hh] = jnp.full((1, BS), NEG_BIG, F32)
        acc_ref[hh] = jnp.zeros((LANES, BS), F32)

    krow = lax.broadcasted_iota(jnp.int32, (BS, BS), 0)
    qcol = lax.broadcasted_iota(jnp.int32, (BS, BS), 1)
    dim_row = lax.broadcasted_iota(jnp.int32, (LANES, 1), 0)
    head_rows = (dim_row < HD, dim_row >= HD)

    def block_start(j):
        return pl.multiple_of(jnp.minimum(j, nb - 1) * BS, BS)

    def produce(g, slot):
        for t in range(unroll):
            kj = k_ref[0, pl.ds(block_start(g * unroll + t), BS), :]
            for hh in range(2):
                s_ref[slot, hh, t * BS:(t + 1) * BS, :] = _dot_nt(kj, q_aug[hh])

    def softmax_update(hh, sts, sels, vts):
        cand = jnp.full((1, BS), NEG_BIG, F32)
        for st, sel in zip(sts, sels):
            mx = jnp.max(st, axis=0, keepdims=True)
            cand = jnp.maximum(cand, mx if sel is None else jnp.where(sel, mx, NEG_BIG))
        m_old = m_ref[hh]
        m_new = jnp.maximum(m_old, cand)
        alpha = jnp.exp2(m_old - m_new)
        ps = []
        for st, sel in zip(sts, sels):
            sub = m_new if sel is None else jnp.where(sel, m_new, -NEG_BIG)
            ps.append(jnp.exp2(st - sub).astype(BF16))
        p = ps[0] if len(ps) == 1 else jnp.concatenate(ps, axis=0)
        vts = [jnp.where(head_rows[hh], vt, jnp.ones((), BF16)) for vt in vts]
        vt = vts[0] if len(vts) == 1 else jnp.concatenate(vts, axis=1)
        acc_ref[hh] = acc_ref[hh] * alpha + _dot(vt, p)
        m_ref[hh] = m_new

    def consume(g, slot):
        for hh in range(2):
            sts, sels, vts = [], [], []
            for t in range(unroll):
                j = g * unroll + t
                sts.append(s_ref[slot, hh, t * BS:(t + 1) * BS, :])
                sels.append(sel_ref[hh, pl.ds(j, 1), :] > 0.5)
                vts.append(vt_ref[:, pl.ds(block_start(j), BS)])
            softmax_update(hh, sts, sels, vts)

    def body(gg, c):
        produce(2 * gg + 1, 1)
        consume(2 * gg, 0)
        produce(2 * gg + 2, 0)
        consume(2 * gg + 1, 1)
        return c

    n_groups = (i + unroll - 1) // unroll
    produce(0, 0)
    lax.fori_loop(0, (n_groups + 1) // 2, body, 0)
    for hh in range(2):
        st = jnp.where(krow <= qcol, sd_ref[hh], -jnp.inf)
        softmax_update(hh, [st], [None], [vt_ref[:, pl.ds(block_start(i), BS)]])

    a0, a1 = acc_ref[0], acc_ref[1]
    out_t = jnp.where(head_rows[0], a0 / a0[HD:HD + 1, :], a1 / a1[0:1, :])
    o_ref[0] = out_t.T.astype(o_ref.dtype)


def moba_attention(proj, B, S):
    nb = S // MOBA_BLOCK
    nbp = -(-nb // 8) * 8
    pairs = A_WIDTH // LANES
    unroll = 2
    return pl.pallas_call(
        functools.partial(_moba_kernel, nb=nb, nbp=nbp, unroll=unroll),
        grid=(B, pairs, nb),
        in_specs=[pl.BlockSpec((1, MOBA_BLOCK, LANES), lambda b, p, i: (b, i, p)),
                  pl.BlockSpec((1, S, LANES), lambda b, p, i: (b, 0, pairs + p)),
                  pl.BlockSpec((1, S, LANES), lambda b, p, i: (b, 0, 2 * pairs + p))],
        out_specs=pl.BlockSpec((1, MOBA_BLOCK, LANES), lambda b, p, i: (b, i, p)),
        out_shape=jax.ShapeDtypeStruct((B, S, A_WIDTH), BF16),
        scratch_shapes=[pltpu.VMEM((2 * nbp, LANES), F32),
                        pltpu.VMEM((LANES, S), BF16),
                        pltpu.VMEM((2, nbp + 8, MOBA_BLOCK), F32),
                        pltpu.VMEM((2, 1, MOBA_BLOCK), F32),
                        pltpu.VMEM((2, LANES, MOBA_BLOCK), F32),
                        pltpu.VMEM((2, 2, unroll * MOBA_BLOCK, MOBA_BLOCK), F32),
                        pltpu.VMEM((2, MOBA_BLOCK, MOBA_BLOCK), F32)],
        compiler_params=_cparams("parallel", "parallel", "arbitrary"),
    )(proj, proj, proj)


def _pool_kernel(p_ref, halo_ref, w_ref, sc_ref, o_ref, *, blocks_per_seq):
    bm = p_ref.shape[0]
    H = BF16_SUBLANES
    i = pl.program_id(0)
    first = (i % blocks_per_seq) == 0
    t1 = (lax.broadcasted_iota(jnp.int32, (bm, 1), 0) + (i % blocks_per_seq) * bm + 1).astype(F32)
    for g, w in enumerate(POOL_WINDOWS):
        cols = slice(g * POOL_GROUP, (g + 1) * POOL_GROUP)
        cur = p_ref[:, cols].astype(F32)
        halo = jnp.where(first, 0.0, halo_ref[:, cols].astype(F32))
        ext = jnp.concatenate([halo, cur], axis=0)
        acc = ext
        sh = 1
        while sh < w:
            acc = acc + pltpu.roll(acc, sh, axis=0)
            sh *= 2
        win = acc[H:, :]
        pooled = win / jnp.minimum(t1, float(w)) - cur
        y = _dot(pooled.astype(BF16), w_ref[g])
        o_ref[:, cols] = (y * sc_ref[:, cols]).astype(o_ref.dtype)


def multiscale_pool(proj, pool_w, pool_scale, M, S, bm):
    H = BF16_SUBLANES
    pcol = 3 * A_WIDTH // POOL_WIDTH
    return pl.pallas_call(
        functools.partial(_pool_kernel, blocks_per_seq=S // bm),
        grid=(M // bm,),
        in_specs=[pl.BlockSpec((bm, POOL_WIDTH), lambda i: (i, pcol)),
                  pl.BlockSpec((H, POOL_WIDTH), lambda i: (jnp.maximum(i * (bm // H) - 1, 0), pcol)),
                  pl.BlockSpec((len(POOL_WINDOWS), POOL_GROUP, POOL_GROUP), lambda i: (0, 0, 0)),
                  pl.BlockSpec((1, POOL_WIDTH), lambda i: (0, 0))],
        out_specs=pl.BlockSpec((bm, POOL_WIDTH), lambda i: (i, 0)),
        out_shape=jax.ShapeDtypeStruct((M, POOL_WIDTH), BF16),
        compiler_params=_cparams("parallel"),
    )(proj, proj, pool_w, pool_scale.reshape(1, POOL_WIDTH))


def _mix_kernel(h_ref, a_ref, b_ref, w_ref, o_ref):
    ka = a_ref.shape[1]
    o_ref[...] = h_ref[...] + _dot(a_ref[...], w_ref[:ka, :]) + _dot(b_ref[...], w_ref[ka:, :])


def mix_residual(h, a, b, w, bm):
    M, D = h.shape
    ka, kb = a.shape[1], b.shape[1]
    return pl.pallas_call(
        _mix_kernel,
        grid=(M // bm,),
        in_specs=[pl.BlockSpec((bm, D), lambda i: (i, 0)),
                  pl.BlockSpec((bm, ka), lambda i: (i, 0)),
                  pl.BlockSpec((bm, kb), lambda i: (i, 0)),
                  pl.BlockSpec((ka + kb, D), lambda i: (0, 0))],
        out_specs=pl.BlockSpec((bm, D), lambda i: (i, 0)),
        out_shape=jax.ShapeDtypeStruct((M, D), F32),
        compiler_params=_cparams("parallel"),
    )(h, a, b, w)


def _xattn_kernel(h_ref, g_ref, wq_ref, k_ref, v_ref, wo_ref, o_ref):
    h = h_ref[...]
    D = h.shape[1]
    hd = D // XATTN_HEADS
    xn = _rmsnorm(h, g_ref[...]).astype(BF16)
    q = (_dot(xn, wq_ref[...]) * hd ** -0.5).astype(BF16)
    outs = []
    for hh in range(XATTN_HEADS):
        cols = slice(hh * hd, (hh + 1) * hd)
        s = _dot_nt(q[:, cols], k_ref[:, cols])
        m = jnp.max(s, axis=1, keepdims=True)
        p = jnp.exp(s - m)
        l = jnp.sum(p, axis=1, keepdims=True)
        outs.append((_dot(p.astype(BF16), v_ref[:, cols]) / l).astype(BF16))
    o = jnp.concatenate(outs, axis=1)
    o_ref[...] = h + _dot(o, wo_ref[...])


def xattn_residual(h, g, wq, kv, wo, S, n_mem, bm):
    M, D = h.shape
    bps = S // bm
    return pl.pallas_call(
        _xattn_kernel,
        grid=(M // bm,),
        in_specs=[pl.BlockSpec((bm, D), lambda i: (i, 0)),
                  pl.BlockSpec((1, D), lambda i: (0, 0)),
                  pl.BlockSpec((D, D), lambda i: (0, 0)),
                  pl.BlockSpec((n_mem, D), lambda i: (i // bps, 0)),
                  pl.BlockSpec((n_mem, D), lambda i: (i // bps, 1)),
                  pl.BlockSpec((D, D), lambda i: (0, 0))],
        out_specs=pl.BlockSpec((bm, D), lambda i: (i, 0)),
        out_shape=jax.ShapeDtypeStruct((M, D), F32),
        compiler_params=_cparams("parallel"),
    )(h, g.reshape(1, D), wq, kv, kv, wo)


def _ffn_kernel(h_ref, halo_ref, g_ref, wg_ref, wu_ref, cwg_ref, cwu_ref, wd_ref, fg_ref, o_ref,
                xn_ref, acc_ref, y_ref, *, blocks_per_seq, final_norm, sub):
    H = BF16_SUBLANES
    c = pl.program_id(1)

    @pl.when(c == 0)
    def _():
        first = (pl.program_id(0) % blocks_per_seq) == 0
        xn_ref[:H, :] = jnp.where(first, 0.0, _rmsnorm(halo_ref[...], g_ref[...])).astype(BF16)
        xn_ref[H:, :] = _rmsnorm(h_ref[...], g_ref[...]).astype(BF16)
        acc_ref[...] = jnp.zeros_like(acc_ref)

    cf = wg_ref.shape[1]
    n_sub = acc_ref.shape[0] // sub

    def up(r):
        xs = xn_ref[r * sub:r * sub + sub + H, :]
        y_ref[r % 2, :, :cf] = _dot(xs, wg_ref[...])
        y_ref[r % 2, :, cf:] = _dot(xs, wu_ref[...])

    def conv(r, part, cw_ref):
        cols = slice(part * cf, (part + 1) * cf)
        cw = cw_ref[...]
        out = y_ref[r % 2, H:, cols] * cw[FFN_CONV - 1:FFN_CONV, :]
        for k in range(1, FFN_CONV):
            out = out + y_ref[r % 2, H - k:H - k + sub, cols] * cw[FFN_CONV - 1 - k:FFN_CONV - k, :]
        return out

    up(0)
    for r in range(n_sub):
        if r + 1 < n_sub:
            up(r + 1)
        act = _silu(conv(r, 0, cwg_ref)) * conv(r, 1, cwu_ref)
        acc_ref[r * sub:(r + 1) * sub, :] += _dot(act.astype(BF16), wd_ref[...])

    @pl.when(c == pl.num_programs(1) - 1)
    def _():
        y = h_ref[...] + acc_ref[...]
        if final_norm:
            y = _rmsnorm(y, fg_ref[...])
        o_ref[...] = y


def ffn_residual(h, g, w_up, conv_w, w_down, final_g, S, bm, cf, final_norm):
    M, D = h.shape
    d_ff = w_down.shape[0]
    H = BF16_SUBLANES
    nc = d_ff // cf
    sub = min(256, bm)
    return pl.pallas_call(
        functools.partial(_ffn_kernel, blocks_per_seq=S // bm, final_norm=final_norm, sub=sub),
        grid=(M // bm, nc),
        in_specs=[pl.BlockSpec((bm, D), lambda i, c: (i, 0)),
                  pl.BlockSpec((H, D), lambda i, c: (jnp.maximum(i * (bm // H) - 1, 0), 0)),
                  pl.BlockSpec((1, D), lambda i, c: (0, 0)),
                  pl.BlockSpec((D, cf), lambda i, c: (0, c)),
                  pl.BlockSpec((D, cf), lambda i, c: (0, nc + c)),
                  pl.BlockSpec((FFN_CONV, cf), lambda i, c: (0, c)),
                  pl.BlockSpec((FFN_CONV, cf), lambda i, c: (0, nc + c)),
                  pl.BlockSpec((cf, D), lambda i, c: (c, 0)),
                  pl.BlockSpec((1, D), lambda i, c: (0, 0))],
        out_specs=pl.BlockSpec((bm, D), lambda i, c: (i, 0)),
        out_shape=jax.ShapeDtypeStruct((M, D), F32),
        scratch_shapes=[pltpu.VMEM((H + bm, D), BF16), pltpu.VMEM((bm, D), F32),
                        pltpu.VMEM((2, H + sub, 2 * cf), F32)],
        compiler_params=_cparams("parallel", "arbitrary"),
    )(h, h, g.reshape(1, D), w_up, w_up, conv_w, conv_w, w_down, final_g.reshape(1, D))


def _gelu_tanh(x):
    return 0.5 * x * (1.0 + jnp.tanh(math.sqrt(2.0 / math.pi) * (x + 0.044715 * (x * x * x))))


def _sgu_kernel(u_ref, v_ref, lg_ref, lb_ref, w_ref, bt_ref, o_ref):
    rows = u_ref.shape[0]
    T = SGU_CHUNK
    v = _gelu_tanh(v_ref[...].astype(F32))
    mu = jnp.mean(v, axis=-1, keepdims=True)
    d = v - mu
    var = jnp.mean(d * d, axis=-1, keepdims=True)
    vn = (d * lax.rsqrt(var + EPS) * lg_ref[...] + lb_ref[...]).astype(BF16)
    causal = (lax.broadcasted_iota(jnp.int32, (T, T), 1) <= lax.broadcasted_iota(jnp.int32, (T, T), 0))
    for g in range(SGU_GROUPS):
        cols = slice(g * SGU_GROUP, (g + 1) * SGU_GROUP)
        wg = jnp.where(causal, w_ref[g], 0.0).astype(BF16)
        bias = bt_ref[:, g:g + 1]
        for c in range(rows // T):
            rs = slice(c * T, (c + 1) * T)
            s = _dot(wg, vn[rs, cols]) + bias
            o_ref[rs, cols] = (_gelu_tanh(u_ref[rs, cols].astype(F32)) * s).astype(o_ref.dtype)


def spatial_gating(proj, ln_g, ln_b, w_s, b_s, M, rows):
    return pl.pallas_call(
        _sgu_kernel,
        grid=(M // rows,),
        in_specs=[pl.BlockSpec((rows, SGU_WIDTH), lambda i: (i, 0)),
                  pl.BlockSpec((rows, SGU_WIDTH), lambda i: (i, 1)),
                  pl.BlockSpec((1, SGU_WIDTH), lambda i: (0, 0)),
                  pl.BlockSpec((1, SGU_WIDTH), lambda i: (0, 0)),
                  pl.BlockSpec((SGU_GROUPS, SGU_CHUNK, SGU_CHUNK), lambda i: (0, 0, 0)),
                  pl.BlockSpec((SGU_CHUNK, SGU_GROUPS), lambda i: (0, 0))],
        out_specs=pl.BlockSpec((rows, SGU_WIDTH), lambda i: (i, 0)),
        out_shape=jax.ShapeDtypeStruct((M, SGU_WIDTH), BF16),
        compiler_params=_cparams("parallel"),
    )(proj, proj, ln_g.reshape(1, -1), ln_b.reshape(1, -1), w_s, b_s.T)


def _gdn_intra_kernel(q_ref, k_ref, v_ref, qh_ref, kh_ref, vh_ref, cw_ref, tail_ref, tailt_ref, alog_ref,
                      dtb_ref, u_ref, w_ref, qd_ref, kd_ref, qk_ref, gl_ref, *, blocks_per_seq):
    rows = q_ref.shape[0]
    C = DN_CHUNK
    HD = DN_HEAD_DIM
    H = BF16_SUBLANES
    first = (pl.program_id(0) % blocks_per_seq) == 0

    def conv_silu(part, ref, halo_ref):
        cur = ref[...].astype(F32)
        halo = jnp.where(first, 0.0, halo_ref[...].astype(F32))
        ext = jnp.concatenate([halo, cur], axis=0)
        cw = cw_ref[:, part * DN_WIDTH:(part + 1) * DN_WIDTH]
        y = cur * cw[DN_CONV - 1:DN_CONV, :]
        for k in range(1, DN_CONV):
            y = y + pltpu.roll(ext, k, axis=0)[H:, :] * cw[DN_CONV - 1 - k:DN_CONV - k, :]
        return _silu(y)

    x = jnp.concatenate([conv_silu(0, q_ref, qh_ref), conv_silu(1, k_ref, kh_ref),
                         conv_silu(2, v_ref, vh_ref)], axis=1)

    ii = lax.broadcasted_iota(jnp.int32, (C, C), 0)
    jj = lax.broadcasted_iota(jnp.int32, (C, C), 1)
    lower = jj <= ii
    strict = jj < ii
    su = lax.broadcasted_iota(jnp.int32, (C, LANES), 0)
    ju = lax.broadcasted_iota(jnp.int32, (C, LANES), 1)
    upper_ext = jnp.where(((ju < C) & (su > ju)) | (ju == C), 1.0, 0.0)

    inst = [(c, hh) for hh in range(DN_HEADS) for c in range(rows // C)]
    qs, ks, vs, bs, stacks = [], [], [], [], []
    for hh in range(DN_HEADS):
        qh = x[:, hh * HD:(hh + 1) * HD]
        kh = x[:, DN_WIDTH + hh * HD:DN_WIDTH + (hh + 1) * HD]
        vh = x[:, 2 * DN_WIDTH + hh * HD:2 * DN_WIDTH + (hh + 1) * HD]
        qh = qh * lax.rsqrt(jnp.sum(qh * qh, axis=-1, keepdims=True) + EPS) * HD ** -0.5
        kh = kh * lax.rsqrt(jnp.sum(kh * kh, axis=-1, keepdims=True) + EPS)
        beta = 1.0 / (1.0 + jnp.exp(-tail_ref[:, hh:hh + 1]))
        a_raw = tailt_ref[DN_HEADS + hh:DN_HEADS + hh + 1, :]
        z = a_raw + dtb_ref[0:1, hh:hh + 1]
        softplus = jnp.maximum(z, 0.0) + jnp.log(1.0 + jnp.exp(-jnp.abs(z)))
        g_row = -jnp.exp(alog_ref[0:1, hh:hh + 1]) * softplus
        for c in range(rows // C):
            rs = slice(c * C, (c + 1) * C)
            qs.append(qh[rs]); ks.append(kh[rs]); vs.append(vh[rs]); bs.append(beta[rs])
            gr = jnp.broadcast_to(g_row[:, rs], (C, C))
            stacks += [jnp.where(lower, gr, 0.0), jnp.where(lower, 0.0, gr)]

    stacked = jnp.concatenate(stacks, axis=0)
    s_hi = stacked.astype(BF16)
    r1 = stacked - s_hi.astype(F32)
    s_mid = r1.astype(BF16)
    s_lo = (r1 - s_mid.astype(F32)).astype(BF16)
    ue = upper_ext.astype(BF16)
    dall = _dot(s_hi, ue) + _dot(s_mid, ue) + _dot(s_lo, ue)

    decays, gcs, gc_revs, k16s, kbs = [], [], [], [], []
    for n, (c, hh) in enumerate(inst):
        dext = dall[n * 2 * C:(n + 1) * 2 * C]
        decays.append(jnp.exp(jnp.where(lower, dext[:C, :C], -jnp.inf)))
        gcs.append(dext[:C, C:C + 1])
        gc_revs.append(dext[C:, C:C + 1])
        kbs.append(ks[n] * bs[n])
        k16s.append(ks[n].astype(BF16))
    kk = [_dot_nt(kbs[n].astype(BF16), k16s[n]) for n in range(len(inst))]
    qk = [_dot_nt(qs[n].astype(BF16), k16s[n]) for n in range(len(inst))]
    pws = [jnp.where(strict, kk[n] * decays[n], 0.0).astype(BF16) for n in range(len(inst))]
    egc = [jnp.exp(g) for g in gcs]
    rhs = [jnp.concatenate([vs[n] * bs[n], kbs[n] * egc[n]], axis=1) for n in range(len(inst))]
    sols = [rhs[n] - _dot(pws[n], rhs[n].astype(BF16)) for n in range(len(inst))]
    for _ in range(int(math.log2(C)) - 1):
        pws = [_dot(p, p).astype(BF16) for p in pws]
        sols = [s + _dot(p, s.astype(BF16)) for p, s in zip(pws, sols)]
    for n, (c, hh) in enumerate(inst):
        rs = slice(c * C, (c + 1) * C)
        hcols = slice(hh * HD, (hh + 1) * HD)
        u_ref[rs, hcols] = sols[n][:, :HD].astype(u_ref.dtype)
        w_ref[rs, hcols] = sols[n][:, HD:].astype(w_ref.dtype)
        qkd = jnp.where(lower, qk[n] * decays[n], 0.0)
        qk_ref[rs, hcols] = jnp.concatenate([qkd, jnp.zeros_like(qkd)], axis=1).astype(qk_ref.dtype)
        qd_ref[rs, hcols] = (qs[n] * egc[n]).astype(qd_ref.dtype)
        kd_ref[rs, hcols] = (ks[n] * jnp.exp(gc_revs[n])).astype(kd_ref.dtype)
        gl_ref[c * 8:(c + 1) * 8, hcols] = jnp.broadcast_to(egc[n][C - 1:C, :], (8, HD))


def gdn_intra(proj, tail, dn_conv, a_log, dt_bias, M, S, rows):
    H = BF16_SUBLANES
    c0 = 2 * SGU_WIDTH // DN_WIDTH
    tail_t = tail[:, :2 * DN_HEADS].T
    pad = lambda p: jnp.pad(p.reshape(1, -1), ((0, 0), (0, LANES - p.shape[0])))
    W3 = 3 * DN_WIDTH
    seq = lambda dt: jax.ShapeDtypeStruct((M, DN_WIDTH), dt)
    row_spec = pl.BlockSpec((rows, DN_WIDTH), lambda i: (i, 0))
    cur_spec = lambda part: pl.BlockSpec((rows, DN_WIDTH), lambda i: (i, c0 + part))
    halo_spec = lambda part: pl.BlockSpec(
        (H, DN_WIDTH), lambda i: (jnp.maximum(i * (rows // H) - 1, 0), c0 + part))
    return pl.pallas_call(
        functools.partial(_gdn_intra_kernel, blocks_per_seq=S // rows),
        grid=(M // rows,),
        in_specs=[cur_spec(0), cur_spec(1), cur_spec(2), halo_spec(0), halo_spec(1), halo_spec(2),
                  pl.BlockSpec((DN_CONV, W3), lambda i: (0, 0)),
                  pl.BlockSpec((rows, LANES), lambda i: (i, 0)),
                  pl.BlockSpec((2 * DN_HEADS, rows), lambda i: (0, i)),
                  pl.BlockSpec((1, LANES), lambda i: (0, 0)),
                  pl.BlockSpec((1, LANES), lambda i: (0, 0))],
        out_specs=[row_spec, row_spec, row_spec, row_spec, row_spec,
                   pl.BlockSpec((rows // DN_CHUNK * 8, DN_WIDTH), lambda i: (i, 0))],
        out_shape=[seq(BF16), seq(BF16), seq(BF16), seq(BF16), seq(BF16),
                   jax.ShapeDtypeStruct((M // DN_CHUNK * 8, DN_WIDTH), F32)],
        compiler_params=_cparams("parallel"),
    )(proj, proj, proj, proj, proj, proj, dn_conv, tail, tail_t, pad(a_log), pad(dt_bias))


def _gdn_scan_kernel(u_ref, w_ref, qd_ref, kd_ref, qk_ref, gl_ref, gate_ref, ng_ref, o_ref, state_ref, *,
                     chunks):
    C = DN_CHUNK
    HD = DN_HEAD_DIM
    B = u_ref.shape[0]

    @pl.when(pl.program_id(0) == 0)
    def _():
        state_ref[...] = jnp.zeros_like(state_ref)

    ng = ng_ref[...]
    inst = [(b, hh) for b in range(B) for hh in range(DN_HEADS)]
    col = lambda hh: slice(hh * HD, (hh + 1) * HD)
    states = [state_ref[b, hh] for b, hh in inst]
    for c in range(chunks):
        rs = slice(c * C, (c + 1) * C)
        kdt = [kd_ref[b, rs, col(hh)].astype(F32).T.astype(BF16) for b, hh in inst]
        st16 = [s.astype(BF16) for s in states]
        ws = [_dot(w_ref[b, rs, col(hh)], st16[n]) for n, (b, hh) in enumerate(inst)]
        qs = [_dot(qd_ref[b, rs, col(hh)], st16[n]) for n, (b, hh) in enumerate(inst)]
        vn16 = [(u_ref[b, rs, col(hh)].astype(F32) - ws[n]).astype(BF16) for n, (b, hh) in enumerate(inst)]
        states = [states[n] * gl_ref[b, c * 8:c * 8 + 1, col(hh)] + _dot(kdt[n], vn16[n])
                  for n, (b, hh) in enumerate(inst)]
        for n, (b, hh) in enumerate(inst):
            o = qs[n] + _dot(qk_ref[b, rs, col(hh)][:, :C], vn16[n])
            o = o * lax.rsqrt(jnp.mean(o * o, axis=-1, keepdims=True) + EPS) * ng
            o_ref[b, rs, col(hh)] = (o * _silu(gate_ref[b, rs, col(hh)].astype(F32))).astype(o_ref.dtype)
    for n, (b, hh) in enumerate(inst):
        state_ref[b, hh] = states[n]


def gdn_scan(u, w, qd, kd, qk, gl, proj3, norm_g, B, S, chunks):
    rows = chunks * DN_CHUNK
    r3 = lambda a: a.reshape(B, S, DN_WIDTH)
    gcol = (2 * SGU_WIDTH + 3 * DN_WIDTH) // DN_WIDTH
    seq_spec = pl.BlockSpec((B, rows, DN_WIDTH), lambda n: (0, n, 0))
    return pl.pallas_call(
        functools.partial(_gdn_scan_kernel, chunks=chunks),
        grid=(S // rows,),
        in_specs=[seq_spec, seq_spec, seq_spec, seq_spec, seq_spec,
                  pl.BlockSpec((B, chunks * 8, DN_WIDTH), lambda n: (0, n, 0)),
                  pl.BlockSpec((B, rows, DN_WIDTH), lambda n: (0, n, gcol)),
                  pl.BlockSpec((1, DN_HEAD_DIM), lambda n: (0, 0))],
        out_specs=seq_spec,
        out_shape=jax.ShapeDtypeStruct((B, S, DN_WIDTH), BF16),
        scratch_shapes=[pltpu.VMEM((B, DN_HEADS, DN_HEAD_DIM, DN_HEAD_DIM), F32)],
        compiler_params=_cparams("arbitrary"),
    )(r3(u), r3(w), r3(qd), r3(kd), r3(qk), gl.reshape(B, S // DN_CHUNK * 8, DN_WIDTH), proj3,
      norm_g.reshape(1, DN_HEAD_DIM))


def _forward(x, mem, mem_norm, norm_mix, norm_xattn, norm_ffn, ev_w_in, pool_w, pool_scale, ev_w_out,
             od_w_in, sgu_ln_g, sgu_ln_b, sgu_w, sgu_b, dn_conv, dn_a_log, dn_dt_bias, dn_norm_g,
             od_w_out, xattn_wq, xattn_wkv, xattn_wo, ffn_w_up, ffn_conv, ffn_w_down, final_norm):
    B, S, D = x.shape
    n_mem = mem.shape[1]
    M = B * S
    depth = norm_mix.shape[0]
    bf = lambda a: a.astype(BF16)
    bm = min(512, S)
    bm_big = min(1024, S)

    h = x.reshape(M, D)
    mem2 = mem.reshape(B * n_mem, D)
    for layer in range(depth):
        i = layer // 2
        if layer % 2 == 0:
            proj, = norm_matmul(h, norm_mix[layer], [bf(ev_w_in[i])], [BF16], bm_big)
            a_out = moba_attention(proj.reshape(B, S, -1), B, S).reshape(M, A_WIDTH)
            b_out = multiscale_pool(proj, bf(pool_w[i]), pool_scale[i], M, S, bm)
            h = mix_residual(h, a_out, b_out, bf(ev_w_out[i]), bm)
        else:
            main_w = 2 * SGU_WIDTH + 4 * DN_WIDTH
            w_in = od_w_in[i]
            w_tail = jnp.pad(w_in[:, main_w:], ((0, 0), (0, LANES - 2 * DN_HEADS)))
            proj, tail = norm_matmul(h, norm_mix[layer], [bf(w_in[:, :main_w]), bf(w_tail)], [BF16, F32],
                                     bm_big)
            c_out = spatial_gating(proj, sgu_ln_g[i], sgu_ln_b[i], sgu_w[i], sgu_b[i], M, min(256, S))
            u, w, qd, kd, qk, gl = gdn_intra(proj, tail, dn_conv[i], dn_a_log[i], dn_dt_bias[i], M, S,
                                             min(256, S))
            d_out = gdn_scan(u, w, qd, kd, qk, gl, proj.reshape(B, S, -1), dn_norm_g[i], B, S, 4)
            h = mix_residual(h, c_out, d_out.reshape(M, DN_WIDTH), bf(od_w_out[i]), bm)
        kv, = norm_matmul(mem2, mem_norm, [bf(xattn_wkv[layer])], [BF16], B * n_mem)
        h = xattn_residual(h, norm_xattn[layer], bf(xattn_wq[layer]), kv, bf(xattn_wo[layer]), S, n_mem, bm)
        h = ffn_residual(h, norm_ffn[layer], bf(ffn_w_up[layer]), ffn_conv[layer], bf(ffn_w_down[layer]),
                         final_norm, S, bm_big, 256, final_norm=(layer == depth - 1))
    return h.reshape(B, S, D)


def kernel(x, mem, mem_norm, norm_mix, norm_xattn, norm_ffn, ev_w_in, pool_w, pool_scale, ev_w_out, od_w_in, sgu_ln_g, sgu_ln_b, sgu_w, sgu_b, dn_conv, dn_a_log, dn_dt_bias, dn_norm_g, od_w_out, xattn_wq, xattn_wkv, xattn_wo, ffn_w_up, ffn_conv, ffn_w_down, final_norm):
    return _forward(x, mem, mem_norm, norm_mix, norm_xattn, norm_ffn, ev_w_in, pool_w, pool_scale, ev_w_out,
                    od_w_in, sgu_ln_g, sgu_ln_b, sgu_w, sgu_b, dn_conv, dn_a_log, dn_dt_bias, dn_norm_g,
                    od_w_out, xattn_wq, xattn_wkv, xattn_wo, ffn_w_up, ffn_conv, ffn_w_down, final_norm)
```

```python
import functools
import math

import jax
import jax.numpy as jnp
from jax import lax
from jax.experimental import pallas as pl
from jax.experimental.pallas import tpu as pltpu

F32 = jnp.float32
BF16 = jnp.bfloat16
EPS = 1e-6
NEG_BIG = -1e30

VMEM_LIMIT_BYTES = 48 * 1024 * 1024
BF16_SUBLANES = 16
LANES = 128

MOBA_HEADS, MOBA_HEAD_DIM, MOBA_BLOCK, MOBA_TOPK = 8, 64, 256, 3
A_WIDTH = MOBA_HEADS * MOBA_HEAD_DIM
POOL_WINDOWS = (2, 4, 8, 16)
POOL_GROUP = 128
POOL_WIDTH = POOL_GROUP * len(POOL_WINDOWS)
SGU_GROUPS, SGU_GROUP, SGU_CHUNK = 4, 128, 128
SGU_WIDTH = SGU_GROUPS * SGU_GROUP
DN_HEADS, DN_HEAD_DIM, DN_CONV, DN_CHUNK = 4, 128, 4, 64
DN_WIDTH = DN_HEADS * DN_HEAD_DIM
XATTN_HEADS = 4
FFN_CONV = 3


def _cparams(*sem):
    return pltpu.CompilerParams(dimension_semantics=sem, vmem_limit_bytes=VMEM_LIMIT_BYTES)


def _rmsnorm(x, g):
    return x * lax.rsqrt(jnp.mean(x * x, axis=-1, keepdims=True) + EPS) * g


def _silu(x):
    return x * (0.5 * jnp.tanh(0.5 * x) + 0.5)


def _dot(a, b):
    return jnp.dot(a, b, preferred_element_type=F32)


def _dot_nt(a, b, precision=None):
    return lax.dot_general(a, b, (((1,), (1,)), ((), ())), preferred_element_type=F32,
                           precision=precision)


def _dot_tn(a, b):
    return lax.dot_general(a, b, (((0,), (0,)), ((), ())), preferred_element_type=F32)


def _norm_mm_kernel(x_ref, g_ref, *refs, bn):
    n = len(refs) // 2
    xn = _rmsnorm(x_ref[...], g_ref[...]).astype(BF16)
    for w_ref, o_ref in zip(refs[:n], refs[n:]):
        N = w_ref.shape[1]
        for c0 in range(0, N, bn):
            c1 = min(c0 + bn, N)
            o_ref[:, c0:c1] = _dot(xn, w_ref[:, c0:c1]).astype(o_ref.dtype)


def norm_matmul(x, g, ws, out_dtypes, bm, bn=512):
    M, D = x.shape
    return pl.pallas_call(
        functools.partial(_norm_mm_kernel, bn=bn),
        grid=(M // bm,),
        in_specs=[pl.BlockSpec((bm, D), lambda i: (i, 0)),
                  pl.BlockSpec((1, D), lambda i: (0, 0))]
                 + [pl.BlockSpec(w.shape, lambda i: (0, 0)) for w in ws],
        out_specs=[pl.BlockSpec((bm, w.shape[1]), lambda i: (i, 0)) for w in ws],
        out_shape=[jax.ShapeDtypeStruct((M, w.shape[1]), dt) for w, dt in zip(ws, out_dtypes)],
        compiler_params=_cparams("parallel"),
    )(x, g.reshape(1, D), *ws)


def _moba_kernel(q_ref, k_ref, v_ref, o_ref, kme_ref, vt_ref, sel_ref, m_ref, acc_ref, s_ref, sd_ref, *,
                 nb, nbp, unroll, pairs):
    BS = MOBA_BLOCK
    HD = MOBA_HEAD_DIM
    n_heads = 2 * pairs
    i = pl.program_id(2)
    lane = lax.broadcasted_iota(jnp.int32, (1, LANES), 1)
    head_lanes = (lane < HD, lane >= HD)
    pair_lanes = lambda u: slice((u // 2) * LANES, (u // 2 + 1) * LANES)

    @pl.when(i == 0)
    def _():
        kme_ref[...] = jnp.zeros_like(kme_ref)
        for n in range(nb):
            rows = slice(n * BS, (n + 1) * BS)
            mean = jnp.sum(k_ref[0, rows, :].astype(F32), axis=0, keepdims=True) / BS
            for u in range(n_heads):
                kme_ref[u // 2, (u % 2) * nbp + n:(u % 2) * nbp + n + 1, :] = jnp.where(
                    head_lanes[u % 2], mean[:, pair_lanes(u)], 0.0)
            vt_ref[:, rows] = v_ref[0, rows, :].astype(F32).T.astype(BF16)

    scale = HD ** -0.5 * math.log2(math.e)
    qs = [q_ref[0, :, p * LANES:(p + 1) * LANES] for p in range(pairs)]
    q_aug = [jnp.where(head_lanes[u % 2], qs[u // 2].astype(F32) * scale, 0.0).astype(BF16)
             for u in range(n_heads)]
    own = pl.ds(pl.multiple_of(i * BS, BS), BS)
    for u in range(n_heads):
        sd_ref[u] = _dot_nt(k_ref[0, own, pair_lanes(u)], q_aug[u]).astype(BF16)

    gates = [_dot_nt(kme_ref[p], qs[p].astype(F32), precision=lax.Precision.HIGHEST)
             for p in range(pairs)]
    blk = lax.broadcasted_iota(jnp.int32, (nbp, 1), 0).astype(F32)
    valid = blk < i.astype(F32)
    for u in range(n_heads):
        g = jnp.where(valid, gates[u // 2][(u % 2) * nbp:(u % 2 + 1) * nbp], -jnp.inf)
        sel = jnp.zeros(g.shape, jnp.bool_)
        for _ in range(MOBA_TOPK):
            mx = jnp.max(g, axis=0, keepdims=True)
            idx = jnp.min(jnp.where(g == mx, blk, float(1 << 20)), axis=0, keepdims=True)
            pick = blk == idx
            sel = sel | pick
            g = jnp.where(pick, -jnp.inf, g)
        sel_ref[u, :nbp, :] = jnp.where(sel & valid, 1.0, 0.0)
        sel_ref[u, nbp:, :] = jnp.zeros((8, BS), F32)
        m_ref[u] = jnp.full((1, BS), NEG_BIG, F32)
        acc_ref[u] = jnp.zeros(acc_ref.shape[1:], F32)

    krow = lax.broadcasted_iota(jnp.int32, (BS, BS), 0)
    qcol = lax.broadcasted_iota(jnp.int32, (BS, BS), 1)
    PVR = HD + BF16_SUBLANES
    pv_rows = (slice(0, PVR), slice(LANES - PVR, LANES))
    pv_row = lax.broadcasted_iota(jnp.int32, (PVR, 1), 0)
    is_dim = (pv_row < HD, pv_row >= PVR - HD)

    def block_start(j):
        return pl.multiple_of(jnp.minimum(j, nb - 1) * BS, BS)

    def produce(g, slot):
        for t in range(unroll):
            rows = pl.ds(block_start(g * unroll + t), BS)
            for u in range(n_heads):
                s_ref[slot, u, t * BS:(t + 1) * BS, :] = _dot_nt(
                    k_ref[0, rows, pair_lanes(u)], q_aug[u]).astype(BF16)

    def softmax_update(u, sts, sels, starts):
        cand = jnp.full((1, BS), NEG_BIG, F32)
        for st, sel in zip(sts, sels):
            mx = jnp.max(st.reshape(BS // BF16_SUBLANES, BF16_SUBLANES, BS), axis=0)
            mx = jnp.max(mx.astype(F32), axis=0, keepdims=True)
            cand = jnp.maximum(cand, mx if sel is None else jnp.where(sel, mx, NEG_BIG))
        m_old = m_ref[u]
        m_new = jnp.maximum(m_old, cand)
        alpha = jnp.exp2(m_old - m_new)
        ps = []
        for st, sel in zip(sts, sels):
            sub = m_new if sel is None else jnp.where(sel, m_new, -NEG_BIG)
            ps.append(jnp.exp2(st - sub.astype(BF16)))
        p = ps[0] if len(ps) == 1 else jnp.concatenate(ps, axis=0)
        rows = slice((u // 2) * LANES + pv_rows[u % 2].start, (u // 2) * LANES + pv_rows[u % 2].stop)
        vts = [jnp.where(is_dim[u % 2], vt_ref[rows, pl.ds(st0, BS)], jnp.ones((), BF16))
               for st0 in starts]
        vt = vts[0] if len(vts) == 1 else jnp.concatenate(vts, axis=1)
        acc_ref[u] = acc_ref[u] * alpha + _dot(vt, p)
        m_ref[u] = m_new

    def consume(g, slot):
        for u in range(n_heads):
            sts, sels, starts = [], [], []
            for t in range(unroll):
                j = g * unroll + t
                sts.append(s_ref[slot, u, t * BS:(t + 1) * BS, :])
                sels.append(sel_ref[u, pl.ds(j, 1), :] > 0.5)
                starts.append(block_start(j))
            softmax_update(u, sts, sels, starts)

    def body(gg, c):
        produce(2 * gg + 1, 1)
        consume(2 * gg, 0)
        produce(2 * gg + 2, 0)
        consume(2 * gg + 1, 1)
        return c

    n_groups = (i + unroll - 1) // unroll
    produce(0, 0)
    lax.fori_loop(0, (n_groups + 1) // 2, body, 0)
    for u in range(n_heads):
        st = jnp.where(krow <= qcol, sd_ref[u], -jnp.inf)
        softmax_update(u, [st], [None], [block_start(i)])

    outs = []
    for u in range(n_heads):
        a = acc_ref[u]
        outs.append(a[:HD] / a[HD:HD + 1, :] if u % 2 == 0 else a[PVR - HD:] / a[0:1, :])
    o_ref[0] = jnp.concatenate(outs, axis=0).T.astype(o_ref.dtype)


def moba_attention(proj, B, S):
    nb = S // MOBA_BLOCK
    nbp = -(-nb // 8) * 8
    pairs = 2
    width = pairs * LANES
    groups = A_WIDTH // width
    unroll = 2
    n_heads = 2 * pairs
    return pl.pallas_call(
        functools.partial(_moba_kernel, nb=nb, nbp=nbp, unroll=unroll, pairs=pairs),
        grid=(B, groups, nb),
        in_specs=[pl.BlockSpec((1, MOBA_BLOCK, width), lambda b, p, i: (b, i, p)),
                  pl.BlockSpec((1, S, width), lambda b, p, i: (b, 0, groups + p)),
                  pl.BlockSpec((1, S, width), lambda b, p, i: (b, 0, 2 * groups + p))],
        out_specs=pl.BlockSpec((1, MOBA_BLOCK, width), lambda b, p, i: (b, i, p)),
        out_shape=jax.ShapeDtypeStruct((B, S, A_WIDTH), BF16),
        scratch_shapes=[pltpu.VMEM((pairs, 2 * nbp, LANES), F32),
                        pltpu.VMEM((width, S), BF16),
                        pltpu.VMEM((n_heads, nbp + 8, MOBA_BLOCK), F32),
                        pltpu.VMEM((n_heads, 1, MOBA_BLOCK), F32),
                        pltpu.VMEM((n_heads, MOBA_HEAD_DIM + BF16_SUBLANES, MOBA_BLOCK), F32),
                        pltpu.VMEM((2, n_heads, unroll * MOBA_BLOCK, MOBA_BLOCK), BF16),
                        pltpu.VMEM((n_heads, MOBA_BLOCK, MOBA_BLOCK), BF16)],
        compiler_params=_cparams("parallel", "parallel", "arbitrary"),
    )(proj, proj, proj)


def _pool_kernel(p_ref, halo_ref, w_ref, sc_ref, o_ref, *, blocks_per_seq):
    bm = p_ref.shape[0]
    H = BF16_SUBLANES
    i = pl.program_id(0)
    first = (i % blocks_per_seq) == 0
    t1 = (lax.broadcasted_iota(jnp.int32, (bm, 1), 0) + (i % blocks_per_seq) * bm + 1).astype(F32)
    for g, w in enumerate(POOL_WINDOWS):
        cols = slice(g * POOL_GROUP, (g + 1) * POOL_GROUP)
        cur = p_ref[:, cols].astype(F32)
        halo = jnp.where(first, 0.0, halo_ref[:, cols].astype(F32))
        ext = jnp.concatenate([halo, cur], axis=0)
        acc = ext
        sh = 1
        while sh < w:
            acc = acc + pltpu.roll(acc, sh, axis=0)
            sh *= 2
        win = acc[H:, :]
        pooled = win / jnp.minimum(t1, float(w)) - cur
        y = _dot(pooled.astype(BF16), w_ref[g])
        o_ref[:, cols] = (y * sc_ref[:, cols]).astype(o_ref.dtype)


def multiscale_pool(proj, pool_w, pool_scale, M, S, bm):
    H = BF16_SUBLANES
    pcol = 3 * A_WIDTH // POOL_WIDTH
    return pl.pallas_call(
        functools.partial(_pool_kernel, blocks_per_seq=S // bm),
        grid=(M // bm,),
        in_specs=[pl.BlockSpec((bm, POOL_WIDTH), lambda i: (i, pcol)),
                  pl.BlockSpec((H, POOL_WIDTH), lambda i: (jnp.maximum(i * (bm // H) - 1, 0), pcol)),
                  pl.BlockSpec((len(POOL_WINDOWS), POOL_GROUP, POOL_GROUP), lambda i: (0, 0, 0)),
                  pl.BlockSpec((1, POOL_WIDTH), lambda i: (0, 0))],
        out_specs=pl.BlockSpec((bm, POOL_WIDTH), lambda i: (i, 0)),
        out_shape=jax.ShapeDtypeStruct((M, POOL_WIDTH), BF16),
        compiler_params=_cparams("parallel"),
    )(proj, proj, pool_w, pool_scale.reshape(1, POOL_WIDTH))


def _mix_kernel(h_ref, a_ref, b_ref, w_ref, o_ref):
    ka = a_ref.shape[1]
    o_ref[...] = h_ref[...] + _dot(a_ref[...], w_ref[:ka, :]) + _dot(b_ref[...], w_ref[ka:, :])


def mix_residual(h, a, b, w, bm):
    M, D = h.shape
    ka, kb = a.shape[1], b.shape[1]
    return pl.pallas_call(
        _mix_kernel,
        grid=(M // bm,),
        in_specs=[pl.BlockSpec((bm, D), lambda i: (i, 0)),
                  pl.BlockSpec((bm, ka), lambda i: (i, 0)),
                  pl.BlockSpec((bm, kb), lambda i: (i, 0)),
                  pl.BlockSpec((ka + kb, D), lambda i: (0, 0))],
        out_specs=pl.BlockSpec((bm, D), lambda i: (i, 0)),
        out_shape=jax.ShapeDtypeStruct((M, D), F32),
        compiler_params=_cparams("parallel"),
    )(h, a, b, w)


def _xattn_kernel(h_ref, g_ref, wq_ref, k_ref, v_ref, wo_ref, o_ref):
    h = h_ref[...]
    D = h.shape[1]
    hd = D // XATTN_HEADS
    xn = _rmsnorm(h, g_ref[...]).astype(BF16)
    q = (_dot(xn, wq_ref[...]) * hd ** -0.5).astype(BF16)
    outs = []
    for hh in range(XATTN_HEADS):
        cols = slice(hh * hd, (hh + 1) * hd)
        s = _dot_nt(q[:, cols], k_ref[:, cols])
        m = jnp.max(s, axis=1, keepdims=True)
        p = jnp.exp(s - m)
        l = jnp.sum(p, axis=1, keepdims=True)
        outs.append((_dot(p.astype(BF16), v_ref[:, cols]) / l).astype(BF16))
    o = jnp.concatenate(outs, axis=1)
    o_ref[...] = h + _dot(o, wo_ref[...])


def xattn_residual(h, g, wq, kv, wo, S, n_mem, bm):
    M, D = h.shape
    bps = S // bm
    return pl.pallas_call(
        _xattn_kernel,
        grid=(M // bm,),
        in_specs=[pl.BlockSpec((bm, D), lambda i: (i, 0)),
                  pl.BlockSpec((1, D), lambda i: (0, 0)),
                  pl.BlockSpec((D, D), lambda i: (0, 0)),
                  pl.BlockSpec((n_mem, D), lambda i: (i // bps, 0)),
                  pl.BlockSpec((n_mem, D), lambda i: (i // bps, 1)),
                  pl.BlockSpec((D, D), lambda i: (0, 0))],
        out_specs=pl.BlockSpec((bm, D), lambda i: (i, 0)),
        out_shape=jax.ShapeDtypeStruct((M, D), F32),
        compiler_params=_cparams("parallel"),
    )(h, g.reshape(1, D), wq, kv, kv, wo)


def _ffn_kernel(h_ref, halo_ref, g_ref, wup_ref, cw_ref, wd_ref, fg_ref, o_ref,
                xn_ref, acc_ref, y_ref, *, blocks_per_seq, final_norm, sub):
    H = BF16_SUBLANES
    nc, cf = wd_ref.shape[0], wd_ref.shape[1]
    n_sub = acc_ref.shape[0] // sub

    first = (pl.program_id(0) % blocks_per_seq) == 0
    xn_ref[:H, :] = jnp.where(first, 0.0, _rmsnorm(halo_ref[...], g_ref[...])).astype(BF16)
    xn_ref[H:, :] = _rmsnorm(h_ref[...], g_ref[...]).astype(BF16)
    acc_ref[...] = jnp.zeros_like(acc_ref)

    def up(c, r):
        xs = xn_ref[r * sub:r * sub + sub + H, :]
        y_ref[r % 2, :, :cf] = _dot(xs, wup_ref[c])
        y_ref[r % 2, :, cf:] = _dot(xs, wup_ref[nc + c])

    def conv(r, part, cw):
        cols = slice(part * cf, (part + 1) * cf)
        out = y_ref[r % 2, H:, cols] * cw[FFN_CONV - 1:FFN_CONV, :]
        for k in range(1, FFN_CONV):
            out = out + y_ref[r % 2, H - k:H - k + sub, cols] * cw[FFN_CONV - 1 - k:FFN_CONV - k, :]
        return out

    def chunk(c, carry):
        cwg, cwu = cw_ref[c], cw_ref[nc + c]
        wd = wd_ref[c]
        for r in range(n_sub):
            if r + 1 < n_sub:
                up(c, r + 1)
            else:
                up(jnp.minimum(c + 1, nc - 1), 0)
            act = _silu(conv(r, 0, cwg)) * conv(r, 1, cwu)
            acc_ref[r * sub:(r + 1) * sub, :] += _dot(act.astype(BF16), wd)
        return carry

    up(0, 0)
    lax.fori_loop(0, nc, chunk, 0)
    y = h_ref[...] + acc_ref[...]
    if final_norm:
        y = _rmsnorm(y, fg_ref[...])
    o_ref[...] = y


def ffn_residual(h, g, w_up, conv_w, w_down, final_g, S, bm, cf, final_norm):
    M, D = h.shape
    d_ff = w_down.shape[0]
    H = BF16_SUBLANES
    nc = d_ff // cf
    sub = min(256, bm)
    assert (bm // sub) % 2 == 0, "the two y_ref slots alternate per sub-block across chunks"
    wup3 = w_up.reshape(D, 2 * nc, cf).transpose(1, 0, 2)
    cw3 = conv_w.reshape(FFN_CONV, 2 * nc, cf).transpose(1, 0, 2)
    wd3 = w_down.reshape(nc, cf, D)
    whole = lambda a: pl.BlockSpec(a.shape, lambda i: (0,) * a.ndim, pipeline_mode=pl.Buffered(1))
    return pl.pallas_call(
        functools.partial(_ffn_kernel, blocks_per_seq=S // bm, final_norm=final_norm, sub=sub),
        grid=(M // bm,),
        in_specs=[pl.BlockSpec((bm, D), lambda i: (i, 0)),
                  pl.BlockSpec((H, D), lambda i: (jnp.maximum(i * (bm // H) - 1, 0), 0)),
                  pl.BlockSpec((1, D), lambda i: (0, 0)),
                  whole(wup3), whole(cw3), whole(wd3),
                  pl.BlockSpec((1, D), lambda i: (0, 0))],
        out_specs=pl.BlockSpec((bm, D), lambda i: (i, 0)),
        out_shape=jax.ShapeDtypeStruct((M, D), F32),
        scratch_shapes=[pltpu.VMEM((H + bm, D), BF16), pltpu.VMEM((bm, D), F32),
                        pltpu.VMEM((2, H + sub, 2 * cf), F32)],
        compiler_params=_cparams("parallel"),
    )(h, h, g.reshape(1, D), wup3, cw3, wd3, final_g.reshape(1, D))


def _gelu_tanh(x):
    return 0.5 * x * (1.0 + jnp.tanh(math.sqrt(2.0 / math.pi) * (x + 0.044715 * (x * x * x))))


def _sgu_kernel(u_ref, v_ref, lg_ref, lb_ref, w_ref, bt_ref, o_ref):
    rows = u_ref.shape[0]
    T = SGU_CHUNK
    v = _gelu_tanh(v_ref[...].astype(F32))
    mu = jnp.mean(v, axis=-1, keepdims=True)
    d = v - mu
    var = jnp.mean(d * d, axis=-1, keepdims=True)
    vn = (d * lax.rsqrt(var + EPS) * lg_ref[...] + lb_ref[...]).astype(BF16)
    causal = (lax.broadcasted_iota(jnp.int32, (T, T), 1) <= lax.broadcasted_iota(jnp.int32, (T, T), 0))
    for g in range(SGU_GROUPS):
        cols = slice(g * SGU_GROUP, (g + 1) * SGU_GROUP)
        wg = jnp.where(causal, w_ref[g], 0.0).astype(BF16)
        bias = bt_ref[:, g:g + 1]
        for c in range(rows // T):
            rs = slice(c * T, (c + 1) * T)
            s = _dot(wg, vn[rs, cols]) + bias
            o_ref[rs, cols] = (_gelu_tanh(u_ref[rs, cols].astype(F32)) * s).astype(o_ref.dtype)


def spatial_gating(proj, ln_g, ln_b, w_s, b_s, M, rows):
    return pl.pallas_call(
        _sgu_kernel,
        grid=(M // rows,),
        in_specs=[pl.BlockSpec((rows, SGU_WIDTH), lambda i: (i, 0)),
                  pl.BlockSpec((rows, SGU_WIDTH), lambda i: (i, 1)),
                  pl.BlockSpec((1, SGU_WIDTH), lambda i: (0, 0)),
                  pl.BlockSpec((1, SGU_WIDTH), lambda i: (0, 0)),
                  pl.BlockSpec((SGU_GROUPS, SGU_CHUNK, SGU_CHUNK), lambda i: (0, 0, 0)),
                  pl.BlockSpec((SGU_CHUNK, SGU_GROUPS), lambda i: (0, 0))],
        out_specs=pl.BlockSpec((rows, SGU_WIDTH), lambda i: (i, 0)),
        out_shape=jax.ShapeDtypeStruct((M, SGU_WIDTH), BF16),
        compiler_params=_cparams("parallel"),
    )(proj, proj, ln_g.reshape(1, -1), ln_b.reshape(1, -1), w_s, b_s.T)


def _gdn_intra_kernel(q_ref, k_ref, v_ref, qh_ref, kh_ref, vh_ref, cw_ref, tail_ref, tailt_ref, alog_ref,
                      dtb_ref, u_ref, w_ref, qd_ref, kd_ref, qk_ref, gl_ref, *, blocks_per_seq):
    rows = q_ref.shape[0]
    C = DN_CHUNK
    HD = DN_HEAD_DIM
    H = BF16_SUBLANES
    first = (pl.program_id(0) % blocks_per_seq) == 0

    def conv_silu(part, ref, halo_ref):
        cur = ref[...].astype(F32)
        halo = jnp.where(first, 0.0, halo_ref[...].astype(F32))
        ext = jnp.concatenate([halo, cur], axis=0)
        cw = cw_ref[:, part * DN_WIDTH:(part + 1) * DN_WIDTH]
        y = cur * cw[DN_CONV - 1:DN_CONV, :]
        for k in range(1, DN_CONV):
            y = y + pltpu.roll(ext, k, axis=0)[H:, :] * cw[DN_CONV - 1 - k:DN_CONV - k, :]
        return _silu(y)

    x = jnp.concatenate([conv_silu(0, q_ref, qh_ref), conv_silu(1, k_ref, kh_ref),
                         conv_silu(2, v_ref, vh_ref)], axis=1)

    ii = lax.broadcasted_iota(jnp.int32, (C, C), 0)
    jj = lax.broadcasted_iota(jnp.int32, (C, C), 1)
    lower = jj <= ii
    strict = jj < ii
    su = lax.broadcasted_iota(jnp.int32, (C, LANES), 0)
    ju = lax.broadcasted_iota(jnp.int32, (C, LANES), 1)
    upper_ext = jnp.where(((ju < C) & (su > ju)) | (ju == C), 1.0, 0.0)

    inst = [(c, hh) for hh in range(DN_HEADS) for c in range(rows // C)]
    qs, ks, vs, bs, stacks = [], [], [], [], []
    for hh in range(DN_HEADS):
        qh = x[:, hh * HD:(hh + 1) * HD]
        kh = x[:, DN_WIDTH + hh * HD:DN_WIDTH + (hh + 1) * HD]
        vh = x[:, 2 * DN_WIDTH + hh * HD:2 * DN_WIDTH + (hh + 1) * HD]
        qh = qh * lax.rsqrt(jnp.sum(qh * qh, axis=-1, keepdims=True) + EPS) * HD ** -0.5
        kh = kh * lax.rsqrt(jnp.sum(kh * kh, axis=-1, keepdims=True) + EPS)
        beta = 1.0 / (1.0 + jnp.exp(-tail_ref[:, hh:hh + 1]))
        a_raw = tailt_ref[DN_HEADS + hh:DN_HEADS + hh + 1, :]
        z = a_raw + dtb_ref[0:1, hh:hh + 1]
        softplus = jnp.maximum(z, 0.0) + jnp.log(1.0 + jnp.exp(-jnp.abs(z)))
        g_row = -jnp.exp(alog_ref[0:1, hh:hh + 1]) * softplus
        for c in range(rows // C):
            rs = slice(c * C, (c + 1) * C)
            qs.append(qh[rs]); ks.append(kh[rs]); vs.append(vh[rs]); bs.append(beta[rs])
            gr = jnp.broadcast_to(g_row[:, rs], (C, C))
            stacks += [jnp.where(lower, gr, 0.0), jnp.where(lower, 0.0, gr)]

    stacked = jnp.concatenate(stacks, axis=0)
    s_hi = stacked.astype(BF16)
    r1 = stacked - s_hi.astype(F32)
    s_mid = r1.astype(BF16)
    s_lo = (r1 - s_mid.astype(F32)).astype(BF16)
    ue = upper_ext.astype(BF16)
    dall = _dot(s_hi, ue) + _dot(s_mid, ue) + _dot(s_lo, ue)

    decays, gcs, gc_revs, k16s, kbs = [], [], [], [], []
    for n, (c, hh) in enumerate(inst):
        dext = dall[n * 2 * C:(n + 1) * 2 * C]
        decays.append(jnp.exp(jnp.where(lower, dext[:C, :C], -jnp.inf)))
        gcs.append(dext[:C, C:C + 1])
        gc_revs.append(dext[C:, C:C + 1])
        kbs.append(ks[n] * bs[n])
        k16s.append(ks[n].astype(BF16))
    kk = [_dot_nt(kbs[n].astype(BF16), k16s[n]) for n in range(len(inst))]
    qk = [_dot_nt(qs[n].astype(BF16), k16s[n]) for n in range(len(inst))]
    pws = [jnp.where(strict, kk[n] * decays[n], 0.0).astype(BF16) for n in range(len(inst))]
    egc = [jnp.exp(g) for g in gcs]
    rhs = [jnp.concatenate([vs[n] * bs[n], kbs[n] * egc[n]], axis=1) for n in range(len(inst))]
    sols = [rhs[n] - _dot(pws[n], rhs[n].astype(BF16)) for n in range(len(inst))]
    for _ in range(int(math.log2(C)) - 1):
        pws = [_dot(p, p).astype(BF16) for p in pws]
        sols = [s + _dot(p, s.astype(BF16)) for p, s in zip(pws, sols)]
    for n, (c, hh) in enumerate(inst):
        rs = slice(c * C, (c + 1) * C)
        hcols = slice(hh * HD, (hh + 1) * HD)
        u_ref[rs, hcols] = sols[n][:, :HD].astype(u_ref.dtype)
        w_ref[rs, hcols] = sols[n][:, HD:].astype(w_ref.dtype)
        qkd = jnp.where(lower, qk[n] * decays[n], 0.0)
        qk_ref[rs, hcols] = jnp.concatenate([qkd, jnp.zeros_like(qkd)], axis=1).astype(qk_ref.dtype)
        qd_ref[rs, hcols] = (qs[n] * egc[n]).astype(qd_ref.dtype)
        kd_ref[rs, hcols] = (ks[n] * jnp.exp(gc_revs[n])).astype(kd_ref.dtype)
        gl_ref[c * 8:(c + 1) * 8, hcols] = jnp.broadcast_to(egc[n][C - 1:C, :], (8, HD))


def gdn_intra(proj, tail, dn_conv, a_log, dt_bias, M, S, rows):
    H = BF16_SUBLANES
    c0 = 2 * SGU_WIDTH // DN_WIDTH
    tail_t = tail[:, :2 * DN_HEADS].T
    pad = lambda p: jnp.pad(p.reshape(1, -1), ((0, 0), (0, LANES - p.shape[0])))
    W3 = 3 * DN_WIDTH
    seq = lambda dt: jax.ShapeDtypeStruct((M, DN_WIDTH), dt)
    row_spec = pl.BlockSpec((rows, DN_WIDTH), lambda i: (i, 0))
    cur_spec = lambda part: pl.BlockSpec((rows, DN_WIDTH), lambda i: (i, c0 + part))
    halo_spec = lambda part: pl.BlockSpec(
        (H, DN_WIDTH), lambda i: (jnp.maximum(i * (rows // H) - 1, 0), c0 + part))
    return pl.pallas_call(
        functools.partial(_gdn_intra_kernel, blocks_per_seq=S // rows),
        grid=(M // rows,),
        in_specs=[cur_spec(0), cur_spec(1), cur_spec(2), halo_spec(0), halo_spec(1), halo_spec(2),
                  pl.BlockSpec((DN_CONV, W3), lambda i: (0, 0)),
                  pl.BlockSpec((rows, LANES), lambda i: (i, 0)),
                  pl.BlockSpec((2 * DN_HEADS, rows), lambda i: (0, i)),
                  pl.BlockSpec((1, LANES), lambda i: (0, 0)),
                  pl.BlockSpec((1, LANES), lambda i: (0, 0))],
        out_specs=[row_spec, row_spec, row_spec, row_spec, row_spec,
                   pl.BlockSpec((rows // DN_CHUNK * 8, DN_WIDTH), lambda i: (i, 0))],
        out_shape=[seq(BF16), seq(BF16), seq(BF16), seq(BF16), seq(BF16),
                   jax.ShapeDtypeStruct((M // DN_CHUNK * 8, DN_WIDTH), F32)],
        compiler_params=_cparams("parallel"),
    )(proj, proj, proj, proj, proj, proj, dn_conv, tail, tail_t, pad(a_log), pad(dt_bias))


def _gdn_scan_kernel(u_ref, w_ref, qd_ref, kd_ref, qk_ref, gl_ref, gate_ref, ng_ref, o_ref, state_ref, *,
                     chunks):
    C = DN_CHUNK
    HD = DN_HEAD_DIM
    B = u_ref.shape[0]

    @pl.when(pl.program_id(0) == 0)
    def _():
        state_ref[...] = jnp.zeros_like(state_ref)

    ng = ng_ref[...]
    inst = [(b, hh) for b in range(B) for hh in range(DN_HEADS)]
    col = lambda hh: slice(hh * HD, (hh + 1) * HD)
    states = [state_ref[b, hh] for b, hh in inst]
    for c in range(chunks):
        rs = slice(c * C, (c + 1) * C)
        kdt = [kd_ref[b, rs, col(hh)].astype(F32).T.astype(BF16) for b, hh in inst]
        st16 = [s.astype(BF16) for s in states]
        ws = [_dot(w_ref[b, rs, col(hh)], st16[n]) for n, (b, hh) in enumerate(inst)]
        qs = [_dot(qd_ref[b, rs, col(hh)], st16[n]) for n, (b, hh) in enumerate(inst)]
        vn16 = [(u_ref[b, rs, col(hh)].astype(F32) - ws[n]).astype(BF16) for n, (b, hh) in enumerate(inst)]
        states = [states[n] * gl_ref[b, c * 8:c * 8 + 1, col(hh)] + _dot(kdt[n], vn16[n])
                  for n, (b, hh) in enumerate(inst)]
        for n, (b, hh) in enumerate(inst):
            o = qs[n] + _dot(qk_ref[b, rs, col(hh)][:, :C], vn16[n])
            o = o * lax.rsqrt(jnp.mean(o * o, axis=-1, keepdims=True) + EPS) * ng
            o_ref[b, rs, col(hh)] = (o * _silu(gate_ref[b, rs, col(hh)].astype(F32))).astype(o_ref.dtype)
    for n, (b, hh) in enumerate(inst):
        state_ref[b, hh] = states[n]


def gdn_scan(u, w, qd, kd, qk, gl, proj3, norm_g, B, S, chunks):
    rows = chunks * DN_CHUNK
    r3 = lambda a: a.reshape(B, S, DN_WIDTH)
    gcol = (2 * SGU_WIDTH + 3 * DN_WIDTH) // DN_WIDTH
    seq_spec = pl.BlockSpec((B, rows, DN_WIDTH), lambda n: (0, n, 0))
    return pl.pallas_call(
        functools.partial(_gdn_scan_kernel, chunks=chunks),
        grid=(S // rows,),
        in_specs=[seq_spec, seq_spec, seq_spec, seq_spec, seq_spec,
                  pl.BlockSpec((B, chunks * 8, DN_WIDTH), lambda n: (0, n, 0)),
                  pl.BlockSpec((B, rows, DN_WIDTH), lambda n: (0, n, gcol)),
                  pl.BlockSpec((1, DN_HEAD_DIM), lambda n: (0, 0))],
        out_specs=seq_spec,
        out_shape=jax.ShapeDtypeStruct((B, S, DN_WIDTH), BF16),
        scratch_shapes=[pltpu.VMEM((B, DN_HEADS, DN_HEAD_DIM, DN_HEAD_DIM), F32)],
        compiler_params=_cparams("arbitrary"),
    )(r3(u), r3(w), r3(qd), r3(kd), r3(qk), gl.reshape(B, S // DN_CHUNK * 8, DN_WIDTH), proj3,
      norm_g.reshape(1, DN_HEAD_DIM))


def _forward(x, mem, mem_norm, norm_mix, norm_xattn, norm_ffn, ev_w_in, pool_w, pool_scale, ev_w_out,
             od_w_in, sgu_ln_g, sgu_ln_b, sgu_w, sgu_b, dn_conv, dn_a_log, dn_dt_bias, dn_norm_g,
             od_w_out, xattn_wq, xattn_wkv, xattn_wo, ffn_w_up, ffn_conv, ffn_w_down, final_norm):
    B, S, D = x.shape
    n_mem = mem.shape[1]
    M = B * S
    depth = norm_mix.shape[0]
    bf = lambda a: a.astype(BF16)
    bm = min(1024, S)
    bm_big = min(1024, S)

    h = x.reshape(M, D)
    mem2 = mem.reshape(B * n_mem, D)
    for layer in range(depth):
        i = layer // 2
        if layer % 2 == 0:
            proj, = norm_matmul(h, norm_mix[layer], [bf(ev_w_in[i])], [BF16], bm_big)
            a_out = moba_attention(proj.reshape(B, S, -1), B, S).reshape(M, A_WIDTH)
            b_out = multiscale_pool(proj, bf(pool_w[i]), pool_scale[i], M, S, bm)
            h = mix_residual(h, a_out, b_out, bf(ev_w_out[i]), bm)
        else:
            main_w = 2 * SGU_WIDTH + 4 * DN_WIDTH
            w_in = od_w_in[i]
            w_tail = jnp.pad(w_in[:, main_w:], ((0, 0), (0, LANES - 2 * DN_HEADS)))
            proj, tail = norm_matmul(h, norm_mix[layer], [bf(w_in[:, :main_w]), bf(w_tail)], [BF16, F32],
                                     bm_big)
            c_out = spatial_gating(proj, sgu_ln_g[i], sgu_ln_b[i], sgu_w[i], sgu_b[i], M, min(512, S))
            u, w, qd, kd, qk, gl = gdn_intra(proj, tail, dn_conv[i], dn_a_log[i], dn_dt_bias[i], M, S,
                                             min(256, S))
            d_out = gdn_scan(u, w, qd, kd, qk, gl, proj.reshape(B, S, -1), dn_norm_g[i], B, S, 4)
            h = mix_residual(h, c_out, d_out.reshape(M, DN_WIDTH), bf(od_w_out[i]), bm)
        kv, = norm_matmul(mem2, mem_norm, [bf(xattn_wkv[layer])], [BF16], B * n_mem)
        h = xattn_residual(h, norm_xattn[layer], bf(xattn_wq[layer]), kv, bf(xattn_wo[layer]), S, n_mem, bm)
        h = ffn_residual(h, norm_ffn[layer], bf(ffn_w_up[layer]), ffn_conv[layer], bf(ffn_w_down[layer]),
                         final_norm, S, bm_big, 256, final_norm=(layer == depth - 1))
    return h.reshape(B, S, D)


def kernel(x, mem, mem_norm, norm_mix, norm_xattn, norm_ffn, ev_w_in, pool_w, pool_scale, ev_w_out, od_w_in, sgu_ln_g, sgu_ln_b, sgu_w, sgu_b, dn_conv, dn_a_log, dn_dt_bias, dn_norm_g, od_w_out, xattn_wq, xattn_wkv, xattn_wo, ffn_w_up, ffn_conv, ffn_w_down, final_norm):
    return _forward(x, mem, mem_norm, norm_mix, norm_xattn, norm_ffn, ev_w_in, pool_w, pool_scale, ev_w_out,
                    od_w_in, sgu_ln_g, sgu_ln_b, sgu_w, sgu_b, dn_conv, dn_a_log, dn_dt_bias, dn_norm_g,
                    od_w_out, xattn_wq, xattn_wkv, xattn_wo, ffn_w_up, ffn_conv, ffn_w_down, final_norm)
```

```python
import functools
import math

import jax
import jax.numpy as jnp
from jax import lax
from jax.experimental import pallas as pl
from jax.experimental.pallas import tpu as pltpu

F32 = jnp.float32
BF16 = jnp.bfloat16
EPS = 1e-6
NEG_BIG = -1e30

VMEM_LIMIT_BYTES = 48 * 1024 * 1024
BF16_SUBLANES = 16
LANES = 128

MOBA_HEADS, MOBA_HEAD_DIM, MOBA_BLOCK, MOBA_TOPK = 8, 64, 256, 3
A_WIDTH = MOBA_HEADS * MOBA_HEAD_DIM
POOL_WINDOWS = (2, 4, 8, 16)
POOL_GROUP = 128
POOL_WIDTH = POOL_GROUP * len(POOL_WINDOWS)
SGU_GROUPS, SGU_GROUP, SGU_CHUNK = 4, 128, 128
SGU_WIDTH = SGU_GROUPS * SGU_GROUP
DN_HEADS, DN_HEAD_DIM, DN_CONV, DN_CHUNK = 4, 128, 4, 64
DN_WIDTH = DN_HEADS * DN_HEAD_DIM
XATTN_HEADS = 4
FFN_CONV = 3


def _cparams(*sem):
    return pltpu.CompilerParams(dimension_semantics=sem, vmem_limit_bytes=VMEM_LIMIT_BYTES)


def _rmsnorm(x, g):
    return x * lax.rsqrt(jnp.mean(x * x, axis=-1, keepdims=True) + EPS) * g


def _silu(x):
    return x * (0.5 * jnp.tanh(0.5 * x) + 0.5)


def _dot(a, b):
    return jnp.dot(a, b, preferred_element_type=F32)


def _dot_nt(a, b, precision=None):
    return lax.dot_general(a, b, (((1,), (1,)), ((), ())), preferred_element_type=F32,
                           precision=precision)


def _dot_tn(a, b):
    return lax.dot_general(a, b, (((0,), (0,)), ((), ())), preferred_element_type=F32)


def _norm_mm_kernel(x_ref, g_ref, *refs, bn):
    n = len(refs) // 2
    xn = _rmsnorm(x_ref[...], g_ref[...]).astype(BF16)
    for w_ref, o_ref in zip(refs[:n], refs[n:]):
        N = w_ref.shape[1]
        for c0 in range(0, N, bn):
            c1 = min(c0 + bn, N)
            o_ref[:, c0:c1] = _dot(xn, w_ref[:, c0:c1]).astype(o_ref.dtype)


def norm_matmul(x, g, ws, out_dtypes, bm, bn=512):
    M, D = x.shape
    return pl.pallas_call(
        functools.partial(_norm_mm_kernel, bn=bn),
        grid=(M // bm,),
        in_specs=[pl.BlockSpec((bm, D), lambda i: (i, 0)),
                  pl.BlockSpec((1, D), lambda i: (0, 0))]
                 + [pl.BlockSpec(w.shape, lambda i: (0, 0)) for w in ws],
        out_specs=[pl.BlockSpec((bm, w.shape[1]), lambda i: (i, 0)) for w in ws],
        out_shape=[jax.ShapeDtypeStruct((M, w.shape[1]), dt) for w, dt in zip(ws, out_dtypes)],
        compiler_params=_cparams("parallel"),
    )(x, g.reshape(1, D), *ws)


def _moba_kernel(q_ref, k_ref, v_ref, o_ref, kme_ref, vt_ref, sel_ref, m_ref, acc_ref, s_ref, sd_ref, *,
                 nb, nbp, unroll, pairs):
    BS = MOBA_BLOCK
    HD = MOBA_HEAD_DIM
    n_heads = 2 * pairs
    i = pl.program_id(2)
    lane = lax.broadcasted_iota(jnp.int32, (1, LANES), 1)
    head_lanes = (lane < HD, lane >= HD)
    pair_lanes = lambda u: slice((u // 2) * LANES, (u // 2 + 1) * LANES)

    @pl.when(i == 0)
    def _():
        kme_ref[...] = jnp.zeros_like(kme_ref)
        for n in range(nb):
            rows = slice(n * BS, (n + 1) * BS)
            mean = jnp.sum(k_ref[0, rows, :].astype(F32), axis=0, keepdims=True) / BS
            for u in range(n_heads):
                kme_ref[u // 2, (u % 2) * nbp + n:(u % 2) * nbp + n + 1, :] = jnp.where(
                    head_lanes[u % 2], mean[:, pair_lanes(u)], 0.0)
            vt_ref[:, rows] = v_ref[0, rows, :].astype(F32).T.astype(BF16)

    scale = HD ** -0.5 * math.log2(math.e)
    qs = [q_ref[0, :, p * LANES:(p + 1) * LANES] for p in range(pairs)]
    q_aug = [jnp.where(head_lanes[u % 2], qs[u // 2].astype(F32) * scale, 0.0).T.astype(BF16)
             for u in range(n_heads)]
    own = pl.ds(pl.multiple_of(i * BS, BS), BS)
    for u in range(n_heads):
        sd_ref[u] = _dot(k_ref[0, own, pair_lanes(u)], q_aug[u]).astype(BF16)

    gates = [_dot_nt(kme_ref[p], qs[p].astype(F32), precision=lax.Precision.HIGHEST)
             for p in range(pairs)]
    blk = lax.broadcasted_iota(jnp.int32, (nbp, 1), 0).astype(F32)
    valid = blk < i.astype(F32)
    for u in range(n_heads):
        g = jnp.where(valid, gates[u // 2][(u % 2) * nbp:(u % 2 + 1) * nbp], -jnp.inf)
        sel = jnp.zeros(g.shape, jnp.bool_)
        for _ in range(MOBA_TOPK):
            mx = jnp.max(g, axis=0, keepdims=True)
            idx = jnp.min(jnp.where(g == mx, blk, float(1 << 20)), axis=0, keepdims=True)
            pick = blk == idx
            sel = sel | pick
            g = jnp.where(pick, -jnp.inf, g)
        sel_ref[u, :nbp, :] = jnp.where(sel & valid, 1.0, 0.0)
        sel_ref[u, nbp:, :] = jnp.zeros((8, BS), F32)
        m_ref[u] = jnp.full((1, BS), NEG_BIG, F32)
        acc_ref[u] = jnp.zeros(acc_ref.shape[1:], F32)

    krow = lax.broadcasted_iota(jnp.int32, (BS, BS), 0)
    qcol = lax.broadcasted_iota(jnp.int32, (BS, BS), 1)
    PVR = HD + BF16_SUBLANES
    pv_rows = (slice(0, PVR), slice(LANES - PVR, LANES))
    pv_row = lax.broadcasted_iota(jnp.int32, (PVR, 1), 0)
    is_dim = (pv_row < HD, pv_row >= PVR - HD)

    def block_start(j):
        return pl.multiple_of(jnp.minimum(j, nb - 1) * BS, BS)

    def produce(g, slot):
        for t in range(unroll):
            rows = pl.ds(block_start(g * unroll + t), BS)
            for u in range(n_heads):
                s_ref[slot, u, t * BS:(t + 1) * BS, :] = _dot(
                    k_ref[0, rows, pair_lanes(u)], q_aug[u]).astype(BF16)

    def softmax_update(sts, sels, starts):
        heads = range(n_heads)
        m_new, alpha = [], []
        for u in heads:
            cand = jnp.full((1, BS), NEG_BIG, F32)
            for st, sel in zip(sts[u], sels[u]):
                mx = jnp.max(st.reshape(BS // BF16_SUBLANES, BF16_SUBLANES, BS), axis=0)
                mx = jnp.max(mx.astype(F32), axis=0, keepdims=True)
                cand = jnp.maximum(cand, mx if sel is None else jnp.where(sel, mx, NEG_BIG))
            m_old = m_ref[u]
            m_new.append(jnp.maximum(m_old, cand))
            alpha.append(jnp.exp2(m_old - m_new[u]))
            m_ref[u] = m_new[u]
        ps = []
        for u in heads:
            pu = []
            for st, sel in zip(sts[u], sels[u]):
                sub = m_new[u] if sel is None else jnp.where(sel, m_new[u], -NEG_BIG)
                pu.append(jnp.exp2(st - sub.astype(BF16)))
            ps.append(pu[0] if len(pu) == 1 else jnp.concatenate(pu, axis=0))
        pv = []
        for u in heads:
            rows = slice((u // 2) * LANES + pv_rows[u % 2].start, (u // 2) * LANES + pv_rows[u % 2].stop)
            vts = [jnp.where(is_dim[u % 2], vt_ref[rows, pl.ds(st0, BS)], jnp.ones((), BF16))
                   for st0 in starts]
            pv.append(_dot(vts[0] if len(vts) == 1 else jnp.concatenate(vts, axis=1), ps[u]))
        for u in heads:
            acc_ref[u] = acc_ref[u] * alpha[u] + pv[u]

    def consume(g, slot):
        js = [g * unroll + t for t in range(unroll)]
        sts = [[s_ref[slot, u, t * BS:(t + 1) * BS, :] for t in range(unroll)] for u in range(n_heads)]
        sels = [[sel_ref[u, pl.ds(j, 1), :] > 0.5 for j in js] for u in range(n_heads)]
        softmax_update(sts, sels, [block_start(j) for j in js])

    slots = s_ref.shape[0]

    def body(gg, c):
        for t in range(slots):
            produce(slots * gg + t + 1, (t + 1) % slots)
            consume(slots * gg + t, t)
        return c

    n_groups = (i + unroll - 1) // unroll
    produce(0, 0)
    lax.fori_loop(0, (n_groups + slots - 1) // slots, body, 0)
    softmax_update([[jnp.where(krow <= qcol, sd_ref[u], -jnp.inf)] for u in range(n_heads)],
                   [[None]] * n_heads, [block_start(i)])

    outs = []
    for u in range(n_heads):
        a = acc_ref[u]
        outs.append(a[:HD] / a[HD:HD + 1, :] if u % 2 == 0 else a[PVR - HD:] / a[0:1, :])
    o_ref[0] = jnp.concatenate(outs, axis=0).T.astype(o_ref.dtype)


def moba_attention(proj, B, S):
    nb = S // MOBA_BLOCK
    nbp = -(-nb // 8) * 8
    pairs = 2
    width = pairs * LANES
    groups = A_WIDTH // width
    unroll = 1
    slots = 3
    n_heads = 2 * pairs
    return pl.pallas_call(
        functools.partial(_moba_kernel, nb=nb, nbp=nbp, unroll=unroll, pairs=pairs),
        grid=(B, groups, nb),
        in_specs=[pl.BlockSpec((1, MOBA_BLOCK, width), lambda b, p, i: (b, i, p)),
                  pl.BlockSpec((1, S, width), lambda b, p, i: (b, 0, groups + p)),
                  pl.BlockSpec((1, S, width), lambda b, p, i: (b, 0, 2 * groups + p))],
        out_specs=pl.BlockSpec((1, MOBA_BLOCK, width), lambda b, p, i: (b, i, p)),
        out_shape=jax.ShapeDtypeStruct((B, S, A_WIDTH), BF16),
        scratch_shapes=[pltpu.VMEM((pairs, 2 * nbp, LANES), F32),
                        pltpu.VMEM((width, S), BF16),
                        pltpu.VMEM((n_heads, nbp + 8, MOBA_BLOCK), F32),
                        pltpu.VMEM((n_heads, 1, MOBA_BLOCK), F32),
                        pltpu.VMEM((n_heads, MOBA_HEAD_DIM + BF16_SUBLANES, MOBA_BLOCK), F32),
                        pltpu.VMEM((slots, n_heads, unroll * MOBA_BLOCK, MOBA_BLOCK), BF16),
                        pltpu.VMEM((n_heads, MOBA_BLOCK, MOBA_BLOCK), BF16)],
        compiler_params=_cparams("parallel", "parallel", "arbitrary"),
    )(proj, proj, proj)


def _pool_kernel(p_ref, halo_ref, w_ref, sc_ref, o_ref, *, blocks_per_seq):
    bm = p_ref.shape[0]
    H = BF16_SUBLANES
    i = pl.program_id(0)
    first = (i % blocks_per_seq) == 0
    t1 = (lax.broadcasted_iota(jnp.int32, (bm, 1), 0) + (i % blocks_per_seq) * bm + 1).astype(F32)
    for g, w in enumerate(POOL_WINDOWS):
        cols = slice(g * POOL_GROUP, (g + 1) * POOL_GROUP)
        cur = p_ref[:, cols].astype(F32)
        halo = jnp.where(first, 0.0, halo_ref[:, cols].astype(F32))
        ext = jnp.concatenate([halo, cur], axis=0)
        acc = ext
        sh = 1
        while sh < w:
            acc = acc + pltpu.roll(acc, sh, axis=0)
            sh *= 2
        win = acc[H:, :]
        pooled = win / jnp.minimum(t1, float(w)) - cur
        y = _dot(pooled.astype(BF16), w_ref[g])
        o_ref[:, cols] = (y * sc_ref[:, cols]).astype(o_ref.dtype)


def multiscale_pool(proj, pool_w, pool_scale, M, S, bm):
    H = BF16_SUBLANES
    pcol = 3 * A_WIDTH // POOL_WIDTH
    return pl.pallas_call(
        functools.partial(_pool_kernel, blocks_per_seq=S // bm),
        grid=(M // bm,),
        in_specs=[pl.BlockSpec((bm, POOL_WIDTH), lambda i: (i, pcol)),
                  pl.BlockSpec((H, POOL_WIDTH), lambda i: (jnp.maximum(i * (bm // H) - 1, 0), pcol)),
                  pl.BlockSpec((len(POOL_WINDOWS), POOL_GROUP, POOL_GROUP), lambda i: (0, 0, 0)),
                  pl.BlockSpec((1, POOL_WIDTH), lambda i: (0, 0))],
        out_specs=pl.BlockSpec((bm, POOL_WIDTH), lambda i: (i, 0)),
        out_shape=jax.ShapeDtypeStruct((M, POOL_WIDTH), BF16),
        compiler_params=_cparams("parallel"),
    )(proj, proj, pool_w, pool_scale.reshape(1, POOL_WIDTH))


def _mix_xattn_kernel(h_ref, a_ref, b_ref, wm_ref, g_ref, wq_ref, k_ref, v_ref, wo_ref, o_ref):
    ka = a_ref.shape[1]
    h = h_ref[...] + _dot(a_ref[...], wm_ref[:ka, :]) + _dot(b_ref[...], wm_ref[ka:, :])
    D = h.shape[1]
    hd = D // XATTN_HEADS
    xn = _rmsnorm(h, g_ref[...]).astype(BF16)
    q = (_dot(xn, wq_ref[...]) * hd ** -0.5).astype(BF16)
    outs = []
    for hh in range(XATTN_HEADS):
        cols = slice(hh * hd, (hh + 1) * hd)
        s = _dot_nt(q[:, cols], k_ref[:, cols])
        m = jnp.max(s, axis=1, keepdims=True)
        p = jnp.exp(s - m)
        l = jnp.sum(p, axis=1, keepdims=True)
        outs.append((_dot(p.astype(BF16), v_ref[:, cols]) / l).astype(BF16))
    o = jnp.concatenate(outs, axis=1)
    o_ref[...] = h + _dot(o, wo_ref[...])


def mix_xattn_residual(h, a, b, w_mix, g, wq, kv, wo, S, n_mem, bm):
    M, D = h.shape
    ka, kb = a.shape[1], b.shape[1]
    bps = S // bm
    whole = lambda arr: pl.BlockSpec(arr.shape, lambda i: (0, 0), pipeline_mode=pl.Buffered(1))
    return pl.pallas_call(
        _mix_xattn_kernel,
        grid=(M // bm,),
        in_specs=[pl.BlockSpec((bm, D), lambda i: (i, 0)),
                  pl.BlockSpec((bm, ka), lambda i: (i, 0)),
                  pl.BlockSpec((bm, kb), lambda i: (i, 0)),
                  whole(w_mix),
                  pl.BlockSpec((1, D), lambda i: (0, 0)),
                  whole(wq),
                  pl.BlockSpec((n_mem, D), lambda i: (i // bps, 0)),
                  pl.BlockSpec((n_mem, D), lambda i: (i // bps, 1)),
                  whole(wo)],
        out_specs=pl.BlockSpec((bm, D), lambda i: (i, 0)),
        out_shape=jax.ShapeDtypeStruct((M, D), F32),
        compiler_params=_cparams("parallel"),
    )(h, a, b, w_mix, g.reshape(1, D), wq, kv, kv, wo)


def _ffn_kernel(h_ref, halo_ref, g_ref, wup_ref, cw_ref, wd_ref, fg_ref, o_ref,
                xn_ref, acc_ref, y_ref, *, blocks_per_seq, final_norm, sub):
    H = BF16_SUBLANES
    nc, cf = wd_ref.shape[0], wd_ref.shape[1]
    n_sub = acc_ref.shape[0] // sub

    first = (pl.program_id(0) % blocks_per_seq) == 0
    xn_ref[:H, :] = jnp.where(first, 0.0, _rmsnorm(halo_ref[...], g_ref[...])).astype(BF16)
    xn_ref[H:, :] = _rmsnorm(h_ref[...], g_ref[...]).astype(BF16)
    acc_ref[...] = jnp.zeros_like(acc_ref)

    chunk_cols = lambda c: pl.ds(pl.multiple_of(c * cf, cf), cf)

    def up(c, r):
        xs = xn_ref[r * sub:r * sub + sub + H, :]
        y_ref[r % 2, :, :cf] = _dot(xs, wup_ref[:, chunk_cols(c)])
        y_ref[r % 2, :, cf:] = _dot(xs, wup_ref[:, chunk_cols(nc + c)])

    def conv(r, part, cw):
        cols = slice(part * cf, (part + 1) * cf)
        out = y_ref[r % 2, H:, cols] * cw[FFN_CONV - 1:FFN_CONV, :]
        for k in range(1, FFN_CONV):
            out = out + y_ref[r % 2, H - k:H - k + sub, cols] * cw[FFN_CONV - 1 - k:FFN_CONV - k, :]
        return out

    def chunk(c, carry):
        cwg, cwu = cw_ref[:, chunk_cols(c)], cw_ref[:, chunk_cols(nc + c)]
        wd = wd_ref[c]
        for r in range(n_sub):
            if r + 1 < n_sub:
                up(c, r + 1)
            else:
                up(jnp.minimum(c + 1, nc - 1), 0)
            act = _silu(conv(r, 0, cwg)) * conv(r, 1, cwu)
            acc_ref[r * sub:(r + 1) * sub, :] += _dot(act.astype(BF16), wd)
        return carry

    up(0, 0)
    lax.fori_loop(0, nc, chunk, 0)
    y = h_ref[...] + acc_ref[...]
    if final_norm:
        y = _rmsnorm(y, fg_ref[...])
    o_ref[...] = y


def ffn_residual(h, g, w_up, conv_w, w_down, final_g, S, bm, cf, final_norm):
    M, D = h.shape
    d_ff = w_down.shape[0]
    H = BF16_SUBLANES
    nc = d_ff // cf
    sub = min(256, bm)
    assert (bm // sub) % 2 == 0, "the two y_ref slots alternate per sub-block across chunks"
    wd3 = w_down.reshape(nc, cf, D)
    whole = lambda a: pl.BlockSpec(a.shape, lambda i: (0,) * a.ndim, pipeline_mode=pl.Buffered(1))
    return pl.pallas_call(
        functools.partial(_ffn_kernel, blocks_per_seq=S // bm, final_norm=final_norm, sub=sub),
        grid=(M // bm,),
        in_specs=[pl.BlockSpec((bm, D), lambda i: (i, 0)),
                  pl.BlockSpec((H, D), lambda i: (jnp.maximum(i * (bm // H) - 1, 0), 0)),
                  pl.BlockSpec((1, D), lambda i: (0, 0)),
                  whole(w_up), whole(conv_w), whole(wd3),
                  pl.BlockSpec((1, D), lambda i: (0, 0))],
        out_specs=pl.BlockSpec((bm, D), lambda i: (i, 0)),
        out_shape=jax.ShapeDtypeStruct((M, D), F32),
        scratch_shapes=[pltpu.VMEM((H + bm, D), BF16), pltpu.VMEM((bm, D), F32),
                        pltpu.VMEM((2, H + sub, 2 * cf), F32)],
        compiler_params=_cparams("parallel"),
    )(h, h, g.reshape(1, D), w_up, conv_w, wd3, final_g.reshape(1, D))


def _gelu_tanh(x):
    return 0.5 * x * (1.0 + jnp.tanh(math.sqrt(2.0 / math.pi) * (x + 0.044715 * (x * x * x))))


def _sgu_kernel(u_ref, v_ref, lg_ref, lb_ref, w_ref, bt_ref, o_ref):
    rows = u_ref.shape[0]
    T = SGU_CHUNK
    v = _gelu_tanh(v_ref[...].astype(F32))
    mu = jnp.mean(v, axis=-1, keepdims=True)
    d = v - mu
    var = jnp.mean(d * d, axis=-1, keepdims=True)
    vn = (d * lax.rsqrt(var + EPS) * lg_ref[...] + lb_ref[...]).astype(BF16)
    causal = (lax.broadcasted_iota(jnp.int32, (T, T), 1) <= lax.broadcasted_iota(jnp.int32, (T, T), 0))
    for g in range(SGU_GROUPS):
        cols = slice(g * SGU_GROUP, (g + 1) * SGU_GROUP)
        wg = jnp.where(causal, w_ref[g], 0.0).astype(BF16)
        bias = bt_ref[:, g:g + 1]
        for c in range(rows // T):
            rs = slice(c * T, (c + 1) * T)
            s = _dot(wg, vn[rs, cols]) + bias
            o_ref[rs, cols] = (_gelu_tanh(u_ref[rs, cols].astype(F32)) * s).astype(o_ref.dtype)


def spatial_gating(proj, ln_g, ln_b, w_s, b_s, M, rows):
    return pl.pallas_call(
        _sgu_kernel,
        grid=(M // rows,),
        in_specs=[pl.BlockSpec((rows, SGU_WIDTH), lambda i: (i, 0)),
                  pl.BlockSpec((rows, SGU_WIDTH), lambda i: (i, 1)),
                  pl.BlockSpec((1, SGU_WIDTH), lambda i: (0, 0)),
                  pl.BlockSpec((1, SGU_WIDTH), lambda i: (0, 0)),
                  pl.BlockSpec((SGU_GROUPS, SGU_CHUNK, SGU_CHUNK), lambda i: (0, 0, 0)),
                  pl.BlockSpec((SGU_CHUNK, SGU_GROUPS), lambda i: (0, 0))],
        out_specs=pl.BlockSpec((rows, SGU_WIDTH), lambda i: (i, 0)),
        out_shape=jax.ShapeDtypeStruct((M, SGU_WIDTH), BF16),
        compiler_params=_cparams("parallel"),
    )(proj, proj, ln_g.reshape(1, -1), ln_b.reshape(1, -1), w_s, b_s.T)


def _gdn_intra_kernel(q_ref, k_ref, v_ref, qh_ref, kh_ref, vh_ref, cw_ref, tail_ref, tailt_ref, alog_ref,
                      dtb_ref, u_ref, w_ref, qd_ref, kd_ref, qk_ref, gl_ref, *, blocks_per_seq):
    rows = q_ref.shape[0]
    C = DN_CHUNK
    HD = DN_HEAD_DIM
    H = BF16_SUBLANES
    first = (pl.program_id(0) % blocks_per_seq) == 0

    def conv_silu(part, ref, halo_ref):
        cur = ref[...].astype(F32)
        halo = jnp.where(first, 0.0, halo_ref[...].astype(F32))
        ext = jnp.concatenate([halo, cur], axis=0)
        cw = cw_ref[:, part * DN_WIDTH:(part + 1) * DN_WIDTH]
        y = cur * cw[DN_CONV - 1:DN_CONV, :]
        for k in range(1, DN_CONV):
            y = y + pltpu.roll(ext, k, axis=0)[H:, :] * cw[DN_CONV - 1 - k:DN_CONV - k, :]
        return _silu(y)

    x = jnp.concatenate([conv_silu(0, q_ref, qh_ref), conv_silu(1, k_ref, kh_ref),
                         conv_silu(2, v_ref, vh_ref)], axis=1)

    ii = lax.broadcasted_iota(jnp.int32, (C, C), 0)
    jj = lax.broadcasted_iota(jnp.int32, (C, C), 1)
    lower = jj <= ii
    strict = jj < ii
    su = lax.broadcasted_iota(jnp.int32, (C, LANES), 0)
    ju = lax.broadcasted_iota(jnp.int32, (C, LANES), 1)
    upper_ext = jnp.where(((ju < C) & (su > ju)) | (ju == C), 1.0, 0.0)

    inst = [(c, hh) for hh in range(DN_HEADS) for c in range(rows // C)]
    qs, ks, vs, bs, stacks = [], [], [], [], []
    for hh in range(DN_HEADS):
        qh = x[:, hh * HD:(hh + 1) * HD]
        kh = x[:, DN_WIDTH + hh * HD:DN_WIDTH + (hh + 1) * HD]
        vh = x[:, 2 * DN_WIDTH + hh * HD:2 * DN_WIDTH + (hh + 1) * HD]
        qh = qh * lax.rsqrt(jnp.sum(qh * qh, axis=-1, keepdims=True) + EPS) * HD ** -0.5
        kh = kh * lax.rsqrt(jnp.sum(kh * kh, axis=-1, keepdims=True) + EPS)
        beta = 1.0 / (1.0 + jnp.exp(-tail_ref[:, hh:hh + 1]))
        a_raw = tailt_ref[DN_HEADS + hh:DN_HEADS + hh + 1, :]
        z = a_raw + dtb_ref[0:1, hh:hh + 1]
        softplus = jnp.maximum(z, 0.0) + jnp.log(1.0 + jnp.exp(-jnp.abs(z)))
        g_row = -jnp.exp(alog_ref[0:1, hh:hh + 1]) * softplus
        for c in range(rows // C):
            rs = slice(c * C, (c + 1) * C)
            qs.append(qh[rs]); ks.append(kh[rs]); vs.append(vh[rs]); bs.append(beta[rs])
            gr = jnp.broadcast_to(g_row[:, rs], (C, C))
            stacks += [jnp.where(lower, gr, 0.0), jnp.where(lower, 0.0, gr)]

    stacked = jnp.concatenate(stacks, axis=0)
    s_hi = stacked.astype(BF16)
    r1 = stacked - s_hi.astype(F32)
    s_mid = r1.astype(BF16)
    s_lo = (r1 - s_mid.astype(F32)).astype(BF16)
    ue = upper_ext.astype(BF16)
    dall = _dot(s_hi, ue) + _dot(s_mid, ue) + _dot(s_lo, ue)

    decays, gcs, gc_revs, k16s, kbs = [], [], [], [], []
    for n, (c, hh) in enumerate(inst):
        dext = dall[n * 2 * C:(n + 1) * 2 * C]
        decays.append(jnp.exp(jnp.where(lower, dext[:C, :C], -jnp.inf)))
        gcs.append(dext[:C, C:C + 1])
        gc_revs.append(dext[C:, C:C + 1])
        kbs.append(ks[n] * bs[n])
        k16s.append(ks[n].astype(BF16))
    kk = [_dot_nt(kbs[n].astype(BF16), k16s[n]) for n in range(len(inst))]
    qk = [_dot_nt(qs[n].astype(BF16), k16s[n]) for n in range(len(inst))]
    pws = [jnp.where(strict, kk[n] * decays[n], 0.0).astype(BF16) for n in range(len(inst))]
    egc = [jnp.exp(g) for g in gcs]
    rhs = [jnp.concatenate([vs[n] * bs[n], kbs[n] * egc[n]], axis=1) for n in range(len(inst))]
    sols = [rhs[n] - _dot(pws[n], rhs[n].astype(BF16)) for n in range(len(inst))]
    for _ in range(int(math.log2(C)) - 1):
        pws = [_dot(p, p).astype(BF16) for p in pws]
        sols = [s + _dot(p, s.astype(BF16)) for p, s in zip(pws, sols)]
    for n, (c, hh) in enumerate(inst):
        rs = slice(c * C, (c + 1) * C)
        hcols = slice(hh * HD, (hh + 1) * HD)
        u_ref[rs, hcols] = sols[n][:, :HD].astype(u_ref.dtype)
        w_ref[rs, hcols] = sols[n][:, HD:].astype(w_ref.dtype)
        qkd = jnp.where(lower, qk[n] * decays[n], 0.0)
        qk_ref[rs, hcols] = jnp.concatenate([qkd, jnp.zeros_like(qkd)], axis=1).astype(qk_ref.dtype)
        qd_ref[rs, hcols] = (qs[n] * egc[n]).astype(qd_ref.dtype)
        kd_ref[rs, hcols] = (ks[n] * jnp.exp(gc_revs[n])).astype(kd_ref.dtype)
        gl_ref[c * 8:(c + 1) * 8, hcols] = jnp.broadcast_to(egc[n][C - 1:C, :], (8, HD))


def gdn_intra(proj, tail, dn_conv, a_log, dt_bias, M, S, rows):
    H = BF16_SUBLANES
    c0 = 2 * SGU_WIDTH // DN_WIDTH
    tail_t = tail[:, :2 * DN_HEADS].T
    pad = lambda p: jnp.pad(p.reshape(1, -1), ((0, 0), (0, LANES - p.shape[0])))
    W3 = 3 * DN_WIDTH
    seq = lambda dt: jax.ShapeDtypeStruct((M, DN_WIDTH), dt)
    row_spec = pl.BlockSpec((rows, DN_WIDTH), lambda i: (i, 0))
    cur_spec = lambda part: pl.BlockSpec((rows, DN_WIDTH), lambda i: (i, c0 + part))
    halo_spec = lambda part: pl.BlockSpec(
        (H, DN_WIDTH), lambda i: (jnp.maximum(i * (rows // H) - 1, 0), c0 + part))
    return pl.pallas_call(
        functools.partial(_gdn_intra_kernel, blocks_per_seq=S // rows),
        grid=(M // rows,),
        in_specs=[cur_spec(0), cur_spec(1), cur_spec(2), halo_spec(0), halo_spec(1), halo_spec(2),
                  pl.BlockSpec((DN_CONV, W3), lambda i: (0, 0)),
                  pl.BlockSpec((rows, LANES), lambda i: (i, 0)),
                  pl.BlockSpec((2 * DN_HEADS, rows), lambda i: (0, i)),
                  pl.BlockSpec((1, LANES), lambda i: (0, 0)),
                  pl.BlockSpec((1, LANES), lambda i: (0, 0))],
        out_specs=[row_spec, row_spec, row_spec, row_spec, row_spec,
                   pl.BlockSpec((rows // DN_CHUNK * 8, DN_WIDTH), lambda i: (i, 0))],
        out_shape=[seq(BF16), seq(BF16), seq(BF16), seq(BF16), seq(BF16),
                   jax.ShapeDtypeStruct((M // DN_CHUNK * 8, DN_WIDTH), F32)],
        compiler_params=_cparams("parallel"),
    )(proj, proj, proj, proj, proj, proj, dn_conv, tail, tail_t, pad(a_log), pad(dt_bias))


def _gdn_scan_kernel(u_ref, w_ref, qd_ref, kd_ref, qk_ref, gl_ref, gate_ref, ng_ref, o_ref, state_ref, *,
                     chunks):
    C = DN_CHUNK
    HD = DN_HEAD_DIM
    B = u_ref.shape[0]

    @pl.when(pl.program_id(0) == 0)
    def _():
        state_ref[...] = jnp.zeros_like(state_ref)

    ng = ng_ref[...]
    inst = [(b, hh) for b in range(B) for hh in range(DN_HEADS)]
    col = lambda hh: slice(hh * HD, (hh + 1) * HD)
    states = [state_ref[b, hh] for b, hh in inst]
    for c in range(chunks):
        rs = slice(c * C, (c + 1) * C)
        kdt = [kd_ref[b, rs, col(hh)].astype(F32).T.astype(BF16) for b, hh in inst]
        st16 = [s.astype(BF16) for s in states]
        ws = [_dot(w_ref[b, rs, col(hh)], st16[n]) for n, (b, hh) in enumerate(inst)]
        qs = [_dot(qd_ref[b, rs, col(hh)], st16[n]) for n, (b, hh) in enumerate(inst)]
        vn16 = [(u_ref[b, rs, col(hh)].astype(F32) - ws[n]).astype(BF16) for n, (b, hh) in enumerate(inst)]
        states = [states[n] * gl_ref[b, c * 8:c * 8 + 1, col(hh)] + _dot(kdt[n], vn16[n])
                  for n, (b, hh) in enumerate(inst)]
        for n, (b, hh) in enumerate(inst):
            o = qs[n] + _dot(qk_ref[b, rs, col(hh)][:, :C], vn16[n])
            o = o * lax.rsqrt(jnp.mean(o * o, axis=-1, keepdims=True) + EPS) * ng
            o_ref[b, rs, col(hh)] = (o * _silu(gate_ref[b, rs, col(hh)].astype(F32))).astype(o_ref.dtype)
    for n, (b, hh) in enumerate(inst):
        state_ref[b, hh] = states[n]


def gdn_scan(u, w, qd, kd, qk, gl, proj3, norm_g, B, S, chunks):
    rows = chunks * DN_CHUNK
    r3 = lambda a: a.reshape(B, S, DN_WIDTH)
    gcol = (2 * SGU_WIDTH + 3 * DN_WIDTH) // DN_WIDTH
    seq_spec = pl.BlockSpec((B, rows, DN_WIDTH), lambda n: (0, n, 0))
    return pl.pallas_call(
        functools.partial(_gdn_scan_kernel, chunks=chunks),
        grid=(S // rows,),
        in_specs=[seq_spec, seq_spec, seq_spec, seq_spec, seq_spec,
                  pl.BlockSpec((B, chunks * 8, DN_WIDTH), lambda n: (0, n, 0)),
                  pl.BlockSpec((B, rows, DN_WIDTH), lambda n: (0, n, gcol)),
                  pl.BlockSpec((1, DN_HEAD_DIM), lambda n: (0, 0))],
        out_specs=seq_spec,
        out_shape=jax.ShapeDtypeStruct((B, S, DN_WIDTH), BF16),
        scratch_shapes=[pltpu.VMEM((B, DN_HEADS, DN_HEAD_DIM, DN_HEAD_DIM), F32)],
        compiler_params=_cparams("arbitrary"),
    )(r3(u), r3(w), r3(qd), r3(kd), r3(qk), gl.reshape(B, S // DN_CHUNK * 8, DN_WIDTH), proj3,
      norm_g.reshape(1, DN_HEAD_DIM))


def _forward(x, mem, mem_norm, norm_mix, norm_xattn, norm_ffn, ev_w_in, pool_w, pool_scale, ev_w_out,
             od_w_in, sgu_ln_g, sgu_ln_b, sgu_w, sgu_b, dn_conv, dn_a_log, dn_dt_bias, dn_norm_g,
             od_w_out, xattn_wq, xattn_wkv, xattn_wo, ffn_w_up, ffn_conv, ffn_w_down, final_norm):
    B, S, D = x.shape
    n_mem = mem.shape[1]
    M = B * S
    depth = norm_mix.shape[0]
    bf = lambda a: a.astype(BF16)
    bm = min(1024, S)
    bm_big = min(1024, S)

    h = x.reshape(M, D)
    mem2 = mem.reshape(B * n_mem, D)
    for layer in range(depth):
        i = layer // 2
        if layer % 2 == 0:
            proj, = norm_matmul(h, norm_mix[layer], [bf(ev_w_in[i])], [BF16], bm_big)
            a_out = moba_attention(proj.reshape(B, S, -1), B, S).reshape(M, A_WIDTH)
            b_out = multiscale_pool(proj, bf(pool_w[i]), pool_scale[i], M, S, bm)
            mix_a, mix_b, w_mix = a_out, b_out, ev_w_out[i]
        else:
            main_w = 2 * SGU_WIDTH + 4 * DN_WIDTH
            w_in = od_w_in[i]
            w_tail = jnp.pad(w_in[:, main_w:], ((0, 0), (0, LANES - 2 * DN_HEADS)))
            proj, tail = norm_matmul(h, norm_mix[layer], [bf(w_in[:, :main_w]), bf(w_tail)], [BF16, F32],
                                     bm_big)
            c_out = spatial_gating(proj, sgu_ln_g[i], sgu_ln_b[i], sgu_w[i], sgu_b[i], M, min(512, S))
            u, w, qd, kd, qk, gl = gdn_intra(proj, tail, dn_conv[i], dn_a_log[i], dn_dt_bias[i], M, S,
                                             min(256, S))
            d_out = gdn_scan(u, w, qd, kd, qk, gl, proj.reshape(B, S, -1), dn_norm_g[i], B, S, 4)
            mix_a, mix_b, w_mix = c_out, d_out.reshape(M, DN_WIDTH), od_w_out[i]
        kv, = norm_matmul(mem2, mem_norm, [bf(xattn_wkv[layer])], [BF16], B * n_mem)
        h = mix_xattn_residual(h, mix_a, mix_b, bf(w_mix), norm_xattn[layer], bf(xattn_wq[layer]), kv,
                               bf(xattn_wo[layer]), S, n_mem, bm)
        h = ffn_residual(h, norm_ffn[layer], bf(ffn_w_up[layer]), ffn_conv[layer], bf(ffn_w_down[layer]),
                         final_norm, S, bm_big, 256, final_norm=(layer == depth - 1))
    return h.reshape(B, S, D)


def kernel(x, mem, mem_norm, norm_mix, norm_xattn, norm_ffn, ev_w_in, pool_w, pool_scale, ev_w_out, od_w_in, sgu_ln_g, sgu_ln_b, sgu_w, sgu_b, dn_conv, dn_a_log, dn_dt_bias, dn_norm_g, od_w_out, xattn_wq, xattn_wkv, xattn_wo, ffn_w_up, ffn_conv, ffn_w_down, final_norm):
    return _forward(x, mem, mem_norm, norm_mix, norm_xattn, norm_ffn, ev_w_in, pool_w, pool_scale, ev_w_out,
                    od_w_in, sgu_ln_g, sgu_ln_b, sgu_w, sgu_b, dn_conv, dn_a_log, dn_dt_bias, dn_norm_g,
                    od_w_out, xattn_wq, xattn_wkv, xattn_wo, ffn_w_up, ffn_conv, ffn_w_down, final_norm)
```

```python
import functools
import math

import jax
import jax.numpy as jnp
from jax import lax
from jax.experimental import pallas as pl
from jax.experimental.pallas import tpu as pltpu

F32 = jnp.float32
BF16 = jnp.bfloat16
EPS = 1e-6
NEG_BIG = -1e30

VMEM_LIMIT_BYTES = 48 * 1024 * 1024
BF16_SUBLANES = 16
LANES = 128

MOBA_HEADS, MOBA_HEAD_DIM, MOBA_BLOCK, MOBA_TOPK = 8, 64, 256, 3
A_WIDTH = MOBA_HEADS * MOBA_HEAD_DIM
POOL_WINDOWS = (2, 4, 8, 16)
POOL_GROUP = 128
POOL_WIDTH = POOL_GROUP * len(POOL_WINDOWS)
SGU_GROUPS, SGU_GROUP, SGU_CHUNK = 4, 128, 128
SGU_WIDTH = SGU_GROUPS * SGU_GROUP
DN_HEADS, DN_HEAD_DIM, DN_CONV, DN_CHUNK = 4, 128, 4, 64
DN_WIDTH = DN_HEADS * DN_HEAD_DIM
XATTN_HEADS = 4
FFN_CONV = 3


def _cparams(*sem):
    return pltpu.CompilerParams(dimension_semantics=sem, vmem_limit_bytes=VMEM_LIMIT_BYTES)


def _rmsnorm(x, g):
    return x * lax.rsqrt(jnp.mean(x * x, axis=-1, keepdims=True) + EPS) * g


def _silu(x):
    return x * (0.5 * jnp.tanh(0.5 * x) + 0.5)


def _dot(a, b):
    return jnp.dot(a, b, preferred_element_type=F32)


def _dot_nt(a, b, precision=None):
    return lax.dot_general(a, b, (((1,), (1,)), ((), ())), preferred_element_type=F32,
                           precision=precision)


def _dot_tn(a, b):
    return lax.dot_general(a, b, (((0,), (0,)), ((), ())), preferred_element_type=F32)


def _norm_mm_kernel(x_ref, g_ref, *refs, bn):
    n = len(refs) // 2
    xn = _rmsnorm(x_ref[...], g_ref[...]).astype(BF16)
    for w_ref, o_ref in zip(refs[:n], refs[n:]):
        N = w_ref.shape[1]
        for c0 in range(0, N, bn):
            c1 = min(c0 + bn, N)
            o_ref[:, c0:c1] = _dot(xn, w_ref[:, c0:c1]).astype(o_ref.dtype)


def norm_matmul(x, g, ws, out_dtypes, bm, bn=512):
    M, D = x.shape
    return pl.pallas_call(
        functools.partial(_norm_mm_kernel, bn=bn),
        grid=(M // bm,),
        in_specs=[pl.BlockSpec((bm, D), lambda i: (i, 0)),
                  pl.BlockSpec((1, D), lambda i: (0, 0))]
                 + [pl.BlockSpec(w.shape, lambda i: (0, 0)) for w in ws],
        out_specs=[pl.BlockSpec((bm, w.shape[1]), lambda i: (i, 0)) for w in ws],
        out_shape=[jax.ShapeDtypeStruct((M, w.shape[1]), dt) for w, dt in zip(ws, out_dtypes)],
        compiler_params=_cparams("parallel"),
    )(x, g.reshape(1, D), *ws)


def _gelu_tanh(x):
    return 0.5 * x * (1.0 + jnp.tanh(math.sqrt(2.0 / math.pi) * (x + 0.044715 * (x * x * x))))


def _od_proj_kernel(x_ref, halo_ref, g_ref, w_ref, wt_ref, cw_ref, o_ref, t_ref, xe_ref, *,
                    blocks_per_seq, bn):
    bm = x_ref.shape[0]
    H = BF16_SUBLANES
    z_w, qkv_w = 2 * SGU_WIDTH, 3 * DN_WIDTH
    first = (pl.program_id(0) % blocks_per_seq) == 0
    xe_ref[:H, :] = jnp.where(first, 0.0, _rmsnorm(halo_ref[...], g_ref[...])).astype(BF16)
    xe_ref[H:, :] = _rmsnorm(x_ref[...], g_ref[...]).astype(BF16)
    xn = xe_ref[H:, :]
    t_ref[...] = _dot(xn, wt_ref[...])
    for c0 in range(0, w_ref.shape[1], bn):
        cols = slice(c0, c0 + bn)
        if c0 < z_w:
            o_ref[:, cols] = _gelu_tanh(_dot(xn, w_ref[:, cols])).astype(o_ref.dtype)
        elif c0 < z_w + qkv_w:
            y = _dot(xe_ref[...], w_ref[:, cols])
            cw = cw_ref[:, c0 - z_w:c0 - z_w + bn]
            out = y[H:, :] * cw[DN_CONV - 1:DN_CONV, :]
            for k in range(1, DN_CONV):
                out = out + pltpu.roll(y, k, axis=0)[H:, :] * cw[DN_CONV - 1 - k:DN_CONV - k, :]
            o_ref[:, cols] = _silu(out).astype(o_ref.dtype)
        else:
            o_ref[:, cols] = _dot(xn, w_ref[:, cols]).astype(o_ref.dtype)


def od_projection(x, g, w, w_tail, conv_w, S, bm, bn=512):
    M, D = x.shape
    N = w.shape[1]
    H = BF16_SUBLANES
    whole = lambda a: pl.BlockSpec(a.shape, lambda i: (0, 0), pipeline_mode=pl.Buffered(1))
    return pl.pallas_call(
        functools.partial(_od_proj_kernel, blocks_per_seq=S // bm, bn=bn),
        grid=(M // bm,),
        in_specs=[pl.BlockSpec((bm, D), lambda i: (i, 0)),
                  pl.BlockSpec((H, D), lambda i: (jnp.maximum(i * (bm // H) - 1, 0), 0)),
                  pl.BlockSpec((1, D), lambda i: (0, 0)),
                  whole(w), whole(w_tail), whole(conv_w)],
        out_specs=[pl.BlockSpec((bm, N), lambda i: (i, 0)), pl.BlockSpec((bm, LANES), lambda i: (i, 0))],
        out_shape=[jax.ShapeDtypeStruct((M, N), BF16), jax.ShapeDtypeStruct((M, LANES), F32)],
        scratch_shapes=[pltpu.VMEM((H + bm, D), BF16)],
        compiler_params=_cparams("parallel"),
    )(x, x, g.reshape(1, D), w, w_tail, conv_w)


def _moba_kernel(q_ref, k_ref, v_ref, o_ref, kme_ref, vt_ref, sel_ref, m_ref, acc_ref, s_ref, sd_ref, *,
                 nb, nbp, unroll, pairs):
    BS = MOBA_BLOCK
    HD = MOBA_HEAD_DIM
    n_heads = 2 * pairs
    i = pl.program_id(2)
    lane = lax.broadcasted_iota(jnp.int32, (1, LANES), 1)
    head_lanes = (lane < HD, lane >= HD)
    pair_lanes = lambda u: slice((u // 2) * LANES, (u // 2 + 1) * LANES)

    @pl.when(i == 0)
    def _():
        kme_ref[...] = jnp.zeros_like(kme_ref)
        for n in range(nb):
            rows = slice(n * BS, (n + 1) * BS)
            mean = jnp.sum(k_ref[0, rows, :].astype(F32), axis=0, keepdims=True) / BS
            for u in range(n_heads):
                kme_ref[u // 2, (u % 2) * nbp + n:(u % 2) * nbp + n + 1, :] = jnp.where(
                    head_lanes[u % 2], mean[:, pair_lanes(u)], 0.0)
            vt_ref[:, rows] = v_ref[0, rows, :].astype(F32).T.astype(BF16)

    scale = HD ** -0.5 * math.log2(math.e)
    qs = [q_ref[0, :, p * LANES:(p + 1) * LANES] for p in range(pairs)]
    q_aug = [jnp.where(head_lanes[u % 2], qs[u // 2].astype(F32) * scale, 0.0).T.astype(BF16)
             for u in range(n_heads)]
    own = pl.ds(pl.multiple_of(i * BS, BS), BS)
    for u in range(n_heads):
        sd_ref[u] = _dot(k_ref[0, own, pair_lanes(u)], q_aug[u]).astype(BF16)

    gates = [_dot_nt(kme_ref[p], qs[p].astype(F32), precision=lax.Precision.HIGHEST)
             for p in range(pairs)]
    blk = lax.broadcasted_iota(jnp.int32, (nbp, 1), 0).astype(F32)
    valid = blk < i.astype(F32)
    for u in range(n_heads):
        g = jnp.where(valid, gates[u // 2][(u % 2) * nbp:(u % 2 + 1) * nbp], -jnp.inf)
        sel = jnp.zeros(g.shape, jnp.bool_)
        for _ in range(MOBA_TOPK):
            mx = jnp.max(g, axis=0, keepdims=True)
            idx = jnp.min(jnp.where(g == mx, blk, float(1 << 20)), axis=0, keepdims=True)
            pick = blk == idx
            sel = sel | pick
            g = jnp.where(pick, -jnp.inf, g)
        sel_ref[u, :nbp, :] = jnp.where(sel & valid, 1.0, 0.0)
        sel_ref[u, nbp:, :] = jnp.zeros((8, BS), F32)
        m_ref[u] = jnp.full((1, BS), NEG_BIG, F32)
        acc_ref[u] = jnp.zeros(acc_ref.shape[1:], F32)

    krow = lax.broadcasted_iota(jnp.int32, (BS, BS), 0)
    qcol = lax.broadcasted_iota(jnp.int32, (BS, BS), 1)
    PVR = HD + BF16_SUBLANES
    pv_rows = (slice(0, PVR), slice(LANES - PVR, LANES))
    pv_row = lax.broadcasted_iota(jnp.int32, (PVR, 1), 0)
    is_dim = (pv_row < HD, pv_row >= PVR - HD)

    def block_start(j):
        return pl.multiple_of(jnp.minimum(j, nb - 1) * BS, BS)

    def produce(g, slot):
        for t in range(unroll):
            rows = pl.ds(block_start(g * unroll + t), BS)
            for u in range(n_heads):
                s_ref[slot, u, t * BS:(t + 1) * BS, :] = _dot(
                    k_ref[0, rows, pair_lanes(u)], q_aug[u]).astype(BF16)

    def softmax_update(sts, sels, starts):
        heads = range(n_heads)
        m_new, alpha = [], []
        for u in heads:
            cand = jnp.full((1, BS), NEG_BIG, F32)
            for st, sel in zip(sts[u], sels[u]):
                mx = jnp.max(st.reshape(BS // BF16_SUBLANES, BF16_SUBLANES, BS), axis=0)
                mx = jnp.max(mx.astype(F32), axis=0, keepdims=True)
                cand = jnp.maximum(cand, mx if sel is None else jnp.where(sel, mx, NEG_BIG))
            m_old = m_ref[u]
            m_new.append(jnp.maximum(m_old, cand))
            alpha.append(jnp.exp2(m_old - m_new[u]))
            m_ref[u] = m_new[u]
        ps = []
        for u in heads:
            pu = []
            for st, sel in zip(sts[u], sels[u]):
                sub = m_new[u] if sel is None else jnp.where(sel, m_new[u], -NEG_BIG)
                pu.append(jnp.exp2(st - sub.astype(BF16)))
            ps.append(pu[0] if len(pu) == 1 else jnp.concatenate(pu, axis=0))
        pv = []
        for u in heads:
            rows = slice((u // 2) * LANES + pv_rows[u % 2].start, (u // 2) * LANES + pv_rows[u % 2].stop)
            vts = [jnp.where(is_dim[u % 2], vt_ref[rows, pl.ds(st0, BS)], jnp.ones((), BF16))
                   for st0 in starts]
            pv.append(_dot(vts[0] if len(vts) == 1 else jnp.concatenate(vts, axis=1), ps[u]))
        for u in heads:
            acc_ref[u] = acc_ref[u] * alpha[u] + pv[u]

    def consume(g, slot):
        js = [g * unroll + t for t in range(unroll)]
        sts = [[s_ref[slot, u, t * BS:(t + 1) * BS, :] for t in range(unroll)] for u in range(n_heads)]
        sels = [[sel_ref[u, pl.ds(j, 1), :] > 0.5 for j in js] for u in range(n_heads)]
        softmax_update(sts, sels, [block_start(j) for j in js])

    slots = s_ref.shape[0]

    def body(gg, c):
        for t in range(slots):
            produce(slots * gg + t + 1, (t + 1) % slots)
            consume(slots * gg + t, t)
        return c

    n_groups = (i + unroll - 1) // unroll
    produce(0, 0)
    lax.fori_loop(0, (n_groups + slots - 1) // slots, body, 0)
    softmax_update([[jnp.where(krow <= qcol, sd_ref[u], -jnp.inf)] for u in range(n_heads)],
                   [[None]] * n_heads, [block_start(i)])

    outs = []
    for u in range(n_heads):
        a = acc_ref[u]
        outs.append(a[:HD] / a[HD:HD + 1, :] if u % 2 == 0 else a[PVR - HD:] / a[0:1, :])
    o_ref[0] = jnp.concatenate(outs, axis=0).T.astype(o_ref.dtype)


def moba_attention(proj, B, S):
    nb = S // MOBA_BLOCK
    nbp = -(-nb // 8) * 8
    pairs = 2
    width = pairs * LANES
    groups = A_WIDTH // width
    unroll = 1
    slots = 3
    n_heads = 2 * pairs
    return pl.pallas_call(
        functools.partial(_moba_kernel, nb=nb, nbp=nbp, unroll=unroll, pairs=pairs),
        grid=(B, groups, nb),
        in_specs=[pl.BlockSpec((1, MOBA_BLOCK, width), lambda b, p, i: (b, i, p)),
                  pl.BlockSpec((1, S, width), lambda b, p, i: (b, 0, groups + p)),
                  pl.BlockSpec((1, S, width), lambda b, p, i: (b, 0, 2 * groups + p))],
        out_specs=pl.BlockSpec((1, MOBA_BLOCK, width), lambda b, p, i: (b, i, p)),
        out_shape=jax.ShapeDtypeStruct((B, S, A_WIDTH), BF16),
        scratch_shapes=[pltpu.VMEM((pairs, 2 * nbp, LANES), F32),
                        pltpu.VMEM((width, S), BF16),
                        pltpu.VMEM((n_heads, nbp + 8, MOBA_BLOCK), F32),
                        pltpu.VMEM((n_heads, 1, MOBA_BLOCK), F32),
                        pltpu.VMEM((n_heads, MOBA_HEAD_DIM + BF16_SUBLANES, MOBA_BLOCK), F32),
                        pltpu.VMEM((slots, n_heads, unroll * MOBA_BLOCK, MOBA_BLOCK), BF16),
                        pltpu.VMEM((n_heads, MOBA_BLOCK, MOBA_BLOCK), BF16)],
        compiler_params=_cparams("parallel", "parallel", "arbitrary"),
    )(proj, proj, proj)


def _pool_kernel(p_ref, halo_ref, w_ref, sc_ref, o_ref, *, blocks_per_seq):
    bm = p_ref.shape[0]
    H = BF16_SUBLANES
    i = pl.program_id(0)
    first = (i % blocks_per_seq) == 0
    t1 = (lax.broadcasted_iota(jnp.int32, (bm, 1), 0) + (i % blocks_per_seq) * bm + 1).astype(F32)
    for g, w in enumerate(POOL_WINDOWS):
        cols = slice(g * POOL_GROUP, (g + 1) * POOL_GROUP)
        cur = p_ref[:, cols].astype(F32)
        halo = jnp.where(first, 0.0, halo_ref[:, cols].astype(F32))
        ext = jnp.concatenate([halo, cur], axis=0)
        acc = ext
        sh = 1
        while sh < w:
            acc = acc + pltpu.roll(acc, sh, axis=0)
            sh *= 2
        win = acc[H:, :]
        pooled = win / jnp.minimum(t1, float(w)) - cur
        y = _dot(pooled.astype(BF16), w_ref[g])
        o_ref[:, cols] = (y * sc_ref[:, cols]).astype(o_ref.dtype)


def multiscale_pool(proj, pool_w, pool_scale, M, S, bm):
    H = BF16_SUBLANES
    pcol = 3 * A_WIDTH // POOL_WIDTH
    return pl.pallas_call(
        functools.partial(_pool_kernel, blocks_per_seq=S // bm),
        grid=(M // bm,),
        in_specs=[pl.BlockSpec((bm, POOL_WIDTH), lambda i: (i, pcol)),
                  pl.BlockSpec((H, POOL_WIDTH), lambda i: (jnp.maximum(i * (bm // H) - 1, 0), pcol)),
                  pl.BlockSpec((len(POOL_WINDOWS), POOL_GROUP, POOL_GROUP), lambda i: (0, 0, 0)),
                  pl.BlockSpec((1, POOL_WIDTH), lambda i: (0, 0))],
        out_specs=pl.BlockSpec((bm, POOL_WIDTH), lambda i: (i, 0)),
        out_shape=jax.ShapeDtypeStruct((M, POOL_WIDTH), BF16),
        compiler_params=_cparams("parallel"),
    )(proj, proj, pool_w, pool_scale.reshape(1, POOL_WIDTH))


def _mix_xattn_kernel(h_ref, a_ref, b_ref, wm_ref, g_ref, wq_ref, k_ref, v_ref, wo_ref, o_ref):
    ka = a_ref.shape[1]
    h = h_ref[...] + _dot(a_ref[...], wm_ref[:ka, :]) + _dot(b_ref[...], wm_ref[ka:, :])
    D = h.shape[1]
    hd = D // XATTN_HEADS
    xn = _rmsnorm(h, g_ref[...]).astype(BF16)
    q = (_dot(xn, wq_ref[...]) * hd ** -0.5).astype(BF16)
    outs = []
    for hh in range(XATTN_HEADS):
        cols = slice(hh * hd, (hh + 1) * hd)
        s = _dot_nt(q[:, cols], k_ref[:, cols])
        m = jnp.max(s, axis=1, keepdims=True)
        p = jnp.exp(s - m)
        l = jnp.sum(p, axis=1, keepdims=True)
        outs.append((_dot(p.astype(BF16), v_ref[:, cols]) / l).astype(BF16))
    o = jnp.concatenate(outs, axis=1)
    o_ref[...] = h + _dot(o, wo_ref[...])


def mix_xattn_residual(h, a, b, w_mix, g, wq, kv, wo, S, n_mem, bm):
    M, D = h.shape
    ka, kb = a.shape[1], b.shape[1]
    bps = S // bm
    whole = lambda arr: pl.BlockSpec(arr.shape, lambda i: (0, 0), pipeline_mode=pl.Buffered(1))
    return pl.pallas_call(
        _mix_xattn_kernel,
        grid=(M // bm,),
        in_specs=[pl.BlockSpec((bm, D), lambda i: (i, 0)),
                  pl.BlockSpec((bm, ka), lambda i: (i, 0)),
                  pl.BlockSpec((bm, kb), lambda i: (i, 0)),
                  whole(w_mix),
                  pl.BlockSpec((1, D), lambda i: (0, 0)),
                  whole(wq),
                  pl.BlockSpec((n_mem, D), lambda i: (i // bps, 0)),
                  pl.BlockSpec((n_mem, D), lambda i: (i // bps, 1)),
                  whole(wo)],
        out_specs=pl.BlockSpec((bm, D), lambda i: (i, 0)),
        out_shape=jax.ShapeDtypeStruct((M, D), F32),
        compiler_params=_cparams("parallel"),
    )(h, a, b, w_mix, g.reshape(1, D), wq, kv, kv, wo)


def _ffn_kernel(h_ref, halo_ref, g_ref, wup_ref, cw_ref, wd_ref, fg_ref, o_ref,
                xn_ref, acc_ref, y_ref, *, blocks_per_seq, final_norm, sub):
    H = BF16_SUBLANES
    nc, cf = wd_ref.shape[0], wd_ref.shape[1]
    n_sub = acc_ref.shape[0] // sub

    first = (pl.program_id(0) % blocks_per_seq) == 0
    xn_ref[:H, :] = jnp.where(first, 0.0, _rmsnorm(halo_ref[...], g_ref[...])).astype(BF16)
    xn_ref[H:, :] = _rmsnorm(h_ref[...], g_ref[...]).astype(BF16)
    acc_ref[...] = jnp.zeros_like(acc_ref)

    chunk_cols = lambda c: pl.ds(pl.multiple_of(c * cf, cf), cf)

    def up(c, r):
        if r == 0:
            xs, dst = xn_ref[:sub + H, :], slice(0, sub + H)
        else:
            xs, dst = xn_ref[H + r * sub:H + (r + 1) * sub, :], slice(H, sub + H)
            y_ref[r % 2, :H, :] = y_ref[(r - 1) % 2, sub:sub + H, :]
        y_ref[r % 2, dst, :cf] = _dot(xs, wup_ref[:, chunk_cols(c)])
        y_ref[r % 2, dst, cf:] = _dot(xs, wup_ref[:, chunk_cols(nc + c)])

    def conv(r, part, cw):
        cols = slice(part * cf, (part + 1) * cf)
        out = y_ref[r % 2, H:, cols] * cw[FFN_CONV - 1:FFN_CONV, :]
        for k in range(1, FFN_CONV):
            out = out + y_ref[r % 2, H - k:H - k + sub, cols] * cw[FFN_CONV - 1 - k:FFN_CONV - k, :]
        return out

    def chunk(c, carry):
        cwg, cwu = cw_ref[:, chunk_cols(c)], cw_ref[:, chunk_cols(nc + c)]
        wd = wd_ref[c]
        for r in range(n_sub):
            if r + 1 < n_sub:
                up(c, r + 1)
            else:
                up(jnp.minimum(c + 1, nc - 1), 0)
            act = _silu(conv(r, 0, cwg)) * conv(r, 1, cwu)
            acc_ref[r * sub:(r + 1) * sub, :] += _dot(act.astype(BF16), wd)
        return carry

    up(0, 0)
    lax.fori_loop(0, nc, chunk, 0)
    y = h_ref[...] + acc_ref[...]
    if final_norm:
        y = _rmsnorm(y, fg_ref[...])
    o_ref[...] = y


def ffn_residual(h, g, w_up, conv_w, w_down, final_g, S, bm, cf, final_norm):
    M, D = h.shape
    d_ff = w_down.shape[0]
    H = BF16_SUBLANES
    nc = d_ff // cf
    sub = min(256, bm)
    assert (bm // sub) % 2 == 0, "the two y_ref slots alternate per sub-block across chunks"
    wd3 = w_down.reshape(nc, cf, D)
    whole = lambda a: pl.BlockSpec(a.shape, lambda i: (0,) * a.ndim, pipeline_mode=pl.Buffered(1))
    return pl.pallas_call(
        functools.partial(_ffn_kernel, blocks_per_seq=S // bm, final_norm=final_norm, sub=sub),
        grid=(M // bm,),
        in_specs=[pl.BlockSpec((bm, D), lambda i: (i, 0)),
                  pl.BlockSpec((H, D), lambda i: (jnp.maximum(i * (bm // H) - 1, 0), 0)),
                  pl.BlockSpec((1, D), lambda i: (0, 0)),
                  whole(w_up), whole(conv_w), whole(wd3),
                  pl.BlockSpec((1, D), lambda i: (0, 0))],
        out_specs=pl.BlockSpec((bm, D), lambda i: (i, 0)),
        out_shape=jax.ShapeDtypeStruct((M, D), F32),
        scratch_shapes=[pltpu.VMEM((H + bm, D), BF16), pltpu.VMEM((bm, D), F32),
                        pltpu.VMEM((2, H + sub, 2 * cf), F32)],
        compiler_params=_cparams("parallel"),
    )(h, h, g.reshape(1, D), w_up, conv_w, wd3, final_g.reshape(1, D))


def _sgu_kernel(u_ref, v_ref, lg_ref, lb_ref, w_ref, bt_ref, o_ref):
    rows = u_ref.shape[0]
    T = SGU_CHUNK
    v = v_ref[...].astype(F32)
    mu = jnp.mean(v, axis=-1, keepdims=True)
    d = v - mu
    var = jnp.mean(d * d, axis=-1, keepdims=True)
    vn = (d * lax.rsqrt(var + EPS) * lg_ref[...] + lb_ref[...]).astype(BF16)
    causal = (lax.broadcasted_iota(jnp.int32, (T, T), 1) <= lax.broadcasted_iota(jnp.int32, (T, T), 0))
    for g in range(SGU_GROUPS):
        cols = slice(g * SGU_GROUP, (g + 1) * SGU_GROUP)
        wg = jnp.where(causal, w_ref[g], 0.0).astype(BF16)
        bias = bt_ref[:, g:g + 1]
        for c in range(rows // T):
            rs = slice(c * T, (c + 1) * T)
            s = _dot(wg, vn[rs, cols]) + bias
            o_ref[rs, cols] = (u_ref[rs, cols].astype(F32) * s).astype(o_ref.dtype)


def spatial_gating(proj, ln_g, ln_b, w_s, b_s, M, rows):
    return pl.pallas_call(
        _sgu_kernel,
        grid=(M // rows,),
        in_specs=[pl.BlockSpec((rows, SGU_WIDTH), lambda i: (i, 0)),
                  pl.BlockSpec((rows, SGU_WIDTH), lambda i: (i, 1)),
                  pl.BlockSpec((1, SGU_WIDTH), lambda i: (0, 0)),
                  pl.BlockSpec((1, SGU_WIDTH), lambda i: (0, 0)),
                  pl.BlockSpec((SGU_GROUPS, SGU_CHUNK, SGU_CHUNK), lambda i: (0, 0, 0)),
                  pl.BlockSpec((SGU_CHUNK, SGU_GROUPS), lambda i: (0, 0))],
        out_specs=pl.BlockSpec((rows, SGU_WIDTH), lambda i: (i, 0)),
        out_shape=jax.ShapeDtypeStruct((M, SGU_WIDTH), BF16),
        compiler_params=_cparams("parallel"),
    )(proj, proj, ln_g.reshape(1, -1), ln_b.reshape(1, -1), w_s, b_s.T)


def _gdn_intra_kernel(q_ref, k_ref, v_ref, tail_ref, tailt_ref, alog_ref,
                      dtb_ref, u_ref, w_ref, qd_ref, kd_ref, qk_ref, gl_ref):
    rows = q_ref.shape[0]
    C = DN_CHUNK
    HD = DN_HEAD_DIM
    x = jnp.concatenate([q_ref[...], k_ref[...], v_ref[...]], axis=1).astype(F32)

    ii = lax.broadcasted_iota(jnp.int32, (C, C), 0)
    jj = lax.broadcasted_iota(jnp.int32, (C, C), 1)
    lower = jj <= ii
    strict = jj < ii
    su = lax.broadcasted_iota(jnp.int32, (C, LANES), 0)
    ju = lax.broadcasted_iota(jnp.int32, (C, LANES), 1)
    upper_ext = jnp.where(((ju < C) & (su > ju)) | (ju == C), 1.0, 0.0)

    inst = [(c, hh) for hh in range(DN_HEADS) for c in range(rows // C)]
    qs, ks, vs, bs, stacks = [], [], [], [], []
    for hh in range(DN_HEADS):
        qh = x[:, hh * HD:(hh + 1) * HD]
        kh = x[:, DN_WIDTH + hh * HD:DN_WIDTH + (hh + 1) * HD]
        vh = x[:, 2 * DN_WIDTH + hh * HD:2 * DN_WIDTH + (hh + 1) * HD]
        qh = qh * lax.rsqrt(jnp.sum(qh * qh, axis=-1, keepdims=True) + EPS) * HD ** -0.5
        kh = kh * lax.rsqrt(jnp.sum(kh * kh, axis=-1, keepdims=True) + EPS)
        beta = 1.0 / (1.0 + jnp.exp(-tail_ref[:, hh:hh + 1]))
        a_raw = tailt_ref[DN_HEADS + hh:DN_HEADS + hh + 1, :]
        z = a_raw + dtb_ref[0:1, hh:hh + 1]
        softplus = jnp.maximum(z, 0.0) + jnp.log(1.0 + jnp.exp(-jnp.abs(z)))
        g_row = -jnp.exp(alog_ref[0:1, hh:hh + 1]) * softplus
        for c in range(rows // C):
            rs = slice(c * C, (c + 1) * C)
            qs.append(qh[rs]); ks.append(kh[rs]); vs.append(vh[rs]); bs.append(beta[rs])
            gr = jnp.broadcast_to(g_row[:, rs], (C, C))
            stacks += [jnp.where(lower, gr, 0.0), jnp.where(lower, 0.0, gr)]

    stacked = jnp.concatenate(stacks, axis=0)
    s_hi = stacked.astype(BF16)
    r1 = stacked - s_hi.astype(F32)
    s_mid = r1.astype(BF16)
    s_lo = (r1 - s_mid.astype(F32)).astype(BF16)
    ue = upper_ext.astype(BF16)
    dall = _dot(s_hi, ue) + _dot(s_mid, ue) + _dot(s_lo, ue)

    decays, gcs, gc_revs, k16s, kbs = [], [], [], [], []
    for n, (c, hh) in enumerate(inst):
        dext = dall[n * 2 * C:(n + 1) * 2 * C]
        decays.append(jnp.exp(jnp.where(lower, dext[:C, :C], -jnp.inf)))
        gcs.append(dext[:C, C:C + 1])
        gc_revs.append(dext[C:, C:C + 1])
        kbs.append(ks[n] * bs[n])
        k16s.append(ks[n].astype(BF16))
    kk = [_dot_nt(kbs[n].astype(BF16), k16s[n]) for n in range(len(inst))]
    qk = [_dot_nt(qs[n].astype(BF16), k16s[n]) for n in range(len(inst))]
    pws = [jnp.where(strict, kk[n] * decays[n], 0.0).astype(BF16) for n in range(len(inst))]
    egc = [jnp.exp(g) for g in gcs]
    rhs = [jnp.concatenate([vs[n] * bs[n], kbs[n] * egc[n]], axis=1) for n in range(len(inst))]
    sols = [rhs[n] - _dot(pws[n], rhs[n].astype(BF16)) for n in range(len(inst))]
    for _ in range(int(math.log2(C)) - 1):
        pws = [_dot(p, p).astype(BF16) for p in pws]
        sols = [s + _dot(p, s.astype(BF16)) for p, s in zip(pws, sols)]
    for n, (c, hh) in enumerate(inst):
        rs = slice(c * C, (c + 1) * C)
        hcols = slice(hh * HD, (hh + 1) * HD)
        u_ref[rs, hcols] = sols[n][:, :HD].astype(u_ref.dtype)
        w_ref[rs, hcols] = sols[n][:, HD:].astype(w_ref.dtype)
        qkd = jnp.where(lower, qk[n] * decays[n], 0.0)
        qk_ref[rs, hcols] = jnp.concatenate([qkd, jnp.zeros_like(qkd)], axis=1).astype(qk_ref.dtype)
        qd_ref[rs, hcols] = (qs[n] * egc[n]).astype(qd_ref.dtype)
        kd_ref[rs, hcols] = (ks[n] * jnp.exp(gc_revs[n])).astype(kd_ref.dtype)
        gl_ref[c * 8:(c + 1) * 8, hcols] = jnp.broadcast_to(egc[n][C - 1:C, :], (8, HD))


def gdn_intra(proj, tail, a_log, dt_bias, M, rows):
    c0 = 2 * SGU_WIDTH // DN_WIDTH
    tail_t = tail[:, :2 * DN_HEADS].T
    pad = lambda p: jnp.pad(p.reshape(1, -1), ((0, 0), (0, LANES - p.shape[0])))
    seq = lambda dt: jax.ShapeDtypeStruct((M, DN_WIDTH), dt)
    row_spec = pl.BlockSpec((rows, DN_WIDTH), lambda i: (i, 0))
    cur_spec = lambda part: pl.BlockSpec((rows, DN_WIDTH), lambda i: (i, c0 + part))
    return pl.pallas_call(
        _gdn_intra_kernel,
        grid=(M // rows,),
        in_specs=[cur_spec(0), cur_spec(1), cur_spec(2),
                  pl.BlockSpec((rows, LANES), lambda i: (i, 0)),
                  pl.BlockSpec((2 * DN_HEADS, rows), lambda i: (0, i)),
                  pl.BlockSpec((1, LANES), lambda i: (0, 0)),
                  pl.BlockSpec((1, LANES), lambda i: (0, 0))],
        out_specs=[row_spec, row_spec, row_spec, row_spec, row_spec,
                   pl.BlockSpec((rows // DN_CHUNK * 8, DN_WIDTH), lambda i: (i, 0))],
        out_shape=[seq(BF16), seq(BF16), seq(BF16), seq(BF16), seq(BF16),
                   jax.ShapeDtypeStruct((M // DN_CHUNK * 8, DN_WIDTH), F32)],
        compiler_params=_cparams("parallel"),
    )(proj, proj, proj, tail, tail_t, pad(a_log), pad(dt_bias))


def _gdn_scan_kernel(u_ref, w_ref, qd_ref, kd_ref, qk_ref, gl_ref, gate_ref, ng_ref, o_ref, state_ref, *,
                     chunks):
    C = DN_CHUNK
    HD = DN_HEAD_DIM
    B = u_ref.shape[0]

    @pl.when(pl.program_id(0) == 0)
    def _():
        state_ref[...] = jnp.zeros_like(state_ref)

    ng = ng_ref[...]
    inst = [(b, hh) for b in range(B) for hh in range(DN_HEADS)]
    col = lambda hh: slice(hh * HD, (hh + 1) * HD)
    states = [state_ref[b, hh] for b, hh in inst]
    for c in range(chunks):
        rs = slice(c * C, (c + 1) * C)
        kdt = [kd_ref[b, rs, col(hh)].astype(F32).T.astype(BF16) for b, hh in inst]
        st16 = [s.astype(BF16) for s in states]
        ws = [_dot(w_ref[b, rs, col(hh)], st16[n]) for n, (b, hh) in enumerate(inst)]
        qs = [_dot(qd_ref[b, rs, col(hh)], st16[n]) for n, (b, hh) in enumerate(inst)]
        vn16 = [(u_ref[b, rs, col(hh)].astype(F32) - ws[n]).astype(BF16) for n, (b, hh) in enumerate(inst)]
        states = [states[n] * gl_ref[b, c * 8:c * 8 + 1, col(hh)] + _dot(kdt[n], vn16[n])
                  for n, (b, hh) in enumerate(inst)]
        for n, (b, hh) in enumerate(inst):
            o = qs[n] + _dot(qk_ref[b, rs, col(hh)][:, :C], vn16[n])
            o = o * lax.rsqrt(jnp.mean(o * o, axis=-1, keepdims=True) + EPS) * ng
            o_ref[b, rs, col(hh)] = (o * _silu(gate_ref[b, rs, col(hh)].astype(F32))).astype(o_ref.dtype)
    for n, (b, hh) in enumerate(inst):
        state_ref[b, hh] = states[n]


def gdn_scan(u, w, qd, kd, qk, gl, proj3, norm_g, B, S, chunks):
    rows = chunks * DN_CHUNK
    r3 = lambda a: a.reshape(B, S, DN_WIDTH)
    gcol = (2 * SGU_WIDTH + 3 * DN_WIDTH) // DN_WIDTH
    seq_spec = pl.BlockSpec((B, rows, DN_WIDTH), lambda n: (0, n, 0))
    return pl.pallas_call(
        functools.partial(_gdn_scan_kernel, chunks=chunks),
        grid=(S // rows,),
        in_specs=[seq_spec, seq_spec, seq_spec, seq_spec, seq_spec,
                  pl.BlockSpec((B, chunks * 8, DN_WIDTH), lambda n: (0, n, 0)),
                  pl.BlockSpec((B, rows, DN_WIDTH), lambda n: (0, n, gcol)),
                  pl.BlockSpec((1, DN_HEAD_DIM), lambda n: (0, 0))],
        out_specs=seq_spec,
        out_shape=jax.ShapeDtypeStruct((B, S, DN_WIDTH), BF16),
        scratch_shapes=[pltpu.VMEM((B, DN_HEADS, DN_HEAD_DIM, DN_HEAD_DIM), F32)],
        compiler_params=_cparams("arbitrary"),
    )(r3(u), r3(w), r3(qd), r3(kd), r3(qk), gl.reshape(B, S // DN_CHUNK * 8, DN_WIDTH), proj3,
      norm_g.reshape(1, DN_HEAD_DIM))


def _forward(x, mem, mem_norm, norm_mix, norm_xattn, norm_ffn, ev_w_in, pool_w, pool_scale, ev_w_out,
             od_w_in, sgu_ln_g, sgu_ln_b, sgu_w, sgu_b, dn_conv, dn_a_log, dn_dt_bias, dn_norm_g,
             od_w_out, xattn_wq, xattn_wkv, xattn_wo, ffn_w_up, ffn_conv, ffn_w_down, final_norm):
    B, S, D = x.shape
    n_mem = mem.shape[1]
    M = B * S
    depth = norm_mix.shape[0]
    bf = lambda a: a.astype(BF16)
    bm = min(1024, S)
    bm_big = min(1024, S)

    h = x.reshape(M, D)
    mem2 = mem.reshape(B * n_mem, D)
    for layer in range(depth):
        i = layer // 2
        if layer % 2 == 0:
            proj, = norm_matmul(h, norm_mix[layer], [bf(ev_w_in[i])], [BF16], bm_big)
            a_out = moba_attention(proj.reshape(B, S, -1), B, S).reshape(M, A_WIDTH)
            b_out = multiscale_pool(proj, bf(pool_w[i]), pool_scale[i], M, S, bm)
            mix_a, mix_b, w_mix = a_out, b_out, ev_w_out[i]
        else:
            main_w = 2 * SGU_WIDTH + 4 * DN_WIDTH
            w_in = od_w_in[i]
            w_tail = jnp.pad(w_in[:, main_w:], ((0, 0), (0, LANES - 2 * DN_HEADS)))
            proj, tail = od_projection(h, norm_mix[layer], bf(w_in[:, :main_w]), bf(w_tail), dn_conv[i], S,
                                       bm_big)
            c_out = spatial_gating(proj, sgu_ln_g[i], sgu_ln_b[i], sgu_w[i], sgu_b[i], M, min(512, S))
            u, w, qd, kd, qk, gl = gdn_intra(proj, tail, dn_a_log[i], dn_dt_bias[i], M, min(512, S))
            d_out = gdn_scan(u, w, qd, kd, qk, gl, proj.reshape(B, S, -1), dn_norm_g[i], B, S, 4)
            mix_a, mix_b, w_mix = c_out, d_out.reshape(M, DN_WIDTH), od_w_out[i]
        kv, = norm_matmul(mem2, mem_norm, [bf(xattn_wkv[layer])], [BF16], B * n_mem)
        h = mix_xattn_residual(h, mix_a, mix_b, bf(w_mix), norm_xattn[layer], bf(xattn_wq[layer]), kv,
                               bf(xattn_wo[layer]), S, n_mem, bm)
        h = ffn_residual(h, norm_ffn[layer], bf(ffn_w_up[layer]), ffn_conv[layer], bf(ffn_w_down[layer]),
                         final_norm, S, bm_big, 256, final_norm=(layer == depth - 1))
    return h.reshape(B, S, D)


def kernel(x, mem, mem_norm, norm_mix, norm_xattn, norm_ffn, ev_w_in, pool_w, pool_scale, ev_w_out, od_w_in, sgu_ln_g, sgu_ln_b, sgu_w, sgu_b, dn_conv, dn_a_log, dn_dt_bias, dn_norm_g, od_w_out, xattn_wq, xattn_wkv, xattn_wo, ffn_w_up, ffn_conv, ffn_w_down, final_norm):
    return _forward(x, mem, mem_norm, norm_mix, norm_xattn, norm_ffn, ev_w_in, pool_w, pool_scale, ev_w_out,
                    od_w_in, sgu_ln_g, sgu_ln_b, sgu_w, sgu_b, dn_conv, dn_a_log, dn_dt_bias, dn_norm_g,
                    od_w_out, xattn_wq, xattn_wkv, xattn_wo, ffn_w_up, ffn_conv, ffn_w_down, final_norm)
```

```python
import functools
import math

import jax
import jax.numpy as jnp
from jax import lax
from jax.experimental import pallas as pl
from jax.experimental.pallas import tpu as pltpu

F32 = jnp.float32
BF16 = jnp.bfloat16
EPS = 1e-6
NEG_BIG = -1e30

VMEM_LIMIT_BYTES = 48 * 1024 * 1024
BF16_SUBLANES = 16
LANES = 128

MOBA_HEADS, MOBA_HEAD_DIM, MOBA_BLOCK, MOBA_TOPK = 8, 64, 256, 3
A_WIDTH = MOBA_HEADS * MOBA_HEAD_DIM
POOL_WINDOWS = (2, 4, 8, 16)
POOL_GROUP = 128
POOL_WIDTH = POOL_GROUP * len(POOL_WINDOWS)
SGU_GROUPS, SGU_GROUP, SGU_CHUNK = 4, 128, 128
SGU_WIDTH = SGU_GROUPS * SGU_GROUP
DN_HEADS, DN_HEAD_DIM, DN_CONV, DN_CHUNK = 4, 128, 4, 64
DN_WIDTH = DN_HEADS * DN_HEAD_DIM
XATTN_HEADS = 4
FFN_CONV = 3


def _cparams(*sem):
    return pltpu.CompilerParams(dimension_semantics=sem, vmem_limit_bytes=VMEM_LIMIT_BYTES)


def _rmsnorm(x, g):
    return x * lax.rsqrt(jnp.mean(x * x, axis=-1, keepdims=True) + EPS) * g


def _silu(x):
    return x * (0.5 * jnp.tanh(0.5 * x) + 0.5)


def _dot(a, b):
    return jnp.dot(a, b, preferred_element_type=F32)


def _resident(arr, layer=None):
    if layer is None:
        return pl.BlockSpec(arr.shape, lambda i: (0,) * arr.ndim, pipeline_mode=pl.Buffered(1))
    return pl.BlockSpec((None,) + arr.shape[1:], lambda i: (layer,) + (0,) * (arr.ndim - 1),
                        pipeline_mode=pl.Buffered(1))


def _dot_nt(a, b, precision=None):
    return lax.dot_general(a, b, (((1,), (1,)), ((), ())), preferred_element_type=F32,
                           precision=precision)


def _dot_tn(a, b):
    return lax.dot_general(a, b, (((0,), (0,)), ((), ())), preferred_element_type=F32)


def _norm_mm_kernel(x_ref, g_ref, *refs, bn):
    n = len(refs) // 2
    xn = _rmsnorm(x_ref[...], g_ref[...]).astype(BF16)
    for w_ref, o_ref in zip(refs[:n], refs[n:]):
        N = w_ref.shape[1]
        for c0 in range(0, N, bn):
            c1 = min(c0 + bn, N)
            o_ref[:, c0:c1] = _dot(xn, w_ref[:, c0:c1]).astype(o_ref.dtype)


def norm_matmul(x, g, ws, out_dtypes, bm, bn=512, layer=None):
    M, D = x.shape
    return pl.pallas_call(
        functools.partial(_norm_mm_kernel, bn=bn),
        grid=(M // bm,),
        in_specs=[pl.BlockSpec((bm, D), lambda i: (i, 0)),
                  pl.BlockSpec((1, D), lambda i: (0, 0))]
                 + [_resident(w, layer) for w in ws],
        out_specs=[pl.BlockSpec((bm, w.shape[-1]), lambda i: (i, 0)) for w in ws],
        out_shape=[jax.ShapeDtypeStruct((M, w.shape[-1]), dt) for w, dt in zip(ws, out_dtypes)],
        compiler_params=_cparams("parallel"),
    )(x, g.reshape(1, D), *ws)


def _gelu_tanh(x):
    return 0.5 * x * (1.0 + jnp.tanh(math.sqrt(2.0 / math.pi) * (x + 0.044715 * (x * x * x))))


def _od_proj_kernel(x_ref, halo_ref, g_ref, w_ref, wt_ref, wtt_ref, cw_ref, o_ref, t_ref, tt_ref, xe_ref, *,
                    blocks_per_seq, bn):
    bm = x_ref.shape[0]
    H = BF16_SUBLANES
    z_w, qkv_w = 2 * SGU_WIDTH, 3 * DN_WIDTH
    first = (pl.program_id(0) % blocks_per_seq) == 0
    xe_ref[:H, :] = jnp.where(first, 0.0, _rmsnorm(halo_ref[...], g_ref[...])).astype(BF16)
    xe_ref[H:, :] = _rmsnorm(x_ref[...], g_ref[...]).astype(BF16)
    xn = xe_ref[H:, :]
    t_ref[...] = _dot(xn, wt_ref[...])
    tt_ref[...] = _dot_nt(wtt_ref[...], xn)
    for c0 in range(0, w_ref.shape[1], bn):
        cols = slice(c0, c0 + bn)
        if c0 < z_w:
            o_ref[:, cols] = _gelu_tanh(_dot(xn, w_ref[:, cols])).astype(o_ref.dtype)
        elif c0 < z_w + qkv_w:
            y = _dot(xe_ref[...], w_ref[:, cols])
            cw = cw_ref[:, c0 - z_w:c0 - z_w + bn]
            out = y[H:, :] * cw[DN_CONV - 1:DN_CONV, :]
            for k in range(1, DN_CONV):
                out = out + pltpu.roll(y, k, axis=0)[H:, :] * cw[DN_CONV - 1 - k:DN_CONV - k, :]
            o_ref[:, cols] = _silu(out).astype(o_ref.dtype)
        else:
            o_ref[:, cols] = _dot(xn, w_ref[:, cols]).astype(o_ref.dtype)


def od_projection(x, g, w, w_tail, w_tail_t, conv_w, S, bm, bn=512):
    M, D = x.shape
    N = w.shape[1]
    H = BF16_SUBLANES
    return pl.pallas_call(
        functools.partial(_od_proj_kernel, blocks_per_seq=S // bm, bn=bn),
        grid=(M // bm,),
        in_specs=[pl.BlockSpec((bm, D), lambda i: (i, 0)),
                  pl.BlockSpec((H, D), lambda i: (jnp.maximum(i * (bm // H) - 1, 0), 0)),
                  pl.BlockSpec((1, D), lambda i: (0, 0)),
                  _resident(w), _resident(w_tail), _resident(w_tail_t), _resident(conv_w)],
        out_specs=[pl.BlockSpec((bm, N), lambda i: (i, 0)), pl.BlockSpec((bm, LANES), lambda i: (i, 0)),
                   pl.BlockSpec((H, bm), lambda i: (0, i))],
        out_shape=[jax.ShapeDtypeStruct((M, N), BF16), jax.ShapeDtypeStruct((M, LANES), F32),
                   jax.ShapeDtypeStruct((H, M), F32)],
        scratch_shapes=[pltpu.VMEM((H + bm, D), BF16)],
        compiler_params=_cparams("parallel"),
    )(x, x, g.reshape(1, D), w, w_tail, w_tail_t, conv_w)


def _moba_kernel(q_ref, k_ref, v_ref, o_ref, kme_ref, vt_ref, sel_ref, m_ref, acc_ref, s_ref, sd_ref, *,
                 nb, nbp, unroll, pairs):
    BS = MOBA_BLOCK
    HD = MOBA_HEAD_DIM
    n_heads = 2 * pairs
    i = pl.program_id(2)
    lane = lax.broadcasted_iota(jnp.int32, (1, LANES), 1)
    head_lanes = (lane < HD, lane >= HD)
    pair_lanes = lambda u: slice((u // 2) * LANES, (u // 2 + 1) * LANES)

    @pl.when(i == 0)
    def _():
        kme_ref[...] = jnp.zeros_like(kme_ref)
        for n in range(nb):
            rows = slice(n * BS, (n + 1) * BS)
            mean = jnp.sum(k_ref[0, rows, :].astype(F32), axis=0, keepdims=True) / BS
            for u in range(n_heads):
                kme_ref[u // 2, (u % 2) * nbp + n:(u % 2) * nbp + n + 1, :] = jnp.where(
                    head_lanes[u % 2], mean[:, pair_lanes(u)], 0.0)
            vt_ref[:, rows] = v_ref[0, rows, :].astype(F32).T.astype(BF16)

    scale = HD ** -0.5 * math.log2(math.e)
    qs = [q_ref[0, :, p * LANES:(p + 1) * LANES] for p in range(pairs)]
    q_aug = [jnp.where(head_lanes[u % 2], qs[u // 2].astype(F32) * scale, 0.0).T.astype(BF16)
             for u in range(n_heads)]
    own = pl.ds(pl.multiple_of(i * BS, BS), BS)
    for u in range(n_heads):
        sd_ref[u] = _dot(k_ref[0, own, pair_lanes(u)], q_aug[u]).astype(BF16)

    gates = [_dot_nt(kme_ref[p], qs[p].astype(F32), precision=lax.Precision.HIGHEST)
             for p in range(pairs)]
    blk = lax.broadcasted_iota(jnp.int32, (nbp, 1), 0).astype(F32)
    valid = blk < i.astype(F32)
    for u in range(n_heads):
        g = jnp.where(valid, gates[u // 2][(u % 2) * nbp:(u % 2 + 1) * nbp], -jnp.inf)
        sel = jnp.zeros(g.shape, jnp.bool_)
        for _ in range(MOBA_TOPK):
            mx = jnp.max(g, axis=0, keepdims=True)
            idx = jnp.min(jnp.where(g == mx, blk, float(1 << 20)), axis=0, keepdims=True)
            pick = blk == idx
            sel = sel | pick
            g = jnp.where(pick, -jnp.inf, g)
        sel_ref[u, :nbp, :] = jnp.where(sel & valid, 1.0, 0.0)
        sel_ref[u, nbp:, :] = jnp.zeros((8, BS), F32)
        m_ref[u] = jnp.full((1, BS), NEG_BIG, F32)
        acc_ref[u] = jnp.zeros(acc_ref.shape[1:], F32)

    krow = lax.broadcasted_iota(jnp.int32, (BS, BS), 0)
    qcol = lax.broadcasted_iota(jnp.int32, (BS, BS), 1)
    PVR = HD + BF16_SUBLANES
    pv_rows = (slice(0, PVR), slice(LANES - PVR, LANES))
    pv_row = lax.broadcasted_iota(jnp.int32, (PVR, 1), 0)
    is_dim = (pv_row < HD, pv_row >= PVR - HD)

    def block_start(j):
        return pl.multiple_of(jnp.minimum(j, nb - 1) * BS, BS)

    def produce(g, slot):
        for t in range(unroll):
            rows = pl.ds(block_start(g * unroll + t), BS)
            for u in range(n_heads):
                s_ref[slot, u, t * BS:(t + 1) * BS, :] = _dot(
                    k_ref[0, rows, pair_lanes(u)], q_aug[u]).astype(BF16)

    def softmax_update(sts, sels, starts):
        heads = range(n_heads)
        m_new, alpha = [], []
        for u in heads:
            cand = jnp.full((1, BS), NEG_BIG, F32)
            for st, sel in zip(sts[u], sels[u]):
                mx = jnp.max(st.reshape(BS // BF16_SUBLANES, BF16_SUBLANES, BS), axis=0)
                mx = jnp.max(mx.astype(F32), axis=0, keepdims=True)
                cand = jnp.maximum(cand, mx if sel is None else jnp.where(sel, mx, NEG_BIG))
            m_old = m_ref[u]
            m_new.append(jnp.maximum(m_old, cand))
            alpha.append(jnp.exp2(m_old - m_new[u]))
            m_ref[u] = m_new[u]
        ps = []
        for u in heads:
            pu = []
            for st, sel in zip(sts[u], sels[u]):
                sub = m_new[u] if sel is None else jnp.where(sel, m_new[u], -NEG_BIG)
                pu.append(jnp.exp2(st - sub.astype(BF16)))
            ps.append(pu[0] if len(pu) == 1 else jnp.concatenate(pu, axis=0))
        pv = []
        for u in heads:
            rows = slice((u // 2) * LANES + pv_rows[u % 2].start, (u // 2) * LANES + pv_rows[u % 2].stop)
            vts = [jnp.where(is_dim[u % 2], vt_ref[rows, pl.ds(st0, BS)], jnp.ones((), BF16))
                   for st0 in starts]
            pv.append(_dot(vts[0] if len(vts) == 1 else jnp.concatenate(vts, axis=1), ps[u]))
        for u in heads:
            acc_ref[u] = acc_ref[u] * alpha[u] + pv[u]

    def consume(g, slot):
        js = [g * unroll + t for t in range(unroll)]
        sts = [[s_ref[slot, u, t * BS:(t + 1) * BS, :] for t in range(unroll)] for u in range(n_heads)]
        sels = [[sel_ref[u, pl.ds(j, 1), :] > 0.5 for j in js] for u in range(n_heads)]
        softmax_update(sts, sels, [block_start(j) for j in js])

    slots = s_ref.shape[0]

    def body(gg, c):
        for t in range(slots):
            produce(slots * gg + t + 1, (t + 1) % slots)
            consume(slots * gg + t, t)
        return c

    n_groups = (i + unroll - 1) // unroll
    produce(0, 0)
    lax.fori_loop(0, (n_groups + slots - 1) // slots, body, 0)
    softmax_update([[jnp.where(krow <= qcol, sd_ref[u], -jnp.inf)] for u in range(n_heads)],
                   [[None]] * n_heads, [block_start(i)])

    outs = []
    for u in range(n_heads):
        a = acc_ref[u]
        outs.append(a[:HD] / a[HD:HD + 1, :] if u % 2 == 0 else a[PVR - HD:] / a[0:1, :])
    o_ref[0] = jnp.concatenate(outs, axis=0).T.astype(o_ref.dtype)


def moba_attention(proj, B, S):
    nb = S // MOBA_BLOCK
    nbp = -(-nb // 8) * 8
    pairs = 2
    width = pairs * LANES
    groups = A_WIDTH // width
    unroll = 1
    slots = 3
    n_heads = 2 * pairs
    return pl.pallas_call(
        functools.partial(_moba_kernel, nb=nb, nbp=nbp, unroll=unroll, pairs=pairs),
        grid=(B, groups, nb),
        in_specs=[pl.BlockSpec((1, MOBA_BLOCK, width), lambda b, p, i: (b, i, p)),
                  pl.BlockSpec((1, S, width), lambda b, p, i: (b, 0, groups + p)),
                  pl.BlockSpec((1, S, width), lambda b, p, i: (b, 0, 2 * groups + p))],
        out_specs=pl.BlockSpec((1, MOBA_BLOCK, width), lambda b, p, i: (b, i, p)),
        out_shape=jax.ShapeDtypeStruct((B, S, A_WIDTH), BF16),
        scratch_shapes=[pltpu.VMEM((pairs, 2 * nbp, LANES), F32),
                        pltpu.VMEM((width, S), BF16),
                        pltpu.VMEM((n_heads, nbp + 8, MOBA_BLOCK), F32),
                        pltpu.VMEM((n_heads, 1, MOBA_BLOCK), F32),
                        pltpu.VMEM((n_heads, MOBA_HEAD_DIM + BF16_SUBLANES, MOBA_BLOCK), F32),
                        pltpu.VMEM((slots, n_heads, unroll * MOBA_BLOCK, MOBA_BLOCK), BF16),
                        pltpu.VMEM((n_heads, MOBA_BLOCK, MOBA_BLOCK), BF16)],
        compiler_params=_cparams("parallel", "parallel", "arbitrary"),
    )(proj, proj, proj)


def _pool_kernel(p_ref, halo_ref, w_ref, sc_ref, o_ref, *, blocks_per_seq):
    bm = p_ref.shape[0]
    H = BF16_SUBLANES
    i = pl.program_id(0)
    first = (i % blocks_per_seq) == 0
    t1 = (lax.broadcasted_iota(jnp.int32, (bm, 1), 0) + (i % blocks_per_seq) * bm + 1).astype(F32)
    for g, w in enumerate(POOL_WINDOWS):
        cols = slice(g * POOL_GROUP, (g + 1) * POOL_GROUP)
        cur = p_ref[:, cols].astype(F32)
        halo = jnp.where(first, 0.0, halo_ref[:, cols].astype(F32))
        ext = jnp.concatenate([halo, cur], axis=0)
        acc = ext
        sh = 1
        while sh < w:
            acc = acc + pltpu.roll(acc, sh, axis=0)
            sh *= 2
        win = acc[H:, :]
        pooled = win / jnp.minimum(t1, float(w)) - cur
        y = _dot(pooled.astype(BF16), w_ref[g])
        o_ref[:, cols] = (y * sc_ref[:, cols]).astype(o_ref.dtype)


def multiscale_pool(proj, pool_w, pool_scale, M, S, bm):
    H = BF16_SUBLANES
    pcol = 3 * A_WIDTH // POOL_WIDTH
    return pl.pallas_call(
        functools.partial(_pool_kernel, blocks_per_seq=S // bm),
        grid=(M // bm,),
        in_specs=[pl.BlockSpec((bm, POOL_WIDTH), lambda i: (i, pcol)),
                  pl.BlockSpec((H, POOL_WIDTH), lambda i: (jnp.maximum(i * (bm // H) - 1, 0), pcol)),
                  pl.BlockSpec((len(POOL_WINDOWS), POOL_GROUP, POOL_GROUP), lambda i: (0, 0, 0)),
                  pl.BlockSpec((1, POOL_WIDTH), lambda i: (0, 0))],
        out_specs=pl.BlockSpec((bm, POOL_WIDTH), lambda i: (i, 0)),
        out_shape=jax.ShapeDtypeStruct((M, POOL_WIDTH), BF16),
        compiler_params=_cparams("parallel"),
    )(proj, proj, pool_w, pool_scale.reshape(1, POOL_WIDTH))


def _mix_xattn_kernel(h_ref, a_ref, b_ref, wm_ref, g_ref, wq_ref, k_ref, v_ref, wo_ref, o_ref):
    ka = a_ref.shape[1]
    h = h_ref[...] + _dot(a_ref[...], wm_ref[:ka, :]) + _dot(b_ref[...], wm_ref[ka:, :])
    D = h.shape[1]
    hd = D // XATTN_HEADS
    xn = _rmsnorm(h, g_ref[...]).astype(BF16)
    q = (_dot(xn, wq_ref[...]) * hd ** -0.5).astype(BF16)
    outs = []
    for hh in range(XATTN_HEADS):
        cols = slice(hh * hd, (hh + 1) * hd)
        s = _dot_nt(q[:, cols], k_ref[:, cols])
        m = jnp.max(s, axis=1, keepdims=True)
        p = jnp.exp(s - m)
        l = jnp.sum(p, axis=1, keepdims=True)
        outs.append((_dot(p.astype(BF16), v_ref[:, cols]) / l).astype(BF16))
    o = jnp.concatenate(outs, axis=1)
    o_ref[...] = h + _dot(o, wo_ref[...])


def mix_xattn_residual(h, a, b, w_mix, g, wq, kv, wo, layer, S, n_mem, bm):
    M, D = h.shape
    ka, kb = a.shape[1], b.shape[1]
    bps = S // bm
    return pl.pallas_call(
        _mix_xattn_kernel,
        grid=(M // bm,),
        in_specs=[pl.BlockSpec((bm, D), lambda i: (i, 0)),
                  pl.BlockSpec((bm, ka), lambda i: (i, 0)),
                  pl.BlockSpec((bm, kb), lambda i: (i, 0)),
                  _resident(w_mix),
                  pl.BlockSpec((1, D), lambda i: (0, 0)),
                  _resident(wq, layer),
                  pl.BlockSpec((n_mem, D), lambda i: (i // bps, 0)),
                  pl.BlockSpec((n_mem, D), lambda i: (i // bps, 1)),
                  _resident(wo, layer)],
        out_specs=pl.BlockSpec((bm, D), lambda i: (i, 0)),
        out_shape=jax.ShapeDtypeStruct((M, D), F32),
        compiler_params=_cparams("parallel"),
    )(h, a, b, w_mix, g.reshape(1, D), wq, kv, kv, wo)


def _ffn_kernel(h_ref, halo_ref, g_ref, wup_ref, cw_ref, wd_ref, fg_ref, o_ref,
                xn_ref, acc_ref, y_ref, *, blocks_per_seq, final_norm, sub):
    H = BF16_SUBLANES
    nc, cf = wd_ref.shape[0], wd_ref.shape[1]
    n_sub = acc_ref.shape[0] // sub

    first = (pl.program_id(0) % blocks_per_seq) == 0
    xn_ref[:H, :] = jnp.where(first, 0.0, _rmsnorm(halo_ref[...], g_ref[...])).astype(BF16)
    xn_ref[H:, :] = _rmsnorm(h_ref[...], g_ref[...]).astype(BF16)
    acc_ref[...] = jnp.zeros_like(acc_ref)

    chunk_cols = lambda c: pl.ds(pl.multiple_of(c * cf, cf), cf)

    def up(c, r):
        if r == 0:
            xs, dst = xn_ref[:sub + H, :], slice(0, sub + H)
        else:
            xs, dst = xn_ref[H + r * sub:H + (r + 1) * sub, :], slice(H, sub + H)
            y_ref[r % 2, :H, :] = y_ref[(r - 1) % 2, sub:sub + H, :]
        y_ref[r % 2, dst, :cf] = _dot(xs, wup_ref[:, chunk_cols(c)])
        y_ref[r % 2, dst, cf:] = _dot(xs, wup_ref[:, chunk_cols(nc + c)])

    def conv(r, part, cw):
        cols = slice(part * cf, (part + 1) * cf)
        out = y_ref[r % 2, H:, cols] * cw[FFN_CONV - 1:FFN_CONV, :]
        for k in range(1, FFN_CONV):
            out = out + y_ref[r % 2, H - k:H - k + sub, cols] * cw[FFN_CONV - 1 - k:FFN_CONV - k, :]
        return out

    def chunk(c, carry):
        cwg, cwu = cw_ref[:, chunk_cols(c)], cw_ref[:, chunk_cols(nc + c)]
        wd = wd_ref[c]
        for r in range(n_sub):
            if r + 1 < n_sub:
                up(c, r + 1)
            else:
                up(jnp.minimum(c + 1, nc - 1), 0)
            act = _silu(conv(r, 0, cwg)) * conv(r, 1, cwu)
            acc_ref[r * sub:(r + 1) * sub, :] += _dot(act.astype(BF16), wd)
        return carry

    up(0, 0)
    lax.fori_loop(0, nc, chunk, 0)
    y = h_ref[...] + acc_ref[...]
    if final_norm:
        y = _rmsnorm(y, fg_ref[...])
    o_ref[...] = y


def ffn_residual(h, g, w_up, conv_w, w_down, layer, final_g, S, bm, cf, final_norm):
    M, D = h.shape
    d_ff = w_down.shape[1]
    H = BF16_SUBLANES
    nc = d_ff // cf
    sub = min(256, bm)
    assert (bm // sub) % 2 == 0, "the two y_ref slots alternate per sub-block across chunks"
    wd3 = w_down.reshape(-1, nc, cf, D)
    return pl.pallas_call(
        functools.partial(_ffn_kernel, blocks_per_seq=S // bm, final_norm=final_norm, sub=sub),
        grid=(M // bm,),
        in_specs=[pl.BlockSpec((bm, D), lambda i: (i, 0)),
                  pl.BlockSpec((H, D), lambda i: (jnp.maximum(i * (bm // H) - 1, 0), 0)),
                  pl.BlockSpec((1, D), lambda i: (0, 0)),
                  _resident(w_up, layer), _resident(conv_w, layer), _resident(wd3, layer),
                  pl.BlockSpec((1, D), lambda i: (0, 0))],
        out_specs=pl.BlockSpec((bm, D), lambda i: (i, 0)),
        out_shape=jax.ShapeDtypeStruct((M, D), F32),
        scratch_shapes=[pltpu.VMEM((H + bm, D), BF16), pltpu.VMEM((bm, D), F32),
                        pltpu.VMEM((2, H + sub, 2 * cf), F32)],
        compiler_params=_cparams("parallel"),
    )(h, h, g.reshape(1, D), w_up, conv_w, wd3, final_g.reshape(1, D))


def _sgu_kernel(u_ref, v_ref, lg_ref, lb_ref, w_ref, bt_ref, o_ref):
    rows = u_ref.shape[0]
    T = SGU_CHUNK
    v = v_ref[...].astype(F32)
    mu = jnp.mean(v, axis=-1, keepdims=True)
    d = v - mu
    var = jnp.mean(d * d, axis=-1, keepdims=True)
    vn = (d * lax.rsqrt(var + EPS) * lg_ref[...] + lb_ref[...]).astype(BF16)
    causal = (lax.broadcasted_iota(jnp.int32, (T, T), 1) <= lax.broadcasted_iota(jnp.int32, (T, T), 0))
    for g in range(SGU_GROUPS):
        cols = slice(g * SGU_GROUP, (g + 1) * SGU_GROUP)
        wg = jnp.where(causal, w_ref[g], 0.0).astype(BF16)
        bias = bt_ref[:, g:g + 1]
        for c in range(rows // T):
            rs = slice(c * T, (c + 1) * T)
            s = _dot(wg, vn[rs, cols]) + bias
            o_ref[rs, cols] = (u_ref[rs, cols].astype(F32) * s).astype(o_ref.dtype)


def spatial_gating(proj, ln_g, ln_b, w_s, b_s, M, rows):
    return pl.pallas_call(
        _sgu_kernel,
        grid=(M // rows,),
        in_specs=[pl.BlockSpec((rows, SGU_WIDTH), lambda i: (i, 0)),
                  pl.BlockSpec((rows, SGU_WIDTH), lambda i: (i, 1)),
                  pl.BlockSpec((1, SGU_WIDTH), lambda i: (0, 0)),
                  pl.BlockSpec((1, SGU_WIDTH), lambda i: (0, 0)),
                  pl.BlockSpec((SGU_GROUPS, SGU_CHUNK, SGU_CHUNK), lambda i: (0, 0, 0)),
                  pl.BlockSpec((SGU_CHUNK, SGU_GROUPS), lambda i: (0, 0))],
        out_specs=pl.BlockSpec((rows, SGU_WIDTH), lambda i: (i, 0)),
        out_shape=jax.ShapeDtypeStruct((M, SGU_WIDTH), BF16),
        compiler_params=_cparams("parallel"),
    )(proj, proj, ln_g.reshape(1, -1), ln_b.reshape(1, -1), w_s, b_s.T)


def _gdn_intra_kernel(q_ref, k_ref, v_ref, tail_ref, tailt_ref, alog_ref,
                      dtb_ref, u_ref, w_ref, qd_ref, kd_ref, qk_ref, gl_ref):
    rows = q_ref.shape[0]
    C = DN_CHUNK
    HD = DN_HEAD_DIM
    x = jnp.concatenate([q_ref[...], k_ref[...], v_ref[...]], axis=1).astype(F32)

    ii = lax.broadcasted_iota(jnp.int32, (C, C), 0)
    jj = lax.broadcasted_iota(jnp.int32, (C, C), 1)
    lower = jj <= ii
    strict = jj < ii
    su = lax.broadcasted_iota(jnp.int32, (C, LANES), 0)
    ju = lax.broadcasted_iota(jnp.int32, (C, LANES), 1)
    upper_ext = jnp.where(((ju < C) & (su > ju)) | (ju == C), 1.0, 0.0)

    inst = [(c, hh) for hh in range(DN_HEADS) for c in range(rows // C)]
    qs, ks, vs, bs, stacks = [], [], [], [], []
    for hh in range(DN_HEADS):
        qh = x[:, hh * HD:(hh + 1) * HD]
        kh = x[:, DN_WIDTH + hh * HD:DN_WIDTH + (hh + 1) * HD]
        vh = x[:, 2 * DN_WIDTH + hh * HD:2 * DN_WIDTH + (hh + 1) * HD]
        qh = qh * lax.rsqrt(jnp.sum(qh * qh, axis=-1, keepdims=True) + EPS) * HD ** -0.5
        kh = kh * lax.rsqrt(jnp.sum(kh * kh, axis=-1, keepdims=True) + EPS)
        beta = 1.0 / (1.0 + jnp.exp(-tail_ref[:, hh:hh + 1]))
        a_raw = tailt_ref[DN_HEADS + hh:DN_HEADS + hh + 1, :]
        z = a_raw + dtb_ref[0:1, hh:hh + 1]
        softplus = jnp.maximum(z, 0.0) + jnp.log(1.0 + jnp.exp(-jnp.abs(z)))
        g_row = -jnp.exp(alog_ref[0:1, hh:hh + 1]) * softplus
        for c in range(rows // C):
            rs = slice(c * C, (c + 1) * C)
            qs.append(qh[rs]); ks.append(kh[rs]); vs.append(vh[rs]); bs.append(beta[rs])
            gr = jnp.broadcast_to(g_row[:, rs], (C, C))
            stacks += [jnp.where(lower, gr, 0.0), jnp.where(lower, 0.0, gr)]

    stacked = jnp.concatenate(stacks, axis=0)
    s_hi = stacked.astype(BF16)
    r1 = stacked - s_hi.astype(F32)
    s_mid = r1.astype(BF16)
    s_lo = (r1 - s_mid.astype(F32)).astype(BF16)
    ue = upper_ext.astype(BF16)
    dall = _dot(s_hi, ue) + _dot(s_mid, ue) + _dot(s_lo, ue)

    decays, gcs, gc_revs, k16s, kbs = [], [], [], [], []
    for n, (c, hh) in enumerate(inst):
        dext = dall[n * 2 * C:(n + 1) * 2 * C]
        decays.append(jnp.exp(jnp.where(lower, dext[:C, :C], -jnp.inf)))
        gcs.append(dext[:C, C:C + 1])
        gc_revs.append(dext[C:, C:C + 1])
        kbs.append(ks[n] * bs[n])
        k16s.append(ks[n].astype(BF16))
    kk = [_dot_nt(kbs[n].astype(BF16), k16s[n]) for n in range(len(inst))]
    qk = [_dot_nt(qs[n].astype(BF16), k16s[n]) for n in range(len(inst))]
    pws = [jnp.where(strict, kk[n] * decays[n], 0.0).astype(BF16) for n in range(len(inst))]
    egc = [jnp.exp(g) for g in gcs]
    rhs = [jnp.concatenate([vs[n] * bs[n], kbs[n] * egc[n]], axis=1) for n in range(len(inst))]
    sols = [rhs[n] - _dot(pws[n], rhs[n].astype(BF16)) for n in range(len(inst))]
    for _ in range(int(math.log2(C)) - 1):
        pws = [_dot(p, p).astype(BF16) for p in pws]
        sols = [s + _dot(p, s.astype(BF16)) for p, s in zip(pws, sols)]
    for n, (c, hh) in enumerate(inst):
        rs = slice(c * C, (c + 1) * C)
        hcols = slice(hh * HD, (hh + 1) * HD)
        u_ref[rs, hcols] = sols[n][:, :HD].astype(u_ref.dtype)
        w_ref[rs, hcols] = sols[n][:, HD:].astype(w_ref.dtype)
        qkd = jnp.where(lower, qk[n] * decays[n], 0.0)
        qk_ref[rs, hcols] = jnp.concatenate([qkd, jnp.zeros_like(qkd)], axis=1).astype(qk_ref.dtype)
        qd_ref[rs, hcols] = (qs[n] * egc[n]).astype(qd_ref.dtype)
        kd_ref[rs, hcols] = (ks[n] * jnp.exp(gc_revs[n])).astype(kd_ref.dtype)
        gl_ref[c * 8:(c + 1) * 8, hcols] = jnp.broadcast_to(egc[n][C - 1:C, :], (8, HD))


def gdn_intra(proj, tail, tail_t, a_log, dt_bias, M, rows):
    c0 = 2 * SGU_WIDTH // DN_WIDTH
    pad = lambda p: jnp.pad(p.reshape(1, -1), ((0, 0), (0, LANES - p.shape[0])))
    seq = lambda dt: jax.ShapeDtypeStruct((M, DN_WIDTH), dt)
    row_spec = pl.BlockSpec((rows, DN_WIDTH), lambda i: (i, 0))
    cur_spec = lambda part: pl.BlockSpec((rows, DN_WIDTH), lambda i: (i, c0 + part))
    return pl.pallas_call(
        _gdn_intra_kernel,
        grid=(M // rows,),
        in_specs=[cur_spec(0), cur_spec(1), cur_spec(2),
                  pl.BlockSpec((rows, LANES), lambda i: (i, 0)),
                  pl.BlockSpec((2 * DN_HEADS, rows), lambda i: (0, i)),
                  pl.BlockSpec((1, LANES), lambda i: (0, 0)),
                  pl.BlockSpec((1, LANES), lambda i: (0, 0))],
        out_specs=[row_spec, row_spec, row_spec, row_spec, row_spec,
                   pl.BlockSpec((rows // DN_CHUNK * 8, DN_WIDTH), lambda i: (i, 0))],
        out_shape=[seq(BF16), seq(BF16), seq(BF16), seq(BF16), seq(BF16),
                   jax.ShapeDtypeStruct((M // DN_CHUNK * 8, DN_WIDTH), F32)],
        compiler_params=_cparams("parallel"),
    )(proj, proj, proj, tail, tail_t, pad(a_log), pad(dt_bias))


def _gdn_scan_kernel(u_ref, w_ref, qd_ref, kd_ref, qk_ref, gl_ref, gate_ref, ng_ref, o_ref, state_ref, *,
                     chunks):
    C = DN_CHUNK
    HD = DN_HEAD_DIM
    B = u_ref.shape[0]

    @pl.when(pl.program_id(0) == 0)
    def _():
        state_ref[...] = jnp.zeros_like(state_ref)

    ng = ng_ref[...]
    inst = [(b, hh) for b in range(B) for hh in range(DN_HEADS)]
    col = lambda hh: slice(hh * HD, (hh + 1) * HD)
    states = [state_ref[b, hh] for b, hh in inst]
    for c in range(chunks):
        rs = slice(c * C, (c + 1) * C)
        kdt = [kd_ref[b, rs, col(hh)].astype(F32).T.astype(BF16) for b, hh in inst]
        st16 = [s.astype(BF16) for s in states]
        ws = [_dot(w_ref[b, rs, col(hh)], st16[n]) for n, (b, hh) in enumerate(inst)]
        qs = [_dot(qd_ref[b, rs, col(hh)], st16[n]) for n, (b, hh) in enumerate(inst)]
        vn16 = [(u_ref[b, rs, col(hh)].astype(F32) - ws[n]).astype(BF16) for n, (b, hh) in enumerate(inst)]
        states = [states[n] * gl_ref[b, c * 8:c * 8 + 1, col(hh)] + _dot(kdt[n], vn16[n])
                  for n, (b, hh) in enumerate(inst)]
        for n, (b, hh) in enumerate(inst):
            o = qs[n] + _dot(qk_ref[b, rs, col(hh)][:, :C], vn16[n])
            o = o * lax.rsqrt(jnp.mean(o * o, axis=-1, keepdims=True) + EPS) * ng
            o_ref[b, rs, col(hh)] = (o * _silu(gate_ref[b, rs, col(hh)].astype(F32))).astype(o_ref.dtype)
    for n, (b, hh) in enumerate(inst):
        state_ref[b, hh] = states[n]


def gdn_scan(u, w, qd, kd, qk, gl, proj3, norm_g, B, S, chunks):
    rows = chunks * DN_CHUNK
    r3 = lambda a: a.reshape(B, S, DN_WIDTH)
    gcol = (2 * SGU_WIDTH + 3 * DN_WIDTH) // DN_WIDTH
    seq_spec = pl.BlockSpec((B, rows, DN_WIDTH), lambda n: (0, n, 0))
    return pl.pallas_call(
        functools.partial(_gdn_scan_kernel, chunks=chunks),
        grid=(S // rows,),
        in_specs=[seq_spec, seq_spec, seq_spec, seq_spec, seq_spec,
                  pl.BlockSpec((B, chunks * 8, DN_WIDTH), lambda n: (0, n, 0)),
                  pl.BlockSpec((B, rows, DN_WIDTH), lambda n: (0, n, gcol)),
                  pl.BlockSpec((1, DN_HEAD_DIM), lambda n: (0, 0))],
        out_specs=seq_spec,
        out_shape=jax.ShapeDtypeStruct((B, S, DN_WIDTH), BF16),
        scratch_shapes=[pltpu.VMEM((B, DN_HEADS, DN_HEAD_DIM, DN_HEAD_DIM), F32)],
        compiler_params=_cparams("arbitrary"),
    )(r3(u), r3(w), r3(qd), r3(kd), r3(qk), gl.reshape(B, S // DN_CHUNK * 8, DN_WIDTH), proj3,
      norm_g.reshape(1, DN_HEAD_DIM))


def _forward(x, mem, mem_norm, norm_mix, norm_xattn, norm_ffn, ev_w_in, pool_w, pool_scale, ev_w_out,
             od_w_in, sgu_ln_g, sgu_ln_b, sgu_w, sgu_b, dn_conv, dn_a_log, dn_dt_bias, dn_norm_g,
             od_w_out, xattn_wq, xattn_wkv, xattn_wo, ffn_w_up, ffn_conv, ffn_w_down, final_norm):
    B, S, D = x.shape
    n_mem = mem.shape[1]
    M = B * S
    depth = norm_mix.shape[0]
    bf = lambda a: a.astype(BF16)
    bm = min(1024, S)
    bm_big = min(1024, S)

    h = x.reshape(M, D)
    mem2 = mem.reshape(B * n_mem, D)
    wkv_all, wq_all, wo_all = bf(xattn_wkv), bf(xattn_wq), bf(xattn_wo)
    w_up_all, w_down_all = bf(ffn_w_up), bf(ffn_w_down)
    for layer in range(depth):
        i = layer // 2
        if layer % 2 == 0:
            proj, = norm_matmul(h, norm_mix[layer], [bf(ev_w_in[i])], [BF16], bm_big)
            a_out = moba_attention(proj.reshape(B, S, -1), B, S).reshape(M, A_WIDTH)
            b_out = multiscale_pool(proj, bf(pool_w[i]), pool_scale[i], M, S, bm)
            mix_a, mix_b, w_mix = a_out, b_out, ev_w_out[i]
        else:
            main_w = 2 * SGU_WIDTH + 4 * DN_WIDTH
            w_in = od_w_in[i]
            w_tail = jnp.pad(w_in[:, main_w:], ((0, 0), (0, LANES - 2 * DN_HEADS)))
            w_tail_t = jnp.pad(w_in[:, main_w:].T, ((0, BF16_SUBLANES - 2 * DN_HEADS), (0, 0)))
            proj, tail, tail_t = od_projection(h, norm_mix[layer], bf(w_in[:, :main_w]), bf(w_tail),
                                               bf(w_tail_t), dn_conv[i], S, bm_big)
            c_out = spatial_gating(proj, sgu_ln_g[i], sgu_ln_b[i], sgu_w[i], sgu_b[i], M, min(512, S))
            u, w, qd, kd, qk, gl = gdn_intra(proj, tail, tail_t, dn_a_log[i], dn_dt_bias[i], M, min(512, S))
            d_out = gdn_scan(u, w, qd, kd, qk, gl, proj.reshape(B, S, -1), dn_norm_g[i], B, S, 4)
            mix_a, mix_b, w_mix = c_out, d_out.reshape(M, DN_WIDTH), od_w_out[i]
        kv, = norm_matmul(mem2, mem_norm, [wkv_all], [BF16], B * n_mem, layer=layer)
        h = mix_xattn_residual(h, mix_a, mix_b, bf(w_mix), norm_xattn[layer], wq_all, kv, wo_all, layer,
                               S, n_mem, bm)
        h = ffn_residual(h, norm_ffn[layer], w_up_all, ffn_conv, w_down_all, layer, final_norm, S, bm_big,
                         256, final_norm=(layer == depth - 1))
    return h.reshape(B, S, D)


def kernel(x, mem, mem_norm, norm_mix, norm_xattn, norm_ffn, ev_w_in, pool_w, pool_scale, ev_w_out, od_w_in, sgu_ln_g, sgu_ln_b, sgu_w, sgu_b, dn_conv, dn_a_log, dn_dt_bias, dn_norm_g, od_w_out, xattn_wq, xattn_wkv, xattn_wo, ffn_w_up, ffn_conv, ffn_w_down, final_norm):
    return _forward(x, mem, mem_norm, norm_mix, norm_xattn, norm_ffn, ev_w_in, pool_w, pool_scale, ev_w_out,
                    od_w_in, sgu_ln_g, sgu_ln_b, sgu_w, sgu_b, dn_conv, dn_a_log, dn_dt_bias, dn_norm_g,
                    od_w_out, xattn_wq, xattn_wkv, xattn_wo, ffn_w_up, ffn_conv, ffn_w_down, final_norm)
```

```python
import functools
import math

import jax
import jax.numpy as jnp
from jax import lax
from jax.experimental import pallas as pl
from jax.experimental.pallas import tpu as pltpu

F32 = jnp.float32
BF16 = jnp.bfloat16
EPS = 1e-6
NEG_BIG = -1e30

VMEM_LIMIT_BYTES = 48 * 1024 * 1024
BF16_SUBLANES = 16
LANES = 128

MOBA_HEADS, MOBA_HEAD_DIM, MOBA_BLOCK, MOBA_TOPK = 8, 64, 256, 3
A_WIDTH = MOBA_HEADS * MOBA_HEAD_DIM
POOL_WINDOWS = (2, 4, 8, 16)
POOL_GROUP = 128
POOL_WIDTH = POOL_GROUP * len(POOL_WINDOWS)
SGU_GROUPS, SGU_GROUP, SGU_CHUNK = 4, 128, 128
SGU_WIDTH = SGU_GROUPS * SGU_GROUP
DN_HEADS, DN_HEAD_DIM, DN_CONV, DN_CHUNK = 4, 128, 4, 64
DN_WIDTH = DN_HEADS * DN_HEAD_DIM
XATTN_HEADS = 4
FFN_CONV = 3


def _cparams(*sem):
    return pltpu.CompilerParams(dimension_semantics=sem, vmem_limit_bytes=VMEM_LIMIT_BYTES)


def _rmsnorm(x, g):
    return x * lax.rsqrt(jnp.mean(x * x, axis=-1, keepdims=True) + EPS) * g


def _silu(x):
    return x * (0.5 * jnp.tanh(0.5 * x) + 0.5)


def _dot(a, b):
    return jnp.dot(a, b, preferred_element_type=F32)


def _resident(arr, layer=None):
    if layer is None:
        return pl.BlockSpec(arr.shape, lambda i: (0,) * arr.ndim, pipeline_mode=pl.Buffered(1))
    return pl.BlockSpec((None,) + arr.shape[1:], lambda i: (layer,) + (0,) * (arr.ndim - 1),
                        pipeline_mode=pl.Buffered(1))


def _dot_nt(a, b, precision=None):
    return lax.dot_general(a, b, (((1,), (1,)), ((), ())), preferred_element_type=F32,
                           precision=precision)


def _norm_mm_kernel(x_ref, g_ref, *refs, bn):
    n = len(refs) // 2
    xn = _rmsnorm(x_ref[...], g_ref[...]).astype(BF16)
    for w_ref, o_ref in zip(refs[:n], refs[n:]):
        N = w_ref.shape[1]
        for c0 in range(0, N, bn):
            c1 = min(c0 + bn, N)
            o_ref[:, c0:c1] = _dot(xn, w_ref[:, c0:c1]).astype(o_ref.dtype)


def norm_matmul(x, g, ws, out_dtypes, bm, bn=512, layer=None):
    M, D = x.shape
    return pl.pallas_call(
        functools.partial(_norm_mm_kernel, bn=bn),
        grid=(M // bm,),
        in_specs=[pl.BlockSpec((bm, D), lambda i: (i, 0)),
                  pl.BlockSpec((1, D), lambda i: (0, 0))]
                 + [_resident(w, layer) for w in ws],
        out_specs=[pl.BlockSpec((bm, w.shape[-1]), lambda i: (i, 0)) for w in ws],
        out_shape=[jax.ShapeDtypeStruct((M, w.shape[-1]), dt) for w, dt in zip(ws, out_dtypes)],
        compiler_params=_cparams("parallel"),
    )(x, g.reshape(1, D), *ws)


def _gelu_tanh(x):
    return 0.5 * x * (1.0 + jnp.tanh(math.sqrt(2.0 / math.pi) * (x + 0.044715 * (x * x * x))))


def _od_proj_kernel(x_ref, halo_ref, g_ref, w_ref, wt_ref, cw_ref, o_ref, t_ref, tt_ref, xe_ref, *,
                    blocks_per_seq, bn):
    bm = x_ref.shape[0]
    H = BF16_SUBLANES
    z_w, qkv_w = 2 * SGU_WIDTH, 3 * DN_WIDTH
    first = (pl.program_id(0) % blocks_per_seq) == 0
    xe_ref[:H, :] = jnp.where(first, 0.0, _rmsnorm(halo_ref[...], g_ref[...])).astype(BF16)
    xe_ref[H:, :] = _rmsnorm(x_ref[...], g_ref[...]).astype(BF16)
    xn = xe_ref[H:, :]
    tail = _dot(xn, wt_ref[...])
    t_ref[...] = tail
    tt_ref[...] = tail.T[:H, :]
    for c0 in range(0, o_ref.shape[1], bn):
        cols = slice(c0, c0 + bn)
        if c0 < z_w:
            o_ref[:, cols] = _gelu_tanh(_dot(xn, w_ref[:, cols])).astype(o_ref.dtype)
        elif c0 < z_w + qkv_w:
            y = _dot(xe_ref[...], w_ref[:, cols])
            cw = cw_ref[:, c0 - z_w:c0 - z_w + bn]
            out = y[H:, :] * cw[DN_CONV - 1:DN_CONV, :]
            for k in range(1, DN_CONV):
                out = out + pltpu.roll(y, k, axis=0)[H:, :] * cw[DN_CONV - 1 - k:DN_CONV - k, :]
            o_ref[:, cols] = _silu(out).astype(o_ref.dtype)
        else:
            o_ref[:, cols] = _dot(xn, w_ref[:, cols]).astype(o_ref.dtype)


def od_projection(x, g, w, w_tail, conv_w, S, bm, bn=512):
    M, D = x.shape
    N = 2 * SGU_WIDTH + 4 * DN_WIDTH
    H = BF16_SUBLANES
    return pl.pallas_call(
        functools.partial(_od_proj_kernel, blocks_per_seq=S // bm, bn=bn),
        grid=(M // bm,),
        in_specs=[pl.BlockSpec((bm, D), lambda i: (i, 0)),
                  pl.BlockSpec((H, D), lambda i: (jnp.maximum(i * (bm // H) - 1, 0), 0)),
                  pl.BlockSpec((1, D), lambda i: (0, 0)),
                  _resident(w), _resident(w_tail), _resident(conv_w)],
        out_specs=[pl.BlockSpec((bm, N), lambda i: (i, 0)), pl.BlockSpec((bm, LANES), lambda i: (i, 0)),
                   pl.BlockSpec((H, bm), lambda i: (0, i))],
        out_shape=[jax.ShapeDtypeStruct((M, N), BF16), jax.ShapeDtypeStruct((M, LANES), F32),
                   jax.ShapeDtypeStruct((H, M), F32)],
        scratch_shapes=[pltpu.VMEM((H + bm, D), BF16)],
        compiler_params=_cparams("parallel"),
    )(x, x, g.reshape(1, D), w, w_tail, conv_w)


def _moba_kernel(q_ref, k_ref, v_ref, o_ref, kme_ref, vt_ref, sel_ref, m_ref, acc_ref, s_ref, *,
                 nb, nbp, unroll, pairs):
    BS = MOBA_BLOCK
    HD = MOBA_HEAD_DIM
    n_heads = 2 * pairs
    i = pl.program_id(2)
    lane = lax.broadcasted_iota(jnp.int32, (1, LANES), 1)
    head_lanes = (lane < HD, lane >= HD)
    pair_lanes = lambda u: slice((u // 2) * LANES, (u // 2 + 1) * LANES)

    @pl.when(i == 0)
    def _():
        kme_ref[...] = jnp.zeros_like(kme_ref)
        for n in range(nb):
            rows = slice(n * BS, (n + 1) * BS)
            mean = jnp.sum(k_ref[0, rows, :].astype(F32), axis=0, keepdims=True) / BS
            for u in range(n_heads):
                kme_ref[u // 2, (u % 2) * nbp + n:(u % 2) * nbp + n + 1, :] = jnp.where(
                    head_lanes[u % 2], mean[:, pair_lanes(u)], 0.0)
            vt_ref[:, rows] = v_ref[0, rows, :].astype(F32).T.astype(BF16)

    scale = HD ** -0.5 * math.log2(math.e)
    q_t = [q_ref[0, :, p * LANES:(p + 1) * LANES].astype(F32).T for p in range(pairs)]
    pair_row = lax.broadcasted_iota(jnp.int32, (LANES, 1), 0)
    head_rows = (pair_row < HD, pair_row >= HD)
    q_aug = [jnp.where(head_rows[u % 2], q_t[u // 2] * scale, 0.0).astype(BF16) for u in range(n_heads)]

    gates = []
    for p in range(pairs):
        km = kme_ref[p]
        k_hi = km.astype(BF16)
        r1 = km - k_hi.astype(F32)
        k_mid = r1.astype(BF16)
        k_lo = (r1 - k_mid.astype(F32)).astype(BF16)
        q16 = q_t[p].astype(BF16)
        gates.append(_dot(k_hi, q16) + _dot(k_mid, q16) + _dot(k_lo, q16))
    blk = lax.broadcasted_iota(jnp.int32, (nbp, 1), 0).astype(F32)
    valid = blk < i.astype(F32)
    for u in range(n_heads):
        g = jnp.where(valid, gates[u // 2][(u % 2) * nbp:(u % 2 + 1) * nbp], -jnp.inf)
        sel = jnp.zeros(g.shape, jnp.bool_)
        for _ in range(MOBA_TOPK):
            mx = jnp.max(g, axis=0, keepdims=True)
            idx = jnp.min(jnp.where(g == mx, blk, float(1 << 20)), axis=0, keepdims=True)
            pick = blk == idx
            sel = sel | pick
            g = jnp.where(pick, -jnp.inf, g)
        sel_ref[u, :nbp, :] = jnp.where(sel & valid, 1.0, 0.0)
        sel_ref[u, nbp:, :] = jnp.zeros((8, BS), F32)
        m_ref[u] = jnp.full((1, BS), NEG_BIG, F32)
        acc_ref[u] = jnp.zeros(acc_ref.shape[1:], F32)

    krow = lax.broadcasted_iota(jnp.int32, (BS, BS), 0)
    qcol = lax.broadcasted_iota(jnp.int32, (BS, BS), 1)
    PVR = HD + BF16_SUBLANES
    pv_rows = (slice(0, PVR), slice(LANES - PVR, LANES))
    pv_row = lax.broadcasted_iota(jnp.int32, (PVR, 1), 0)
    is_dim = (pv_row < HD, pv_row >= PVR - HD)

    def block_start(j):
        return pl.multiple_of(jnp.minimum(j, i) * BS, BS)

    def produce(g, slot):
        for t in range(unroll):
            rows = pl.ds(block_start(g * unroll + t), BS)
            for u in range(n_heads):
                s_ref[slot, u, t * BS:(t + 1) * BS, :] = _dot(
                    k_ref[0, rows, pair_lanes(u)], q_aug[u]).astype(BF16)

    def softmax_update(sts, sels, starts):
        heads = range(n_heads)
        m_new, alpha = [], []
        for u in heads:
            cand = jnp.full((1, BS), NEG_BIG, F32)
            for st, sel in zip(sts[u], sels[u]):
                mx = jnp.max(st.reshape(BS // BF16_SUBLANES, BF16_SUBLANES, BS), axis=0)
                mx = jnp.max(mx.astype(F32), axis=0, keepdims=True)
                cand = jnp.maximum(cand, mx if sel is None else jnp.where(sel, mx, NEG_BIG))
            m_old = m_ref[u]
            m_new.append(jnp.maximum(m_old, cand))
            alpha.append(jnp.exp2(m_old - m_new[u]))
            m_ref[u] = m_new[u]
        ps = []
        for u in heads:
            pu = []
            for st, sel in zip(sts[u], sels[u]):
                sub = m_new[u] if sel is None else jnp.where(sel, m_new[u], -NEG_BIG)
                pu.append(jnp.exp2(st - sub.astype(BF16)))
            ps.append(pu[0] if len(pu) == 1 else jnp.concatenate(pu, axis=0))
        pv = []
        for u in heads:
            rows = slice((u // 2) * LANES + pv_rows[u % 2].start, (u // 2) * LANES + pv_rows[u % 2].stop)
            vts = [jnp.where(is_dim[u % 2], vt_ref[rows, pl.ds(st0, BS)], jnp.ones((), BF16))
                   for st0 in starts]
            pv.append(_dot(vts[0] if len(vts) == 1 else jnp.concatenate(vts, axis=1), ps[u]))
        for u in heads:
            acc_ref[u] = acc_ref[u] * alpha[u] + pv[u]

    def consume(g, slot):
        js = [g * unroll + t for t in range(unroll)]
        sts = [[s_ref[slot, u, t * BS:(t + 1) * BS, :] for t in range(unroll)] for u in range(n_heads)]
        sels = [[sel_ref[u, pl.ds(j, 1), :] > 0.5 for j in js] for u in range(n_heads)]
        softmax_update(sts, sels, [block_start(j) for j in js])

    slots = s_ref.shape[0]
    assert unroll == 1, "the group after the last past block must be exactly the tile's own block"

    def body(gg, c):
        for t in range(slots):
            produce(slots * gg + t + 1, (t + 1) % slots)
            consume(slots * gg + t, t)
        return c

    n_groups = (i + unroll - 1) // unroll
    produce(0, 0)
    lax.fori_loop(0, (n_groups + slots - 1) // slots, body, 0)
    softmax_update([[jnp.where(krow <= qcol, s_ref[0, u, :BS, :], -jnp.inf)] for u in range(n_heads)],
                   [[None]] * n_heads, [block_start(i)])

    outs = []
    for u in range(n_heads):
        a = acc_ref[u]
        outs.append(a[:HD] / a[HD:HD + 1, :] if u % 2 == 0 else a[PVR - HD:] / a[0:1, :])
    o_ref[0] = jnp.concatenate(outs, axis=0).T.astype(o_ref.dtype)


def moba_attention(proj, B, S):
    nb = S // MOBA_BLOCK
    nbp = -(-nb // 8) * 8
    pairs = 2
    width = pairs * LANES
    groups = A_WIDTH // width
    unroll = 1
    slots = 3
    n_heads = 2 * pairs
    return pl.pallas_call(
        functools.partial(_moba_kernel, nb=nb, nbp=nbp, unroll=unroll, pairs=pairs),
        grid=(B, groups, nb),
        in_specs=[pl.BlockSpec((1, MOBA_BLOCK, width), lambda b, p, i: (b, i, p)),
                  pl.BlockSpec((1, S, width), lambda b, p, i: (b, 0, groups + p)),
                  pl.BlockSpec((1, S, width), lambda b, p, i: (b, 0, 2 * groups + p))],
        out_specs=pl.BlockSpec((1, MOBA_BLOCK, width), lambda b, p, i: (b, i, p)),
        out_shape=jax.ShapeDtypeStruct((B, S, A_WIDTH), BF16),
        scratch_shapes=[pltpu.VMEM((pairs, 2 * nbp, LANES), F32),
                        pltpu.VMEM((width, S), BF16),
                        pltpu.VMEM((n_heads, nbp + 8, MOBA_BLOCK), F32),
                        pltpu.VMEM((n_heads, 1, MOBA_BLOCK), F32),
                        pltpu.VMEM((n_heads, MOBA_HEAD_DIM + BF16_SUBLANES, MOBA_BLOCK), F32),
                        pltpu.VMEM((slots, n_heads, unroll * MOBA_BLOCK, MOBA_BLOCK), BF16)],
        compiler_params=_cparams("parallel", "parallel", "arbitrary"),
    )(proj, proj, proj)


def _pool_kernel(p_ref, halo_ref, w_ref, sc_ref, o_ref, *, blocks_per_seq):
    bm = p_ref.shape[0]
    H = BF16_SUBLANES
    i = pl.program_id(0)
    first = (i % blocks_per_seq) == 0
    t1 = (lax.broadcasted_iota(jnp.int32, (bm, 1), 0) + (i % blocks_per_seq) * bm + 1).astype(F32)
    for g, w in enumerate(POOL_WINDOWS):
        cols = slice(g * POOL_GROUP, (g + 1) * POOL_GROUP)
        cur = p_ref[:, cols].astype(F32)
        halo = jnp.where(first, 0.0, halo_ref[:, cols].astype(F32))
        ext = jnp.concatenate([halo, cur], axis=0)
        acc = ext
        sh = 1
        while sh < w:
            acc = acc + pltpu.roll(acc, sh, axis=0)
            sh *= 2
        win = acc[H:, :]
        pooled = win / jnp.minimum(t1, float(w)) - cur
        y = _dot(pooled.astype(BF16), w_ref[g])
        o_ref[:, cols] = (y * sc_ref[:, cols]).astype(o_ref.dtype)


def multiscale_pool(proj, pool_w, pool_scale, M, S, bm):
    H = BF16_SUBLANES
    pcol = 3 * A_WIDTH // POOL_WIDTH
    return pl.pallas_call(
        functools.partial(_pool_kernel, blocks_per_seq=S // bm),
        grid=(M // bm,),
        in_specs=[pl.BlockSpec((bm, POOL_WIDTH), lambda i: (i, pcol)),
                  pl.BlockSpec((H, POOL_WIDTH), lambda i: (jnp.maximum(i * (bm // H) - 1, 0), pcol)),
                  pl.BlockSpec((len(POOL_WINDOWS), POOL_GROUP, POOL_GROUP), lambda i: (0, 0, 0)),
                  pl.BlockSpec((1, POOL_WIDTH), lambda i: (0, 0))],
        out_specs=pl.BlockSpec((bm, POOL_WIDTH), lambda i: (i, 0)),
        out_shape=jax.ShapeDtypeStruct((M, POOL_WIDTH), BF16),
        compiler_params=_cparams("parallel"),
    )(proj, proj, pool_w, pool_scale.reshape(1, POOL_WIDTH))


def _mix_xattn_kernel(h_ref, a_ref, b_ref, wm_ref, g_ref, wq_ref, k_ref, v_ref, wo_ref, o_ref):
    ka = a_ref.shape[1]
    h = h_ref[...] + _dot(a_ref[...], wm_ref[:ka, :]) + _dot(b_ref[...], wm_ref[ka:, :])
    D = h.shape[1]
    hd = D // XATTN_HEADS
    xn = _rmsnorm(h, g_ref[...]).astype(BF16)
    q = (_dot(xn, wq_ref[...]) * hd ** -0.5).astype(BF16)
    outs = []
    for hh in range(XATTN_HEADS):
        cols = slice(hh * hd, (hh + 1) * hd)
        s = _dot_nt(q[:, cols], k_ref[:, cols])
        m = jnp.max(s, axis=1, keepdims=True)
        p = jnp.exp(s - m)
        l = jnp.sum(p, axis=1, keepdims=True)
        outs.append((_dot(p.astype(BF16), v_ref[:, cols]) / l).astype(BF16))
    o = jnp.concatenate(outs, axis=1)
    o_ref[...] = h + _dot(o, wo_ref[...])


def mix_xattn_residual(h, a, b, w_mix, g, wq, kv, wo, layer, S, n_mem, bm):
    M, D = h.shape
    ka, kb = a.shape[1], b.shape[1]
    bps = S // bm
    return pl.pallas_call(
        _mix_xattn_kernel,
        grid=(M // bm,),
        in_specs=[pl.BlockSpec((bm, D), lambda i: (i, 0)),
                  pl.BlockSpec((bm, ka), lambda i: (i, 0)),
                  pl.BlockSpec((bm, kb), lambda i: (i, 0)),
                  _resident(w_mix),
                  pl.BlockSpec((1, D), lambda i: (0, 0)),
                  _resident(wq, layer),
                  pl.BlockSpec((n_mem, D), lambda i: (i // bps, 0)),
                  pl.BlockSpec((n_mem, D), lambda i: (i // bps, 1)),
                  _resident(wo, layer)],
        out_specs=pl.BlockSpec((bm, D), lambda i: (i, 0)),
        out_shape=jax.ShapeDtypeStruct((M, D), F32),
        compiler_params=_cparams("parallel"),
    )(h, a, b, w_mix, g.reshape(1, D), wq, kv, kv, wo)


def _ffn_kernel(h_ref, halo_ref, g_ref, wup_ref, cw_ref, wd_ref, fg_ref, o_ref,
                xn_ref, acc_ref, y_ref, *, blocks_per_seq, final_norm, sub):
    H = BF16_SUBLANES
    nc, cf = wd_ref.shape[0], wd_ref.shape[1]
    n_sub = acc_ref.shape[0] // sub

    first = (pl.program_id(0) % blocks_per_seq) == 0
    xn_ref[:H, :] = jnp.where(first, 0.0, _rmsnorm(halo_ref[...], g_ref[...])).astype(BF16)
    xn_ref[H:, :] = _rmsnorm(h_ref[...], g_ref[...]).astype(BF16)
    acc_ref[...] = jnp.zeros_like(acc_ref)

    chunk_cols = lambda c: pl.ds(pl.multiple_of(c * cf, cf), cf)

    def up(c, r):
        if r == 0:
            xs, dst = xn_ref[:sub + H, :], slice(0, sub + H)
        else:
            xs, dst = xn_ref[H + r * sub:H + (r + 1) * sub, :], slice(H, sub + H)
            y_ref[r % 2, :H, :] = y_ref[(r - 1) % 2, sub:sub + H, :]
        y_ref[r % 2, dst, :cf] = _dot(xs, wup_ref[:, chunk_cols(c)])
        y_ref[r % 2, dst, cf:] = _dot(xs, wup_ref[:, chunk_cols(nc + c)])

    def conv(r, part, cw):
        cols = slice(part * cf, (part + 1) * cf)
        out = y_ref[r % 2, H:, cols] * cw[FFN_CONV - 1:FFN_CONV, :]
        for k in range(1, FFN_CONV):
            out = out + y_ref[r % 2, H - k:H - k + sub, cols] * cw[FFN_CONV - 1 - k:FFN_CONV - k, :]
        return out

    def chunk(c, carry):
        cwg, cwu = cw_ref[:, chunk_cols(c)], cw_ref[:, chunk_cols(nc + c)]
        wd = wd_ref[c]
        for r in range(n_sub):
            if r + 1 < n_sub:
                up(c, r + 1)
            else:
                up(jnp.minimum(c + 1, nc - 1), 0)
            act = _silu(conv(r, 0, cwg)) * conv(r, 1, cwu)
            acc_ref[r * sub:(r + 1) * sub, :] += _dot(act.astype(BF16), wd)
        return carry

    up(0, 0)
    lax.fori_loop(0, nc, chunk, 0)
    y = h_ref[...] + acc_ref[...]
    if final_norm:
        y = _rmsnorm(y, fg_ref[...])
    o_ref[...] = y


def ffn_residual(h, g, w_up, conv_w, w_down, layer, final_g, S, bm, cf, final_norm):
    M, D = h.shape
    d_ff = w_down.shape[1]
    H = BF16_SUBLANES
    nc = d_ff // cf
    sub = min(256, bm)
    assert (bm // sub) % 2 == 0, "the two y_ref slots alternate per sub-block across chunks"
    wd3 = w_down.reshape(-1, nc, cf, D)
    return pl.pallas_call(
        functools.partial(_ffn_kernel, blocks_per_seq=S // bm, final_norm=final_norm, sub=sub),
        grid=(M // bm,),
        in_specs=[pl.BlockSpec((bm, D), lambda i: (i, 0)),
                  pl.BlockSpec((H, D), lambda i: (jnp.maximum(i * (bm // H) - 1, 0), 0)),
                  pl.BlockSpec((1, D), lambda i: (0, 0)),
                  _resident(w_up, layer), _resident(conv_w, layer), _resident(wd3, layer),
                  pl.BlockSpec((1, D), lambda i: (0, 0))],
        out_specs=pl.BlockSpec((bm, D), lambda i: (i, 0)),
        out_shape=jax.ShapeDtypeStruct((M, D), F32),
        scratch_shapes=[pltpu.VMEM((H + bm, D), BF16), pltpu.VMEM((bm, D), F32),
                        pltpu.VMEM((2, H + sub, 2 * cf), F32)],
        compiler_params=_cparams("parallel"),
    )(h, h, g.reshape(1, D), w_up, conv_w, wd3, final_g.reshape(1, D))


def _sgu_kernel(u_ref, v_ref, lg_ref, lb_ref, w_ref, bt_ref, o_ref):
    rows = u_ref.shape[0]
    T = SGU_CHUNK
    v = v_ref[...].astype(F32)
    mu = jnp.mean(v, axis=-1, keepdims=True)
    d = v - mu
    var = jnp.mean(d * d, axis=-1, keepdims=True)
    vn = (d * lax.rsqrt(var + EPS) * lg_ref[...] + lb_ref[...]).astype(BF16)
    causal = (lax.broadcasted_iota(jnp.int32, (T, T), 1) <= lax.broadcasted_iota(jnp.int32, (T, T), 0))
    for g in range(SGU_GROUPS):
        cols = slice(g * SGU_GROUP, (g + 1) * SGU_GROUP)
        wg = jnp.where(causal, w_ref[g], 0.0).astype(BF16)
        bias = bt_ref[:, g:g + 1]
        for c in range(rows // T):
            rs = slice(c * T, (c + 1) * T)
            s = _dot(wg, vn[rs, cols]) + bias
            o_ref[rs, cols] = (u_ref[rs, cols].astype(F32) * s).astype(o_ref.dtype)


def spatial_gating(proj, ln_g, ln_b, w_s, b_s, M, rows):
    return pl.pallas_call(
        _sgu_kernel,
        grid=(M // rows,),
        in_specs=[pl.BlockSpec((rows, SGU_WIDTH), lambda i: (i, 0)),
                  pl.BlockSpec((rows, SGU_WIDTH), lambda i: (i, 1)),
                  pl.BlockSpec((1, SGU_WIDTH), lambda i: (0, 0)),
                  pl.BlockSpec((1, SGU_WIDTH), lambda i: (0, 0)),
                  pl.BlockSpec((SGU_GROUPS, SGU_CHUNK, SGU_CHUNK), lambda i: (0, 0, 0)),
                  pl.BlockSpec((SGU_CHUNK, SGU_GROUPS), lambda i: (0, 0))],
        out_specs=pl.BlockSpec((rows, SGU_WIDTH), lambda i: (i, 0)),
        out_shape=jax.ShapeDtypeStruct((M, SGU_WIDTH), BF16),
        compiler_params=_cparams("parallel"),
    )(proj, proj, ln_g.reshape(1, -1), ln_b.reshape(1, -1), w_s, b_s.T)


def _gdn_intra_kernel(q_ref, k_ref, v_ref, tail_ref, tailt_ref, alog_ref,
                      dtb_ref, u_ref, w_ref, qd_ref, kd_ref, qk_ref, gl_ref):
    rows = q_ref.shape[0]
    C = DN_CHUNK
    HD = DN_HEAD_DIM
    x = jnp.concatenate([q_ref[...], k_ref[...], v_ref[...]], axis=1).astype(F32)

    ii = lax.broadcasted_iota(jnp.int32, (C, C), 0)
    jj = lax.broadcasted_iota(jnp.int32, (C, C), 1)
    lower = jj <= ii
    strict = jj < ii
    su = lax.broadcasted_iota(jnp.int32, (C, LANES), 0)
    ju = lax.broadcasted_iota(jnp.int32, (C, LANES), 1)
    upper_ext = jnp.where(((ju < C) & (su > ju)) | (ju == C), 1.0, 0.0)

    inst = [(c, hh) for hh in range(DN_HEADS) for c in range(rows // C)]
    qs, ks, vs, bs, stacks = [], [], [], [], []
    for hh in range(DN_HEADS):
        qh = x[:, hh * HD:(hh + 1) * HD]
        kh = x[:, DN_WIDTH + hh * HD:DN_WIDTH + (hh + 1) * HD]
        vh = x[:, 2 * DN_WIDTH + hh * HD:2 * DN_WIDTH + (hh + 1) * HD]
        qh = qh * lax.rsqrt(jnp.sum(qh * qh, axis=-1, keepdims=True) + EPS) * HD ** -0.5
        kh = kh * lax.rsqrt(jnp.sum(kh * kh, axis=-1, keepdims=True) + EPS)
        beta = 1.0 / (1.0 + jnp.exp(-tail_ref[:, hh:hh + 1]))
        a_raw = tailt_ref[DN_HEADS + hh:DN_HEADS + hh + 1, :]
        z = a_raw + dtb_ref[0:1, hh:hh + 1]
        softplus = jnp.maximum(z, 0.0) + jnp.log(1.0 + jnp.exp(-jnp.abs(z)))
        g_row = -jnp.exp(alog_ref[0:1, hh:hh + 1]) * softplus
        for c in range(rows // C):
            rs = slice(c * C, (c + 1) * C)
            qs.append(qh[rs]); ks.append(kh[rs]); vs.append(vh[rs]); bs.append(beta[rs])
            gr = jnp.broadcast_to(g_row[:, rs], (C, C))
            stacks += [jnp.where(lower, gr, 0.0), jnp.where(lower, 0.0, gr)]

    stacked = jnp.concatenate(stacks, axis=0)
    s_hi = stacked.astype(BF16)
    s_lo = (stacked - s_hi.astype(F32)).astype(BF16)
    ue = upper_ext.astype(BF16)
    dall = _dot(s_hi, ue) + _dot(s_lo, ue)

    decays, gcs, gc_revs, k16s, kbs = [], [], [], [], []
    for n, (c, hh) in enumerate(inst):
        dext = dall[n * 2 * C:(n + 1) * 2 * C]
        decays.append(jnp.exp(jnp.where(lower, dext[:C, :C], -jnp.inf)))
        gcs.append(dext[:C, C:C + 1])
        gc_revs.append(dext[C:, C:C + 1])
        kbs.append(ks[n] * bs[n])
        k16s.append(ks[n].astype(BF16))
    kq = [_dot_nt(jnp.concatenate([kbs[n], qs[n]], axis=0).astype(BF16), k16s[n]) for n in range(len(inst))]
    kk = [m[:C] for m in kq]
    qk = [m[C:] for m in kq]
    pws = [jnp.where(strict, kk[n] * decays[n], 0.0).astype(BF16) for n in range(len(inst))]
    egc = [jnp.exp(g) for g in gcs]
    rhs = [jnp.concatenate([vs[n] * bs[n], kbs[n] * egc[n]], axis=1) for n in range(len(inst))]
    sols = [rhs[n] - _dot(pws[n], rhs[n].astype(BF16)) for n in range(len(inst))]
    for _ in range(int(math.log2(C)) - 1):
        pws = [_dot(p, p).astype(BF16) for p in pws]
        sols = [s + _dot(p, s.astype(BF16)) for p, s in zip(pws, sols)]
    for n, (c, hh) in enumerate(inst):
        rs = slice(c * C, (c + 1) * C)
        hcols = slice(hh * HD, (hh + 1) * HD)
        u_ref[rs, hcols] = sols[n][:, :HD].astype(u_ref.dtype)
        w_ref[rs, hcols] = sols[n][:, HD:].astype(w_ref.dtype)
        qkd = jnp.where(lower, qk[n] * decays[n], 0.0)
        qk_ref[rs, hcols] = jnp.concatenate([qkd, jnp.zeros_like(qkd)], axis=1).astype(qk_ref.dtype)
        qd_ref[rs, hcols] = (qs[n] * egc[n]).astype(qd_ref.dtype)
        kd_ref[rs, hcols] = (ks[n] * jnp.exp(gc_revs[n])).astype(kd_ref.dtype)
        gl_ref[c * 8:(c + 1) * 8, hcols] = jnp.broadcast_to(egc[n][C - 1:C, :], (8, HD))


def gdn_intra(proj, tail, tail_t, a_log, dt_bias, M, rows):
    c0 = 2 * SGU_WIDTH // DN_WIDTH
    pad = lambda p: jnp.pad(p.reshape(1, -1), ((0, 0), (0, LANES - p.shape[0])))
    seq = lambda dt: jax.ShapeDtypeStruct((M, DN_WIDTH), dt)
    row_spec = pl.BlockSpec((rows, DN_WIDTH), lambda i: (i, 0))
    cur_spec = lambda part: pl.BlockSpec((rows, DN_WIDTH), lambda i: (i, c0 + part))
    return pl.pallas_call(
        _gdn_intra_kernel,
        grid=(M // rows,),
        in_specs=[cur_spec(0), cur_spec(1), cur_spec(2),
                  pl.BlockSpec((rows, LANES), lambda i: (i, 0)),
                  pl.BlockSpec((2 * DN_HEADS, rows), lambda i: (0, i)),
                  pl.BlockSpec((1, LANES), lambda i: (0, 0)),
                  pl.BlockSpec((1, LANES), lambda i: (0, 0))],
        out_specs=[row_spec, row_spec, row_spec, row_spec, row_spec,
                   pl.BlockSpec((rows // DN_CHUNK * 8, DN_WIDTH), lambda i: (i, 0))],
        out_shape=[seq(BF16), seq(BF16), seq(BF16), seq(BF16), seq(BF16),
                   jax.ShapeDtypeStruct((M // DN_CHUNK * 8, DN_WIDTH), F32)],
        compiler_params=_cparams("parallel"),
    )(proj, proj, proj, tail, tail_t, pad(a_log), pad(dt_bias))


def _gdn_scan_kernel(u_ref, w_ref, qd_ref, kd_ref, qk_ref, gl_ref, gate_ref, ng_ref, o_ref, state_ref, *,
                     chunks):
    C = DN_CHUNK
    HD = DN_HEAD_DIM
    B = u_ref.shape[0]

    @pl.when(pl.program_id(0) == 0)
    def _():
        state_ref[...] = jnp.zeros_like(state_ref)

    ng = ng_ref[...]
    inst = [(b, hh) for b in range(B) for hh in range(DN_HEADS)]
    col = lambda hh: slice(hh * HD, (hh + 1) * HD)
    states = [state_ref[b, hh] for b, hh in inst]
    for c in range(chunks):
        rs = slice(c * C, (c + 1) * C)
        kdt = [kd_ref[b, rs, col(hh)].astype(F32).T.astype(BF16) for b, hh in inst]
        st16 = [s.astype(BF16) for s in states]
        ws = [_dot(w_ref[b, rs, col(hh)], st16[n]) for n, (b, hh) in enumerate(inst)]
        qs = [_dot(qd_ref[b, rs, col(hh)], st16[n]) for n, (b, hh) in enumerate(inst)]
        vn16 = [(u_ref[b, rs, col(hh)].astype(F32) - ws[n]).astype(BF16) for n, (b, hh) in enumerate(inst)]
        states = [states[n] * gl_ref[b, c * 8:c * 8 + 1, col(hh)] + _dot(kdt[n], vn16[n])
                  for n, (b, hh) in enumerate(inst)]
        for n, (b, hh) in enumerate(inst):
            o = qs[n] + _dot(qk_ref[b, rs, col(hh)][:, :C], vn16[n])
            o = o * lax.rsqrt(jnp.mean(o * o, axis=-1, keepdims=True) + EPS) * ng
            o_ref[b, rs, col(hh)] = (o * _silu(gate_ref[b, rs, col(hh)].astype(F32))).astype(o_ref.dtype)
    for n, (b, hh) in enumerate(inst):
        state_ref[b, hh] = states[n]


def gdn_scan(u, w, qd, kd, qk, gl, proj3, norm_g, B, S, chunks):
    rows = chunks * DN_CHUNK
    r3 = lambda a: a.reshape(B, S, DN_WIDTH)
    gcol = (2 * SGU_WIDTH + 3 * DN_WIDTH) // DN_WIDTH
    seq_spec = pl.BlockSpec((B, rows, DN_WIDTH), lambda n: (0, n, 0))
    return pl.pallas_call(
        functools.partial(_gdn_scan_kernel, chunks=chunks),
        grid=(S // rows,),
        in_specs=[seq_spec, seq_spec, seq_spec, seq_spec, seq_spec,
                  pl.BlockSpec((B, chunks * 8, DN_WIDTH), lambda n: (0, n, 0)),
                  pl.BlockSpec((B, rows, DN_WIDTH), lambda n: (0, n, gcol)),
                  pl.BlockSpec((1, DN_HEAD_DIM), lambda n: (0, 0))],
        out_specs=seq_spec,
        out_shape=jax.ShapeDtypeStruct((B, S, DN_WIDTH), BF16),
        scratch_shapes=[pltpu.VMEM((B, DN_HEADS, DN_HEAD_DIM, DN_HEAD_DIM), F32)],
        compiler_params=_cparams("arbitrary"),
    )(r3(u), r3(w), r3(qd), r3(kd), r3(qk), gl.reshape(B, S // DN_CHUNK * 8, DN_WIDTH), proj3,
      norm_g.reshape(1, DN_HEAD_DIM))


def _tiles(S):
    rows = min(1024, S)
    return dict(rows=rows, sgu_rows=min(512, S), gdn_rows=min(512, S), scan_chunks=min(4, S // DN_CHUNK),
                ffn_chunk=256)


def _forward(x, mem, mem_norm, norm_mix, norm_xattn, norm_ffn, ev_w_in, pool_w, pool_scale, ev_w_out,
             od_w_in, sgu_ln_g, sgu_ln_b, sgu_w, sgu_b, dn_conv, dn_a_log, dn_dt_bias, dn_norm_g,
             od_w_out, xattn_wq, xattn_wkv, xattn_wo, ffn_w_up, ffn_conv, ffn_w_down, final_norm):
    B, S, D = x.shape
    n_mem = mem.shape[1]
    M = B * S
    depth = norm_mix.shape[0]
    bf = lambda a: a.astype(BF16)
    t = _tiles(S)

    h = x.reshape(M, D)
    mem2 = mem.reshape(B * n_mem, D)
    wkv_all, wq_all, wo_all = bf(xattn_wkv), bf(xattn_wq), bf(xattn_wo)
    w_up_all, w_down_all = bf(ffn_w_up), bf(ffn_w_down)
    for layer in range(depth):
        i = layer // 2
        if layer % 2 == 0:
            proj, = norm_matmul(h, norm_mix[layer], [bf(ev_w_in[i])], [BF16], t["rows"])
            a_out = moba_attention(proj.reshape(B, S, -1), B, S).reshape(M, A_WIDTH)
            b_out = multiscale_pool(proj, bf(pool_w[i]), pool_scale[i], M, S, t["rows"])
            mix_a, mix_b, w_mix = a_out, b_out, ev_w_out[i]
        else:
            main_w = 2 * SGU_WIDTH + 4 * DN_WIDTH
            w_in = od_w_in[i]
            w_tail = jnp.pad(w_in[:, main_w:], ((0, 0), (0, LANES - 2 * DN_HEADS)))
            proj, tail, tail_t = od_projection(h, norm_mix[layer], bf(w_in), bf(w_tail), dn_conv[i],
                                               S, t["rows"])
            c_out = spatial_gating(proj, sgu_ln_g[i], sgu_ln_b[i], sgu_w[i], sgu_b[i], M, t["sgu_rows"])
            u, w, qd, kd, qk, gl = gdn_intra(proj, tail, tail_t, dn_a_log[i], dn_dt_bias[i], M, t["gdn_rows"])
            d_out = gdn_scan(u, w, qd, kd, qk, gl, proj.reshape(B, S, -1), dn_norm_g[i], B, S,
                             t["scan_chunks"])
            mix_a, mix_b, w_mix = c_out, d_out.reshape(M, DN_WIDTH), od_w_out[i]
        kv, = norm_matmul(mem2, mem_norm, [wkv_all], [BF16], B * n_mem, layer=layer)
        h = mix_xattn_residual(h, mix_a, mix_b, bf(w_mix), norm_xattn[layer], wq_all, kv, wo_all, layer,
                               S, n_mem, t["rows"])
        h = ffn_residual(h, norm_ffn[layer], w_up_all, ffn_conv, w_down_all, layer, final_norm, S, t["rows"],
                         t["ffn_chunk"], final_norm=(layer == depth - 1))
    return h.reshape(B, S, D)


def kernel(x, mem, mem_norm, norm_mix, norm_xattn, norm_ffn, ev_w_in, pool_w, pool_scale, ev_w_out, od_w_in, sgu_ln_g, sgu_ln_b, sgu_w, sgu_b, dn_conv, dn_a_log, dn_dt_bias, dn_norm_g, od_w_out, xattn_wq, xattn_wkv, xattn_wo, ffn_w_up, ffn_conv, ffn_w_down, final_norm):
    return _forward(x, mem, mem_norm, norm_mix, norm_xattn, norm_ffn, ev_w_in, pool_w, pool_scale, ev_w_out,
                    od_w_in, sgu_ln_g, sgu_ln_b, sgu_w, sgu_b, dn_conv, dn_a_log, dn_dt_bias, dn_norm_g,
                    od_w_out, xattn_wq, xattn_wkv, xattn_wo, ffn_w_up, ffn_conv, ffn_w_down, final_norm)
```

```python
import functools
import math

import jax
import jax.numpy as jnp
from jax import lax
from jax.experimental import pallas as pl
from jax.experimental.pallas import tpu as pltpu

F32 = jnp.float32
BF16 = jnp.bfloat16
EPS = 1e-6
NEG_BIG = -1e30

VMEM_LIMIT_BYTES = 48 * 1024 * 1024
BF16_SUBLANES = 16
LANES = 128

MOBA_HEADS, MOBA_HEAD_DIM, MOBA_BLOCK, MOBA_TOPK = 8, 64, 256, 3
A_WIDTH = MOBA_HEADS * MOBA_HEAD_DIM
POOL_WINDOWS = (2, 4, 8, 16)
POOL_GROUP = 128
POOL_WIDTH = POOL_GROUP * len(POOL_WINDOWS)
SGU_GROUPS, SGU_GROUP, SGU_CHUNK = 4, 128, 128
SGU_WIDTH = SGU_GROUPS * SGU_GROUP
DN_HEADS, DN_HEAD_DIM, DN_CONV, DN_CHUNK = 4, 128, 4, 64
DN_WIDTH = DN_HEADS * DN_HEAD_DIM
XATTN_HEADS = 4
FFN_CONV = 3


def _cparams(*sem):
    return pltpu.CompilerParams(dimension_semantics=sem, vmem_limit_bytes=VMEM_LIMIT_BYTES)


def _rmsnorm(x, g):
    return x * lax.rsqrt(jnp.mean(x * x, axis=-1, keepdims=True) + EPS) * g


def _silu(x):
    return x * (0.5 * jnp.tanh(0.5 * x) + 0.5)


def _dot(a, b):
    return jnp.dot(a, b, preferred_element_type=F32)


def _resident(arr, layer=None):
    if layer is None:
        return pl.BlockSpec(arr.shape, lambda i: (0,) * arr.ndim, pipeline_mode=pl.Buffered(1))
    return pl.BlockSpec((None,) + arr.shape[1:], lambda i: (layer,) + (0,) * (arr.ndim - 1),
                        pipeline_mode=pl.Buffered(1))


def _dot_nt(a, b, precision=None):
    return lax.dot_general(a, b, (((1,), (1,)), ((), ())), preferred_element_type=F32,
                           precision=precision)


def _norm_mm_kernel(x_ref, g_ref, *refs, bn):
    n = len(refs) // 2
    xn = _rmsnorm(x_ref[...], g_ref[...]).astype(BF16)
    for w_ref, o_ref in zip(refs[:n], refs[n:]):
        N = w_ref.shape[1]
        for c0 in range(0, N, bn):
            c1 = min(c0 + bn, N)
            o_ref[:, c0:c1] = _dot(xn, w_ref[:, c0:c1]).astype(o_ref.dtype)


def norm_matmul(x, g, ws, out_dtypes, bm, bn=512, layer=None):
    M, D = x.shape
    return pl.pallas_call(
        functools.partial(_norm_mm_kernel, bn=bn),
        grid=(M // bm,),
        in_specs=[pl.BlockSpec((bm, D), lambda i: (i, 0)),
                  pl.BlockSpec((1, D), lambda i: (0, 0))]
                 + [_resident(w, layer) for w in ws],
        out_specs=[pl.BlockSpec((bm, w.shape[-1]), lambda i: (i, 0)) for w in ws],
        out_shape=[jax.ShapeDtypeStruct((M, w.shape[-1]), dt) for w, dt in zip(ws, out_dtypes)],
        compiler_params=_cparams("parallel"),
    )(x, g.reshape(1, D), *ws)


def _gelu_tanh(x):
    return 0.5 * x * (1.0 + jnp.tanh(math.sqrt(2.0 / math.pi) * (x + 0.044715 * (x * x * x))))


def _od_proj_kernel(x_ref, halo_ref, g_ref, w_ref, wt_ref, cw_ref, o_ref, t_ref, tt_ref, xe_ref, *,
                    blocks_per_seq, bn):
    bm = x_ref.shape[0]
    H = BF16_SUBLANES
    z_w, qkv_w = 2 * SGU_WIDTH, 3 * DN_WIDTH
    first = (pl.program_id(0) % blocks_per_seq) == 0
    xe_ref[:H, :] = jnp.where(first, 0.0, _rmsnorm(halo_ref[...], g_ref[...])).astype(BF16)
    xe_ref[H:, :] = _rmsnorm(x_ref[...], g_ref[...]).astype(BF16)
    xn = xe_ref[H:, :]
    tail = _dot(xn, wt_ref[...])
    t_ref[...] = tail
    tt_ref[...] = tail.T[:H, :]
    for c0 in range(0, o_ref.shape[1], bn):
        cols = slice(c0, c0 + bn)
        if c0 < z_w:
            o_ref[:, cols] = _gelu_tanh(_dot(xn, w_ref[:, cols])).astype(o_ref.dtype)
        elif c0 < z_w + qkv_w:
            y = _dot(xe_ref[...], w_ref[:, cols])
            cw = cw_ref[:, c0 - z_w:c0 - z_w + bn]
            out = y[H:, :] * cw[DN_CONV - 1:DN_CONV, :]
            for k in range(1, DN_CONV):
                out = out + pltpu.roll(y, k, axis=0)[H:, :] * cw[DN_CONV - 1 - k:DN_CONV - k, :]
            o_ref[:, cols] = _silu(out).astype(o_ref.dtype)
        else:
            o_ref[:, cols] = _dot(xn, w_ref[:, cols]).astype(o_ref.dtype)


def od_projection(x, g, w, w_tail, conv_w, S, bm, bn=512):
    M, D = x.shape
    N = 2 * SGU_WIDTH + 4 * DN_WIDTH
    H = BF16_SUBLANES
    return pl.pallas_call(
        functools.partial(_od_proj_kernel, blocks_per_seq=S // bm, bn=bn),
        grid=(M // bm,),
        in_specs=[pl.BlockSpec((bm, D), lambda i: (i, 0)),
                  pl.BlockSpec((H, D), lambda i: (jnp.maximum(i * (bm // H) - 1, 0), 0)),
                  pl.BlockSpec((1, D), lambda i: (0, 0)),
                  _resident(w), _resident(w_tail), _resident(conv_w)],
        out_specs=[pl.BlockSpec((bm, N), lambda i: (i, 0)), pl.BlockSpec((bm, LANES), lambda i: (i, 0)),
                   pl.BlockSpec((H, bm), lambda i: (0, i))],
        out_shape=[jax.ShapeDtypeStruct((M, N), BF16), jax.ShapeDtypeStruct((M, LANES), F32),
                   jax.ShapeDtypeStruct((H, M), F32)],
        scratch_shapes=[pltpu.VMEM((H + bm, D), BF16)],
        compiler_params=_cparams("parallel"),
    )(x, x, g.reshape(1, D), w, w_tail, conv_w)


def _moba_kernel(q_ref, k_ref, v_ref, o_ref, kme_ref, vt_ref, sel_ref, m_ref, acc_ref, s_ref, *,
                 nb, nbp, unroll, pairs):
    BS = MOBA_BLOCK
    HD = MOBA_HEAD_DIM
    n_heads = 2 * pairs
    i = pl.program_id(2)
    lane = lax.broadcasted_iota(jnp.int32, (1, LANES), 1)
    head_lanes = (lane < HD, lane >= HD)
    pair_lanes = lambda u: slice((u // 2) * LANES, (u // 2 + 1) * LANES)

    @pl.when(i == 0)
    def _():
        kme_ref[...] = jnp.zeros_like(kme_ref)
        for n in range(nb):
            rows = slice(n * BS, (n + 1) * BS)
            mean = jnp.sum(k_ref[0, rows, :].astype(F32), axis=0, keepdims=True) / BS
            for u in range(n_heads):
                kme_ref[u // 2, (u % 2) * nbp + n:(u % 2) * nbp + n + 1, :] = jnp.where(
                    head_lanes[u % 2], mean[:, pair_lanes(u)], 0.0)
            vt_ref[:, rows] = v_ref[0, rows, :].astype(F32).T.astype(BF16)

    scale = HD ** -0.5 * math.log2(math.e)
    q_t = [q_ref[0, :, p * LANES:(p + 1) * LANES].astype(F32).T for p in range(pairs)]
    pair_row = lax.broadcasted_iota(jnp.int32, (LANES, 1), 0)
    head_rows = (pair_row < HD, pair_row >= HD)
    q_aug = [jnp.where(head_rows[u % 2], q_t[u // 2] * scale, 0.0).astype(BF16) for u in range(n_heads)]

    gates = []
    for p in range(pairs):
        km = kme_ref[p]
        k_hi = km.astype(BF16)
        r1 = km - k_hi.astype(F32)
        k_mid = r1.astype(BF16)
        k_lo = (r1 - k_mid.astype(F32)).astype(BF16)
        q16 = q_t[p].astype(BF16)
        gates.append(_dot(k_hi, q16) + _dot(k_mid, q16) + _dot(k_lo, q16))
    blk = lax.broadcasted_iota(jnp.int32, (nbp, 1), 0).astype(F32)
    valid = blk < i.astype(F32)
    for u in range(n_heads):
        g = jnp.where(valid, gates[u // 2][(u % 2) * nbp:(u % 2 + 1) * nbp], -jnp.inf)
        sel = jnp.zeros(g.shape, jnp.bool_)
        for _ in range(MOBA_TOPK):
            mx = jnp.max(g, axis=0, keepdims=True)
            idx = jnp.min(jnp.where(g == mx, blk, float(1 << 20)), axis=0, keepdims=True)
            pick = blk == idx
            sel = sel | pick
            g = jnp.where(pick, -jnp.inf, g)
        sel_ref[u, :nbp, :] = jnp.where(sel & valid, 1.0, 0.0)
        sel_ref[u, nbp:, :] = jnp.zeros((8, BS), F32)
        m_ref[u] = jnp.full((1, BS), NEG_BIG, F32)
        acc_ref[u] = jnp.zeros(acc_ref.shape[1:], F32)

    krow = lax.broadcasted_iota(jnp.int32, (BS, BS), 0)
    qcol = lax.broadcasted_iota(jnp.int32, (BS, BS), 1)
    PVR = HD + BF16_SUBLANES
    pv_rows = (slice(0, PVR), slice(LANES - PVR, LANES))
    pv_row = lax.broadcasted_iota(jnp.int32, (PVR, 1), 0)
    is_dim = (pv_row < HD, pv_row >= PVR - HD)

    def block_start(j):
        return pl.multiple_of(jnp.minimum(j, i) * BS, BS)

    def produce(g, slot):
        for t in range(unroll):
            rows = pl.ds(block_start(g * unroll + t), BS)
            for u in range(n_heads):
                s_ref[slot, u, t * BS:(t + 1) * BS, :] = _dot(
                    k_ref[0, rows, pair_lanes(u)], q_aug[u]).astype(BF16)

    def softmax_update(sts, sels, starts):
        heads = range(n_heads)
        m_new, alpha = [], []
        for u in heads:
            cand = jnp.full((1, BS), NEG_BIG, F32)
            for st, sel in zip(sts[u], sels[u]):
                mx = jnp.max(st.reshape(BS // BF16_SUBLANES, BF16_SUBLANES, BS), axis=0)
                mx = jnp.max(mx.astype(F32), axis=0, keepdims=True)
                cand = jnp.maximum(cand, mx if sel is None else jnp.where(sel, mx, NEG_BIG))
            m_old = m_ref[u]
            m_new.append(jnp.maximum(m_old, cand))
            alpha.append(jnp.exp2(m_old - m_new[u]))
            m_ref[u] = m_new[u]
        ps = []
        for u in heads:
            pu = []
            for st, sel in zip(sts[u], sels[u]):
                sub = m_new[u] if sel is None else jnp.where(sel, m_new[u], -NEG_BIG)
                pu.append(jnp.exp2(st - sub.astype(BF16)))
            ps.append(pu[0] if len(pu) == 1 else jnp.concatenate(pu, axis=0))
        pv = []
        for u in heads:
            rows = slice((u // 2) * LANES + pv_rows[u % 2].start, (u // 2) * LANES + pv_rows[u % 2].stop)
            vts = [jnp.where(is_dim[u % 2], vt_ref[rows, pl.ds(st0, BS)], jnp.ones((), BF16))
                   for st0 in starts]
            pv.append(_dot(vts[0] if len(vts) == 1 else jnp.concatenate(vts, axis=1), ps[u]))
        for u in heads:
            acc_ref[u] = acc_ref[u] * alpha[u] + pv[u]

    def consume(g, slot):
        js = [g * unroll + t for t in range(unroll)]
        sts = [[s_ref[slot, u, t * BS:(t + 1) * BS, :] for t in range(unroll)] for u in range(n_heads)]
        sels = [[sel_ref[u, pl.ds(j, 1), :] > 0.5 for j in js] for u in range(n_heads)]
        softmax_update(sts, sels, [block_start(j) for j in js])

    slots = s_ref.shape[0]
    assert unroll == 1, "the group after the last past block must be exactly the tile's own block"

    def body(gg, c):
        for t in range(slots):
            produce(slots * gg + t + 1, (t + 1) % slots)
            consume(slots * gg + t, t)
        return c

    n_groups = (i + unroll - 1) // unroll
    produce(0, 0)
    lax.fori_loop(0, (n_groups + slots - 1) // slots, body, 0)
    softmax_update([[jnp.where(krow <= qcol, s_ref[0, u, :BS, :], -jnp.inf)] for u in range(n_heads)],
                   [[None]] * n_heads, [block_start(i)])

    outs = []
    for u in range(n_heads):
        a = acc_ref[u]
        outs.append(a[:HD] / a[HD:HD + 1, :] if u % 2 == 0 else a[PVR - HD:] / a[0:1, :])
    o_ref[0] = jnp.concatenate(outs, axis=0).T.astype(o_ref.dtype)


def moba_attention(proj, B, S):
    nb = S // MOBA_BLOCK
    nbp = -(-nb // 8) * 8
    pairs = 4
    width = pairs * LANES
    groups = A_WIDTH // width
    unroll = 1
    slots = 3
    n_heads = 2 * pairs
    return pl.pallas_call(
        functools.partial(_moba_kernel, nb=nb, nbp=nbp, unroll=unroll, pairs=pairs),
        grid=(B, groups, nb),
        in_specs=[pl.BlockSpec((1, MOBA_BLOCK, width), lambda b, p, i: (b, i, p)),
                  pl.BlockSpec((1, S, width), lambda b, p, i: (b, 0, groups + p), pipeline_mode=pl.Buffered(1)),
                  pl.BlockSpec((1, S, width), lambda b, p, i: (b, 0, 2 * groups + p), pipeline_mode=pl.Buffered(1))],
        out_specs=pl.BlockSpec((1, MOBA_BLOCK, width), lambda b, p, i: (b, i, p)),
        out_shape=jax.ShapeDtypeStruct((B, S, A_WIDTH), BF16),
        scratch_shapes=[pltpu.VMEM((pairs, 2 * nbp, LANES), F32),
                        pltpu.VMEM((width, S), BF16),
                        pltpu.VMEM((n_heads, nbp + 8, MOBA_BLOCK), F32),
                        pltpu.VMEM((n_heads, 1, MOBA_BLOCK), F32),
                        pltpu.VMEM((n_heads, MOBA_HEAD_DIM + BF16_SUBLANES, MOBA_BLOCK), F32),
                        pltpu.VMEM((slots, n_heads, unroll * MOBA_BLOCK, MOBA_BLOCK), BF16)],
        compiler_params=_cparams("parallel", "parallel", "arbitrary"),
    )(proj, proj, proj)


def _pool_kernel(p_ref, halo_ref, w_ref, sc_ref, o_ref, *, blocks_per_seq):
    bm = p_ref.shape[0]
    H = BF16_SUBLANES
    i = pl.program_id(0)
    first = (i % blocks_per_seq) == 0
    t1 = (lax.broadcasted_iota(jnp.int32, (bm, 1), 0) + (i % blocks_per_seq) * bm + 1).astype(F32)
    for g, w in enumerate(POOL_WINDOWS):
        cols = slice(g * POOL_GROUP, (g + 1) * POOL_GROUP)
        cur = p_ref[:, cols].astype(F32)
        halo = jnp.where(first, 0.0, halo_ref[:, cols].astype(F32))
        ext = jnp.concatenate([halo, cur], axis=0)
        acc = ext
        sh = 1
        while sh < w:
            acc = acc + pltpu.roll(acc, sh, axis=0)
            sh *= 2
        win = acc[H:, :]
        pooled = win / jnp.minimum(t1, float(w)) - cur
        y = _dot(pooled.astype(BF16), w_ref[g])
        o_ref[:, cols] = (y * sc_ref[:, cols]).astype(o_ref.dtype)


def multiscale_pool(proj, pool_w, pool_scale, M, S, bm):
    H = BF16_SUBLANES
    pcol = 3 * A_WIDTH // POOL_WIDTH
    return pl.pallas_call(
        functools.partial(_pool_kernel, blocks_per_seq=S // bm),
        grid=(M // bm,),
        in_specs=[pl.BlockSpec((bm, POOL_WIDTH), lambda i: (i, pcol)),
                  pl.BlockSpec((H, POOL_WIDTH), lambda i: (jnp.maximum(i * (bm // H) - 1, 0), pcol)),
                  pl.BlockSpec((len(POOL_WINDOWS), POOL_GROUP, POOL_GROUP), lambda i: (0, 0, 0)),
                  pl.BlockSpec((1, POOL_WIDTH), lambda i: (0, 0))],
        out_specs=pl.BlockSpec((bm, POOL_WIDTH), lambda i: (i, 0)),
        out_shape=jax.ShapeDtypeStruct((M, POOL_WIDTH), BF16),
        compiler_params=_cparams("parallel"),
    )(proj, proj, pool_w, pool_scale.reshape(1, POOL_WIDTH))


def _mix_xattn_kernel(h_ref, a_ref, b_ref, wm_ref, g_ref, wq_ref, k_ref, v_ref, wo_ref, o_ref):
    ka = a_ref.shape[1]
    h = h_ref[...] + _dot(a_ref[...], wm_ref[:ka, :]) + _dot(b_ref[...], wm_ref[ka:, :])
    D = h.shape[1]
    hd = D // XATTN_HEADS
    xn = _rmsnorm(h, g_ref[...]).astype(BF16)
    q = (_dot(xn, wq_ref[...]) * hd ** -0.5).astype(BF16)
    outs = []
    for hh in range(XATTN_HEADS):
        cols = slice(hh * hd, (hh + 1) * hd)
        s = _dot_nt(q[:, cols], k_ref[:, cols])
        m = jnp.max(s, axis=1, keepdims=True)
        p = jnp.exp(s - m)
        l = jnp.sum(p, axis=1, keepdims=True)
        outs.append((_dot(p.astype(BF16), v_ref[:, cols]) / l).astype(BF16))
    o = jnp.concatenate(outs, axis=1)
    o_ref[...] = h + _dot(o, wo_ref[...])


def mix_xattn_residual(h, a, b, w_mix, g, wq, kv, wo, layer, S, n_mem, bm):
    M, D = h.shape
    ka, kb = a.shape[1], b.shape[1]
    bps = S // bm
    return pl.pallas_call(
        _mix_xattn_kernel,
        grid=(M // bm,),
        in_specs=[pl.BlockSpec((bm, D), lambda i: (i, 0)),
                  pl.BlockSpec((bm, ka), lambda i: (i, 0)),
                  pl.BlockSpec((bm, kb), lambda i: (i, 0)),
                  _resident(w_mix),
                  pl.BlockSpec((1, D), lambda i: (0, 0)),
                  _resident(wq, layer),
                  pl.BlockSpec((n_mem, D), lambda i: (i // bps, 0)),
                  pl.BlockSpec((n_mem, D), lambda i: (i // bps, 1)),
                  _resident(wo, layer)],
        out_specs=pl.BlockSpec((bm, D), lambda i: (i, 0)),
        out_shape=jax.ShapeDtypeStruct((M, D), F32),
        compiler_params=_cparams("parallel"),
    )(h, a, b, w_mix, g.reshape(1, D), wq, kv, kv, wo)


def _ffn_kernel(h_ref, halo_ref, g_ref, wup_ref, cw_ref, wd_ref, fg_ref, o_ref,
                xn_ref, acc_ref, y_ref, *, blocks_per_seq, final_norm, sub):
    H = BF16_SUBLANES
    nc, cf = wd_ref.shape[0], wd_ref.shape[1]
    n_sub = acc_ref.shape[0] // sub

    first = (pl.program_id(0) % blocks_per_seq) == 0
    xn_ref[:H, :] = jnp.where(first, 0.0, _rmsnorm(halo_ref[...], g_ref[...])).astype(BF16)
    xn_ref[H:, :] = _rmsnorm(h_ref[...], g_ref[...]).astype(BF16)
    acc_ref[...] = jnp.zeros_like(acc_ref)

    chunk_cols = lambda c: pl.ds(pl.multiple_of(c * cf, cf), cf)

    def up(c, r):
        if r == 0:
            xs, dst = xn_ref[:sub + H, :], slice(0, sub + H)
        else:
            xs, dst = xn_ref[H + r * sub:H + (r + 1) * sub, :], slice(H, sub + H)
            y_ref[r % 2, :H, :] = y_ref[(r - 1) % 2, sub:sub + H, :]
        y_ref[r % 2, dst, :cf] = _dot(xs, wup_ref[:, chunk_cols(c)])
        y_ref[r % 2, dst, cf:] = _dot(xs, wup_ref[:, chunk_cols(nc + c)])

    def conv(r, part, cw):
        cols = slice(part * cf, (part + 1) * cf)
        out = y_ref[r % 2, H:, cols] * cw[FFN_CONV - 1:FFN_CONV, :]
        for k in range(1, FFN_CONV):
            out = out + y_ref[r % 2, H - k:H - k + sub, cols] * cw[FFN_CONV - 1 - k:FFN_CONV - k, :]
        return out

    def chunk(c, carry):
        cwg, cwu = cw_ref[:, chunk_cols(c)], cw_ref[:, chunk_cols(nc + c)]
        wd = wd_ref[c]
        for r in range(n_sub):
            if r + 1 < n_sub:
                up(c, r + 1)
            else:
                up(jnp.minimum(c + 1, nc - 1), 0)
            act = _silu(conv(r, 0, cwg)) * conv(r, 1, cwu)
            acc_ref[r * sub:(r + 1) * sub, :] += _dot(act.astype(BF16), wd)
        return carry

    up(0, 0)
    lax.fori_loop(0, nc, chunk, 0)
    y = h_ref[...] + acc_ref[...]
    if final_norm:
        y = _rmsnorm(y, fg_ref[...])
    o_ref[...] = y


def ffn_residual(h, g, w_up, conv_w, w_down, layer, final_g, S, bm, cf, final_norm):
    M, D = h.shape
    d_ff = w_down.shape[1]
    H = BF16_SUBLANES
    nc = d_ff // cf
    sub = min(256, bm)
    assert (bm // sub) % 2 == 0, "the two y_ref slots alternate per sub-block across chunks"
    wd3 = w_down.reshape(-1, nc, cf, D)
    return pl.pallas_call(
        functools.partial(_ffn_kernel, blocks_per_seq=S // bm, final_norm=final_norm, sub=sub),
        grid=(M // bm,),
        in_specs=[pl.BlockSpec((bm, D), lambda i: (i, 0)),
                  pl.BlockSpec((H, D), lambda i: (jnp.maximum(i * (bm // H) - 1, 0), 0)),
                  pl.BlockSpec((1, D), lambda i: (0, 0)),
                  _resident(w_up, layer), _resident(conv_w, layer), _resident(wd3, layer),
                  pl.BlockSpec((1, D), lambda i: (0, 0))],
        out_specs=pl.BlockSpec((bm, D), lambda i: (i, 0)),
        out_shape=jax.ShapeDtypeStruct((M, D), F32),
        scratch_shapes=[pltpu.VMEM((H + bm, D), BF16), pltpu.VMEM((bm, D), F32),
                        pltpu.VMEM((2, H + sub, 2 * cf), F32)],
        compiler_params=_cparams("parallel"),
    )(h, h, g.reshape(1, D), w_up, conv_w, wd3, final_g.reshape(1, D))


def _sgu_kernel(u_ref, v_ref, lg_ref, lb_ref, w_ref, bt_ref, o_ref):
    rows = u_ref.shape[0]
    T = SGU_CHUNK
    v = v_ref[...].astype(F32)
    mu = jnp.mean(v, axis=-1, keepdims=True)
    d = v - mu
    var = jnp.mean(d * d, axis=-1, keepdims=True)
    vn = (d * lax.rsqrt(var + EPS) * lg_ref[...] + lb_ref[...]).astype(BF16)
    causal = (lax.broadcasted_iota(jnp.int32, (T, T), 1) <= lax.broadcasted_iota(jnp.int32, (T, T), 0))
    for g in range(SGU_GROUPS):
        cols = slice(g * SGU_GROUP, (g + 1) * SGU_GROUP)
        wg = jnp.where(causal, w_ref[g], 0.0).astype(BF16)
        bias = bt_ref[:, g:g + 1]
        for c in range(rows // T):
            rs = slice(c * T, (c + 1) * T)
            s = _dot(wg, vn[rs, cols]) + bias
            o_ref[rs, cols] = (u_ref[rs, cols].astype(F32) * s).astype(o_ref.dtype)


def spatial_gating(proj, ln_g, ln_b, w_s, b_s, M, rows):
    return pl.pallas_call(
        _sgu_kernel,
        grid=(M // rows,),
        in_specs=[pl.BlockSpec((rows, SGU_WIDTH), lambda i: (i, 0)),
                  pl.BlockSpec((rows, SGU_WIDTH), lambda i: (i, 1)),
                  pl.BlockSpec((1, SGU_WIDTH), lambda i: (0, 0)),
                  pl.BlockSpec((1, SGU_WIDTH), lambda i: (0, 0)),
                  pl.BlockSpec((SGU_GROUPS, SGU_CHUNK, SGU_CHUNK), lambda i: (0, 0, 0)),
                  pl.BlockSpec((SGU_CHUNK, SGU_GROUPS), lambda i: (0, 0))],
        out_specs=pl.BlockSpec((rows, SGU_WIDTH), lambda i: (i, 0)),
        out_shape=jax.ShapeDtypeStruct((M, SGU_WIDTH), BF16),
        compiler_params=_cparams("parallel"),
    )(proj, proj, ln_g.reshape(1, -1), ln_b.reshape(1, -1), w_s, b_s.T)


def _gdn_intra_kernel(q_ref, k_ref, v_ref, tail_ref, tailt_ref, alog_ref,
                      dtb_ref, u_ref, w_ref, qd_ref, kd_ref, qk_ref, gl_ref):
    rows = q_ref.shape[0]
    C = DN_CHUNK
    HD = DN_HEAD_DIM
    x = jnp.concatenate([q_ref[...], k_ref[...], v_ref[...]], axis=1).astype(F32)

    ii = lax.broadcasted_iota(jnp.int32, (C, C), 0)
    jj = lax.broadcasted_iota(jnp.int32, (C, C), 1)
    lower = jj <= ii
    strict = jj < ii
    su = lax.broadcasted_iota(jnp.int32, (C, LANES), 0)
    ju = lax.broadcasted_iota(jnp.int32, (C, LANES), 1)
    upper_ext = jnp.where(((ju < C) & (su > ju)) | (ju == C), 1.0, 0.0)

    inst = [(c, hh) for hh in range(DN_HEADS) for c in range(rows // C)]
    qs, ks, vs, bs, stacks = [], [], [], [], []
    for hh in range(DN_HEADS):
        qh = x[:, hh * HD:(hh + 1) * HD]
        kh = x[:, DN_WIDTH + hh * HD:DN_WIDTH + (hh + 1) * HD]
        vh = x[:, 2 * DN_WIDTH + hh * HD:2 * DN_WIDTH + (hh + 1) * HD]
        qh = qh * lax.rsqrt(jnp.sum(qh * qh, axis=-1, keepdims=True) + EPS) * HD ** -0.5
        kh = kh * lax.rsqrt(jnp.sum(kh * kh, axis=-1, keepdims=True) + EPS)
        beta = 1.0 / (1.0 + jnp.exp(-tail_ref[:, hh:hh + 1]))
        a_raw = tailt_ref[DN_HEADS + hh:DN_HEADS + hh + 1, :]
        z = a_raw + dtb_ref[0:1, hh:hh + 1]
        softplus = jnp.maximum(z, 0.0) + jnp.log(1.0 + jnp.exp(-jnp.abs(z)))
        g_row = -jnp.exp(alog_ref[0:1, hh:hh + 1]) * softplus
        for c in range(rows // C):
            rs = slice(c * C, (c + 1) * C)
            qs.append(qh[rs]); ks.append(kh[rs]); vs.append(vh[rs]); bs.append(beta[rs])
            gr = jnp.broadcast_to(g_row[:, rs], (C, C))
            stacks += [jnp.where(lower, gr, 0.0), jnp.where(lower, 0.0, gr)]

    stacked = jnp.concatenate(stacks, axis=0)
    s_hi = stacked.astype(BF16)
    s_lo = (stacked - s_hi.astype(F32)).astype(BF16)
    ue = upper_ext.astype(BF16)
    dall = _dot(s_hi, ue) + _dot(s_lo, ue)

    decays, gcs, gc_revs, k16s, kbs = [], [], [], [], []
    for n, (c, hh) in enumerate(inst):
        dext = dall[n * 2 * C:(n + 1) * 2 * C]
        decays.append(jnp.exp(jnp.where(lower, dext[:C, :C], -jnp.inf)))
        gcs.append(dext[:C, C:C + 1])
        gc_revs.append(dext[C:, C:C + 1])
        kbs.append(ks[n] * bs[n])
        k16s.append(ks[n].astype(BF16))
    kq = [_dot_nt(jnp.concatenate([kbs[n], qs[n]], axis=0).astype(BF16), k16s[n]) for n in range(len(inst))]
    kk = [m[:C] for m in kq]
    qk = [m[C:] for m in kq]
    pws = [jnp.where(strict, kk[n] * decays[n], 0.0).astype(BF16) for n in range(len(inst))]
    egc = [jnp.exp(g) for g in gcs]
    rhs = [jnp.concatenate([vs[n] * bs[n], kbs[n] * egc[n]], axis=1) for n in range(len(inst))]
    sols = [rhs[n] - _dot(pws[n], rhs[n].astype(BF16)) for n in range(len(inst))]
    for _ in range(int(math.log2(C)) - 1):
        pws = [_dot(p, p).astype(BF16) for p in pws]
        sols = [s + _dot(p, s.astype(BF16)) for p, s in zip(pws, sols)]
    for n, (c, hh) in enumerate(inst):
        rs = slice(c * C, (c + 1) * C)
        hcols = slice(hh * HD, (hh + 1) * HD)
        u_ref[rs, hcols] = sols[n][:, :HD].astype(u_ref.dtype)
        w_ref[rs, hcols] = sols[n][:, HD:].astype(w_ref.dtype)
        qkd = jnp.where(lower, qk[n] * decays[n], 0.0)
        qk_ref[rs, hcols] = jnp.concatenate([qkd, jnp.zeros_like(qkd)], axis=1).astype(qk_ref.dtype)
        qd_ref[rs, hcols] = (qs[n] * egc[n]).astype(qd_ref.dtype)
        kd_ref[rs, hcols] = (ks[n] * jnp.exp(gc_revs[n])).astype(kd_ref.dtype)
        gl_ref[c * 8:(c + 1) * 8, hcols] = jnp.broadcast_to(egc[n][C - 1:C, :], (8, HD))


def gdn_intra(proj, tail, tail_t, a_log, dt_bias, M, rows):
    c0 = 2 * SGU_WIDTH // DN_WIDTH
    pad = lambda p: jnp.pad(p.reshape(1, -1), ((0, 0), (0, LANES - p.shape[0])))
    seq = lambda dt: jax.ShapeDtypeStruct((M, DN_WIDTH), dt)
    row_spec = pl.BlockSpec((rows, DN_WIDTH), lambda i: (i, 0))
    cur_spec = lambda part: pl.BlockSpec((rows, DN_WIDTH), lambda i: (i, c0 + part))
    return pl.pallas_call(
        _gdn_intra_kernel,
        grid=(M // rows,),
        in_specs=[cur_spec(0), cur_spec(1), cur_spec(2),
                  pl.BlockSpec((rows, LANES), lambda i: (i, 0)),
                  pl.BlockSpec((2 * DN_HEADS, rows), lambda i: (0, i)),
                  pl.BlockSpec((1, LANES), lambda i: (0, 0)),
                  pl.BlockSpec((1, LANES), lambda i: (0, 0))],
        out_specs=[row_spec, row_spec, row_spec, row_spec, row_spec,
                   pl.BlockSpec((rows // DN_CHUNK * 8, DN_WIDTH), lambda i: (i, 0))],
        out_shape=[seq(BF16), seq(BF16), seq(BF16), seq(BF16), seq(BF16),
                   jax.ShapeDtypeStruct((M // DN_CHUNK * 8, DN_WIDTH), F32)],
        compiler_params=_cparams("parallel"),
    )(proj, proj, proj, tail, tail_t, pad(a_log), pad(dt_bias))


def _gdn_scan_kernel(u_ref, w_ref, qd_ref, kd_ref, qk_ref, gl_ref, gate_ref, ng_ref, o_ref, state_ref, *,
                     chunks):
    C = DN_CHUNK
    HD = DN_HEAD_DIM
    B = u_ref.shape[0]

    @pl.when(pl.program_id(0) == 0)
    def _():
        state_ref[...] = jnp.zeros_like(state_ref)

    ng = ng_ref[...]
    inst = [(b, hh) for b in range(B) for hh in range(DN_HEADS)]
    col = lambda hh: slice(hh * HD, (hh + 1) * HD)
    states = [state_ref[b, hh] for b, hh in inst]
    for c in range(chunks):
        rs = slice(c * C, (c + 1) * C)
        kdt = [kd_ref[b, rs, col(hh)].astype(F32).T.astype(BF16) for b, hh in inst]
        st16 = [s.astype(BF16) for s in states]
        ws = [_dot(w_ref[b, rs, col(hh)], st16[n]) for n, (b, hh) in enumerate(inst)]
        qs = [_dot(qd_ref[b, rs, col(hh)], st16[n]) for n, (b, hh) in enumerate(inst)]
        vn16 = [(u_ref[b, rs, col(hh)].astype(F32) - ws[n]).astype(BF16) for n, (b, hh) in enumerate(inst)]
        states = [states[n] * gl_ref[b, c * 8:c * 8 + 1, col(hh)] + _dot(kdt[n], vn16[n])
                  for n, (b, hh) in enumerate(inst)]
        for n, (b, hh) in enumerate(inst):
            o = qs[n] + _dot(qk_ref[b, rs, col(hh)][:, :C], vn16[n])
            o = o * lax.rsqrt(jnp.mean(o * o, axis=-1, keepdims=True) + EPS) * ng
            o_ref[b, rs, col(hh)] = (o * _silu(gate_ref[b, rs, col(hh)].astype(F32))).astype(o_ref.dtype)
    for n, (b, hh) in enumerate(inst):
        state_ref[b, hh] = states[n]


def gdn_scan(u, w, qd, kd, qk, gl, proj3, norm_g, B, S, chunks):
    rows = chunks * DN_CHUNK
    r3 = lambda a: a.reshape(B, S, DN_WIDTH)
    gcol = (2 * SGU_WIDTH + 3 * DN_WIDTH) // DN_WIDTH
    seq_spec = pl.BlockSpec((B, rows, DN_WIDTH), lambda n: (0, n, 0))
    return pl.pallas_call(
        functools.partial(_gdn_scan_kernel, chunks=chunks),
        grid=(S // rows,),
        in_specs=[seq_spec, seq_spec, seq_spec, seq_spec, seq_spec,
                  pl.BlockSpec((B, chunks * 8, DN_WIDTH), lambda n: (0, n, 0)),
                  pl.BlockSpec((B, rows, DN_WIDTH), lambda n: (0, n, gcol)),
                  pl.BlockSpec((1, DN_HEAD_DIM), lambda n: (0, 0))],
        out_specs=seq_spec,
        out_shape=jax.ShapeDtypeStruct((B, S, DN_WIDTH), BF16),
        scratch_shapes=[pltpu.VMEM((B, DN_HEADS, DN_HEAD_DIM, DN_HEAD_DIM), F32)],
        compiler_params=_cparams("arbitrary"),
    )(r3(u), r3(w), r3(qd), r3(kd), r3(qk), gl.reshape(B, S // DN_CHUNK * 8, DN_WIDTH), proj3,
      norm_g.reshape(1, DN_HEAD_DIM))


def _tiles(S):
    rows = min(1024, S)
    return dict(rows=rows, sgu_rows=min(512, S), gdn_rows=min(512, S), scan_chunks=min(4, S // DN_CHUNK),
                ffn_chunk=256)


def _forward(x, mem, mem_norm, norm_mix, norm_xattn, norm_ffn, ev_w_in, pool_w, pool_scale, ev_w_out,
             od_w_in, sgu_ln_g, sgu_ln_b, sgu_w, sgu_b, dn_conv, dn_a_log, dn_dt_bias, dn_norm_g,
             od_w_out, xattn_wq, xattn_wkv, xattn_wo, ffn_w_up, ffn_conv, ffn_w_down, final_norm):
    B, S, D = x.shape
    n_mem = mem.shape[1]
    M = B * S
    depth = norm_mix.shape[0]
    bf = lambda a: a.astype(BF16)
    t = _tiles(S)

    h = x.reshape(M, D)
    mem2 = mem.reshape(B * n_mem, D)
    wkv_all, wq_all, wo_all = bf(xattn_wkv), bf(xattn_wq), bf(xattn_wo)
    w_up_all, w_down_all = bf(ffn_w_up), bf(ffn_w_down)
    for layer in range(depth):
        i = layer // 2
        if layer % 2 == 0:
            proj, = norm_matmul(h, norm_mix[layer], [bf(ev_w_in[i])], [BF16], t["rows"])
            a_out = moba_attention(proj.reshape(B, S, -1), B, S).reshape(M, A_WIDTH)
            b_out = multiscale_pool(proj, bf(pool_w[i]), pool_scale[i], M, S, t["rows"])
            mix_a, mix_b, w_mix = a_out, b_out, ev_w_out[i]
        else:
            main_w = 2 * SGU_WIDTH + 4 * DN_WIDTH
            w_in = od_w_in[i]
            w_tail = jnp.pad(w_in[:, main_w:], ((0, 0), (0, LANES - 2 * DN_HEADS)))
            proj, tail, tail_t = od_projection(h, norm_mix[layer], bf(w_in), bf(w_tail), dn_conv[i],
                                               S, t["rows"])
            c_out = spatial_gating(proj, sgu_ln_g[i], sgu_ln_b[i], sgu_w[i], sgu_b[i], M, t["sgu_rows"])
            u, w, qd, kd, qk, gl = gdn_intra(proj, tail, tail_t, dn_a_log[i], dn_dt_bias[i], M, t["gdn_rows"])
            d_out = gdn_scan(u, w, qd, kd, qk, gl, proj.reshape(B, S, -1), dn_norm_g[i], B, S,
                             t["scan_chunks"])
            mix_a, mix_b, w_mix = c_out, d_out.reshape(M, DN_WIDTH), od_w_out[i]
        kv, = norm_matmul(mem2, mem_norm, [wkv_all], [BF16], B * n_mem, layer=layer)
        h = mix_xattn_residual(h, mix_a, mix_b, bf(w_mix), norm_xattn[layer], wq_all, kv, wo_all, layer,
                               S, n_mem, t["rows"])
        h = ffn_residual(h, norm_ffn[layer], w_up_all, ffn_conv, w_down_all, layer, final_norm, S, t["rows"],
                         t["ffn_chunk"], final_norm=(layer == depth - 1))
    return h.reshape(B, S, D)


def kernel(x, mem, mem_norm, norm_mix, norm_xattn, norm_ffn, ev_w_in, pool_w, pool_scale, ev_w_out, od_w_in, sgu_ln_g, sgu_ln_b, sgu_w, sgu_b, dn_conv, dn_a_log, dn_dt_bias, dn_norm_g, od_w_out, xattn_wq, xattn_wkv, xattn_wo, ffn_w_up, ffn_conv, ffn_w_down, final_norm):
    return _forward(x, mem, mem_norm, norm_mix, norm_xattn, norm_ffn, ev_w_in, pool_w, pool_scale, ev_w_out,
                    od_w_in, sgu_ln_g, sgu_ln_b, sgu_w, sgu_b, dn_conv, dn_a_log, dn_dt_bias, dn_norm_g,
                    od_w_out, xattn_wq, xattn_wkv, xattn_wo, ffn_w_up, ffn_conv, ffn_w_down, final_norm)
```

```python
import functools
import math

import jax
import jax.numpy as jnp
from jax import lax
from jax.experimental import pallas as pl
from jax.experimental.pallas import tpu as pltpu

F32 = jnp.float32
BF16 = jnp.bfloat16
EPS = 1e-6
NEG_BIG = -1e30

VMEM_LIMIT_BYTES = 48 * 1024 * 1024
BF16_SUBLANES = 16
LANES = 128

MOBA_HEADS, MOBA_HEAD_DIM, MOBA_BLOCK, MOBA_TOPK = 8, 64, 256, 3
A_WIDTH = MOBA_HEADS * MOBA_HEAD_DIM
POOL_WINDOWS = (2, 4, 8, 16)
POOL_GROUP = 128
POOL_WIDTH = POOL_GROUP * len(POOL_WINDOWS)
SGU_GROUPS, SGU_GROUP, SGU_CHUNK = 4, 128, 128
SGU_WIDTH = SGU_GROUPS * SGU_GROUP
DN_HEADS, DN_HEAD_DIM, DN_CONV, DN_CHUNK = 4, 128, 4, 64
DN_WIDTH = DN_HEADS * DN_HEAD_DIM
XATTN_HEADS = 4
FFN_CONV = 3


def _cparams(*sem):
    return pltpu.CompilerParams(dimension_semantics=sem, vmem_limit_bytes=VMEM_LIMIT_BYTES)


def _rmsnorm(x, g):
    return x * lax.rsqrt(jnp.mean(x * x, axis=-1, keepdims=True) + EPS) * g


def _silu(x):
    return x * (0.5 * jnp.tanh(0.5 * x) + 0.5)


def _dot(a, b):
    return jnp.dot(a, b, preferred_element_type=F32)


def _resident(arr, layer=None):
    if layer is None:
        return pl.BlockSpec(arr.shape, lambda i: (0,) * arr.ndim, pipeline_mode=pl.Buffered(1))
    return pl.BlockSpec((None,) + arr.shape[1:], lambda i: (layer,) + (0,) * (arr.ndim - 1),
                        pipeline_mode=pl.Buffered(1))


def _dot_nt(a, b, precision=None):
    return lax.dot_general(a, b, (((1,), (1,)), ((), ())), preferred_element_type=F32,
                           precision=precision)


def _norm_mm_kernel(x_ref, g_ref, *refs, bn):
    n = len(refs) // 2
    xn = _rmsnorm(x_ref[...], g_ref[...]).astype(BF16)
    for w_ref, o_ref in zip(refs[:n], refs[n:]):
        N = w_ref.shape[1]
        for c0 in range(0, N, bn):
            c1 = min(c0 + bn, N)
            o_ref[:, c0:c1] = _dot(xn, w_ref[:, c0:c1]).astype(o_ref.dtype)


def norm_matmul(x, g, ws, out_dtypes, bm, bn=512, layer=None):
    M, D = x.shape
    return pl.pallas_call(
        functools.partial(_norm_mm_kernel, bn=bn),
        grid=(M // bm,),
        in_specs=[pl.BlockSpec((bm, D), lambda i: (i, 0)),
                  pl.BlockSpec((1, D), lambda i: (0, 0))]
                 + [_resident(w, layer) for w in ws],
        out_specs=[pl.BlockSpec((bm, w.shape[-1]), lambda i: (i, 0)) for w in ws],
        out_shape=[jax.ShapeDtypeStruct((M, w.shape[-1]), dt) for w, dt in zip(ws, out_dtypes)],
        compiler_params=_cparams("parallel"),
    )(x, g.reshape(1, D), *ws)


def _gelu_tanh(x):
    return 0.5 * x * (1.0 + jnp.tanh(math.sqrt(2.0 / math.pi) * (x + 0.044715 * (x * x * x))))


def _od_proj_kernel(x_ref, halo_ref, g_ref, w_ref, wt_ref, cw_ref, o_ref, t_ref, tt_ref, xe_ref, *,
                    blocks_per_seq, bn):
    bm = x_ref.shape[0]
    H = BF16_SUBLANES
    z_w, qkv_w = 2 * SGU_WIDTH, 3 * DN_WIDTH
    first = (pl.program_id(0) % blocks_per_seq) == 0
    xe_ref[:H, :] = jnp.where(first, 0.0, _rmsnorm(halo_ref[...], g_ref[...])).astype(BF16)
    xe_ref[H:, :] = _rmsnorm(x_ref[...], g_ref[...]).astype(BF16)
    xn = xe_ref[H:, :]
    tail = _dot(xn, wt_ref[...])
    t_ref[...] = tail
    tt_ref[...] = tail.T[:H, :]
    for c0 in range(0, o_ref.shape[1], bn):
        cols = slice(c0, c0 + bn)
        if c0 < z_w:
            o_ref[:, cols] = _gelu_tanh(_dot(xn, w_ref[:, cols])).astype(o_ref.dtype)
        elif c0 < z_w + qkv_w:
            y = _dot(xe_ref[...], w_ref[:, cols])
            cw = cw_ref[:, c0 - z_w:c0 - z_w + bn]
            out = y[H:, :] * cw[DN_CONV - 1:DN_CONV, :]
            for k in range(1, DN_CONV):
                out = out + pltpu.roll(y, k, axis=0)[H:, :] * cw[DN_CONV - 1 - k:DN_CONV - k, :]
            o_ref[:, cols] = _silu(out).astype(o_ref.dtype)
        else:
            o_ref[:, cols] = _dot(xn, w_ref[:, cols]).astype(o_ref.dtype)


def od_projection(x, g, w, w_tail, conv_w, S, bm, bn=512):
    M, D = x.shape
    N = 2 * SGU_WIDTH + 4 * DN_WIDTH
    H = BF16_SUBLANES
    return pl.pallas_call(
        functools.partial(_od_proj_kernel, blocks_per_seq=S // bm, bn=bn),
        grid=(M // bm,),
        in_specs=[pl.BlockSpec((bm, D), lambda i: (i, 0)),
                  pl.BlockSpec((H, D), lambda i: (jnp.maximum(i * (bm // H) - 1, 0), 0)),
                  pl.BlockSpec((1, D), lambda i: (0, 0)),
                  _resident(w), _resident(w_tail), _resident(conv_w)],
        out_specs=[pl.BlockSpec((bm, N), lambda i: (i, 0)), pl.BlockSpec((bm, LANES), lambda i: (i, 0)),
                   pl.BlockSpec((H, bm), lambda i: (0, i))],
        out_shape=[jax.ShapeDtypeStruct((M, N), BF16), jax.ShapeDtypeStruct((M, LANES), F32),
                   jax.ShapeDtypeStruct((H, M), F32)],
        scratch_shapes=[pltpu.VMEM((H + bm, D), BF16)],
        compiler_params=_cparams("parallel"),
    )(x, x, g.reshape(1, D), w, w_tail, conv_w)


def _moba_kernel(q_ref, k_ref, v_ref, o_ref, kme_ref, vt_ref, sel_ref, m_ref, acc_ref, s_ref, *,
                 nb, nbp, unroll, pairs):
    BS = MOBA_BLOCK
    HD = MOBA_HEAD_DIM
    n_heads = 2 * pairs
    i = pl.program_id(2)
    lane = lax.broadcasted_iota(jnp.int32, (1, LANES), 1)
    head_lanes = (lane < HD, lane >= HD)
    pair_lanes = lambda u: slice((u // 2) * LANES, (u // 2 + 1) * LANES)

    @pl.when(i == 0)
    def _():
        kme_ref[...] = jnp.zeros_like(kme_ref)
        for n in range(nb):
            rows = slice(n * BS, (n + 1) * BS)
            mean = jnp.sum(k_ref[0, rows, :].astype(F32), axis=0, keepdims=True) / BS
            for u in range(n_heads):
                kme_ref[u // 2, (u % 2) * nbp + n:(u % 2) * nbp + n + 1, :] = jnp.where(
                    head_lanes[u % 2], mean[:, pair_lanes(u)], 0.0)
            vt_ref[:, rows] = v_ref[0, rows, :].astype(F32).T.astype(BF16)

    scale = HD ** -0.5 * math.log2(math.e)
    q_t = [q_ref[0, :, p * LANES:(p + 1) * LANES].astype(F32).T for p in range(pairs)]
    pair_row = lax.broadcasted_iota(jnp.int32, (LANES, 1), 0)
    head_rows = (pair_row < HD, pair_row >= HD)
    q_aug = [jnp.where(head_rows[u % 2], q_t[u // 2] * scale, 0.0).astype(BF16) for u in range(n_heads)]

    gates = []
    for p in range(pairs):
        km = kme_ref[p]
        k_hi = km.astype(BF16)
        r1 = km - k_hi.astype(F32)
        k_mid = r1.astype(BF16)
        k_lo = (r1 - k_mid.astype(F32)).astype(BF16)
        q16 = q_t[p].astype(BF16)
        gates.append(_dot(k_hi, q16) + _dot(k_mid, q16) + _dot(k_lo, q16))
    blk = lax.broadcasted_iota(jnp.int32, (nbp, 1), 0).astype(F32)
    valid = blk < i.astype(F32)
    for u in range(n_heads):
        g = jnp.where(valid, gates[u // 2][(u % 2) * nbp:(u % 2 + 1) * nbp], -jnp.inf)
        sel = jnp.zeros(g.shape, jnp.bool_)
        for _ in range(MOBA_TOPK):
            mx = jnp.max(g, axis=0, keepdims=True)
            idx = jnp.min(jnp.where(g == mx, blk, float(1 << 20)), axis=0, keepdims=True)
            pick = blk == idx
            sel = sel | pick
            g = jnp.where(pick, -jnp.inf, g)
        sel_ref[u, :nbp, :] = jnp.where(sel & valid, 1.0, 0.0)
        sel_ref[u, nbp:, :] = jnp.zeros((8, BS), F32)
        m_ref[u] = jnp.full((1, BS), NEG_BIG, F32)
        acc_ref[u] = jnp.zeros(acc_ref.shape[1:], F32)

    krow = lax.broadcasted_iota(jnp.int32, (BS, BS), 0)
    qcol = lax.broadcasted_iota(jnp.int32, (BS, BS), 1)
    PVR = HD + BF16_SUBLANES
    pv_rows = (slice(0, PVR), slice(LANES - PVR, LANES))
    pv_row = lax.broadcasted_iota(jnp.int32, (PVR, 1), 0)
    is_dim = (pv_row < HD, pv_row >= PVR - HD)

    def block_start(j):
        return pl.multiple_of(jnp.minimum(j, i) * BS, BS)

    def produce(g, slot):
        for t in range(unroll):
            rows = pl.ds(block_start(g * unroll + t), BS)
            for u in range(n_heads):
                s_ref[slot, u, t * BS:(t + 1) * BS, :] = _dot(
                    k_ref[0, rows, pair_lanes(u)], q_aug[u]).astype(BF16)

    def softmax_update(sts, sels, starts):
        heads = range(n_heads)
        m_new, alpha = [], []
        for u in heads:
            cand = jnp.full((1, BS), NEG_BIG, F32)
            for st, sel in zip(sts[u], sels[u]):
                mx = jnp.max(st.reshape(BS // BF16_SUBLANES, BF16_SUBLANES, BS), axis=0)
                mx = jnp.max(mx.astype(F32), axis=0, keepdims=True)
                cand = jnp.maximum(cand, mx if sel is None else jnp.where(sel, mx, NEG_BIG))
            m_old = m_ref[u]
            m_new.append(jnp.maximum(m_old, cand))
            alpha.append(jnp.exp2(m_old - m_new[u]))
            m_ref[u] = m_new[u]
        ps = []
        for u in heads:
            pu = []
            for st, sel in zip(sts[u], sels[u]):
                sub = m_new[u] if sel is None else jnp.where(sel, m_new[u], -NEG_BIG)
                pu.append(jnp.exp2(st - sub.astype(BF16)))
            ps.append(pu[0] if len(pu) == 1 else jnp.concatenate(pu, axis=0))
        pv = []
        for u in heads:
            rows = slice((u // 2) * LANES + pv_rows[u % 2].start, (u // 2) * LANES + pv_rows[u % 2].stop)
            vts = [jnp.where(is_dim[u % 2], vt_ref[rows, pl.ds(st0, BS)], jnp.ones((), BF16))
                   for st0 in starts]
            pv.append(_dot(vts[0] if len(vts) == 1 else jnp.concatenate(vts, axis=1), ps[u]))
        for u in heads:
            acc_ref[u] = acc_ref[u] * alpha[u] + pv[u]

    def consume(g, slot):
        js = [g * unroll + t for t in range(unroll)]
        sts = [[s_ref[slot, u, t * BS:(t + 1) * BS, :] for t in range(unroll)] for u in range(n_heads)]
        sels = [[sel_ref[u, pl.ds(j, 1), :] > 0.5 for j in js] for u in range(n_heads)]
        softmax_update(sts, sels, [block_start(j) for j in js])

    slots = s_ref.shape[0]
    assert unroll == 1, "the group after the last past block must be exactly the tile's own block"

    def body(gg, c):
        for t in range(slots):
            produce(slots * gg + t + 1, (t + 1) % slots)
            consume(slots * gg + t, t)
        return c

    n_groups = (i + unroll - 1) // unroll
    produce(0, 0)
    lax.fori_loop(0, (n_groups + slots - 1) // slots, body, 0)
    softmax_update([[jnp.where(krow <= qcol, s_ref[0, u, :BS, :], -jnp.inf)] for u in range(n_heads)],
                   [[None]] * n_heads, [block_start(i)])

    outs = []
    for u in range(n_heads):
        a = acc_ref[u]
        outs.append(a[:HD] / a[HD:HD + 1, :] if u % 2 == 0 else a[PVR - HD:] / a[0:1, :])
    o_ref[0] = jnp.concatenate(outs, axis=0).T.astype(o_ref.dtype)


def moba_attention(proj, B, S):
    nb = S // MOBA_BLOCK
    nbp = -(-nb // 8) * 8
    pairs = 4
    width = pairs * LANES
    groups = A_WIDTH // width
    unroll = 1
    slots = 3
    n_heads = 2 * pairs
    return pl.pallas_call(
        functools.partial(_moba_kernel, nb=nb, nbp=nbp, unroll=unroll, pairs=pairs),
        grid=(B, groups, nb),
        in_specs=[pl.BlockSpec((1, MOBA_BLOCK, width), lambda b, p, i: (b, i, p)),
                  pl.BlockSpec((1, S, width), lambda b, p, i: (b, 0, groups + p), pipeline_mode=pl.Buffered(1)),
                  pl.BlockSpec((1, S, width), lambda b, p, i: (b, 0, 2 * groups + p), pipeline_mode=pl.Buffered(1))],
        out_specs=pl.BlockSpec((1, MOBA_BLOCK, width), lambda b, p, i: (b, i, p)),
        out_shape=jax.ShapeDtypeStruct((B, S, A_WIDTH), BF16),
        scratch_shapes=[pltpu.VMEM((pairs, 2 * nbp, LANES), F32),
                        pltpu.VMEM((width, S), BF16),
                        pltpu.VMEM((n_heads, nbp + 8, MOBA_BLOCK), F32),
                        pltpu.VMEM((n_heads, 1, MOBA_BLOCK), F32),
                        pltpu.VMEM((n_heads, MOBA_HEAD_DIM + BF16_SUBLANES, MOBA_BLOCK), F32),
                        pltpu.VMEM((slots, n_heads, unroll * MOBA_BLOCK, MOBA_BLOCK), BF16)],
        compiler_params=_cparams("parallel", "parallel", "arbitrary"),
    )(proj, proj, proj)


def _pool_kernel(p_ref, halo_ref, w_ref, sc_ref, o_ref, *, blocks_per_seq):
    bm = p_ref.shape[0]
    H = BF16_SUBLANES
    i = pl.program_id(0)
    first = (i % blocks_per_seq) == 0
    t1 = (lax.broadcasted_iota(jnp.int32, (bm, 1), 0) + (i % blocks_per_seq) * bm + 1).astype(F32)
    for g, w in enumerate(POOL_WINDOWS):
        cols = slice(g * POOL_GROUP, (g + 1) * POOL_GROUP)
        cur = p_ref[:, cols].astype(F32)
        halo = jnp.where(first, 0.0, halo_ref[:, cols].astype(F32))
        ext = jnp.concatenate([halo, cur], axis=0)
        acc = ext
        sh = 1
        while sh < w:
            acc = acc + pltpu.roll(acc, sh, axis=0)
            sh *= 2
        win = acc[H:, :]
        pooled = win / jnp.minimum(t1, float(w)) - cur
        y = _dot(pooled.astype(BF16), w_ref[g])
        o_ref[:, cols] = (y * sc_ref[:, cols]).astype(o_ref.dtype)


def multiscale_pool(proj, pool_w, pool_scale, M, S, bm):
    H = BF16_SUBLANES
    pcol = 3 * A_WIDTH // POOL_WIDTH
    return pl.pallas_call(
        functools.partial(_pool_kernel, blocks_per_seq=S // bm),
        grid=(M // bm,),
        in_specs=[pl.BlockSpec((bm, POOL_WIDTH), lambda i: (i, pcol)),
                  pl.BlockSpec((H, POOL_WIDTH), lambda i: (jnp.maximum(i * (bm // H) - 1, 0), pcol)),
                  pl.BlockSpec((len(POOL_WINDOWS), POOL_GROUP, POOL_GROUP), lambda i: (0, 0, 0)),
                  pl.BlockSpec((1, POOL_WIDTH), lambda i: (0, 0))],
        out_specs=pl.BlockSpec((bm, POOL_WIDTH), lambda i: (i, 0)),
        out_shape=jax.ShapeDtypeStruct((M, POOL_WIDTH), BF16),
        compiler_params=_cparams("parallel"),
    )(proj, proj, pool_w, pool_scale.reshape(1, POOL_WIDTH))


def _mix_xattn_kernel(h_ref, a_ref, b_ref, wm_ref, g_ref, wq_ref, k_ref, v_ref, wo_ref, o_ref):
    ka = a_ref.shape[1]
    h = h_ref[...] + _dot(a_ref[...], wm_ref[:ka, :]) + _dot(b_ref[...], wm_ref[ka:, :])
    D = h.shape[1]
    hd = D // XATTN_HEADS
    xn = _rmsnorm(h, g_ref[...]).astype(BF16)
    q = (_dot(xn, wq_ref[...]) * hd ** -0.5).astype(BF16)
    outs = []
    for hh in range(XATTN_HEADS):
        cols = slice(hh * hd, (hh + 1) * hd)
        s = _dot_nt(q[:, cols], k_ref[:, cols])
        m = jnp.max(s, axis=1, keepdims=True)
        p = jnp.exp(s - m)
        l = jnp.sum(p, axis=1, keepdims=True)
        outs.append((_dot(p.astype(BF16), v_ref[:, cols]) / l).astype(BF16))
    o = jnp.concatenate(outs, axis=1)
    o_ref[...] = h + _dot(o, wo_ref[...])


def mix_xattn_residual(h, a, b, w_mix, g, wq, kv, wo, layer, S, n_mem, bm):
    M, D = h.shape
    ka, kb = a.shape[1], b.shape[1]
    bps = S // bm
    return pl.pallas_call(
        _mix_xattn_kernel,
        grid=(M // bm,),
        in_specs=[pl.BlockSpec((bm, D), lambda i: (i, 0)),
                  pl.BlockSpec((bm, ka), lambda i: (i, 0)),
                  pl.BlockSpec((bm, kb), lambda i: (i, 0)),
                  _resident(w_mix),
                  pl.BlockSpec((1, D), lambda i: (0, 0)),
                  _resident(wq, layer),
                  pl.BlockSpec((n_mem, D), lambda i: (i // bps, 0)),
                  pl.BlockSpec((n_mem, D), lambda i: (i // bps, 1)),
                  _resident(wo, layer)],
        out_specs=pl.BlockSpec((bm, D), lambda i: (i, 0)),
        out_shape=jax.ShapeDtypeStruct((M, D), F32),
        compiler_params=_cparams("parallel"),
    )(h, a, b, w_mix, g.reshape(1, D), wq, kv, kv, wo)


def _ffn_kernel(h_ref, halo_ref, g_ref, wup_ref, cw_ref, wd_ref, fg_ref, o_ref,
                xn_ref, acc_ref, y_ref, *, blocks_per_seq, final_norm, sub):
    H = BF16_SUBLANES
    nc, cf = wd_ref.shape[0], wd_ref.shape[1]
    n_sub = acc_ref.shape[0] // sub

    first = (pl.program_id(0) % blocks_per_seq) == 0
    xn_ref[:H, :] = jnp.where(first, 0.0, _rmsnorm(halo_ref[...], g_ref[...])).astype(BF16)
    xn_ref[H:, :] = _rmsnorm(h_ref[...], g_ref[...]).astype(BF16)
    acc_ref[...] = jnp.zeros_like(acc_ref)

    chunk_cols = lambda c: pl.ds(pl.multiple_of(c * cf, cf), cf)

    def up(c, r):
        if r == 0:
            xs, dst = xn_ref[:sub + H, :], slice(0, sub + H)
        else:
            xs, dst = xn_ref[H + r * sub:H + (r + 1) * sub, :], slice(H, sub + H)
            y_ref[r % 2, :H, :] = y_ref[(r - 1) % 2, sub:sub + H, :]
        y_ref[r % 2, dst, :cf] = _dot(xs, wup_ref[:, chunk_cols(c)])
        y_ref[r % 2, dst, cf:] = _dot(xs, wup_ref[:, chunk_cols(nc + c)])

    def conv(r, part, cw):
        cols = slice(part * cf, (part + 1) * cf)
        out = y_ref[r % 2, H:, cols] * cw[FFN_CONV - 1:FFN_CONV, :]
        for k in range(1, FFN_CONV):
            out = out + y_ref[r % 2, H - k:H - k + sub, cols] * cw[FFN_CONV - 1 - k:FFN_CONV - k, :]
        return out

    def chunk(c, carry):
        cwg, cwu = cw_ref[:, chunk_cols(c)], cw_ref[:, chunk_cols(nc + c)]
        wd = wd_ref[c]
        for r in range(n_sub):
            if r + 1 < n_sub:
                up(c, r + 1)
            else:
                up(jnp.minimum(c + 1, nc - 1), 0)
            act = _silu(conv(r, 0, cwg)) * conv(r, 1, cwu)
            acc_ref[r * sub:(r + 1) * sub, :] += _dot(act.astype(BF16), wd)
        return carry

    up(0, 0)
    lax.fori_loop(0, nc, chunk, 0)
    y = h_ref[...] + acc_ref[...]
    if final_norm:
        y = _rmsnorm(y, fg_ref[...])
    o_ref[...] = y


def ffn_residual(h, g, w_up, conv_w, w_down, layer, final_g, S, bm, cf, final_norm):
    M, D = h.shape
    d_ff = w_down.shape[1]
    H = BF16_SUBLANES
    nc = d_ff // cf
    sub = min(256, bm)
    assert (bm // sub) % 2 == 0, "the two y_ref slots alternate per sub-block across chunks"
    wd3 = w_down.reshape(-1, nc, cf, D)
    return pl.pallas_call(
        functools.partial(_ffn_kernel, blocks_per_seq=S // bm, final_norm=final_norm, sub=sub),
        grid=(M // bm,),
        in_specs=[pl.BlockSpec((bm, D), lambda i: (i, 0)),
                  pl.BlockSpec((H, D), lambda i: (jnp.maximum(i * (bm // H) - 1, 0), 0)),
                  pl.BlockSpec((1, D), lambda i: (0, 0)),
                  _resident(w_up, layer), _resident(conv_w, layer), _resident(wd3, layer),
                  pl.BlockSpec((1, D), lambda i: (0, 0))],
        out_specs=pl.BlockSpec((bm, D), lambda i: (i, 0)),
        out_shape=jax.ShapeDtypeStruct((M, D), F32),
        scratch_shapes=[pltpu.VMEM((H + bm, D), BF16), pltpu.VMEM((bm, D), F32),
                        pltpu.VMEM((2, H + sub, 2 * cf), F32)],
        compiler_params=_cparams("parallel"),
    )(h, h, g.reshape(1, D), w_up, conv_w, wd3, final_g.reshape(1, D))


def _sgu_kernel(u_ref, v_ref, lg_ref, lb_ref, w_ref, bt_ref, o_ref):
    rows = u_ref.shape[0]
    T = SGU_CHUNK
    v = v_ref[...].astype(F32)
    mu = jnp.mean(v, axis=-1, keepdims=True)
    d = v - mu
    var = jnp.mean(d * d, axis=-1, keepdims=True)
    vn = (d * lax.rsqrt(var + EPS) * lg_ref[...] + lb_ref[...]).astype(BF16)
    causal = (lax.broadcasted_iota(jnp.int32, (T, T), 1) <= lax.broadcasted_iota(jnp.int32, (T, T), 0))
    for g in range(SGU_GROUPS):
        cols = slice(g * SGU_GROUP, (g + 1) * SGU_GROUP)
        wg = jnp.where(causal, w_ref[g], 0.0).astype(BF16)
        bias = bt_ref[:, g:g + 1]
        for c in range(rows // T):
            rs = slice(c * T, (c + 1) * T)
            s = _dot(wg, vn[rs, cols]) + bias
            o_ref[rs, cols] = (u_ref[rs, cols].astype(F32) * s).astype(o_ref.dtype)


def spatial_gating(proj, ln_g, ln_b, w_s, b_s, M, rows):
    return pl.pallas_call(
        _sgu_kernel,
        grid=(M // rows,),
        in_specs=[pl.BlockSpec((rows, SGU_WIDTH), lambda i: (i, 0)),
                  pl.BlockSpec((rows, SGU_WIDTH), lambda i: (i, 1)),
                  pl.BlockSpec((1, SGU_WIDTH), lambda i: (0, 0)),
                  pl.BlockSpec((1, SGU_WIDTH), lambda i: (0, 0)),
                  pl.BlockSpec((SGU_GROUPS, SGU_CHUNK, SGU_CHUNK), lambda i: (0, 0, 0)),
                  pl.BlockSpec((SGU_CHUNK, SGU_GROUPS), lambda i: (0, 0))],
        out_specs=pl.BlockSpec((rows, SGU_WIDTH), lambda i: (i, 0)),
        out_shape=jax.ShapeDtypeStruct((M, SGU_WIDTH), BF16),
        compiler_params=_cparams("parallel"),
    )(proj, proj, ln_g.reshape(1, -1), ln_b.reshape(1, -1), w_s, b_s.T)


def _gdn_intra_kernel(q_ref, k_ref, v_ref, tail_ref, tailt_ref, alog_ref,
                      dtb_ref, u_ref, w_ref, qd_ref, kd_ref, qk_ref, gl_ref):
    rows = q_ref.shape[0]
    C = DN_CHUNK
    HD = DN_HEAD_DIM
    x = jnp.concatenate([q_ref[...], k_ref[...], v_ref[...]], axis=1).astype(F32)

    ii = lax.broadcasted_iota(jnp.int32, (C, C), 0)
    jj = lax.broadcasted_iota(jnp.int32, (C, C), 1)
    lower = jj <= ii
    strict = jj < ii
    su = lax.broadcasted_iota(jnp.int32, (C, LANES), 0)
    ju = lax.broadcasted_iota(jnp.int32, (C, LANES), 1)
    upper_ext = jnp.where(((ju < C) & (su > ju)) | (ju == C), 1.0, 0.0)

    inst = [(c, hh) for hh in range(DN_HEADS) for c in range(rows // C)]
    qs, ks, vs, bs, stacks = [], [], [], [], []
    for hh in range(DN_HEADS):
        qh = x[:, hh * HD:(hh + 1) * HD]
        kh = x[:, DN_WIDTH + hh * HD:DN_WIDTH + (hh + 1) * HD]
        vh = x[:, 2 * DN_WIDTH + hh * HD:2 * DN_WIDTH + (hh + 1) * HD]
        qh = qh * lax.rsqrt(jnp.sum(qh * qh, axis=-1, keepdims=True) + EPS) * HD ** -0.5
        kh = kh * lax.rsqrt(jnp.sum(kh * kh, axis=-1, keepdims=True) + EPS)
        beta = 1.0 / (1.0 + jnp.exp(-tail_ref[:, hh:hh + 1]))
        a_raw = tailt_ref[DN_HEADS + hh:DN_HEADS + hh + 1, :]
        z = a_raw + dtb_ref[0:1, hh:hh + 1]
        softplus = jnp.maximum(z, 0.0) + jnp.log(1.0 + jnp.exp(-jnp.abs(z)))
        g_row = -jnp.exp(alog_ref[0:1, hh:hh + 1]) * softplus
        for c in range(rows // C):
            rs = slice(c * C, (c + 1) * C)
            qs.append(qh[rs]); ks.append(kh[rs]); vs.append(vh[rs]); bs.append(beta[rs])
            gr = jnp.broadcast_to(g_row[:, rs], (C, C))
            stacks += [jnp.where(lower, gr, 0.0), jnp.where(lower, 0.0, gr)]

    stacked = jnp.concatenate(stacks, axis=0)
    s_hi = stacked.astype(BF16)
    s_lo = (stacked - s_hi.astype(F32)).astype(BF16)
    ue = upper_ext.astype(BF16)
    dall = _dot(s_hi, ue) + _dot(s_lo, ue)

    decays, gcs, gc_revs, k16s, kbs = [], [], [], [], []
    for n, (c, hh) in enumerate(inst):
        dext = dall[n * 2 * C:(n + 1) * 2 * C]
        decays.append(jnp.exp(jnp.where(lower, dext[:C, :C], -jnp.inf)))
        gcs.append(dext[:C, C:C + 1])
        gc_revs.append(dext[C:, C:C + 1])
        kbs.append(ks[n] * bs[n])
        k16s.append(ks[n].astype(BF16))
    kq = [_dot_nt(jnp.concatenate([kbs[n], qs[n]], axis=0).astype(BF16), k16s[n]) for n in range(len(inst))]
    kk = [m[:C] for m in kq]
    qk = [m[C:] for m in kq]
    pws = [jnp.where(strict, kk[n] * decays[n], 0.0).astype(BF16) for n in range(len(inst))]
    egc = [jnp.exp(g) for g in gcs]
    rhs = [jnp.concatenate([vs[n] * bs[n], kbs[n] * egc[n]], axis=1) for n in range(len(inst))]
    sols = [rhs[n] - _dot(pws[n], rhs[n].astype(BF16)) for n in range(len(inst))]
    for _ in range(int(math.log2(C)) - 1):
        pws = [_dot(p, p).astype(BF16) for p in pws]
        sols = [s + _dot(p, s.astype(BF16)) for p, s in zip(pws, sols)]
    for n, (c, hh) in enumerate(inst):
        rs = slice(c * C, (c + 1) * C)
        hcols = slice(hh * HD, (hh + 1) * HD)
        u_ref[rs, hcols] = sols[n][:, :HD].astype(u_ref.dtype)
        w_ref[rs, hcols] = sols[n][:, HD:].astype(w_ref.dtype)
        qkd = jnp.where(lower, qk[n] * decays[n], 0.0)
        qk_ref[rs, hcols] = jnp.concatenate([qkd, jnp.zeros_like(qkd)], axis=1).astype(qk_ref.dtype)
        qd_ref[rs, hcols] = (qs[n] * egc[n]).astype(qd_ref.dtype)
        kd_ref[rs, hcols] = (ks[n] * jnp.exp(gc_revs[n])).astype(kd_ref.dtype)
        gl_ref[c * 8:(c + 1) * 8, hcols] = jnp.broadcast_to(egc[n][C - 1:C, :], (8, HD))


def gdn_intra(proj, tail, tail_t, a_log, dt_bias, M, rows):
    c0 = 2 * SGU_WIDTH // DN_WIDTH
    pad = lambda p: jnp.pad(p.reshape(1, -1), ((0, 0), (0, LANES - p.shape[0])))
    seq = lambda dt: jax.ShapeDtypeStruct((M, DN_WIDTH), dt)
    row_spec = pl.BlockSpec((rows, DN_WIDTH), lambda i: (i, 0))
    cur_spec = lambda part: pl.BlockSpec((rows, DN_WIDTH), lambda i: (i, c0 + part))
    return pl.pallas_call(
        _gdn_intra_kernel,
        grid=(M // rows,),
        in_specs=[cur_spec(0), cur_spec(1), cur_spec(2),
                  pl.BlockSpec((rows, LANES), lambda i: (i, 0)),
                  pl.BlockSpec((2 * DN_HEADS, rows), lambda i: (0, i)),
                  pl.BlockSpec((1, LANES), lambda i: (0, 0)),
                  pl.BlockSpec((1, LANES), lambda i: (0, 0))],
        out_specs=[row_spec, row_spec, row_spec, row_spec, row_spec,
                   pl.BlockSpec((rows // DN_CHUNK * 8, DN_WIDTH), lambda i: (i, 0))],
        out_shape=[seq(BF16), seq(BF16), seq(BF16), seq(BF16), seq(BF16),
                   jax.ShapeDtypeStruct((M // DN_CHUNK * 8, DN_WIDTH), F32)],
        compiler_params=_cparams("parallel"),
    )(proj, proj, proj, tail, tail_t, pad(a_log), pad(dt_bias))


def _gdn_scan_kernel(u_ref, w_ref, qd_ref, kd_ref, qk_ref, gl_ref, gate_ref, ng_ref, o_ref, state_ref, *,
                     chunks):
    C = DN_CHUNK
    HD = DN_HEAD_DIM
    B = u_ref.shape[0]

    @pl.when(pl.program_id(0) == 0)
    def _():
        state_ref[...] = jnp.zeros_like(state_ref)

    ng = ng_ref[...]
    inst = [(b, hh) for b in range(B) for hh in range(DN_HEADS)]
    col = lambda hh: slice(hh * HD, (hh + 1) * HD)
    states = [state_ref[b, hh] for b, hh in inst]
    for c in range(chunks):
        rs = slice(c * C, (c + 1) * C)
        kdt = [kd_ref[b, rs, col(hh)].astype(F32).T.astype(BF16) for b, hh in inst]
        st16 = [s.astype(BF16) for s in states]
        ws = [_dot(w_ref[b, rs, col(hh)], st16[n]) for n, (b, hh) in enumerate(inst)]
        qs = [_dot(qd_ref[b, rs, col(hh)], st16[n]) for n, (b, hh) in enumerate(inst)]
        vn16 = [(u_ref[b, rs, col(hh)].astype(F32) - ws[n]).astype(BF16) for n, (b, hh) in enumerate(inst)]
        states = [states[n] * gl_ref[b, c * 8:c * 8 + 1, col(hh)] + _dot(kdt[n], vn16[n])
                  for n, (b, hh) in enumerate(inst)]
        for n, (b, hh) in enumerate(inst):
            o = qs[n] + _dot(qk_ref[b, rs, col(hh)][:, :C], vn16[n])
            o = o * lax.rsqrt(jnp.mean(o * o, axis=-1, keepdims=True) + EPS) * ng
            o_ref[b, rs, col(hh)] = (o * _silu(gate_ref[b, rs, col(hh)].astype(F32))).astype(o_ref.dtype)
    for n, (b, hh) in enumerate(inst):
        state_ref[b, hh] = states[n]


def gdn_scan(u, w, qd, kd, qk, gl, proj3, norm_g, B, S, chunks):
    rows = chunks * DN_CHUNK
    r3 = lambda a: a.reshape(B, S, DN_WIDTH)
    gcol = (2 * SGU_WIDTH + 3 * DN_WIDTH) // DN_WIDTH
    seq_spec = pl.BlockSpec((B, rows, DN_WIDTH), lambda n: (0, n, 0))
    return pl.pallas_call(
        functools.partial(_gdn_scan_kernel, chunks=chunks),
        grid=(S // rows,),
        in_specs=[seq_spec, seq_spec, seq_spec, seq_spec, seq_spec,
                  pl.BlockSpec((B, chunks * 8, DN_WIDTH), lambda n: (0, n, 0)),
                  pl.BlockSpec((B, rows, DN_WIDTH), lambda n: (0, n, gcol)),
                  pl.BlockSpec((1, DN_HEAD_DIM), lambda n: (0, 0))],
        out_specs=seq_spec,
        out_shape=jax.ShapeDtypeStruct((B, S, DN_WIDTH), BF16),
        scratch_shapes=[pltpu.VMEM((B, DN_HEADS, DN_HEAD_DIM, DN_HEAD_DIM), F32)],
        compiler_params=_cparams("arbitrary"),
    )(r3(u), r3(w), r3(qd), r3(kd), r3(qk), gl.reshape(B, S // DN_CHUNK * 8, DN_WIDTH), proj3,
      norm_g.reshape(1, DN_HEAD_DIM))


def _tiles(S):
    rows = min(1024, S)
    return dict(rows=rows, sgu_rows=rows, gdn_rows=min(512, S), scan_chunks=min(8, S // DN_CHUNK),
                ffn_chunk=256)


def _forward(x, mem, mem_norm, norm_mix, norm_xattn, norm_ffn, ev_w_in, pool_w, pool_scale, ev_w_out,
             od_w_in, sgu_ln_g, sgu_ln_b, sgu_w, sgu_b, dn_conv, dn_a_log, dn_dt_bias, dn_norm_g,
             od_w_out, xattn_wq, xattn_wkv, xattn_wo, ffn_w_up, ffn_conv, ffn_w_down, final_norm):
    B, S, D = x.shape
    n_mem = mem.shape[1]
    M = B * S
    depth = norm_mix.shape[0]
    bf = lambda a: a.astype(BF16)
    t = _tiles(S)

    h = x.reshape(M, D)
    mem2 = mem.reshape(B * n_mem, D)
    wkv_all, wq_all, wo_all = bf(xattn_wkv), bf(xattn_wq), bf(xattn_wo)
    w_up_all, w_down_all = bf(ffn_w_up), bf(ffn_w_down)
    for layer in range(depth):
        i = layer // 2
        if layer % 2 == 0:
            proj, = norm_matmul(h, norm_mix[layer], [bf(ev_w_in[i])], [BF16], t["rows"])
            a_out = moba_attention(proj.reshape(B, S, -1), B, S).reshape(M, A_WIDTH)
            b_out = multiscale_pool(proj, bf(pool_w[i]), pool_scale[i], M, S, t["rows"])
            mix_a, mix_b, w_mix = a_out, b_out, ev_w_out[i]
        else:
            main_w = 2 * SGU_WIDTH + 4 * DN_WIDTH
            w_in = od_w_in[i]
            w_tail = jnp.pad(w_in[:, main_w:], ((0, 0), (0, LANES - 2 * DN_HEADS)))
            proj, tail, tail_t = od_projection(h, norm_mix[layer], bf(w_in), bf(w_tail), dn_conv[i],
                                               S, t["rows"])
            c_out = spatial_gating(proj, sgu_ln_g[i], sgu_ln_b[i], sgu_w[i], sgu_b[i], M, t["sgu_rows"])
            u, w, qd, kd, qk, gl = gdn_intra(proj, tail, tail_t, dn_a_log[i], dn_dt_bias[i], M, t["gdn_rows"])
            d_out = gdn_scan(u, w, qd, kd, qk, gl, proj.reshape(B, S, -1), dn_norm_g[i], B, S,
                             t["scan_chunks"])
            mix_a, mix_b, w_mix = c_out, d_out.reshape(M, DN_WIDTH), od_w_out[i]
        kv, = norm_matmul(mem2, mem_norm, [wkv_all], [BF16], B * n_mem, layer=layer)
        h = mix_xattn_residual(h, mix_a, mix_b, bf(w_mix), norm_xattn[layer], wq_all, kv, wo_all, layer,
                               S, n_mem, t["rows"])
        h = ffn_residual(h, norm_ffn[layer], w_up_all, ffn_conv, w_down_all, layer, final_norm, S, t["rows"],
                         t["ffn_chunk"], final_norm=(layer == depth - 1))
    return h.reshape(B, S, D)


def kernel(x, mem, mem_norm, norm_mix, norm_xattn, norm_ffn, ev_w_in, pool_w, pool_scale, ev_w_out, od_w_in, sgu_ln_g, sgu_ln_b, sgu_w, sgu_b, dn_conv, dn_a_log, dn_dt_bias, dn_norm_g, od_w_out, xattn_wq, xattn_wkv, xattn_wo, ffn_w_up, ffn_conv, ffn_w_down, final_norm):
    return _forward(x, mem, mem_norm, norm_mix, norm_xattn, norm_ffn, ev_w_in, pool_w, pool_scale, ev_w_out,
                    od_w_in, sgu_ln_g, sgu_ln_b, sgu_w, sgu_b, dn_conv, dn_a_log, dn_dt_bias, dn_norm_g,
                    od_w_out, xattn_wq, xattn_wkv, xattn_wo, ffn_w_up, ffn_conv, ffn_w_down, final_norm)
```

```python
import functools
import math

import jax
import jax.numpy as jnp
from jax import lax
from jax.experimental import pallas as pl
from jax.experimental.pallas import tpu as pltpu

F32 = jnp.float32
BF16 = jnp.bfloat16
EPS = 1e-6
NEG_BIG = -1e30

VMEM_LIMIT_BYTES = 48 * 1024 * 1024
BF16_SUBLANES = 16
LANES = 128

MOBA_HEADS, MOBA_HEAD_DIM, MOBA_BLOCK, MOBA_TOPK = 8, 64, 256, 3
A_WIDTH = MOBA_HEADS * MOBA_HEAD_DIM
POOL_WINDOWS = (2, 4, 8, 16)
POOL_GROUP = 128
POOL_WIDTH = POOL_GROUP * len(POOL_WINDOWS)
SGU_GROUPS, SGU_GROUP, SGU_CHUNK = 4, 128, 128
SGU_WIDTH = SGU_GROUPS * SGU_GROUP
DN_HEADS, DN_HEAD_DIM, DN_CONV, DN_CHUNK = 4, 128, 4, 64
DN_WIDTH = DN_HEADS * DN_HEAD_DIM
XATTN_HEADS = 4
FFN_CONV = 3


def _cparams(*sem):
    return pltpu.CompilerParams(dimension_semantics=sem, vmem_limit_bytes=VMEM_LIMIT_BYTES)


def _rmsnorm(x, g):
    return x * lax.rsqrt(jnp.mean(x * x, axis=-1, keepdims=True) + EPS) * g


def _silu(x):
    return x * (0.5 * jnp.tanh(0.5 * x) + 0.5)


def _dot(a, b):
    return jnp.dot(a, b, preferred_element_type=F32)


def _resident(arr, layer=None):
    if layer is None:
        return pl.BlockSpec(arr.shape, lambda i: (0,) * arr.ndim, pipeline_mode=pl.Buffered(1))
    return pl.BlockSpec((None,) + arr.shape[1:], lambda i: (layer,) + (0,) * (arr.ndim - 1),
                        pipeline_mode=pl.Buffered(1))


def _dot_nt(a, b, precision=None):
    return lax.dot_general(a, b, (((1,), (1,)), ((), ())), preferred_element_type=F32,
                           precision=precision)


def _norm_mm_kernel(x_ref, g_ref, *refs, bn):
    n = len(refs) // 2
    xn = _rmsnorm(x_ref[...], g_ref[...]).astype(BF16)
    for w_ref, o_ref in zip(refs[:n], refs[n:]):
        N = w_ref.shape[1]
        for c0 in range(0, N, bn):
            c1 = min(c0 + bn, N)
            o_ref[:, c0:c1] = _dot(xn, w_ref[:, c0:c1]).astype(o_ref.dtype)


def norm_matmul(x, g, ws, out_dtypes, bm, bn=512, layer=None):
    M, D = x.shape
    return pl.pallas_call(
        functools.partial(_norm_mm_kernel, bn=bn),
        grid=(M // bm,),
        in_specs=[pl.BlockSpec((bm, D), lambda i: (i, 0)),
                  pl.BlockSpec((1, D), lambda i: (0, 0))]
                 + [_resident(w, layer) for w in ws],
        out_specs=[pl.BlockSpec((bm, w.shape[-1]), lambda i: (i, 0)) for w in ws],
        out_shape=[jax.ShapeDtypeStruct((M, w.shape[-1]), dt) for w, dt in zip(ws, out_dtypes)],
        compiler_params=_cparams("parallel"),
    )(x, g.reshape(1, D), *ws)


def _gelu_tanh(x):
    return 0.5 * x * (1.0 + jnp.tanh(math.sqrt(2.0 / math.pi) * (x + 0.044715 * (x * x * x))))


def _od_proj_kernel(x_ref, halo_ref, g_ref, w_ref, wt_ref, cw_ref, o_ref, t_ref, tt_ref, xe_ref, *,
                    blocks_per_seq, bn):
    bm = x_ref.shape[0]
    H = BF16_SUBLANES
    z_w, qkv_w = 2 * SGU_WIDTH, 3 * DN_WIDTH
    first = (pl.program_id(0) % blocks_per_seq) == 0
    xe_ref[:H, :] = jnp.where(first, 0.0, _rmsnorm(halo_ref[...], g_ref[...])).astype(BF16)
    xe_ref[H:, :] = _rmsnorm(x_ref[...], g_ref[...]).astype(BF16)
    xn = xe_ref[H:, :]
    tail = _dot(xn, wt_ref[...])
    t_ref[...] = tail
    tt_ref[...] = tail.T[:H, :]
    for c0 in range(0, o_ref.shape[1], bn):
        cols = slice(c0, c0 + bn)
        if c0 < z_w:
            o_ref[:, cols] = _gelu_tanh(_dot(xn, w_ref[:, cols])).astype(o_ref.dtype)
        elif c0 < z_w + qkv_w:
            y = _dot(xe_ref[...], w_ref[:, cols])
            cw = cw_ref[:, c0 - z_w:c0 - z_w + bn]
            out = y[H:, :] * cw[DN_CONV - 1:DN_CONV, :]
            for k in range(1, DN_CONV):
                out = out + pltpu.roll(y, k, axis=0)[H:, :] * cw[DN_CONV - 1 - k:DN_CONV - k, :]
            o_ref[:, cols] = _silu(out).astype(o_ref.dtype)
        else:
            o_ref[:, cols] = _dot(xn, w_ref[:, cols]).astype(o_ref.dtype)


def od_projection(x, g, w, w_tail, conv_w, S, bm, bn=512):
    M, D = x.shape
    N = 2 * SGU_WIDTH + 4 * DN_WIDTH
    H = BF16_SUBLANES
    return pl.pallas_call(
        functools.partial(_od_proj_kernel, blocks_per_seq=S // bm, bn=bn),
        grid=(M // bm,),
        in_specs=[pl.BlockSpec((bm, D), lambda i: (i, 0)),
                  pl.BlockSpec((H, D), lambda i: (jnp.maximum(i * (bm // H) - 1, 0), 0)),
                  pl.BlockSpec((1, D), lambda i: (0, 0)),
                  _resident(w), _resident(w_tail), _resident(conv_w)],
        out_specs=[pl.BlockSpec((bm, N), lambda i: (i, 0)), pl.BlockSpec((bm, LANES), lambda i: (i, 0)),
                   pl.BlockSpec((H, bm), lambda i: (0, i))],
        out_shape=[jax.ShapeDtypeStruct((M, N), BF16), jax.ShapeDtypeStruct((M, LANES), F32),
                   jax.ShapeDtypeStruct((H, M), F32)],
        scratch_shapes=[pltpu.VMEM((H + bm, D), BF16)],
        compiler_params=_cparams("parallel"),
    )(x, x, g.reshape(1, D), w, w_tail, conv_w)


def _moba_kernel(q_ref, k_ref, v_ref, o_ref, kme_ref, vt_ref, sel_ref, m_ref, acc_ref, s_ref, *,
                 nb, nbp, unroll, pairs):
    BS = MOBA_BLOCK
    HD = MOBA_HEAD_DIM
    n_heads = 2 * pairs
    i = pl.program_id(2)
    lane = lax.broadcasted_iota(jnp.int32, (1, LANES), 1)
    head_lanes = (lane < HD, lane >= HD)
    pair_lanes = lambda u: slice((u // 2) * LANES, (u // 2 + 1) * LANES)

    @pl.when(i == 0)
    def _():
        kme_ref[...] = jnp.zeros_like(kme_ref)
        for n in range(nb):
            rows = slice(n * BS, (n + 1) * BS)
            mean = jnp.sum(k_ref[0, rows, :].astype(F32), axis=0, keepdims=True) / BS
            for u in range(n_heads):
                kme_ref[u // 2, (u % 2) * nbp + n:(u % 2) * nbp + n + 1, :] = jnp.where(
                    head_lanes[u % 2], mean[:, pair_lanes(u)], 0.0)
            vt_ref[:, rows] = v_ref[0, rows, :].astype(F32).T.astype(BF16)

    scale = HD ** -0.5 * math.log2(math.e)
    q_t = [q_ref[0, :, p * LANES:(p + 1) * LANES].astype(F32).T for p in range(pairs)]
    pair_row = lax.broadcasted_iota(jnp.int32, (LANES, 1), 0)
    head_rows = (pair_row < HD, pair_row >= HD)
    q_aug = [jnp.where(head_rows[u % 2], q_t[u // 2] * scale, 0.0).astype(BF16) for u in range(n_heads)]

    gates = []
    for p in range(pairs):
        km = kme_ref[p]
        k_hi = km.astype(BF16)
        r1 = km - k_hi.astype(F32)
        k_mid = r1.astype(BF16)
        k_lo = (r1 - k_mid.astype(F32)).astype(BF16)
        q16 = q_t[p].astype(BF16)
        gates.append(_dot(k_hi, q16) + _dot(k_mid, q16) + _dot(k_lo, q16))
    blk = lax.broadcasted_iota(jnp.int32, (nbp, 1), 0).astype(F32)
    valid = blk < i.astype(F32)
    for u in range(n_heads):
        g = jnp.where(valid, gates[u // 2][(u % 2) * nbp:(u % 2 + 1) * nbp], -jnp.inf)
        sel = jnp.zeros(g.shape, jnp.bool_)
        for _ in range(MOBA_TOPK):
            mx = jnp.max(g, axis=0, keepdims=True)
            idx = jnp.min(jnp.where(g == mx, blk, float(1 << 20)), axis=0, keepdims=True)
            pick = blk == idx
            sel = sel | pick
            g = jnp.where(pick, -jnp.inf, g)
        sel_ref[u, :nbp, :] = jnp.where(sel & valid, 1.0, 0.0)
        sel_ref[u, nbp:, :] = jnp.zeros((8, BS), F32)
        m_ref[u] = jnp.full((1, BS), NEG_BIG, F32)
        acc_ref[u] = jnp.zeros(acc_ref.shape[1:], F32)

    krow = lax.broadcasted_iota(jnp.int32, (BS, BS), 0)
    qcol = lax.broadcasted_iota(jnp.int32, (BS, BS), 1)
    PVR = HD + BF16_SUBLANES
    pv_rows = (slice(0, PVR), slice(LANES - PVR, LANES))
    pv_row = lax.broadcasted_iota(jnp.int32, (PVR, 1), 0)
    is_dim = (pv_row < HD, pv_row >= PVR - HD)

    def block_start(j):
        return pl.multiple_of(jnp.minimum(j, i) * BS, BS)

    def produce(g, slot):
        for t in range(unroll):
            rows = pl.ds(block_start(g * unroll + t), BS)
            for u in range(n_heads):
                s_ref[slot, u, t * BS:(t + 1) * BS, :] = _dot(
                    k_ref[0, rows, pair_lanes(u)], q_aug[u]).astype(BF16)

    def softmax_update(sts, sels, starts):
        heads = range(n_heads)
        m_new, alpha = [], []
        for u in heads:
            cand = jnp.full((1, BS), NEG_BIG, F32)
            for st, sel in zip(sts[u], sels[u]):
                mx = jnp.max(st.reshape(BS // BF16_SUBLANES, BF16_SUBLANES, BS), axis=0)
                mx = jnp.max(mx.astype(F32), axis=0, keepdims=True)
                cand = jnp.maximum(cand, mx if sel is None else jnp.where(sel, mx, NEG_BIG))
            m_old = m_ref[u]
            m_new.append(jnp.maximum(m_old, cand))
            alpha.append(jnp.exp2(m_old - m_new[u]))
            m_ref[u] = m_new[u]
        ps = []
        for u in heads:
            pu = []
            for st, sel in zip(sts[u], sels[u]):
                sub = m_new[u] if sel is None else jnp.where(sel, m_new[u], -NEG_BIG)
                pu.append(jnp.exp2(st - sub.astype(BF16)))
            ps.append(pu[0] if len(pu) == 1 else jnp.concatenate(pu, axis=0))
        pv = []
        for u in heads:
            rows = slice((u // 2) * LANES + pv_rows[u % 2].start, (u // 2) * LANES + pv_rows[u % 2].stop)
            vts = [jnp.where(is_dim[u % 2], vt_ref[rows, pl.ds(st0, BS)], jnp.ones((), BF16))
                   for st0 in starts]
            pv.append(_dot(vts[0] if len(vts) == 1 else jnp.concatenate(vts, axis=1), ps[u]))
        for u in heads:
            acc_ref[u] = acc_ref[u] * alpha[u] + pv[u]

    def consume(g, slot):
        js = [g * unroll + t for t in range(unroll)]
        sts = [[s_ref[slot, u, t * BS:(t + 1) * BS, :] for t in range(unroll)] for u in range(n_heads)]
        sels = [[sel_ref[u, pl.ds(j, 1), :] > 0.5 for j in js] for u in range(n_heads)]
        softmax_update(sts, sels, [block_start(j) for j in js])

    slots = s_ref.shape[0]
    assert unroll == 1, "the group after the last past block must be exactly the tile's own block"

    def body(gg, c):
        for t in range(slots):
            produce(slots * gg + t + 1, (t + 1) % slots)
            consume(slots * gg + t, t)
        return c

    n_groups = (i + unroll - 1) // unroll
    produce(0, 0)
    lax.fori_loop(0, (n_groups + slots - 1) // slots, body, 0)
    softmax_update([[jnp.where(krow <= qcol, s_ref[0, u, :BS, :], -jnp.inf)] for u in range(n_heads)],
                   [[None]] * n_heads, [block_start(i)])

    outs = []
    for u in range(n_heads):
        a = acc_ref[u]
        outs.append(a[:HD] / a[HD:HD + 1, :] if u % 2 == 0 else a[PVR - HD:] / a[0:1, :])
    o_ref[0] = jnp.concatenate(outs, axis=0).T.astype(o_ref.dtype)


def moba_attention(proj, B, S):
    nb = S // MOBA_BLOCK
    nbp = -(-nb // 8) * 8
    pairs = 4
    width = pairs * LANES
    groups = A_WIDTH // width
    unroll = 1
    slots = 3
    n_heads = 2 * pairs
    return pl.pallas_call(
        functools.partial(_moba_kernel, nb=nb, nbp=nbp, unroll=unroll, pairs=pairs),
        grid=(B, groups, nb),
        in_specs=[pl.BlockSpec((1, MOBA_BLOCK, width), lambda b, p, i: (b, i, p)),
                  pl.BlockSpec((1, S, width), lambda b, p, i: (b, 0, groups + p), pipeline_mode=pl.Buffered(1)),
                  pl.BlockSpec((1, S, width), lambda b, p, i: (b, 0, 2 * groups + p), pipeline_mode=pl.Buffered(1))],
        out_specs=pl.BlockSpec((1, MOBA_BLOCK, width), lambda b, p, i: (b, i, p)),
        out_shape=jax.ShapeDtypeStruct((B, S, A_WIDTH), BF16),
        scratch_shapes=[pltpu.VMEM((pairs, 2 * nbp, LANES), F32),
                        pltpu.VMEM((width, S), BF16),
                        pltpu.VMEM((n_heads, nbp + 8, MOBA_BLOCK), F32),
                        pltpu.VMEM((n_heads, 1, MOBA_BLOCK), F32),
                        pltpu.VMEM((n_heads, MOBA_HEAD_DIM + BF16_SUBLANES, MOBA_BLOCK), F32),
                        pltpu.VMEM((slots, n_heads, unroll * MOBA_BLOCK, MOBA_BLOCK), BF16)],
        compiler_params=_cparams("parallel", "parallel", "arbitrary"),
    )(proj, proj, proj)


def _pool_kernel(p_ref, halo_ref, w_ref, sc_ref, o_ref, *, blocks_per_seq):
    bm = p_ref.shape[0]
    H = BF16_SUBLANES
    i = pl.program_id(0)
    first = (i % blocks_per_seq) == 0
    t1 = (lax.broadcasted_iota(jnp.int32, (bm, 1), 0) + (i % blocks_per_seq) * bm + 1).astype(F32)
    for g, w in enumerate(POOL_WINDOWS):
        cols = slice(g * POOL_GROUP, (g + 1) * POOL_GROUP)
        cur = p_ref[:, cols].astype(F32)
        halo = jnp.where(first, 0.0, halo_ref[:, cols].astype(F32))
        ext = jnp.concatenate([halo, cur], axis=0)
        acc = ext
        sh = 1
        while sh < w:
            acc = acc + pltpu.roll(acc, sh, axis=0)
            sh *= 2
        win = acc[H:, :]
        pooled = win / jnp.minimum(t1, float(w)) - cur
        y = _dot(pooled.astype(BF16), w_ref[g])
        o_ref[:, cols] = (y * sc_ref[:, cols]).astype(o_ref.dtype)


def multiscale_pool(proj, pool_w, pool_scale, M, S, bm):
    H = BF16_SUBLANES
    pcol = 3 * A_WIDTH // POOL_WIDTH
    return pl.pallas_call(
        functools.partial(_pool_kernel, blocks_per_seq=S // bm),
        grid=(M // bm,),
        in_specs=[pl.BlockSpec((bm, POOL_WIDTH), lambda i: (i, pcol)),
                  pl.BlockSpec((H, POOL_WIDTH), lambda i: (jnp.maximum(i * (bm // H) - 1, 0), pcol)),
                  pl.BlockSpec((len(POOL_WINDOWS), POOL_GROUP, POOL_GROUP), lambda i: (0, 0, 0)),
                  pl.BlockSpec((1, POOL_WIDTH), lambda i: (0, 0))],
        out_specs=pl.BlockSpec((bm, POOL_WIDTH), lambda i: (i, 0)),
        out_shape=jax.ShapeDtypeStruct((M, POOL_WIDTH), BF16),
        compiler_params=_cparams("parallel"),
    )(proj, proj, pool_w, pool_scale.reshape(1, POOL_WIDTH))


def _mix_xattn_kernel(h_ref, a_ref, b_ref, wm_ref, g_ref, wq_ref, k_ref, v_ref, wo_ref, o_ref):
    ka = a_ref.shape[1]
    h = h_ref[...] + _dot(a_ref[...], wm_ref[:ka, :]) + _dot(b_ref[...], wm_ref[ka:, :])
    D = h.shape[1]
    hd = D // XATTN_HEADS
    xn = _rmsnorm(h, g_ref[...]).astype(BF16)
    q = (_dot(xn, wq_ref[...]) * hd ** -0.5).astype(BF16)
    outs = []
    for hh in range(XATTN_HEADS):
        cols = slice(hh * hd, (hh + 1) * hd)
        s = _dot_nt(q[:, cols], k_ref[:, cols])
        m = jnp.max(s, axis=1, keepdims=True)
        p = jnp.exp(s - m)
        l = jnp.sum(p, axis=1, keepdims=True)
        outs.append((_dot(p.astype(BF16), v_ref[:, cols]) / l).astype(BF16))
    o = jnp.concatenate(outs, axis=1)
    o_ref[...] = h + _dot(o, wo_ref[...])


def mix_xattn_residual(h, a, b, w_mix, g, wq, kv, wo, layer, S, n_mem, bm):
    M, D = h.shape
    ka, kb = a.shape[1], b.shape[1]
    bps = S // bm
    return pl.pallas_call(
        _mix_xattn_kernel,
        grid=(M // bm,),
        in_specs=[pl.BlockSpec((bm, D), lambda i: (i, 0)),
                  pl.BlockSpec((bm, ka), lambda i: (i, 0)),
                  pl.BlockSpec((bm, kb), lambda i: (i, 0)),
                  _resident(w_mix),
                  pl.BlockSpec((1, D), lambda i: (0, 0)),
                  _resident(wq, layer),
                  pl.BlockSpec((n_mem, D), lambda i: (i // bps, 0)),
                  pl.BlockSpec((n_mem, D), lambda i: (i // bps, 1)),
                  _resident(wo, layer)],
        out_specs=pl.BlockSpec((bm, D), lambda i: (i, 0)),
        out_shape=jax.ShapeDtypeStruct((M, D), F32),
        compiler_params=_cparams("parallel"),
    )(h, a, b, w_mix, g.reshape(1, D), wq, kv, kv, wo)


def _ffn_kernel(h_ref, halo_ref, g_ref, wup_ref, cw_ref, wd_ref, fg_ref, o_ref,
                xn_ref, acc_ref, y_ref, *, blocks_per_seq, final_norm, sub):
    H = BF16_SUBLANES
    nc, cf = wd_ref.shape[0], wd_ref.shape[1]
    n_sub = acc_ref.shape[0] // sub

    first = (pl.program_id(0) % blocks_per_seq) == 0
    xn_ref[:H, :] = jnp.where(first, 0.0, _rmsnorm(halo_ref[...], g_ref[...])).astype(BF16)

    def normalize(r):
        rows = slice(r * sub, (r + 1) * sub)
        xn_ref[H + r * sub:H + (r + 1) * sub, :] = _rmsnorm(h_ref[rows, :], g_ref[...]).astype(BF16)

    chunk_cols = lambda c: pl.ds(pl.multiple_of(c * cf, cf), cf)

    def up(c, r):
        if r == 0:
            xs, dst = xn_ref[:sub + H, :], slice(0, sub + H)
        else:
            xs, dst = xn_ref[H + r * sub:H + (r + 1) * sub, :], slice(H, sub + H)
            y_ref[r % 2, :H, :] = y_ref[(r - 1) % 2, sub:sub + H, :]
        y_ref[r % 2, dst, :cf] = _dot(xs, wup_ref[:, chunk_cols(c)])
        y_ref[r % 2, dst, cf:] = _dot(xs, wup_ref[:, chunk_cols(nc + c)])

    def conv(r, part, cw):
        cols = slice(part * cf, (part + 1) * cf)
        out = y_ref[r % 2, H:, cols] * cw[FFN_CONV - 1:FFN_CONV, :]
        for k in range(1, FFN_CONV):
            out = out + y_ref[r % 2, H - k:H - k + sub, cols] * cw[FFN_CONV - 1 - k:FFN_CONV - k, :]
        return out

    def chunk(c, leading=False, trailing=False):
        cwg, cwu = cw_ref[:, chunk_cols(c)], cw_ref[:, chunk_cols(nc + c)]
        wd = wd_ref[c]
        for r in range(n_sub):
            if r + 1 < n_sub:
                if leading:
                    normalize(r + 1)
                up(c, r + 1)
            elif not trailing:
                up(c + 1, 0)
            down = _dot((_silu(conv(r, 0, cwg)) * conv(r, 1, cwu)).astype(BF16), wd)
            rows = slice(r * sub, (r + 1) * sub)
            if leading:
                acc_ref[rows, :] = down
            elif not trailing:
                acc_ref[rows, :] += down
            else:
                y = h_ref[rows, :] + acc_ref[rows, :] + down
                o_ref[rows, :] = _rmsnorm(y, fg_ref[...]) if final_norm else y

    assert nc >= 3
    normalize(0)
    up(0, 0)
    chunk(0, leading=True)
    lax.fori_loop(1, nc - 1, lambda c, carry: (chunk(c), carry)[1], 0)
    chunk(nc - 1, trailing=True)


def ffn_residual(h, g, w_up, conv_w, w_down, layer, final_g, S, bm, cf, final_norm):
    M, D = h.shape
    d_ff = w_down.shape[1]
    H = BF16_SUBLANES
    nc = d_ff // cf
    sub = min(256, bm)
    assert (bm // sub) % 2 == 0, "the two y_ref slots alternate per sub-block across chunks"
    wd3 = w_down.reshape(-1, nc, cf, D)
    return pl.pallas_call(
        functools.partial(_ffn_kernel, blocks_per_seq=S // bm, final_norm=final_norm, sub=sub),
        grid=(M // bm,),
        in_specs=[pl.BlockSpec((bm, D), lambda i: (i, 0)),
                  pl.BlockSpec((H, D), lambda i: (jnp.maximum(i * (bm // H) - 1, 0), 0)),
                  pl.BlockSpec((1, D), lambda i: (0, 0)),
                  _resident(w_up, layer), _resident(conv_w, layer), _resident(wd3, layer),
                  pl.BlockSpec((1, D), lambda i: (0, 0))],
        out_specs=pl.BlockSpec((bm, D), lambda i: (i, 0)),
        out_shape=jax.ShapeDtypeStruct((M, D), F32),
        scratch_shapes=[pltpu.VMEM((H + bm, D), BF16), pltpu.VMEM((bm, D), F32),
                        pltpu.VMEM((2, H + sub, 2 * cf), F32)],
        compiler_params=_cparams("parallel"),
    )(h, h, g.reshape(1, D), w_up, conv_w, wd3, final_g.reshape(1, D))


def _sgu_kernel(u_ref, v_ref, lg_ref, lb_ref, w_ref, bt_ref, o_ref):
    rows = u_ref.shape[0]
    T = SGU_CHUNK
    v = v_ref[...].astype(F32)
    mu = jnp.mean(v, axis=-1, keepdims=True)
    d = v - mu
    var = jnp.mean(d * d, axis=-1, keepdims=True)
    vn = (d * lax.rsqrt(var + EPS) * lg_ref[...] + lb_ref[...]).astype(BF16)
    causal = (lax.broadcasted_iota(jnp.int32, (T, T), 1) <= lax.broadcasted_iota(jnp.int32, (T, T), 0))
    for g in range(SGU_GROUPS):
        cols = slice(g * SGU_GROUP, (g + 1) * SGU_GROUP)
        wg = jnp.where(causal, w_ref[g], 0.0).astype(BF16)
        bias = bt_ref[:, g:g + 1]
        for c in range(rows // T):
            rs = slice(c * T, (c + 1) * T)
            s = _dot(wg, vn[rs, cols]) + bias
            o_ref[rs, cols] = (u_ref[rs, cols].astype(F32) * s).astype(o_ref.dtype)


def spatial_gating(proj, ln_g, ln_b, w_s, b_s, M, rows):
    return pl.pallas_call(
        _sgu_kernel,
        grid=(M // rows,),
        in_specs=[pl.BlockSpec((rows, SGU_WIDTH), lambda i: (i, 0)),
                  pl.BlockSpec((rows, SGU_WIDTH), lambda i: (i, 1)),
                  pl.BlockSpec((1, SGU_WIDTH), lambda i: (0, 0)),
                  pl.BlockSpec((1, SGU_WIDTH), lambda i: (0, 0)),
                  pl.BlockSpec((SGU_GROUPS, SGU_CHUNK, SGU_CHUNK), lambda i: (0, 0, 0)),
                  pl.BlockSpec((SGU_CHUNK, SGU_GROUPS), lambda i: (0, 0))],
        out_specs=pl.BlockSpec((rows, SGU_WIDTH), lambda i: (i, 0)),
        out_shape=jax.ShapeDtypeStruct((M, SGU_WIDTH), BF16),
        compiler_params=_cparams("parallel"),
    )(proj, proj, ln_g.reshape(1, -1), ln_b.reshape(1, -1), w_s, b_s.T)


def _gdn_intra_kernel(q_ref, k_ref, v_ref, tail_ref, tailt_ref, alog_ref,
                      dtb_ref, u_ref, w_ref, qd_ref, kd_ref, qk_ref, gl_ref):
    rows = q_ref.shape[0]
    C = DN_CHUNK
    HD = DN_HEAD_DIM
    x = jnp.concatenate([q_ref[...], k_ref[...], v_ref[...]], axis=1).astype(F32)

    ii = lax.broadcasted_iota(jnp.int32, (C, C), 0)
    jj = lax.broadcasted_iota(jnp.int32, (C, C), 1)
    lower = jj <= ii
    strict = jj < ii
    su = lax.broadcasted_iota(jnp.int32, (C, LANES), 0)
    ju = lax.broadcasted_iota(jnp.int32, (C, LANES), 1)
    upper_ext = jnp.where(((ju < C) & (su > ju)) | (ju == C), 1.0, 0.0)

    inst = [(c, hh) for hh in range(DN_HEADS) for c in range(rows // C)]
    qs, ks, vs, bs, stacks = [], [], [], [], []
    for hh in range(DN_HEADS):
        qh = x[:, hh * HD:(hh + 1) * HD]
        kh = x[:, DN_WIDTH + hh * HD:DN_WIDTH + (hh + 1) * HD]
        vh = x[:, 2 * DN_WIDTH + hh * HD:2 * DN_WIDTH + (hh + 1) * HD]
        qh = qh * lax.rsqrt(jnp.sum(qh * qh, axis=-1, keepdims=True) + EPS) * HD ** -0.5
        kh = kh * lax.rsqrt(jnp.sum(kh * kh, axis=-1, keepdims=True) + EPS)
        beta = 1.0 / (1.0 + jnp.exp(-tail_ref[:, hh:hh + 1]))
        a_raw = tailt_ref[DN_HEADS + hh:DN_HEADS + hh + 1, :]
        z = a_raw + dtb_ref[0:1, hh:hh + 1]
        softplus = jnp.maximum(z, 0.0) + jnp.log(1.0 + jnp.exp(-jnp.abs(z)))
        g_row = -jnp.exp(alog_ref[0:1, hh:hh + 1]) * softplus
        for c in range(rows // C):
            rs = slice(c * C, (c + 1) * C)
            qs.append(qh[rs]); ks.append(kh[rs]); vs.append(vh[rs]); bs.append(beta[rs])
            gr = jnp.broadcast_to(g_row[:, rs], (C, C))
            stacks += [jnp.where(lower, gr, 0.0), jnp.where(lower, 0.0, gr)]

    stacked = jnp.concatenate(stacks, axis=0)
    s_hi = stacked.astype(BF16)
    s_lo = (stacked - s_hi.astype(F32)).astype(BF16)
    ue = upper_ext.astype(BF16)
    dall = _dot(s_hi, ue) + _dot(s_lo, ue)

    decays, gcs, gc_revs, k16s, kbs = [], [], [], [], []
    for n, (c, hh) in enumerate(inst):
        dext = dall[n * 2 * C:(n + 1) * 2 * C]
        decays.append(jnp.exp(jnp.where(lower, dext[:C, :C], -jnp.inf)))
        gcs.append(dext[:C, C:C + 1])
        gc_revs.append(dext[C:, C:C + 1])
        kbs.append(ks[n] * bs[n])
        k16s.append(ks[n].astype(BF16))
    kq = [_dot_nt(jnp.concatenate([kbs[n], qs[n]], axis=0).astype(BF16), k16s[n]) for n in range(len(inst))]
    kk = [m[:C] for m in kq]
    qk = [m[C:] for m in kq]
    pws = [jnp.where(strict, kk[n] * decays[n], 0.0).astype(BF16) for n in range(len(inst))]
    egc = [jnp.exp(g) for g in gcs]
    rhs = [jnp.concatenate([vs[n] * bs[n], kbs[n] * egc[n]], axis=1) for n in range(len(inst))]
    sols = [rhs[n] - _dot(pws[n], rhs[n].astype(BF16)) for n in range(len(inst))]
    for _ in range(int(math.log2(C)) - 1):
        pws = [_dot(p, p).astype(BF16) for p in pws]
        sols = [s + _dot(p, s.astype(BF16)) for p, s in zip(pws, sols)]
    for n, (c, hh) in enumerate(inst):
        rs = slice(c * C, (c + 1) * C)
        hcols = slice(hh * HD, (hh + 1) * HD)
        u_ref[rs, hcols] = sols[n][:, :HD].astype(u_ref.dtype)
        w_ref[rs, hcols] = sols[n][:, HD:].astype(w_ref.dtype)
        qkd = jnp.where(lower, qk[n] * decays[n], 0.0)
        qk_ref[rs, hcols] = jnp.concatenate([qkd, jnp.zeros_like(qkd)], axis=1).astype(qk_ref.dtype)
        qd_ref[rs, hcols] = (qs[n] * egc[n]).astype(qd_ref.dtype)
        kd_ref[rs, hcols] = (ks[n] * jnp.exp(gc_revs[n])).astype(kd_ref.dtype)
        gl_ref[c * 8:(c + 1) * 8, hcols] = jnp.broadcast_to(egc[n][C - 1:C, :], (8, HD))


def gdn_intra(proj, tail, tail_t, a_log, dt_bias, M, rows):
    c0 = 2 * SGU_WIDTH // DN_WIDTH
    pad = lambda p: jnp.pad(p.reshape(1, -1), ((0, 0), (0, LANES - p.shape[0])))
    seq = lambda dt: jax.ShapeDtypeStruct((M, DN_WIDTH), dt)
    row_spec = pl.BlockSpec((rows, DN_WIDTH), lambda i: (i, 0))
    cur_spec = lambda part: pl.BlockSpec((rows, DN_WIDTH), lambda i: (i, c0 + part))
    return pl.pallas_call(
        _gdn_intra_kernel,
        grid=(M // rows,),
        in_specs=[cur_spec(0), cur_spec(1), cur_spec(2),
                  pl.BlockSpec((rows, LANES), lambda i: (i, 0)),
                  pl.BlockSpec((2 * DN_HEADS, rows), lambda i: (0, i)),
                  pl.BlockSpec((1, LANES), lambda i: (0, 0)),
                  pl.BlockSpec((1, LANES), lambda i: (0, 0))],
        out_specs=[row_spec, row_spec, row_spec, row_spec, row_spec,
                   pl.BlockSpec((rows // DN_CHUNK * 8, DN_WIDTH), lambda i: (i, 0))],
        out_shape=[seq(BF16), seq(BF16), seq(BF16), seq(BF16), seq(BF16),
                   jax.ShapeDtypeStruct((M // DN_CHUNK * 8, DN_WIDTH), F32)],
        compiler_params=_cparams("parallel"),
    )(proj, proj, proj, tail, tail_t, pad(a_log), pad(dt_bias))


def _gdn_scan_kernel(u_ref, w_ref, qd_ref, kd_ref, qk_ref, gl_ref, gate_ref, ng_ref, o_ref, state_ref, *,
                     chunks):
    C = DN_CHUNK
    HD = DN_HEAD_DIM
    B = u_ref.shape[0]

    @pl.when(pl.program_id(0) == 0)
    def _():
        state_ref[...] = jnp.zeros_like(state_ref)

    ng = ng_ref[...]
    inst = [(b, hh) for b in range(B) for hh in range(DN_HEADS)]
    col = lambda hh: slice(hh * HD, (hh + 1) * HD)
    states = [state_ref[b, hh] for b, hh in inst]
    for c in range(chunks):
        rs = slice(c * C, (c + 1) * C)
        kdt = [kd_ref[b, rs, col(hh)].astype(F32).T.astype(BF16) for b, hh in inst]
        st16 = [s.astype(BF16) for s in states]
        ws = [_dot(w_ref[b, rs, col(hh)], st16[n]) for n, (b, hh) in enumerate(inst)]
        qs = [_dot(qd_ref[b, rs, col(hh)], st16[n]) for n, (b, hh) in enumerate(inst)]
        vn16 = [(u_ref[b, rs, col(hh)].astype(F32) - ws[n]).astype(BF16) for n, (b, hh) in enumerate(inst)]
        states = [states[n] * gl_ref[b, c * 8:c * 8 + 1, col(hh)] + _dot(kdt[n], vn16[n])
                  for n, (b, hh) in enumerate(inst)]
        for n, (b, hh) in enumerate(inst):
            o = qs[n] + _dot(qk_ref[b, rs, col(hh)][:, :C], vn16[n])
            o = o * lax.rsqrt(jnp.mean(o * o, axis=-1, keepdims=True) + EPS) * ng
            o_ref[b, rs, col(hh)] = (o * _silu(gate_ref[b, rs, col(hh)].astype(F32))).astype(o_ref.dtype)
    for n, (b, hh) in enumerate(inst):
        state_ref[b, hh] = states[n]


def gdn_scan(u, w, qd, kd, qk, gl, proj3, norm_g, B, S, chunks):
    rows = chunks * DN_CHUNK
    r3 = lambda a: a.reshape(B, S, DN_WIDTH)
    gcol = (2 * SGU_WIDTH + 3 * DN_WIDTH) // DN_WIDTH
    seq_spec = pl.BlockSpec((B, rows, DN_WIDTH), lambda n: (0, n, 0))
    return pl.pallas_call(
        functools.partial(_gdn_scan_kernel, chunks=chunks),
        grid=(S // rows,),
        in_specs=[seq_spec, seq_spec, seq_spec, seq_spec, seq_spec,
                  pl.BlockSpec((B, chunks * 8, DN_WIDTH), lambda n: (0, n, 0)),
                  pl.BlockSpec((B, rows, DN_WIDTH), lambda n: (0, n, gcol)),
                  pl.BlockSpec((1, DN_HEAD_DIM), lambda n: (0, 0))],
        out_specs=seq_spec,
        out_shape=jax.ShapeDtypeStruct((B, S, DN_WIDTH), BF16),
        scratch_shapes=[pltpu.VMEM((B, DN_HEADS, DN_HEAD_DIM, DN_HEAD_DIM), F32)],
        compiler_params=_cparams("arbitrary"),
    )(r3(u), r3(w), r3(qd), r3(kd), r3(qk), gl.reshape(B, S // DN_CHUNK * 8, DN_WIDTH), proj3,
      norm_g.reshape(1, DN_HEAD_DIM))


def _tiles(S):
    rows = min(1024, S)
    return dict(rows=rows, sgu_rows=rows, gdn_rows=min(512, S), scan_chunks=min(8, S // DN_CHUNK),
                ffn_chunk=256)


def _forward(x, mem, mem_norm, norm_mix, norm_xattn, norm_ffn, ev_w_in, pool_w, pool_scale, ev_w_out,
             od_w_in, sgu_ln_g, sgu_ln_b, sgu_w, sgu_b, dn_conv, dn_a_log, dn_dt_bias, dn_norm_g,
             od_w_out, xattn_wq, xattn_wkv, xattn_wo, ffn_w_up, ffn_conv, ffn_w_down, final_norm):
    B, S, D = x.shape
    n_mem = mem.shape[1]
    M = B * S
    depth = norm_mix.shape[0]
    bf = lambda a: a.astype(BF16)
    t = _tiles(S)

    h = x.reshape(M, D)
    mem2 = mem.reshape(B * n_mem, D)
    wkv_all, wq_all, wo_all = bf(xattn_wkv), bf(xattn_wq), bf(xattn_wo)
    w_up_all, w_down_all = bf(ffn_w_up), bf(ffn_w_down)
    for layer in range(depth):
        i = layer // 2
        if layer % 2 == 0:
            proj, = norm_matmul(h, norm_mix[layer], [bf(ev_w_in[i])], [BF16], t["rows"])
            a_out = moba_attention(proj.reshape(B, S, -1), B, S).reshape(M, A_WIDTH)
            b_out = multiscale_pool(proj, bf(pool_w[i]), pool_scale[i], M, S, t["rows"])
            mix_a, mix_b, w_mix = a_out, b_out, ev_w_out[i]
        else:
            main_w = 2 * SGU_WIDTH + 4 * DN_WIDTH
            w_in = od_w_in[i]
            w_tail = jnp.pad(w_in[:, main_w:], ((0, 0), (0, LANES - 2 * DN_HEADS)))
            proj, tail, tail_t = od_projection(h, norm_mix[layer], bf(w_in), bf(w_tail), dn_conv[i],
                                               S, t["rows"])
            c_out = spatial_gating(proj, sgu_ln_g[i], sgu_ln_b[i], sgu_w[i], sgu_b[i], M, t["sgu_rows"])
            u, w, qd, kd, qk, gl = gdn_intra(proj, tail, tail_t, dn_a_log[i], dn_dt_bias[i], M, t["gdn_rows"])
            d_out = gdn_scan(u, w, qd, kd, qk, gl, proj.reshape(B, S, -1), dn_norm_g[i], B, S,
                             t["scan_chunks"])
            mix_a, mix_b, w_mix = c_out, d_out.reshape(M, DN_WIDTH), od_w_out[i]
        kv, = norm_matmul(mem2, mem_norm, [wkv_all], [BF16], B * n_mem, layer=layer)
        h = mix_xattn_residual(h, mix_a, mix_b, bf(w_mix), norm_xattn[layer], wq_all, kv, wo_all, layer,
                               S, n_mem, t["rows"])
        h = ffn_residual(h, norm_ffn[layer], w_up_all, ffn_conv, w_down_all, layer, final_norm, S, t["rows"],
                         t["ffn_chunk"], final_norm=(layer == depth - 1))
    return h.reshape(B, S, D)


def kernel(x, mem, mem_norm, norm_mix, norm_xattn, norm_ffn, ev_w_in, pool_w, pool_scale, ev_w_out, od_w_in, sgu_ln_g, sgu_ln_b, sgu_w, sgu_b, dn_conv, dn_a_log, dn_dt_bias, dn_norm_g, od_w_out, xattn_wq, xattn_wkv, xattn_wo, ffn_w_up, ffn_conv, ffn_w_down, final_norm):
    return _forward(x, mem, mem_norm, norm_mix, norm_xattn, norm_ffn, ev_w_in, pool_w, pool_scale, ev_w_out,
                    od_w_in, sgu_ln_g, sgu_ln_b, sgu_w, sgu_b, dn_conv, dn_a_log, dn_dt_bias, dn_norm_g,
                    od_w_out, xattn_wq, xattn_wkv, xattn_wo, ffn_w_up, ffn_conv, ffn_w_down, final_norm)
```

```python
import functools
import math

import jax
import jax.numpy as jnp
from jax import lax
from jax.experimental import pallas as pl
from jax.experimental.pallas import tpu as pltpu

F32 = jnp.float32
BF16 = jnp.bfloat16
EPS = 1e-6
NEG_BIG = -1e30

VMEM_LIMIT_BYTES = 48 * 1024 * 1024
BF16_SUBLANES = 16
LANES = 128

MOBA_HEADS, MOBA_HEAD_DIM, MOBA_BLOCK, MOBA_TOPK = 8, 64, 256, 3
A_WIDTH = MOBA_HEADS * MOBA_HEAD_DIM
POOL_WINDOWS = (2, 4, 8, 16)
POOL_GROUP = 128
POOL_WIDTH = POOL_GROUP * len(POOL_WINDOWS)
SGU_GROUPS, SGU_GROUP, SGU_CHUNK = 4, 128, 128
SGU_WIDTH = SGU_GROUPS * SGU_GROUP
DN_HEADS, DN_HEAD_DIM, DN_CONV, DN_CHUNK = 4, 128, 4, 64
DN_WIDTH = DN_HEADS * DN_HEAD_DIM
XATTN_HEADS = 4
FFN_CONV = 3


def _cparams(*sem):
    return pltpu.CompilerParams(dimension_semantics=sem, vmem_limit_bytes=VMEM_LIMIT_BYTES)


def _rmsnorm(x, g):
    return x * lax.rsqrt(jnp.mean(x * x, axis=-1, keepdims=True) + EPS) * g


def _silu(x):
    return x * (0.5 * jnp.tanh(0.5 * x) + 0.5)


def _dot(a, b):
    return jnp.dot(a, b, preferred_element_type=F32)


def _resident(arr, layer=None):
    if layer is None:
        return pl.BlockSpec(arr.shape, lambda i: (0,) * arr.ndim, pipeline_mode=pl.Buffered(1))
    return pl.BlockSpec((None,) + arr.shape[1:], lambda i: (layer,) + (0,) * (arr.ndim - 1),
                        pipeline_mode=pl.Buffered(1))


def _dot_nt(a, b, precision=None):
    return lax.dot_general(a, b, (((1,), (1,)), ((), ())), preferred_element_type=F32,
                           precision=precision)


def _norm_mm_kernel(x_ref, g_ref, *refs, bn):
    n = len(refs) // 2
    xn = _rmsnorm(x_ref[...], g_ref[...]).astype(BF16)
    for w_ref, o_ref in zip(refs[:n], refs[n:]):
        N = w_ref.shape[1]
        for c0 in range(0, N, bn):
            c1 = min(c0 + bn, N)
            o_ref[:, c0:c1] = _dot(xn, w_ref[:, c0:c1]).astype(o_ref.dtype)


def norm_matmul(x, g, ws, out_dtypes, bm, bn=512, layer=None):
    M, D = x.shape
    return pl.pallas_call(
        functools.partial(_norm_mm_kernel, bn=bn),
        grid=(M // bm,),
        in_specs=[pl.BlockSpec((bm, D), lambda i: (i, 0)),
                  pl.BlockSpec((1, D), lambda i: (0, 0))]
                 + [_resident(w, layer) for w in ws],
        out_specs=[pl.BlockSpec((bm, w.shape[-1]), lambda i: (i, 0)) for w in ws],
        out_shape=[jax.ShapeDtypeStruct((M, w.shape[-1]), dt) for w, dt in zip(ws, out_dtypes)],
        compiler_params=_cparams("parallel"),
    )(x, g.reshape(1, D), *ws)


def _gelu_tanh(x):
    return 0.5 * x * (1.0 + jnp.tanh(math.sqrt(2.0 / math.pi) * (x + 0.044715 * (x * x * x))))


def _od_proj_kernel(x_ref, halo_ref, g_ref, w_ref, wt_ref, cw_ref, o_ref, t_ref, tt_ref, xe_ref, *,
                    blocks_per_seq, bn):
    bm = x_ref.shape[0]
    H = BF16_SUBLANES
    z_w, qkv_w = 2 * SGU_WIDTH, 3 * DN_WIDTH
    first = (pl.program_id(0) % blocks_per_seq) == 0
    xe_ref[:H, :] = jnp.where(first, 0.0, _rmsnorm(halo_ref[...], g_ref[...])).astype(BF16)
    xe_ref[H:, :] = _rmsnorm(x_ref[...], g_ref[...]).astype(BF16)
    xn = xe_ref[H:, :]
    tail = _dot(xn, wt_ref[...])
    t_ref[...] = tail
    tt_ref[...] = tail.T[:H, :]
    for c0 in range(0, o_ref.shape[1], bn):
        cols = slice(c0, c0 + bn)
        if c0 < z_w:
            o_ref[:, cols] = _gelu_tanh(_dot(xn, w_ref[:, cols])).astype(o_ref.dtype)
        elif c0 < z_w + qkv_w:
            y = _dot(xe_ref[...], w_ref[:, cols])
            cw = cw_ref[:, c0 - z_w:c0 - z_w + bn]
            out = y[H:, :] * cw[DN_CONV - 1:DN_CONV, :]
            for k in range(1, DN_CONV):
                out = out + pltpu.roll(y, k, axis=0)[H:, :] * cw[DN_CONV - 1 - k:DN_CONV - k, :]
            o_ref[:, cols] = _silu(out).astype(o_ref.dtype)
        else:
            o_ref[:, cols] = _dot(xn, w_ref[:, cols]).astype(o_ref.dtype)


def od_projection(x, g, w, w_tail, conv_w, S, bm, bn=512):
    M, D = x.shape
    N = 2 * SGU_WIDTH + 4 * DN_WIDTH
    H = BF16_SUBLANES
    return pl.pallas_call(
        functools.partial(_od_proj_kernel, blocks_per_seq=S // bm, bn=bn),
        grid=(M // bm,),
        in_specs=[pl.BlockSpec((bm, D), lambda i: (i, 0)),
                  pl.BlockSpec((H, D), lambda i: (jnp.maximum(i * (bm // H) - 1, 0), 0)),
                  pl.BlockSpec((1, D), lambda i: (0, 0)),
                  _resident(w), _resident(w_tail), _resident(conv_w)],
        out_specs=[pl.BlockSpec((bm, N), lambda i: (i, 0)), pl.BlockSpec((bm, LANES), lambda i: (i, 0)),
                   pl.BlockSpec((H, bm), lambda i: (0, i))],
        out_shape=[jax.ShapeDtypeStruct((M, N), BF16), jax.ShapeDtypeStruct((M, LANES), F32),
                   jax.ShapeDtypeStruct((H, M), F32)],
        scratch_shapes=[pltpu.VMEM((H + bm, D), BF16)],
        compiler_params=_cparams("parallel"),
    )(x, x, g.reshape(1, D), w, w_tail, conv_w)


def _moba_kernel(q_ref, k_ref, v_ref, o_ref, kme_ref, vt_ref, sel_ref, m_ref, acc_ref, s_ref, *,
                 nb, nbp, unroll, pairs):
    BS = MOBA_BLOCK
    HD = MOBA_HEAD_DIM
    n_heads = 2 * pairs
    i = pl.program_id(2)
    lane = lax.broadcasted_iota(jnp.int32, (1, LANES), 1)
    head_lanes = (lane < HD, lane >= HD)
    pair_lanes = lambda u: slice((u // 2) * LANES, (u // 2 + 1) * LANES)

    @pl.when(i == 0)
    def _():
        kme_ref[...] = jnp.zeros_like(kme_ref)
        for n in range(nb):
            rows = slice(n * BS, (n + 1) * BS)
            mean = jnp.sum(k_ref[0, rows, :].astype(F32), axis=0, keepdims=True) / BS
            for u in range(n_heads):
                kme_ref[u // 2, (u % 2) * nbp + n:(u % 2) * nbp + n + 1, :] = jnp.where(
                    head_lanes[u % 2], mean[:, pair_lanes(u)], 0.0)
            vt_ref[:, rows] = v_ref[0, rows, :].astype(F32).T.astype(BF16)

    scale = HD ** -0.5 * math.log2(math.e)
    q_t = [q_ref[0, :, p * LANES:(p + 1) * LANES].astype(F32).T for p in range(pairs)]
    pair_row = lax.broadcasted_iota(jnp.int32, (LANES, 1), 0)
    head_rows = (pair_row < HD, pair_row >= HD)
    q_aug = [jnp.where(head_rows[u % 2], q_t[u // 2] * scale, 0.0).astype(BF16) for u in range(n_heads)]

    gates = []
    for p in range(pairs):
        km = kme_ref[p]
        k_hi = km.astype(BF16)
        r1 = km - k_hi.astype(F32)
        k_mid = r1.astype(BF16)
        k_lo = (r1 - k_mid.astype(F32)).astype(BF16)
        q16 = q_t[p].astype(BF16)
        gates.append(_dot(k_hi, q16) + _dot(k_mid, q16) + _dot(k_lo, q16))
    blk = lax.broadcasted_iota(jnp.int32, (nbp, 1), 0).astype(F32)
    valid = blk < i.astype(F32)
    for u in range(n_heads):
        g = jnp.where(valid, gates[u // 2][(u % 2) * nbp:(u % 2 + 1) * nbp], -jnp.inf)
        sel = jnp.zeros(g.shape, jnp.bool_)
        for _ in range(MOBA_TOPK):
            mx = jnp.max(g, axis=0, keepdims=True)
            idx = jnp.min(jnp.where(g == mx, blk, float(1 << 20)), axis=0, keepdims=True)
            pick = blk == idx
            sel = sel | pick
            g = jnp.where(pick, -jnp.inf, g)
        sel_ref[u, :nbp, :] = jnp.where(sel & valid, 1.0, 0.0)
        sel_ref[u, nbp:, :] = jnp.zeros((8, BS), F32)
        m_ref[u] = jnp.full((1, BS), NEG_BIG, F32)
        acc_ref[u] = jnp.zeros(acc_ref.shape[1:], F32)

    krow = lax.broadcasted_iota(jnp.int32, (BS, BS), 0)
    qcol = lax.broadcasted_iota(jnp.int32, (BS, BS), 1)
    PVR = HD + BF16_SUBLANES
    pv_rows = (slice(0, PVR), slice(LANES - PVR, LANES))
    pv_row = lax.broadcasted_iota(jnp.int32, (PVR, 1), 0)
    is_dim = (pv_row < HD, pv_row >= PVR - HD)

    def block_start(j):
        return pl.multiple_of(jnp.minimum(j, i) * BS, BS)

    def produce(g, slot):
        for t in range(unroll):
            rows = pl.ds(block_start(g * unroll + t), BS)
            for u in range(n_heads):
                s_ref[slot, u, t * BS:(t + 1) * BS, :] = _dot(
                    k_ref[0, rows, pair_lanes(u)], q_aug[u]).astype(BF16)

    def softmax_update(sts, sels, starts):
        heads = range(n_heads)
        m_new, alpha = [], []
        for u in heads:
            cand = jnp.full((1, BS), NEG_BIG, F32)
            for st, sel in zip(sts[u], sels[u]):
                mx = jnp.max(st.reshape(BS // BF16_SUBLANES, BF16_SUBLANES, BS), axis=0)
                mx = jnp.max(mx.astype(F32), axis=0, keepdims=True)
                cand = jnp.maximum(cand, mx if sel is None else jnp.where(sel, mx, NEG_BIG))
            m_old = m_ref[u]
            m_new.append(jnp.maximum(m_old, cand))
            alpha.append(jnp.exp2(m_old - m_new[u]))
            m_ref[u] = m_new[u]
        ps = []
        for u in heads:
            pu = []
            for st, sel in zip(sts[u], sels[u]):
                sub = m_new[u] if sel is None else jnp.where(sel, m_new[u], -NEG_BIG)
                pu.append(jnp.exp2(st - sub.astype(BF16)))
            ps.append(pu[0] if len(pu) == 1 else jnp.concatenate(pu, axis=0))
        pv = []
        for u in heads:
            rows = slice((u // 2) * LANES + pv_rows[u % 2].start, (u // 2) * LANES + pv_rows[u % 2].stop)
            vts = [jnp.where(is_dim[u % 2], vt_ref[rows, pl.ds(st0, BS)], jnp.ones((), BF16))
                   for st0 in starts]
            pv.append(_dot(vts[0] if len(vts) == 1 else jnp.concatenate(vts, axis=1), ps[u]))
        for u in heads:
            acc_ref[u] = acc_ref[u] * alpha[u] + pv[u]

    def consume(g, slot):
        js = [g * unroll + t for t in range(unroll)]
        sts = [[s_ref[slot, u, t * BS:(t + 1) * BS, :] for t in range(unroll)] for u in range(n_heads)]
        sels = [[sel_ref[u, pl.ds(j, 1), :] > 0.5 for j in js] for u in range(n_heads)]
        softmax_update(sts, sels, [block_start(j) for j in js])

    slots = s_ref.shape[0]
    assert unroll == 1, "the group after the last past block must be exactly the tile's own block"

    def body(gg, c):
        for t in range(slots):
            produce(slots * gg + t + 1, (t + 1) % slots)
            consume(slots * gg + t, t)
        return c

    n_groups = (i + unroll - 1) // unroll
    produce(0, 0)
    lax.fori_loop(0, (n_groups + slots - 1) // slots, body, 0)
    softmax_update([[jnp.where(krow <= qcol, s_ref[0, u, :BS, :], -jnp.inf)] for u in range(n_heads)],
                   [[None]] * n_heads, [block_start(i)])

    outs = []
    for u in range(n_heads):
        a = acc_ref[u]
        outs.append(a[:HD] / a[HD:HD + 1, :] if u % 2 == 0 else a[PVR - HD:] / a[0:1, :])
    o_ref[0] = jnp.concatenate(outs, axis=0).T.astype(o_ref.dtype)


def moba_attention(proj, B, S):
    nb = S // MOBA_BLOCK
    nbp = -(-nb // 8) * 8
    pairs = 4
    width = pairs * LANES
    groups = A_WIDTH // width
    unroll = 1
    slots = 3
    n_heads = 2 * pairs
    return pl.pallas_call(
        functools.partial(_moba_kernel, nb=nb, nbp=nbp, unroll=unroll, pairs=pairs),
        grid=(B, groups, nb),
        in_specs=[pl.BlockSpec((1, MOBA_BLOCK, width), lambda b, p, i: (b, i, p)),
                  pl.BlockSpec((1, S, width), lambda b, p, i: (b, 0, groups + p), pipeline_mode=pl.Buffered(1)),
                  pl.BlockSpec((1, S, width), lambda b, p, i: (b, 0, 2 * groups + p), pipeline_mode=pl.Buffered(1))],
        out_specs=pl.BlockSpec((1, MOBA_BLOCK, width), lambda b, p, i: (b, i, p)),
        out_shape=jax.ShapeDtypeStruct((B, S, A_WIDTH), BF16),
        scratch_shapes=[pltpu.VMEM((pairs, 2 * nbp, LANES), F32),
                        pltpu.VMEM((width, S), BF16),
                        pltpu.VMEM((n_heads, nbp + 8, MOBA_BLOCK), F32),
                        pltpu.VMEM((n_heads, 1, MOBA_BLOCK), F32),
                        pltpu.VMEM((n_heads, MOBA_HEAD_DIM + BF16_SUBLANES, MOBA_BLOCK), F32),
                        pltpu.VMEM((slots, n_heads, unroll * MOBA_BLOCK, MOBA_BLOCK), BF16)],
        compiler_params=_cparams("parallel", "parallel", "arbitrary"),
    )(proj, proj, proj)


def _pool_kernel(p_ref, halo_ref, w_ref, sc_ref, o_ref, *, blocks_per_seq):
    bm = p_ref.shape[0]
    H = BF16_SUBLANES
    i = pl.program_id(0)
    first = (i % blocks_per_seq) == 0
    t1 = (lax.broadcasted_iota(jnp.int32, (bm, 1), 0) + (i % blocks_per_seq) * bm + 1).astype(F32)
    for g, w in enumerate(POOL_WINDOWS):
        cols = slice(g * POOL_GROUP, (g + 1) * POOL_GROUP)
        cur = p_ref[:, cols].astype(F32)
        halo = jnp.where(first, 0.0, halo_ref[:, cols].astype(F32))
        ext = jnp.concatenate([halo, cur], axis=0)
        acc = ext
        sh = 1
        while sh < w:
            acc = acc + pltpu.roll(acc, sh, axis=0)
            sh *= 2
        win = acc[H:, :]
        pooled = win / jnp.minimum(t1, float(w)) - cur
        y = _dot(pooled.astype(BF16), w_ref[g])
        o_ref[:, cols] = (y * sc_ref[:, cols]).astype(o_ref.dtype)


def multiscale_pool(proj, pool_w, pool_scale, M, S, bm):
    H = BF16_SUBLANES
    pcol = 3 * A_WIDTH // POOL_WIDTH
    return pl.pallas_call(
        functools.partial(_pool_kernel, blocks_per_seq=S // bm),
        grid=(M // bm,),
        in_specs=[pl.BlockSpec((bm, POOL_WIDTH), lambda i: (i, pcol)),
                  pl.BlockSpec((H, POOL_WIDTH), lambda i: (jnp.maximum(i * (bm // H) - 1, 0), pcol)),
                  pl.BlockSpec((len(POOL_WINDOWS), POOL_GROUP, POOL_GROUP), lambda i: (0, 0, 0)),
                  pl.BlockSpec((1, POOL_WIDTH), lambda i: (0, 0))],
        out_specs=pl.BlockSpec((bm, POOL_WIDTH), lambda i: (i, 0)),
        out_shape=jax.ShapeDtypeStruct((M, POOL_WIDTH), BF16),
        compiler_params=_cparams("parallel"),
    )(proj, proj, pool_w, pool_scale.reshape(1, POOL_WIDTH))


def _xattn_absorb_kernel(mem_ref, g_ref, wkv_ref, wq_ref, wo_ref, qk_ref, vo_ref):
    n_mem, D = mem_ref.shape
    hd = D // XATTN_HEADS
    mn = _rmsnorm(mem_ref[...], g_ref[...]).astype(BF16)
    kv = _dot(mn, wkv_ref[...]).astype(BF16)
    for hh in range(XATTN_HEADS):
        cols = slice(hh * hd, (hh + 1) * hd)
        mcols = slice(hh * n_mem, (hh + 1) * n_mem)
        qk_ref[0, :, mcols] = (_dot_nt(wq_ref[:, cols], kv[:, cols]) * hd ** -0.5).astype(BF16)
        vo_ref[0, mcols, :] = _dot(kv[:, D + hh * hd:D + (hh + 1) * hd], wo_ref[cols, :]).astype(BF16)


def xattn_absorb(mem2, g, wkv, wq, wo, layer, B, n_mem):
    D = mem2.shape[1]
    HM = XATTN_HEADS * n_mem
    return pl.pallas_call(
        _xattn_absorb_kernel,
        grid=(B,),
        in_specs=[pl.BlockSpec((n_mem, D), lambda b: (b, 0)),
                  pl.BlockSpec((1, D), lambda b: (0, 0)),
                  _resident(wkv, layer), _resident(wq, layer), _resident(wo, layer)],
        out_specs=[pl.BlockSpec((1, D, HM), lambda b: (b, 0, 0)), pl.BlockSpec((1, HM, D), lambda b: (b, 0, 0))],
        out_shape=[jax.ShapeDtypeStruct((B, D, HM), BF16), jax.ShapeDtypeStruct((B, HM, D), BF16)],
        compiler_params=_cparams("parallel"),
    )(mem2, g.reshape(1, D), wkv, wq, wo)


def _mix_xattn_kernel(h_ref, a_ref, b_ref, wm_ref, g_ref, qk_ref, vo_ref, o_ref, *, n_mem):
    ka = a_ref.shape[1]
    h = h_ref[...] + _dot(a_ref[...], wm_ref[:ka, :]) + _dot(b_ref[...], wm_ref[ka:, :])
    xn = _rmsnorm(h, g_ref[...]).astype(BF16)
    s = _dot(xn, qk_ref[...])
    ps = []
    for hh in range(XATTN_HEADS):
        sh = s[:, hh * n_mem:(hh + 1) * n_mem]
        p = jnp.exp(sh - jnp.max(sh, axis=1, keepdims=True))
        ps.append((p / jnp.sum(p, axis=1, keepdims=True)).astype(BF16))
    o_ref[...] = h + _dot(jnp.concatenate(ps, axis=1), vo_ref[...])


def mix_xattn_residual(h, a, b, w_mix, g, qk, vo, S, n_mem, bm):
    M, D = h.shape
    ka, kb = a.shape[1], b.shape[1]
    bps = S // bm
    HM = qk.shape[2]
    return pl.pallas_call(
        functools.partial(_mix_xattn_kernel, n_mem=n_mem),
        grid=(M // bm,),
        in_specs=[pl.BlockSpec((bm, D), lambda i: (i, 0)),
                  pl.BlockSpec((bm, ka), lambda i: (i, 0)),
                  pl.BlockSpec((bm, kb), lambda i: (i, 0)),
                  _resident(w_mix),
                  pl.BlockSpec((1, D), lambda i: (0, 0)),
                  pl.BlockSpec((None, D, HM), lambda i: (i // bps, 0, 0)),
                  pl.BlockSpec((None, HM, D), lambda i: (i // bps, 0, 0))],
        out_specs=pl.BlockSpec((bm, D), lambda i: (i, 0)),
        out_shape=jax.ShapeDtypeStruct((M, D), F32),
        compiler_params=_cparams("parallel"),
    )(h, a, b, w_mix, g.reshape(1, D), qk, vo)


def _ffn_kernel(h_ref, halo_ref, g_ref, wup_ref, cw_ref, wd_ref, fg_ref, o_ref,
                xn_ref, acc_ref, y_ref, *, blocks_per_seq, final_norm, sub):
    H = BF16_SUBLANES
    nc, cf = wd_ref.shape[0], wd_ref.shape[1]
    n_sub = acc_ref.shape[0] // sub

    first = (pl.program_id(0) % blocks_per_seq) == 0
    xn_ref[:H, :] = jnp.where(first, 0.0, _rmsnorm(halo_ref[...], g_ref[...])).astype(BF16)

    def normalize(r):
        rows = slice(r * sub, (r + 1) * sub)
        xn_ref[H + r * sub:H + (r + 1) * sub, :] = _rmsnorm(h_ref[rows, :], g_ref[...]).astype(BF16)

    chunk_cols = lambda c: pl.ds(pl.multiple_of(c * cf, cf), cf)

    def up(c, r):
        if r == 0:
            xs, dst = xn_ref[:sub + H, :], slice(0, sub + H)
        else:
            xs, dst = xn_ref[H + r * sub:H + (r + 1) * sub, :], slice(H, sub + H)
            y_ref[r % 2, :H, :] = y_ref[(r - 1) % 2, sub:sub + H, :]
        y_ref[r % 2, dst, :cf] = _dot(xs, wup_ref[:, chunk_cols(c)])
        y_ref[r % 2, dst, cf:] = _dot(xs, wup_ref[:, chunk_cols(nc + c)])

    def conv(r, part, cw):
        cols = slice(part * cf, (part + 1) * cf)
        out = y_ref[r % 2, H:, cols] * cw[FFN_CONV - 1:FFN_CONV, :]
        for k in range(1, FFN_CONV):
            out = out + y_ref[r % 2, H - k:H - k + sub, cols] * cw[FFN_CONV - 1 - k:FFN_CONV - k, :]
        return out

    def chunk(c, leading=False, trailing=False):
        cwg, cwu = cw_ref[:, chunk_cols(c)], cw_ref[:, chunk_cols(nc + c)]
        wd = wd_ref[c]
        for r in range(n_sub):
            if r + 1 < n_sub:
                if leading:
                    normalize(r + 1)
                up(c, r + 1)
            elif not trailing:
                up(c + 1, 0)
            down = _dot((_silu(conv(r, 0, cwg)) * conv(r, 1, cwu)).astype(BF16), wd)
            rows = slice(r * sub, (r + 1) * sub)
            if leading:
                acc_ref[rows, :] = down
            elif not trailing:
                acc_ref[rows, :] += down
            else:
                y = h_ref[rows, :] + acc_ref[rows, :] + down
                o_ref[rows, :] = _rmsnorm(y, fg_ref[...]) if final_norm else y

    assert nc >= 3
    normalize(0)
    up(0, 0)
    chunk(0, leading=True)
    lax.fori_loop(1, nc - 1, lambda c, carry: (chunk(c), carry)[1], 0)
    chunk(nc - 1, trailing=True)


def ffn_residual(h, g, w_up, conv_w, w_down, layer, final_g, S, bm, cf, final_norm):
    M, D = h.shape
    d_ff = w_down.shape[1]
    H = BF16_SUBLANES
    nc = d_ff // cf
    sub = min(256, bm)
    assert (bm // sub) % 2 == 0, "the two y_ref slots alternate per sub-block across chunks"
    wd3 = w_down.reshape(-1, nc, cf, D)
    return pl.pallas_call(
        functools.partial(_ffn_kernel, blocks_per_seq=S // bm, final_norm=final_norm, sub=sub),
        grid=(M // bm,),
        in_specs=[pl.BlockSpec((bm, D), lambda i: (i, 0)),
                  pl.BlockSpec((H, D), lambda i: (jnp.maximum(i * (bm // H) - 1, 0), 0)),
                  pl.BlockSpec((1, D), lambda i: (0, 0)),
                  _resident(w_up, layer), _resident(conv_w, layer), _resident(wd3, layer),
                  pl.BlockSpec((1, D), lambda i: (0, 0))],
        out_specs=pl.BlockSpec((bm, D), lambda i: (i, 0)),
        out_shape=jax.ShapeDtypeStruct((M, D), F32),
        scratch_shapes=[pltpu.VMEM((H + bm, D), BF16), pltpu.VMEM((bm, D), F32),
                        pltpu.VMEM((2, H + sub, 2 * cf), F32)],
        compiler_params=_cparams("parallel"),
    )(h, h, g.reshape(1, D), w_up, conv_w, wd3, final_g.reshape(1, D))


def _sgu_kernel(u_ref, v_ref, lg_ref, lb_ref, w_ref, bt_ref, o_ref):
    rows = u_ref.shape[0]
    T = SGU_CHUNK
    v = v_ref[...].astype(F32)
    mu = jnp.mean(v, axis=-1, keepdims=True)
    d = v - mu
    var = jnp.mean(d * d, axis=-1, keepdims=True)
    vn = (d * lax.rsqrt(var + EPS) * lg_ref[...] + lb_ref[...]).astype(BF16)
    causal = (lax.broadcasted_iota(jnp.int32, (T, T), 1) <= lax.broadcasted_iota(jnp.int32, (T, T), 0))
    for g in range(SGU_GROUPS):
        cols = slice(g * SGU_GROUP, (g + 1) * SGU_GROUP)
        wg = jnp.where(causal, w_ref[g], 0.0).astype(BF16)
        bias = bt_ref[:, g:g + 1]
        for c in range(rows // T):
            rs = slice(c * T, (c + 1) * T)
            s = _dot(wg, vn[rs, cols]) + bias
            o_ref[rs, cols] = (u_ref[rs, cols].astype(F32) * s).astype(o_ref.dtype)


def spatial_gating(proj, ln_g, ln_b, w_s, b_s, M, rows):
    return pl.pallas_call(
        _sgu_kernel,
        grid=(M // rows,),
        in_specs=[pl.BlockSpec((rows, SGU_WIDTH), lambda i: (i, 0)),
                  pl.BlockSpec((rows, SGU_WIDTH), lambda i: (i, 1)),
                  pl.BlockSpec((1, SGU_WIDTH), lambda i: (0, 0)),
                  pl.BlockSpec((1, SGU_WIDTH), lambda i: (0, 0)),
                  pl.BlockSpec((SGU_GROUPS, SGU_CHUNK, SGU_CHUNK), lambda i: (0, 0, 0)),
                  pl.BlockSpec((SGU_CHUNK, SGU_GROUPS), lambda i: (0, 0))],
        out_specs=pl.BlockSpec((rows, SGU_WIDTH), lambda i: (i, 0)),
        out_shape=jax.ShapeDtypeStruct((M, SGU_WIDTH), BF16),
        compiler_params=_cparams("parallel"),
    )(proj, proj, ln_g.reshape(1, -1), ln_b.reshape(1, -1), w_s, b_s.T)


def _gdn_intra_kernel(q_ref, k_ref, v_ref, tail_ref, tailt_ref, alog_ref,
                      dtb_ref, u_ref, w_ref, qd_ref, kd_ref, qk_ref, gl_ref):
    rows = q_ref.shape[0]
    C = DN_CHUNK
    HD = DN_HEAD_DIM
    x = jnp.concatenate([q_ref[...], k_ref[...], v_ref[...]], axis=1).astype(F32)

    ii = lax.broadcasted_iota(jnp.int32, (C, C), 0)
    jj = lax.broadcasted_iota(jnp.int32, (C, C), 1)
    lower = jj <= ii
    strict = jj < ii
    su = lax.broadcasted_iota(jnp.int32, (C, LANES), 0)
    ju = lax.broadcasted_iota(jnp.int32, (C, LANES), 1)
    upper_ext = jnp.where(((ju < C) & (su > ju)) | (ju == C), 1.0, 0.0)

    inst = [(c, hh) for hh in range(DN_HEADS) for c in range(rows // C)]
    qs, ks, vs, bs, stacks = [], [], [], [], []
    for hh in range(DN_HEADS):
        qh = x[:, hh * HD:(hh + 1) * HD]
        kh = x[:, DN_WIDTH + hh * HD:DN_WIDTH + (hh + 1) * HD]
        vh = x[:, 2 * DN_WIDTH + hh * HD:2 * DN_WIDTH + (hh + 1) * HD]
        qh = qh * lax.rsqrt(jnp.sum(qh * qh, axis=-1, keepdims=True) + EPS) * HD ** -0.5
        kh = kh * lax.rsqrt(jnp.sum(kh * kh, axis=-1, keepdims=True) + EPS)
        beta = 1.0 / (1.0 + jnp.exp(-tail_ref[:, hh:hh + 1]))
        a_raw = tailt_ref[DN_HEADS + hh:DN_HEADS + hh + 1, :]
        z = a_raw + dtb_ref[0:1, hh:hh + 1]
        softplus = jnp.maximum(z, 0.0) + jnp.log(1.0 + jnp.exp(-jnp.abs(z)))
        g_row = -jnp.exp(alog_ref[0:1, hh:hh + 1]) * softplus
        for c in range(rows // C):
            rs = slice(c * C, (c + 1) * C)
            qs.append(qh[rs]); ks.append(kh[rs]); vs.append(vh[rs]); bs.append(beta[rs])
            gr = jnp.broadcast_to(g_row[:, rs], (C, C))
            stacks += [jnp.where(lower, gr, 0.0), jnp.where(lower, 0.0, gr)]

    stacked = jnp.concatenate(stacks, axis=0)
    s_hi = stacked.astype(BF16)
    s_lo = (stacked - s_hi.astype(F32)).astype(BF16)
    ue = upper_ext.astype(BF16)
    dall = _dot(s_hi, ue) + _dot(s_lo, ue)

    decays, gcs, gc_revs, k16s, kbs = [], [], [], [], []
    for n, (c, hh) in enumerate(inst):
        dext = dall[n * 2 * C:(n + 1) * 2 * C]
        decays.append(jnp.exp(jnp.where(lower, dext[:C, :C], -jnp.inf)))
        gcs.append(dext[:C, C:C + 1])
        gc_revs.append(dext[C:, C:C + 1])
        kbs.append(ks[n] * bs[n])
        k16s.append(ks[n].astype(BF16))
    kq = [_dot_nt(jnp.concatenate([kbs[n], qs[n]], axis=0).astype(BF16), k16s[n]) for n in range(len(inst))]
    kk = [m[:C] for m in kq]
    qk = [m[C:] for m in kq]
    pws = [jnp.where(strict, kk[n] * decays[n], 0.0).astype(BF16) for n in range(len(inst))]
    egc = [jnp.exp(g) for g in gcs]
    rhs = [jnp.concatenate([vs[n] * bs[n], kbs[n] * egc[n]], axis=1) for n in range(len(inst))]
    sols = [rhs[n] - _dot(pws[n], rhs[n].astype(BF16)) for n in range(len(inst))]
    for _ in range(int(math.log2(C)) - 1):
        pws = [_dot(p, p).astype(BF16) for p in pws]
        sols = [s + _dot(p, s.astype(BF16)) for p, s in zip(pws, sols)]
    for n, (c, hh) in enumerate(inst):
        rs = slice(c * C, (c + 1) * C)
        hcols = slice(hh * HD, (hh + 1) * HD)
        u_ref[rs, hcols] = sols[n][:, :HD].astype(u_ref.dtype)
        w_ref[rs, hcols] = sols[n][:, HD:].astype(w_ref.dtype)
        qkd = jnp.where(lower, qk[n] * decays[n], 0.0)
        qk_ref[rs, hcols] = jnp.concatenate([qkd, jnp.zeros_like(qkd)], axis=1).astype(qk_ref.dtype)
        qd_ref[rs, hcols] = (qs[n] * egc[n]).astype(qd_ref.dtype)
        kd_ref[rs, hcols] = (ks[n] * jnp.exp(gc_revs[n])).astype(kd_ref.dtype)
        gl_ref[c * 8:(c + 1) * 8, hcols] = jnp.broadcast_to(egc[n][C - 1:C, :], (8, HD))


def gdn_intra(proj, tail, tail_t, a_log, dt_bias, M, rows):
    c0 = 2 * SGU_WIDTH // DN_WIDTH
    pad = lambda p: jnp.pad(p.reshape(1, -1), ((0, 0), (0, LANES - p.shape[0])))
    seq = lambda dt: jax.ShapeDtypeStruct((M, DN_WIDTH), dt)
    row_spec = pl.BlockSpec((rows, DN_WIDTH), lambda i: (i, 0))
    cur_spec = lambda part: pl.BlockSpec((rows, DN_WIDTH), lambda i: (i, c0 + part))
    return pl.pallas_call(
        _gdn_intra_kernel,
        grid=(M // rows,),
        in_specs=[cur_spec(0), cur_spec(1), cur_spec(2),
                  pl.BlockSpec((rows, LANES), lambda i: (i, 0)),
                  pl.BlockSpec((2 * DN_HEADS, rows), lambda i: (0, i)),
                  pl.BlockSpec((1, LANES), lambda i: (0, 0)),
                  pl.BlockSpec((1, LANES), lambda i: (0, 0))],
        out_specs=[row_spec, row_spec, row_spec, row_spec, row_spec,
                   pl.BlockSpec((rows // DN_CHUNK * 8, DN_WIDTH), lambda i: (i, 0))],
        out_shape=[seq(BF16), seq(BF16), seq(BF16), seq(BF16), seq(BF16),
                   jax.ShapeDtypeStruct((M // DN_CHUNK * 8, DN_WIDTH), F32)],
        compiler_params=_cparams("parallel"),
    )(proj, proj, proj, tail, tail_t, pad(a_log), pad(dt_bias))


def _gdn_scan_kernel(u_ref, w_ref, qd_ref, kd_ref, qk_ref, gl_ref, gate_ref, ng_ref, o_ref, state_ref, *,
                     chunks):
    C = DN_CHUNK
    HD = DN_HEAD_DIM
    B = u_ref.shape[0]

    @pl.when(pl.program_id(0) == 0)
    def _():
        state_ref[...] = jnp.zeros_like(state_ref)

    ng = ng_ref[...]
    inst = [(b, hh) for b in range(B) for hh in range(DN_HEADS)]
    col = lambda hh: slice(hh * HD, (hh + 1) * HD)
    states = [state_ref[b, hh] for b, hh in inst]
    for c in range(chunks):
        rs = slice(c * C, (c + 1) * C)
        kdt = [kd_ref[b, rs, col(hh)].astype(F32).T.astype(BF16) for b, hh in inst]
        st16 = [s.astype(BF16) for s in states]
        ws = [_dot(w_ref[b, rs, col(hh)], st16[n]) for n, (b, hh) in enumerate(inst)]
        qs = [_dot(qd_ref[b, rs, col(hh)], st16[n]) for n, (b, hh) in enumerate(inst)]
        vn16 = [(u_ref[b, rs, col(hh)].astype(F32) - ws[n]).astype(BF16) for n, (b, hh) in enumerate(inst)]
        states = [states[n] * gl_ref[b, c * 8:c * 8 + 1, col(hh)] + _dot(kdt[n], vn16[n])
                  for n, (b, hh) in enumerate(inst)]
        for n, (b, hh) in enumerate(inst):
            o = qs[n] + _dot(qk_ref[b, rs, col(hh)][:, :C], vn16[n])
            o = o * lax.rsqrt(jnp.mean(o * o, axis=-1, keepdims=True) + EPS) * ng
            o_ref[b, rs, col(hh)] = (o * _silu(gate_ref[b, rs, col(hh)].astype(F32))).astype(o_ref.dtype)
    for n, (b, hh) in enumerate(inst):
        state_ref[b, hh] = states[n]


def gdn_scan(u, w, qd, kd, qk, gl, proj3, norm_g, B, S, chunks):
    rows = chunks * DN_CHUNK
    r3 = lambda a: a.reshape(B, S, DN_WIDTH)
    gcol = (2 * SGU_WIDTH + 3 * DN_WIDTH) // DN_WIDTH
    seq_spec = pl.BlockSpec((B, rows, DN_WIDTH), lambda n: (0, n, 0))
    return pl.pallas_call(
        functools.partial(_gdn_scan_kernel, chunks=chunks),
        grid=(S // rows,),
        in_specs=[seq_spec, seq_spec, seq_spec, seq_spec, seq_spec,
                  pl.BlockSpec((B, chunks * 8, DN_WIDTH), lambda n: (0, n, 0)),
                  pl.BlockSpec((B, rows, DN_WIDTH), lambda n: (0, n, gcol)),
                  pl.BlockSpec((1, DN_HEAD_DIM), lambda n: (0, 0))],
        out_specs=seq_spec,
        out_shape=jax.ShapeDtypeStruct((B, S, DN_WIDTH), BF16),
        scratch_shapes=[pltpu.VMEM((B, DN_HEADS, DN_HEAD_DIM, DN_HEAD_DIM), F32)],
        compiler_params=_cparams("arbitrary"),
    )(r3(u), r3(w), r3(qd), r3(kd), r3(qk), gl.reshape(B, S // DN_CHUNK * 8, DN_WIDTH), proj3,
      norm_g.reshape(1, DN_HEAD_DIM))


def _tiles(S):
    rows = min(1024, S)
    return dict(rows=rows, sgu_rows=rows, gdn_rows=min(512, S), scan_chunks=min(8, S // DN_CHUNK),
                ffn_chunk=256)


def _forward(x, mem, mem_norm, norm_mix, norm_xattn, norm_ffn, ev_w_in, pool_w, pool_scale, ev_w_out,
             od_w_in, sgu_ln_g, sgu_ln_b, sgu_w, sgu_b, dn_conv, dn_a_log, dn_dt_bias, dn_norm_g,
             od_w_out, xattn_wq, xattn_wkv, xattn_wo, ffn_w_up, ffn_conv, ffn_w_down, final_norm):
    B, S, D = x.shape
    n_mem = mem.shape[1]
    M = B * S
    depth = norm_mix.shape[0]
    bf = lambda a: a.astype(BF16)
    t = _tiles(S)

    h = x.reshape(M, D)
    mem2 = mem.reshape(B * n_mem, D)
    wkv_all, wq_all, wo_all = bf(xattn_wkv), bf(xattn_wq), bf(xattn_wo)
    w_up_all, w_down_all = bf(ffn_w_up), bf(ffn_w_down)
    for layer in range(depth):
        i = layer // 2
        if layer % 2 == 0:
            proj, = norm_matmul(h, norm_mix[layer], [bf(ev_w_in[i])], [BF16], t["rows"])
            a_out = moba_attention(proj.reshape(B, S, -1), B, S).reshape(M, A_WIDTH)
            b_out = multiscale_pool(proj, bf(pool_w[i]), pool_scale[i], M, S, t["rows"])
            mix_a, mix_b, w_mix = a_out, b_out, ev_w_out[i]
        else:
            main_w = 2 * SGU_WIDTH + 4 * DN_WIDTH
            w_in = od_w_in[i]
            w_tail = jnp.pad(w_in[:, main_w:], ((0, 0), (0, LANES - 2 * DN_HEADS)))
            proj, tail, tail_t = od_projection(h, norm_mix[layer], bf(w_in), bf(w_tail), dn_conv[i],
                                               S, t["rows"])
            c_out = spatial_gating(proj, sgu_ln_g[i], sgu_ln_b[i], sgu_w[i], sgu_b[i], M, t["sgu_rows"])
            u, w, qd, kd, qk, gl = gdn_intra(proj, tail, tail_t, dn_a_log[i], dn_dt_bias[i], M, t["gdn_rows"])
            d_out = gdn_scan(u, w, qd, kd, qk, gl, proj.reshape(B, S, -1), dn_norm_g[i], B, S,
                             t["scan_chunks"])
            mix_a, mix_b, w_mix = c_out, d_out.reshape(M, DN_WIDTH), od_w_out[i]
        qk, vo = xattn_absorb(mem2, mem_norm, wkv_all, wq_all, wo_all, layer, B, n_mem)
        h = mix_xattn_residual(h, mix_a, mix_b, bf(w_mix), norm_xattn[layer], qk, vo, S, n_mem, t["rows"])
        h = ffn_residual(h, norm_ffn[layer], w_up_all, ffn_conv, w_down_all, layer, final_norm, S, t["rows"],
                         t["ffn_chunk"], final_norm=(layer == depth - 1))
    return h.reshape(B, S, D)


def kernel(x, mem, mem_norm, norm_mix, norm_xattn, norm_ffn, ev_w_in, pool_w, pool_scale, ev_w_out, od_w_in, sgu_ln_g, sgu_ln_b, sgu_w, sgu_b, dn_conv, dn_a_log, dn_dt_bias, dn_norm_g, od_w_out, xattn_wq, xattn_wkv, xattn_wo, ffn_w_up, ffn_conv, ffn_w_down, final_norm):
    return _forward(x, mem, mem_norm, norm_mix, norm_xattn, norm_ffn, ev_w_in, pool_w, pool_scale, ev_w_out,
                    od_w_in, sgu_ln_g, sgu_ln_b, sgu_w, sgu_b, dn_conv, dn_a_log, dn_dt_bias, dn_norm_g,
                    od_w_out, xattn_wq, xattn_wkv, xattn_wo, ffn_w_up, ffn_conv, ffn_w_down, final_norm)
```

```python
import functools
import math

import jax
import jax.numpy as jnp
from jax import lax
from jax.experimental import pallas as pl
from jax.experimental.pallas import tpu as pltpu

F32 = jnp.float32
BF16 = jnp.bfloat16
EPS = 1e-6
NEG_BIG = -1e30

VMEM_LIMIT_BYTES = 48 * 1024 * 1024
BF16_SUBLANES = 16
LANES = 128

MOBA_HEADS, MOBA_HEAD_DIM, MOBA_BLOCK, MOBA_TOPK = 8, 64, 256, 3
A_WIDTH = MOBA_HEADS * MOBA_HEAD_DIM
POOL_WINDOWS = (2, 4, 8, 16)
POOL_GROUP = 128
POOL_WIDTH = POOL_GROUP * len(POOL_WINDOWS)
SGU_GROUPS, SGU_GROUP, SGU_CHUNK = 4, 128, 128
SGU_WIDTH = SGU_GROUPS * SGU_GROUP
DN_HEADS, DN_HEAD_DIM, DN_CONV, DN_CHUNK = 4, 128, 4, 64
DN_WIDTH = DN_HEADS * DN_HEAD_DIM
XATTN_HEADS = 4
FFN_CONV = 3


def _cparams(*sem):
    return pltpu.CompilerParams(dimension_semantics=sem, vmem_limit_bytes=VMEM_LIMIT_BYTES)


def _rmsnorm(x, g):
    return x * lax.rsqrt(jnp.mean(x * x, axis=-1, keepdims=True) + EPS) * g


def _silu(x):
    return x * (0.5 * jnp.tanh(0.5 * x) + 0.5)


def _dot(a, b):
    return jnp.dot(a, b, preferred_element_type=F32)


def _resident(arr, layer=None):
    if layer is None:
        return pl.BlockSpec(arr.shape, lambda i: (0,) * arr.ndim, pipeline_mode=pl.Buffered(1))
    return pl.BlockSpec((None,) + arr.shape[1:], lambda i: (layer,) + (0,) * (arr.ndim - 1),
                        pipeline_mode=pl.Buffered(1))


def _dot_nt(a, b, precision=None):
    return lax.dot_general(a, b, (((1,), (1,)), ((), ())), preferred_element_type=F32,
                           precision=precision)


def _norm_mm_kernel(x_ref, g_ref, *refs, bn):
    n = len(refs) // 2
    xn = _rmsnorm(x_ref[...], g_ref[...]).astype(BF16)
    for w_ref, o_ref in zip(refs[:n], refs[n:]):
        N = w_ref.shape[1]
        for c0 in range(0, N, bn):
            c1 = min(c0 + bn, N)
            o_ref[:, c0:c1] = _dot(xn, w_ref[:, c0:c1]).astype(o_ref.dtype)


def norm_matmul(x, g, ws, out_dtypes, bm, bn=512, layer=None):
    M, D = x.shape
    return pl.pallas_call(
        functools.partial(_norm_mm_kernel, bn=bn),
        grid=(M // bm,),
        in_specs=[pl.BlockSpec((bm, D), lambda i: (i, 0)),
                  pl.BlockSpec((1, D), lambda i: (0, 0))]
                 + [_resident(w, layer) for w in ws],
        out_specs=[pl.BlockSpec((bm, w.shape[-1]), lambda i: (i, 0)) for w in ws],
        out_shape=[jax.ShapeDtypeStruct((M, w.shape[-1]), dt) for w, dt in zip(ws, out_dtypes)],
        compiler_params=_cparams("parallel"),
    )(x, g.reshape(1, D), *ws)


def _gelu_tanh(x):
    return 0.5 * x * (1.0 + jnp.tanh(math.sqrt(2.0 / math.pi) * (x + 0.044715 * (x * x * x))))


def _od_proj_kernel(x_ref, halo_ref, g_ref, w_ref, wt_ref, cw_ref, o_ref, t_ref, tt_ref, xe_ref, *,
                    blocks_per_seq, bn):
    bm = x_ref.shape[0]
    H = BF16_SUBLANES
    z_w, qkv_w = 2 * SGU_WIDTH, 3 * DN_WIDTH
    first = (pl.program_id(0) % blocks_per_seq) == 0
    xe_ref[:H, :] = jnp.where(first, 0.0, _rmsnorm(halo_ref[...], g_ref[...])).astype(BF16)
    xe_ref[H:, :] = _rmsnorm(x_ref[...], g_ref[...]).astype(BF16)
    xn = xe_ref[H:, :]
    tail = _dot(xn, wt_ref[...])
    t_ref[...] = tail
    tt_ref[...] = tail.T[:H, :]
    for c0 in range(0, o_ref.shape[1], bn):
        cols = slice(c0, c0 + bn)
        if c0 < z_w:
            o_ref[:, cols] = _gelu_tanh(_dot(xn, w_ref[:, cols])).astype(o_ref.dtype)
        elif c0 < z_w + qkv_w:
            y = _dot(xe_ref[...], w_ref[:, cols])
            cw = cw_ref[:, c0 - z_w:c0 - z_w + bn]
            out = y[H:, :] * cw[DN_CONV - 1:DN_CONV, :]
            for k in range(1, DN_CONV):
                out = out + pltpu.roll(y, k, axis=0)[H:, :] * cw[DN_CONV - 1 - k:DN_CONV - k, :]
            o_ref[:, cols] = _silu(out).astype(o_ref.dtype)
        else:
            o_ref[:, cols] = _dot(xn, w_ref[:, cols]).astype(o_ref.dtype)


def od_projection(x, g, w, w_tail, conv_w, S, bm, bn=512):
    M, D = x.shape
    N = 2 * SGU_WIDTH + 4 * DN_WIDTH
    H = BF16_SUBLANES
    return pl.pallas_call(
        functools.partial(_od_proj_kernel, blocks_per_seq=S // bm, bn=bn),
        grid=(M // bm,),
        in_specs=[pl.BlockSpec((bm, D), lambda i: (i, 0)),
                  pl.BlockSpec((H, D), lambda i: (jnp.maximum(i * (bm // H) - 1, 0), 0)),
                  pl.BlockSpec((1, D), lambda i: (0, 0)),
                  _resident(w), _resident(w_tail), _resident(conv_w)],
        out_specs=[pl.BlockSpec((bm, N), lambda i: (i, 0)), pl.BlockSpec((bm, LANES), lambda i: (i, 0)),
                   pl.BlockSpec((H, bm), lambda i: (0, i))],
        out_shape=[jax.ShapeDtypeStruct((M, N), BF16), jax.ShapeDtypeStruct((M, LANES), F32),
                   jax.ShapeDtypeStruct((H, M), F32)],
        scratch_shapes=[pltpu.VMEM((H + bm, D), BF16)],
        compiler_params=_cparams("parallel"),
    )(x, x, g.reshape(1, D), w, w_tail, conv_w)


def _moba_kernel(q_ref, k_ref, v_ref, o_ref, kme_ref, vt_ref, sel_ref, m_ref, acc_ref, s_ref, *,
                 nb, nbp, unroll, pairs):
    BS = MOBA_BLOCK
    HD = MOBA_HEAD_DIM
    n_heads = 2 * pairs
    i = pl.program_id(2)
    lane = lax.broadcasted_iota(jnp.int32, (1, LANES), 1)
    head_lanes = (lane < HD, lane >= HD)
    pair_lanes = lambda u: slice((u // 2) * LANES, (u // 2 + 1) * LANES)

    @pl.when(i == 0)
    def _():
        kme_ref[...] = jnp.zeros_like(kme_ref)
        for n in range(nb):
            rows = slice(n * BS, (n + 1) * BS)
            mean = jnp.sum(k_ref[0, rows, :].astype(F32), axis=0, keepdims=True) / BS
            for u in range(n_heads):
                kme_ref[u // 2, (u % 2) * nbp + n:(u % 2) * nbp + n + 1, :] = jnp.where(
                    head_lanes[u % 2], mean[:, pair_lanes(u)], 0.0)
            vt_ref[:, rows] = v_ref[0, rows, :].astype(F32).T.astype(BF16)

    scale = HD ** -0.5 * math.log2(math.e)
    q_t = [q_ref[0, :, p * LANES:(p + 1) * LANES].astype(F32).T for p in range(pairs)]
    pair_row = lax.broadcasted_iota(jnp.int32, (LANES, 1), 0)
    head_rows = (pair_row < HD, pair_row >= HD)
    q_aug = [jnp.where(head_rows[u % 2], q_t[u // 2] * scale, 0.0).astype(BF16) for u in range(n_heads)]

    gates = []
    for p in range(pairs):
        km = kme_ref[p]
        k_hi = km.astype(BF16)
        r1 = km - k_hi.astype(F32)
        k_mid = r1.astype(BF16)
        k_lo = (r1 - k_mid.astype(F32)).astype(BF16)
        q16 = q_t[p].astype(BF16)
        gates.append(_dot(k_hi, q16) + _dot(k_mid, q16) + _dot(k_lo, q16))
    blk = lax.broadcasted_iota(jnp.int32, (nbp, 1), 0).astype(F32)
    valid = blk < i.astype(F32)
    for u in range(n_heads):
        g = jnp.where(valid, gates[u // 2][(u % 2) * nbp:(u % 2 + 1) * nbp], -jnp.inf)
        sel = jnp.zeros(g.shape, jnp.bool_)
        for _ in range(MOBA_TOPK):
            mx = jnp.max(g, axis=0, keepdims=True)
            idx = jnp.min(jnp.where(g == mx, blk, float(1 << 20)), axis=0, keepdims=True)
            pick = blk == idx
            sel = sel | pick
            g = jnp.where(pick, -jnp.inf, g)
        sel_ref[u, :nbp, :] = jnp.where(sel & valid, 1.0, 0.0)
        sel_ref[u, nbp:, :] = jnp.zeros((8, BS), F32)
        m_ref[u] = jnp.full((1, BS), NEG_BIG, F32)
        acc_ref[u] = jnp.zeros(acc_ref.shape[1:], F32)

    krow = lax.broadcasted_iota(jnp.int32, (BS, BS), 0)
    qcol = lax.broadcasted_iota(jnp.int32, (BS, BS), 1)
    PVR = HD + BF16_SUBLANES
    pv_rows = (slice(0, PVR), slice(LANES - PVR, LANES))
    pv_row = lax.broadcasted_iota(jnp.int32, (PVR, 1), 0)
    is_dim = (pv_row < HD, pv_row >= PVR - HD)

    def block_start(j):
        return pl.multiple_of(jnp.minimum(j, i) * BS, BS)

    def produce(g, slot):
        for t in range(unroll):
            rows = pl.ds(block_start(g * unroll + t), BS)
            for u in range(n_heads):
                s_ref[slot, u, t * BS:(t + 1) * BS, :] = _dot(
                    k_ref[0, rows, pair_lanes(u)], q_aug[u]).astype(BF16)

    def softmax_update(sts, sels, starts):
        heads = range(n_heads)
        m_new, alpha = [], []
        for u in heads:
            cand = jnp.full((1, BS), NEG_BIG, F32)
            for st, sel in zip(sts[u], sels[u]):
                mx = jnp.max(st.reshape(BS // BF16_SUBLANES, BF16_SUBLANES, BS), axis=0)
                mx = jnp.max(mx.astype(F32), axis=0, keepdims=True)
                cand = jnp.maximum(cand, mx if sel is None else jnp.where(sel, mx, NEG_BIG))
            m_old = m_ref[u]
            m_new.append(jnp.maximum(m_old, cand))
            alpha.append(jnp.exp2(m_old - m_new[u]))
            m_ref[u] = m_new[u]
        ps = []
        for u in heads:
            pu = []
            for st, sel in zip(sts[u], sels[u]):
                sub = m_new[u] if sel is None else jnp.where(sel, m_new[u], -NEG_BIG)
                pu.append(jnp.exp2(st - sub.astype(BF16)))
            ps.append(pu[0] if len(pu) == 1 else jnp.concatenate(pu, axis=0))
        pv = []
        for u in heads:
            rows = slice((u // 2) * LANES + pv_rows[u % 2].start, (u // 2) * LANES + pv_rows[u % 2].stop)
            vts = [jnp.where(is_dim[u % 2], vt_ref[rows, pl.ds(st0, BS)], jnp.ones((), BF16))
                   for st0 in starts]
            pv.append(_dot(vts[0] if len(vts) == 1 else jnp.concatenate(vts, axis=1), ps[u]))
        for u in heads:
            acc_ref[u] = acc_ref[u] * alpha[u] + pv[u]

    def consume(g, slot):
        js = [g * unroll + t for t in range(unroll)]
        sts = [[s_ref[slot, u, t * BS:(t + 1) * BS, :] for t in range(unroll)] for u in range(n_heads)]
        sels = [[sel_ref[u, pl.ds(j, 1), :] > 0.5 for j in js] for u in range(n_heads)]
        softmax_update(sts, sels, [block_start(j) for j in js])

    slots = s_ref.shape[0]
    assert unroll == 1, "the group after the last past block must be exactly the tile's own block"

    def body(gg, c):
        for t in range(slots):
            produce(slots * gg + t + 1, (t + 1) % slots)
            consume(slots * gg + t, t)
        return c

    n_groups = (i + unroll - 1) // unroll
    produce(0, 0)
    lax.fori_loop(0, (n_groups + slots - 1) // slots, body, 0)
    softmax_update([[jnp.where(krow <= qcol, s_ref[0, u, :BS, :], -jnp.inf)] for u in range(n_heads)],
                   [[None]] * n_heads, [block_start(i)])

    outs = []
    for u in range(n_heads):
        a = acc_ref[u]
        outs.append(a[:HD] / a[HD:HD + 1, :] if u % 2 == 0 else a[PVR - HD:] / a[0:1, :])
    o_ref[0] = jnp.concatenate(outs, axis=0).T.astype(o_ref.dtype)


def moba_attention(proj, B, S):
    nb = S // MOBA_BLOCK
    nbp = -(-nb // 8) * 8
    pairs = 4
    width = pairs * LANES
    groups = A_WIDTH // width
    unroll = 1
    slots = 3
    n_heads = 2 * pairs
    return pl.pallas_call(
        functools.partial(_moba_kernel, nb=nb, nbp=nbp, unroll=unroll, pairs=pairs),
        grid=(B, groups, nb),
        in_specs=[pl.BlockSpec((1, MOBA_BLOCK, width), lambda b, p, i: (b, i, p)),
                  pl.BlockSpec((1, S, width), lambda b, p, i: (b, 0, groups + p), pipeline_mode=pl.Buffered(1)),
                  pl.BlockSpec((1, S, width), lambda b, p, i: (b, 0, 2 * groups + p), pipeline_mode=pl.Buffered(1))],
        out_specs=pl.BlockSpec((1, MOBA_BLOCK, width), lambda b, p, i: (b, i, p)),
        out_shape=jax.ShapeDtypeStruct((B, S, A_WIDTH), BF16),
        scratch_shapes=[pltpu.VMEM((pairs, 2 * nbp, LANES), F32),
                        pltpu.VMEM((width, S), BF16),
                        pltpu.VMEM((n_heads, nbp + 8, MOBA_BLOCK), F32),
                        pltpu.VMEM((n_heads, 1, MOBA_BLOCK), F32),
                        pltpu.VMEM((n_heads, MOBA_HEAD_DIM + BF16_SUBLANES, MOBA_BLOCK), F32),
                        pltpu.VMEM((slots, n_heads, unroll * MOBA_BLOCK, MOBA_BLOCK), BF16)],
        compiler_params=_cparams("parallel", "parallel", "arbitrary"),
    )(proj, proj, proj)


def _pool_kernel(p_ref, halo_ref, w_ref, sc_ref, o_ref, *, blocks_per_seq):
    bm = p_ref.shape[0]
    H = BF16_SUBLANES
    i = pl.program_id(0)
    first = (i % blocks_per_seq) == 0
    t1 = (lax.broadcasted_iota(jnp.int32, (bm, 1), 0) + (i % blocks_per_seq) * bm + 1).astype(F32)
    for g, w in enumerate(POOL_WINDOWS):
        cols = slice(g * POOL_GROUP, (g + 1) * POOL_GROUP)
        cur = p_ref[:, cols].astype(F32)
        halo = jnp.where(first, 0.0, halo_ref[:, cols].astype(F32))
        ext = jnp.concatenate([halo, cur], axis=0)
        acc = ext
        sh = 1
        while sh < w:
            acc = acc + pltpu.roll(acc, sh, axis=0)
            sh *= 2
        win = acc[H:, :]
        pooled = win / jnp.minimum(t1, float(w)) - cur
        y = _dot(pooled.astype(BF16), w_ref[g])
        o_ref[:, cols] = (y * sc_ref[:, cols]).astype(o_ref.dtype)


def multiscale_pool(proj, pool_w, pool_scale, M, S, bm):
    H = BF16_SUBLANES
    pcol = 3 * A_WIDTH // POOL_WIDTH
    return pl.pallas_call(
        functools.partial(_pool_kernel, blocks_per_seq=S // bm),
        grid=(M // bm,),
        in_specs=[pl.BlockSpec((bm, POOL_WIDTH), lambda i: (i, pcol)),
                  pl.BlockSpec((H, POOL_WIDTH), lambda i: (jnp.maximum(i * (bm // H) - 1, 0), pcol)),
                  pl.BlockSpec((len(POOL_WINDOWS), POOL_GROUP, POOL_GROUP), lambda i: (0, 0, 0)),
                  pl.BlockSpec((1, POOL_WIDTH), lambda i: (0, 0))],
        out_specs=pl.BlockSpec((bm, POOL_WIDTH), lambda i: (i, 0)),
        out_shape=jax.ShapeDtypeStruct((M, POOL_WIDTH), BF16),
        compiler_params=_cparams("parallel"),
    )(proj, proj, pool_w, pool_scale.reshape(1, POOL_WIDTH))


def _xattn_absorb_kernel(mem_ref, g_ref, wkv_ref, wq_ref, wo_ref, qk_ref, vo_ref):
    n_mem, D = mem_ref.shape
    hd = D // XATTN_HEADS
    mn = _rmsnorm(mem_ref[...], g_ref[...]).astype(BF16)
    kv = _dot(mn, wkv_ref[...].astype(BF16)).astype(BF16)
    for hh in range(XATTN_HEADS):
        cols = slice(hh * hd, (hh + 1) * hd)
        mcols = slice(hh * n_mem, (hh + 1) * n_mem)
        qk_ref[0, :, mcols] = (_dot_nt(wq_ref[:, cols].astype(BF16), kv[:, cols]) * hd ** -0.5).astype(BF16)
        vo_ref[0, mcols, :] = _dot(kv[:, D + hh * hd:D + (hh + 1) * hd], wo_ref[cols, :].astype(BF16)).astype(BF16)


def xattn_absorb(mem2, g, wkv, wq, wo, layer, B, n_mem):
    D = mem2.shape[1]
    HM = XATTN_HEADS * n_mem
    return pl.pallas_call(
        _xattn_absorb_kernel,
        grid=(B,),
        in_specs=[pl.BlockSpec((n_mem, D), lambda b: (b, 0)),
                  pl.BlockSpec((1, D), lambda b: (0, 0)),
                  _resident(wkv, layer), _resident(wq, layer), _resident(wo, layer)],
        out_specs=[pl.BlockSpec((1, D, HM), lambda b: (b, 0, 0)), pl.BlockSpec((1, HM, D), lambda b: (b, 0, 0))],
        out_shape=[jax.ShapeDtypeStruct((B, D, HM), BF16), jax.ShapeDtypeStruct((B, HM, D), BF16)],
        compiler_params=_cparams("parallel"),
    )(mem2, g.reshape(1, D), wkv, wq, wo)


def _mix_xattn_kernel(h_ref, a_ref, b_ref, wm_ref, g_ref, qk_ref, vo_ref, o_ref, *, n_mem):
    ka = a_ref.shape[1]
    h = h_ref[...] + _dot(a_ref[...], wm_ref[:ka, :]) + _dot(b_ref[...], wm_ref[ka:, :])
    xn = _rmsnorm(h, g_ref[...]).astype(BF16)
    s = _dot(xn, qk_ref[...])
    ps = []
    for hh in range(XATTN_HEADS):
        sh = s[:, hh * n_mem:(hh + 1) * n_mem]
        p = jnp.exp(sh - jnp.max(sh, axis=1, keepdims=True))
        ps.append((p / jnp.sum(p, axis=1, keepdims=True)).astype(BF16))
    o_ref[...] = h + _dot(jnp.concatenate(ps, axis=1), vo_ref[...])


def mix_xattn_residual(h, a, b, w_mix, g, qk, vo, S, n_mem, bm):
    M, D = h.shape
    ka, kb = a.shape[1], b.shape[1]
    bps = S // bm
    HM = qk.shape[2]
    return pl.pallas_call(
        functools.partial(_mix_xattn_kernel, n_mem=n_mem),
        grid=(M // bm,),
        in_specs=[pl.BlockSpec((bm, D), lambda i: (i, 0)),
                  pl.BlockSpec((bm, ka), lambda i: (i, 0)),
                  pl.BlockSpec((bm, kb), lambda i: (i, 0)),
                  _resident(w_mix),
                  pl.BlockSpec((1, D), lambda i: (0, 0)),
                  pl.BlockSpec((None, D, HM), lambda i: (i // bps, 0, 0)),
                  pl.BlockSpec((None, HM, D), lambda i: (i // bps, 0, 0))],
        out_specs=pl.BlockSpec((bm, D), lambda i: (i, 0)),
        out_shape=jax.ShapeDtypeStruct((M, D), F32),
        compiler_params=_cparams("parallel"),
    )(h, a, b, w_mix, g.reshape(1, D), qk, vo)


def _ffn_kernel(h_ref, halo_ref, g_ref, wup_ref, cw_ref, wd_ref, fg_ref, o_ref,
                xn_ref, acc_ref, y_ref, *, blocks_per_seq, final_norm, sub):
    H = BF16_SUBLANES
    nc, cf = wd_ref.shape[0], wd_ref.shape[1]
    n_sub = acc_ref.shape[0] // sub

    first = (pl.program_id(0) % blocks_per_seq) == 0
    xn_ref[:H, :] = jnp.where(first, 0.0, _rmsnorm(halo_ref[...], g_ref[...])).astype(BF16)

    def normalize(r):
        rows = slice(r * sub, (r + 1) * sub)
        xn_ref[H + r * sub:H + (r + 1) * sub, :] = _rmsnorm(h_ref[rows, :], g_ref[...]).astype(BF16)

    chunk_cols = lambda c: pl.ds(pl.multiple_of(c * cf, cf), cf)

    def up(c, r):
        if r == 0:
            xs, dst = xn_ref[:sub + H, :], slice(0, sub + H)
        else:
            xs, dst = xn_ref[H + r * sub:H + (r + 1) * sub, :], slice(H, sub + H)
            y_ref[r % 2, :H, :] = y_ref[(r - 1) % 2, sub:sub + H, :]
        y_ref[r % 2, dst, :cf] = _dot(xs, wup_ref[:, chunk_cols(c)])
        y_ref[r % 2, dst, cf:] = _dot(xs, wup_ref[:, chunk_cols(nc + c)])

    def conv(r, part, cw):
        cols = slice(part * cf, (part + 1) * cf)
        out = y_ref[r % 2, H:, cols] * cw[FFN_CONV - 1:FFN_CONV, :]
        for k in range(1, FFN_CONV):
            out = out + y_ref[r % 2, H - k:H - k + sub, cols] * cw[FFN_CONV - 1 - k:FFN_CONV - k, :]
        return out

    def chunk(c, leading=False, trailing=False):
        cwg, cwu = cw_ref[:, chunk_cols(c)], cw_ref[:, chunk_cols(nc + c)]
        wd = wd_ref[c]
        for r in range(n_sub):
            if r + 1 < n_sub:
                if leading:
                    normalize(r + 1)
                up(c, r + 1)
            elif not trailing:
                up(c + 1, 0)
            down = _dot((_silu(conv(r, 0, cwg)) * conv(r, 1, cwu)).astype(BF16), wd)
            rows = slice(r * sub, (r + 1) * sub)
            if leading:
                acc_ref[rows, :] = down
            elif not trailing:
                acc_ref[rows, :] += down
            else:
                y = h_ref[rows, :] + acc_ref[rows, :] + down
                o_ref[rows, :] = _rmsnorm(y, fg_ref[...]) if final_norm else y

    assert nc >= 3
    normalize(0)
    up(0, 0)
    chunk(0, leading=True)
    lax.fori_loop(1, nc - 1, lambda c, carry: (chunk(c), carry)[1], 0)
    chunk(nc - 1, trailing=True)


def ffn_residual(h, g, w_up, conv_w, w_down, layer, final_g, S, bm, cf, final_norm):
    M, D = h.shape
    d_ff = w_down.shape[1]
    H = BF16_SUBLANES
    nc = d_ff // cf
    sub = min(256, bm)
    assert (bm // sub) % 2 == 0, "the two y_ref slots alternate per sub-block across chunks"
    wd3 = w_down.reshape(-1, nc, cf, D)
    return pl.pallas_call(
        functools.partial(_ffn_kernel, blocks_per_seq=S // bm, final_norm=final_norm, sub=sub),
        grid=(M // bm,),
        in_specs=[pl.BlockSpec((bm, D), lambda i: (i, 0)),
                  pl.BlockSpec((H, D), lambda i: (jnp.maximum(i * (bm // H) - 1, 0), 0)),
                  pl.BlockSpec((1, D), lambda i: (0, 0)),
                  _resident(w_up, layer), _resident(conv_w, layer), _resident(wd3, layer),
                  pl.BlockSpec((1, D), lambda i: (0, 0))],
        out_specs=pl.BlockSpec((bm, D), lambda i: (i, 0)),
        out_shape=jax.ShapeDtypeStruct((M, D), F32),
        scratch_shapes=[pltpu.VMEM((H + bm, D), BF16), pltpu.VMEM((bm, D), F32),
                        pltpu.VMEM((2, H + sub, 2 * cf), F32)],
        compiler_params=_cparams("parallel"),
    )(h, h, g.reshape(1, D), w_up, conv_w, wd3, final_g.reshape(1, D))


def _sgu_kernel(u_ref, v_ref, lg_ref, lb_ref, w_ref, bt_ref, o_ref):
    rows = u_ref.shape[0]
    T = SGU_CHUNK
    v = v_ref[...].astype(F32)
    mu = jnp.mean(v, axis=-1, keepdims=True)
    d = v - mu
    var = jnp.mean(d * d, axis=-1, keepdims=True)
    vn = (d * lax.rsqrt(var + EPS) * lg_ref[...] + lb_ref[...]).astype(BF16)
    causal = (lax.broadcasted_iota(jnp.int32, (T, T), 1) <= lax.broadcasted_iota(jnp.int32, (T, T), 0))
    for g in range(SGU_GROUPS):
        cols = slice(g * SGU_GROUP, (g + 1) * SGU_GROUP)
        wg = jnp.where(causal, w_ref[g], 0.0).astype(BF16)
        bias = bt_ref[:, g:g + 1]
        for c in range(rows // T):
            rs = slice(c * T, (c + 1) * T)
            s = _dot(wg, vn[rs, cols]) + bias
            o_ref[rs, cols] = (u_ref[rs, cols].astype(F32) * s).astype(o_ref.dtype)


def spatial_gating(proj, ln_g, ln_b, w_s, b_s, M, rows):
    return pl.pallas_call(
        _sgu_kernel,
        grid=(M // rows,),
        in_specs=[pl.BlockSpec((rows, SGU_WIDTH), lambda i: (i, 0)),
                  pl.BlockSpec((rows, SGU_WIDTH), lambda i: (i, 1)),
                  pl.BlockSpec((1, SGU_WIDTH), lambda i: (0, 0)),
                  pl.BlockSpec((1, SGU_WIDTH), lambda i: (0, 0)),
                  pl.BlockSpec((SGU_GROUPS, SGU_CHUNK, SGU_CHUNK), lambda i: (0, 0, 0)),
                  pl.BlockSpec((SGU_CHUNK, SGU_GROUPS), lambda i: (0, 0))],
        out_specs=pl.BlockSpec((rows, SGU_WIDTH), lambda i: (i, 0)),
        out_shape=jax.ShapeDtypeStruct((M, SGU_WIDTH), BF16),
        compiler_params=_cparams("parallel"),
    )(proj, proj, ln_g.reshape(1, -1), ln_b.reshape(1, -1), w_s, b_s.T)


def _gdn_intra_kernel(q_ref, k_ref, v_ref, tail_ref, tailt_ref, alog_ref,
                      dtb_ref, u_ref, w_ref, qd_ref, kd_ref, qk_ref, gl_ref):
    rows = q_ref.shape[0]
    C = DN_CHUNK
    HD = DN_HEAD_DIM
    x = jnp.concatenate([q_ref[...], k_ref[...], v_ref[...]], axis=1).astype(F32)

    ii = lax.broadcasted_iota(jnp.int32, (C, C), 0)
    jj = lax.broadcasted_iota(jnp.int32, (C, C), 1)
    lower = jj <= ii
    strict = jj < ii
    su = lax.broadcasted_iota(jnp.int32, (C, LANES), 0)
    ju = lax.broadcasted_iota(jnp.int32, (C, LANES), 1)
    upper_ext = jnp.where(((ju < C) & (su > ju)) | (ju == C), 1.0, 0.0)

    inst = [(c, hh) for hh in range(DN_HEADS) for c in range(rows // C)]
    qs, ks, vs, bs, stacks = [], [], [], [], []
    for hh in range(DN_HEADS):
        qh = x[:, hh * HD:(hh + 1) * HD]
        kh = x[:, DN_WIDTH + hh * HD:DN_WIDTH + (hh + 1) * HD]
        vh = x[:, 2 * DN_WIDTH + hh * HD:2 * DN_WIDTH + (hh + 1) * HD]
        qh = qh * lax.rsqrt(jnp.sum(qh * qh, axis=-1, keepdims=True) + EPS) * HD ** -0.5
        kh = kh * lax.rsqrt(jnp.sum(kh * kh, axis=-1, keepdims=True) + EPS)
        beta = 1.0 / (1.0 + jnp.exp(-tail_ref[:, hh:hh + 1]))
        a_raw = tailt_ref[DN_HEADS + hh:DN_HEADS + hh + 1, :]
        z = a_raw + dtb_ref[0:1, hh:hh + 1]
        softplus = jnp.maximum(z, 0.0) + jnp.log(1.0 + jnp.exp(-jnp.abs(z)))
        g_row = -jnp.exp(alog_ref[0:1, hh:hh + 1]) * softplus
        for c in range(rows // C):
            rs = slice(c * C, (c + 1) * C)
            qs.append(qh[rs]); ks.append(kh[rs]); vs.append(vh[rs]); bs.append(beta[rs])
            gr = jnp.broadcast_to(g_row[:, rs], (C, C))
            stacks += [jnp.where(lower, gr, 0.0), jnp.where(lower, 0.0, gr)]

    stacked = jnp.concatenate(stacks, axis=0)
    s_hi = stacked.astype(BF16)
    s_lo = (stacked - s_hi.astype(F32)).astype(BF16)
    ue = upper_ext.astype(BF16)
    dall = _dot(s_hi, ue) + _dot(s_lo, ue)

    decays, gcs, gc_revs, k16s, kbs = [], [], [], [], []
    for n, (c, hh) in enumerate(inst):
        dext = dall[n * 2 * C:(n + 1) * 2 * C]
        decays.append(jnp.exp(jnp.where(lower, dext[:C, :C], -jnp.inf)))
        gcs.append(dext[:C, C:C + 1])
        gc_revs.append(dext[C:, C:C + 1])
        kbs.append(ks[n] * bs[n])
        k16s.append(ks[n].astype(BF16))
    kq = [_dot_nt(jnp.concatenate([kbs[n], qs[n]], axis=0).astype(BF16), k16s[n]) for n in range(len(inst))]
    kk = [m[:C] for m in kq]
    qk = [m[C:] for m in kq]
    pws = [jnp.where(strict, kk[n] * decays[n], 0.0).astype(BF16) for n in range(len(inst))]
    egc = [jnp.exp(g) for g in gcs]
    rhs = [jnp.concatenate([vs[n] * bs[n], kbs[n] * egc[n]], axis=1) for n in range(len(inst))]
    sols = [rhs[n] - _dot(pws[n], rhs[n].astype(BF16)) for n in range(len(inst))]
    for _ in range(int(math.log2(C)) - 1):
        pws = [_dot(p, p).astype(BF16) for p in pws]
        sols = [s + _dot(p, s.astype(BF16)) for p, s in zip(pws, sols)]
    for n, (c, hh) in enumerate(inst):
        rs = slice(c * C, (c + 1) * C)
        hcols = slice(hh * HD, (hh + 1) * HD)
        u_ref[rs, hcols] = sols[n][:, :HD].astype(u_ref.dtype)
        w_ref[rs, hcols] = sols[n][:, HD:].astype(w_ref.dtype)
        qkd = jnp.where(lower, qk[n] * decays[n], 0.0)
        qk_ref[rs, hcols] = jnp.concatenate([qkd, jnp.zeros_like(qkd)], axis=1).astype(qk_ref.dtype)
        qd_ref[rs, hcols] = (qs[n] * egc[n]).astype(qd_ref.dtype)
        kd_ref[rs, hcols] = (ks[n] * jnp.exp(gc_revs[n])).astype(kd_ref.dtype)
        gl_ref[c * 8:(c + 1) * 8, hcols] = jnp.broadcast_to(egc[n][C - 1:C, :], (8, HD))


def gdn_intra(proj, tail, tail_t, a_log, dt_bias, M, rows):
    c0 = 2 * SGU_WIDTH // DN_WIDTH
    pad = lambda p: jnp.pad(p.reshape(1, -1), ((0, 0), (0, LANES - p.shape[0])))
    seq = lambda dt: jax.ShapeDtypeStruct((M, DN_WIDTH), dt)
    row_spec = pl.BlockSpec((rows, DN_WIDTH), lambda i: (i, 0))
    cur_spec = lambda part: pl.BlockSpec((rows, DN_WIDTH), lambda i: (i, c0 + part))
    return pl.pallas_call(
        _gdn_intra_kernel,
        grid=(M // rows,),
        in_specs=[cur_spec(0), cur_spec(1), cur_spec(2),
                  pl.BlockSpec((rows, LANES), lambda i: (i, 0)),
                  pl.BlockSpec((2 * DN_HEADS, rows), lambda i: (0, i)),
                  pl.BlockSpec((1, LANES), lambda i: (0, 0)),
                  pl.BlockSpec((1, LANES), lambda i: (0, 0))],
        out_specs=[row_spec, row_spec, row_spec, row_spec, row_spec,
                   pl.BlockSpec((rows // DN_CHUNK * 8, DN_WIDTH), lambda i: (i, 0))],
        out_shape=[seq(BF16), seq(BF16), seq(BF16), seq(BF16), seq(BF16),
                   jax.ShapeDtypeStruct((M // DN_CHUNK * 8, DN_WIDTH), F32)],
        compiler_params=_cparams("parallel"),
    )(proj, proj, proj, tail, tail_t, pad(a_log), pad(dt_bias))


def _gdn_scan_kernel(u_ref, w_ref, qd_ref, kd_ref, qk_ref, gl_ref, gate_ref, ng_ref, o_ref, state_ref, *,
                     chunks):
    C = DN_CHUNK
    HD = DN_HEAD_DIM
    B = u_ref.shape[0]

    @pl.when(pl.program_id(0) == 0)
    def _():
        state_ref[...] = jnp.zeros_like(state_ref)

    ng = ng_ref[...]
    inst = [(b, hh) for b in range(B) for hh in range(DN_HEADS)]
    col = lambda hh: slice(hh * HD, (hh + 1) * HD)
    states = [state_ref[b, hh] for b, hh in inst]
    for c in range(chunks):
        rs = slice(c * C, (c + 1) * C)
        kdt = [kd_ref[b, rs, col(hh)].astype(F32).T.astype(BF16) for b, hh in inst]
        st16 = [s.astype(BF16) for s in states]
        ws = [_dot(w_ref[b, rs, col(hh)], st16[n]) for n, (b, hh) in enumerate(inst)]
        qs = [_dot(qd_ref[b, rs, col(hh)], st16[n]) for n, (b, hh) in enumerate(inst)]
        vn16 = [(u_ref[b, rs, col(hh)].astype(F32) - ws[n]).astype(BF16) for n, (b, hh) in enumerate(inst)]
        states = [states[n] * gl_ref[b, c * 8:c * 8 + 1, col(hh)] + _dot(kdt[n], vn16[n])
                  for n, (b, hh) in enumerate(inst)]
        for n, (b, hh) in enumerate(inst):
            o = qs[n] + _dot(qk_ref[b, rs, col(hh)][:, :C], vn16[n])
            o = o * lax.rsqrt(jnp.mean(o * o, axis=-1, keepdims=True) + EPS) * ng
            o_ref[b, rs, col(hh)] = (o * _silu(gate_ref[b, rs, col(hh)].astype(F32))).astype(o_ref.dtype)
    for n, (b, hh) in enumerate(inst):
        state_ref[b, hh] = states[n]


def gdn_scan(u, w, qd, kd, qk, gl, proj3, norm_g, B, S, chunks):
    rows = chunks * DN_CHUNK
    r3 = lambda a: a.reshape(B, S, DN_WIDTH)
    gcol = (2 * SGU_WIDTH + 3 * DN_WIDTH) // DN_WIDTH
    seq_spec = pl.BlockSpec((B, rows, DN_WIDTH), lambda n: (0, n, 0))
    return pl.pallas_call(
        functools.partial(_gdn_scan_kernel, chunks=chunks),
        grid=(S // rows,),
        in_specs=[seq_spec, seq_spec, seq_spec, seq_spec, seq_spec,
                  pl.BlockSpec((B, chunks * 8, DN_WIDTH), lambda n: (0, n, 0)),
                  pl.BlockSpec((B, rows, DN_WIDTH), lambda n: (0, n, gcol)),
                  pl.BlockSpec((1, DN_HEAD_DIM), lambda n: (0, 0))],
        out_specs=seq_spec,
        out_shape=jax.ShapeDtypeStruct((B, S, DN_WIDTH), BF16),
        scratch_shapes=[pltpu.VMEM((B, DN_HEADS, DN_HEAD_DIM, DN_HEAD_DIM), F32)],
        compiler_params=_cparams("arbitrary"),
    )(r3(u), r3(w), r3(qd), r3(kd), r3(qk), gl.reshape(B, S // DN_CHUNK * 8, DN_WIDTH), proj3,
      norm_g.reshape(1, DN_HEAD_DIM))


def _tiles(S):
    rows = min(1024, S)
    return dict(rows=rows, sgu_rows=rows, gdn_rows=min(512, S), scan_chunks=min(8, S // DN_CHUNK),
                ffn_chunk=256)


def _forward(x, mem, mem_norm, norm_mix, norm_xattn, norm_ffn, ev_w_in, pool_w, pool_scale, ev_w_out,
             od_w_in, sgu_ln_g, sgu_ln_b, sgu_w, sgu_b, dn_conv, dn_a_log, dn_dt_bias, dn_norm_g,
             od_w_out, xattn_wq, xattn_wkv, xattn_wo, ffn_w_up, ffn_conv, ffn_w_down, final_norm):
    B, S, D = x.shape
    n_mem = mem.shape[1]
    M = B * S
    depth = norm_mix.shape[0]
    bf = lambda a: a.astype(BF16)
    t = _tiles(S)

    h = x.reshape(M, D)
    mem2 = mem.reshape(B * n_mem, D)
    w_up_all, w_down_all = bf(ffn_w_up), bf(ffn_w_down)
    for layer in range(depth):
        i = layer // 2
        if layer % 2 == 0:
            proj, = norm_matmul(h, norm_mix[layer], [bf(ev_w_in[i])], [BF16], t["rows"])
            a_out = moba_attention(proj.reshape(B, S, -1), B, S).reshape(M, A_WIDTH)
            b_out = multiscale_pool(proj, bf(pool_w[i]), pool_scale[i], M, S, t["rows"])
            mix_a, mix_b, w_mix = a_out, b_out, ev_w_out[i]
        else:
            main_w = 2 * SGU_WIDTH + 4 * DN_WIDTH
            w_in = od_w_in[i]
            w_tail = jnp.pad(w_in[:, main_w:], ((0, 0), (0, LANES - 2 * DN_HEADS)))
            proj, tail, tail_t = od_projection(h, norm_mix[layer], bf(w_in), bf(w_tail), dn_conv[i],
                                               S, t["rows"])
            c_out = spatial_gating(proj, sgu_ln_g[i], sgu_ln_b[i], sgu_w[i], sgu_b[i], M, t["sgu_rows"])
            u, w, qd, kd, qk, gl = gdn_intra(proj, tail, tail_t, dn_a_log[i], dn_dt_bias[i], M, t["gdn_rows"])
            d_out = gdn_scan(u, w, qd, kd, qk, gl, proj.reshape(B, S, -1), dn_norm_g[i], B, S,
                             t["scan_chunks"])
            mix_a, mix_b, w_mix = c_out, d_out.reshape(M, DN_WIDTH), od_w_out[i]
        qk, vo = xattn_absorb(mem2, mem_norm, xattn_wkv, xattn_wq, xattn_wo, layer, B, n_mem)
        h = mix_xattn_residual(h, mix_a, mix_b, bf(w_mix), norm_xattn[layer], qk, vo, S, n_mem, t["rows"])
        h = ffn_residual(h, norm_ffn[layer], w_up_all, ffn_conv, w_down_all, layer, final_norm, S, t["rows"],
                         t["ffn_chunk"], final_norm=(layer == depth - 1))
    return h.reshape(B, S, D)


def kernel(x, mem, mem_norm, norm_mix, norm_xattn, norm_ffn, ev_w_in, pool_w, pool_scale, ev_w_out, od_w_in, sgu_ln_g, sgu_ln_b, sgu_w, sgu_b, dn_conv, dn_a_log, dn_dt_bias, dn_norm_g, od_w_out, xattn_wq, xattn_wkv, xattn_wo, ffn_w_up, ffn_conv, ffn_w_down, final_norm):
    return _forward(x, mem, mem_norm, norm_mix, norm_xattn, norm_ffn, ev_w_in, pool_w, pool_scale, ev_w_out,
                    od_w_in, sgu_ln_g, sgu_ln_b, sgu_w, sgu_b, dn_conv, dn_a_log, dn_dt_bias, dn_norm_g,
                    od_w_out, xattn_wq, xattn_wkv, xattn_wo, ffn_w_up, ffn_conv, ffn_w_down, final_norm)
```

```python
import functools
import math

import jax
import jax.numpy as jnp
from jax import lax
from jax.experimental import pallas as pl
from jax.experimental.pallas import tpu as pltpu

F32 = jnp.float32
BF16 = jnp.bfloat16
EPS = 1e-6
NEG_BIG = -1e30

VMEM_LIMIT_BYTES = 48 * 1024 * 1024
BF16_SUBLANES = 16
LANES = 128

MOBA_HEADS, MOBA_HEAD_DIM, MOBA_BLOCK, MOBA_TOPK = 8, 64, 256, 3
A_WIDTH = MOBA_HEADS * MOBA_HEAD_DIM
POOL_WINDOWS = (2, 4, 8, 16)
POOL_GROUP = 128
POOL_WIDTH = POOL_GROUP * len(POOL_WINDOWS)
SGU_GROUPS, SGU_GROUP, SGU_CHUNK = 4, 128, 128
SGU_WIDTH = SGU_GROUPS * SGU_GROUP
DN_HEADS, DN_HEAD_DIM, DN_CONV, DN_CHUNK = 4, 128, 4, 64
DN_WIDTH = DN_HEADS * DN_HEAD_DIM
XATTN_HEADS = 4
FFN_CONV = 3


def _cparams(*sem):
    return pltpu.CompilerParams(dimension_semantics=sem, vmem_limit_bytes=VMEM_LIMIT_BYTES)


def _rmsnorm(x, g):
    return x * lax.rsqrt(jnp.mean(x * x, axis=-1, keepdims=True) + EPS) * g


def _silu(x):
    return x * (0.5 * jnp.tanh(0.5 * x) + 0.5)


def _dot(a, b):
    return jnp.dot(a, b, preferred_element_type=F32)


def _resident(arr, layer=None):
    if layer is None:
        return pl.BlockSpec(arr.shape, lambda i: (0,) * arr.ndim, pipeline_mode=pl.Buffered(1))
    return pl.BlockSpec((None,) + arr.shape[1:], lambda i: (layer,) + (0,) * (arr.ndim - 1),
                        pipeline_mode=pl.Buffered(1))


def _dot_nt(a, b, precision=None):
    return lax.dot_general(a, b, (((1,), (1,)), ((), ())), preferred_element_type=F32,
                           precision=precision)


def _norm_mm_kernel(x_ref, g_ref, *refs, bn):
    n = len(refs) // 2
    xn = _rmsnorm(x_ref[...], g_ref[...]).astype(BF16)
    for w_ref, o_ref in zip(refs[:n], refs[n:]):
        N = w_ref.shape[1]
        for c0 in range(0, N, bn):
            c1 = min(c0 + bn, N)
            o_ref[:, c0:c1] = _dot(xn, w_ref[:, c0:c1]).astype(o_ref.dtype)


def norm_matmul(x, g, ws, out_dtypes, bm, bn=512, layer=None):
    M, D = x.shape
    return pl.pallas_call(
        functools.partial(_norm_mm_kernel, bn=bn),
        grid=(M // bm,),
        in_specs=[pl.BlockSpec((bm, D), lambda i: (i, 0)),
                  pl.BlockSpec((1, D), lambda i: (0, 0))]
                 + [_resident(w, layer) for w in ws],
        out_specs=[pl.BlockSpec((bm, w.shape[-1]), lambda i: (i, 0)) for w in ws],
        out_shape=[jax.ShapeDtypeStruct((M, w.shape[-1]), dt) for w, dt in zip(ws, out_dtypes)],
        compiler_params=_cparams("parallel"),
    )(x, g.reshape(1, D), *ws)


def _gelu_tanh(x):
    return 0.5 * x * (1.0 + jnp.tanh(math.sqrt(2.0 / math.pi) * (x + 0.044715 * (x * x * x))))


def _od_proj_kernel(x_ref, halo_ref, g_ref, w_ref, wt_ref, cw_ref, o_ref, t_ref, tt_ref, xe_ref, *,
                    blocks_per_seq, bn):
    bm = x_ref.shape[0]
    H = BF16_SUBLANES
    z_w, qkv_w = 2 * SGU_WIDTH, 3 * DN_WIDTH
    first = (pl.program_id(0) % blocks_per_seq) == 0
    xe_ref[:H, :] = jnp.where(first, 0.0, _rmsnorm(halo_ref[...], g_ref[...])).astype(BF16)
    xe_ref[H:, :] = _rmsnorm(x_ref[...], g_ref[...]).astype(BF16)
    xn = xe_ref[H:, :]
    tail = _dot(xn, wt_ref[...])
    t_ref[...] = tail
    tt_ref[...] = tail.T[:H, :]
    for c0 in range(0, o_ref.shape[1], bn):
        cols = slice(c0, c0 + bn)
        if c0 < z_w:
            o_ref[:, cols] = _gelu_tanh(_dot(xn, w_ref[:, cols])).astype(o_ref.dtype)
        elif c0 < z_w + qkv_w:
            y = _dot(xe_ref[...], w_ref[:, cols])
            cw = cw_ref[:, c0 - z_w:c0 - z_w + bn]
            out = y[H:, :] * cw[DN_CONV - 1:DN_CONV, :]
            for k in range(1, DN_CONV):
                out = out + pltpu.roll(y, k, axis=0)[H:, :] * cw[DN_CONV - 1 - k:DN_CONV - k, :]
            o_ref[:, cols] = _silu(out).astype(o_ref.dtype)
        else:
            o_ref[:, cols] = _dot(xn, w_ref[:, cols]).astype(o_ref.dtype)


def od_projection(x, g, w, w_tail, conv_w, S, bm, bn=512):
    M, D = x.shape
    N = 2 * SGU_WIDTH + 4 * DN_WIDTH
    H = BF16_SUBLANES
    return pl.pallas_call(
        functools.partial(_od_proj_kernel, blocks_per_seq=S // bm, bn=bn),
        grid=(M // bm,),
        in_specs=[pl.BlockSpec((bm, D), lambda i: (i, 0)),
                  pl.BlockSpec((H, D), lambda i: (jnp.maximum(i * (bm // H) - 1, 0), 0)),
                  pl.BlockSpec((1, D), lambda i: (0, 0)),
                  _resident(w), _resident(w_tail), _resident(conv_w)],
        out_specs=[pl.BlockSpec((bm, N), lambda i: (i, 0)), pl.BlockSpec((bm, LANES), lambda i: (i, 0)),
                   pl.BlockSpec((H, bm), lambda i: (0, i))],
        out_shape=[jax.ShapeDtypeStruct((M, N), BF16), jax.ShapeDtypeStruct((M, LANES), F32),
                   jax.ShapeDtypeStruct((H, M), F32)],
        scratch_shapes=[pltpu.VMEM((H + bm, D), BF16)],
        compiler_params=_cparams("parallel"),
    )(x, x, g.reshape(1, D), w, w_tail, conv_w)


def _moba_kernel(q_ref, k_ref, v_ref, o_ref, kme_ref, vt_ref, sel_ref, m_ref, acc_ref, s_ref, *,
                 nb, nbp, unroll, pairs):
    BS = MOBA_BLOCK
    HD = MOBA_HEAD_DIM
    n_heads = 2 * pairs
    i = pl.program_id(2)
    lane = lax.broadcasted_iota(jnp.int32, (1, LANES), 1)
    head_lanes = (lane < HD, lane >= HD)
    pair_lanes = lambda u: slice((u // 2) * LANES, (u // 2 + 1) * LANES)

    @pl.when(i == 0)
    def _():
        kme_ref[...] = jnp.zeros_like(kme_ref)
        for n in range(nb):
            rows = slice(n * BS, (n + 1) * BS)
            mean = jnp.sum(k_ref[0, rows, :].astype(F32), axis=0, keepdims=True) / BS
            for u in range(n_heads):
                kme_ref[u // 2, (u % 2) * nbp + n:(u % 2) * nbp + n + 1, :] = jnp.where(
                    head_lanes[u % 2], mean[:, pair_lanes(u)], 0.0)
            vt_ref[:, rows] = v_ref[0, rows, :].astype(F32).T.astype(BF16)

    scale = HD ** -0.5 * math.log2(math.e)
    q_t = [q_ref[0, :, p * LANES:(p + 1) * LANES].astype(F32).T for p in range(pairs)]
    pair_row = lax.broadcasted_iota(jnp.int32, (LANES, 1), 0)
    head_rows = (pair_row < HD, pair_row >= HD)
    q_aug = [jnp.where(head_rows[u % 2], q_t[u // 2] * scale, 0.0).astype(BF16) for u in range(n_heads)]

    gates = []
    for p in range(pairs):
        km = kme_ref[p]
        k_hi = km.astype(BF16)
        r1 = km - k_hi.astype(F32)
        k_mid = r1.astype(BF16)
        k_lo = (r1 - k_mid.astype(F32)).astype(BF16)
        q16 = q_t[p].astype(BF16)
        gates.append(_dot(k_hi, q16) + _dot(k_mid, q16) + _dot(k_lo, q16))
    blk = lax.broadcasted_iota(jnp.int32, (nbp, 1), 0).astype(F32)
    valid = blk < i.astype(F32)
    for u in range(n_heads):
        g = jnp.where(valid, gates[u // 2][(u % 2) * nbp:(u % 2 + 1) * nbp], -jnp.inf)
        sel = jnp.zeros(g.shape, jnp.bool_)
        for _ in range(MOBA_TOPK):
            mx = jnp.max(g, axis=0, keepdims=True)
            idx = jnp.min(jnp.where(g == mx, blk, float(1 << 20)), axis=0, keepdims=True)
            pick = blk == idx
            sel = sel | pick
            g = jnp.where(pick, -jnp.inf, g)
        sel_ref[u, :nbp, :] = jnp.where(sel & valid, 1.0, 0.0)
        sel_ref[u, nbp:, :] = jnp.zeros((8, BS), F32)
        m_ref[u] = jnp.full((1, BS), NEG_BIG, F32)
        acc_ref[u] = jnp.zeros(acc_ref.shape[1:], F32)

    krow = lax.broadcasted_iota(jnp.int32, (BS, BS), 0)
    qcol = lax.broadcasted_iota(jnp.int32, (BS, BS), 1)
    PVR = HD + BF16_SUBLANES
    pv_rows = (slice(0, PVR), slice(LANES - PVR, LANES))
    pv_row = lax.broadcasted_iota(jnp.int32, (PVR, 1), 0)
    is_dim = (pv_row < HD, pv_row >= PVR - HD)

    def block_start(j):
        return pl.multiple_of(jnp.minimum(j, i) * BS, BS)

    def produce(g, slot):
        for t in range(unroll):
            rows = pl.ds(block_start(g * unroll + t), BS)
            for u in range(n_heads):
                s_ref[slot, u, t * BS:(t + 1) * BS, :] = _dot(
                    k_ref[0, rows, pair_lanes(u)], q_aug[u]).astype(BF16)

    def softmax_update(sts, sels, starts):
        heads = range(n_heads)
        m_new, alpha = [], []
        for u in heads:
            cand = jnp.full((1, BS), NEG_BIG, F32)
            for st, sel in zip(sts[u], sels[u]):
                mx = jnp.max(st.reshape(BS // BF16_SUBLANES, BF16_SUBLANES, BS), axis=0)
                mx = jnp.max(mx.astype(F32), axis=0, keepdims=True)
                cand = jnp.maximum(cand, mx if sel is None else jnp.where(sel, mx, NEG_BIG))
            m_old = m_ref[u]
            m_new.append(jnp.maximum(m_old, cand))
            alpha.append(jnp.exp2(m_old - m_new[u]))
            m_ref[u] = m_new[u]
        ps = []
        for u in heads:
            pu = []
            for st, sel in zip(sts[u], sels[u]):
                sub = m_new[u] if sel is None else jnp.where(sel, m_new[u], -NEG_BIG)
                pu.append(jnp.exp2(st - sub.astype(BF16)))
            ps.append(pu[0] if len(pu) == 1 else jnp.concatenate(pu, axis=0))
        pv = []
        for u in heads:
            rows = slice((u // 2) * LANES + pv_rows[u % 2].start, (u // 2) * LANES + pv_rows[u % 2].stop)
            vts = [jnp.where(is_dim[u % 2], vt_ref[rows, pl.ds(st0, BS)], jnp.ones((), BF16))
                   for st0 in starts]
            pv.append(_dot(vts[0] if len(vts) == 1 else jnp.concatenate(vts, axis=1), ps[u]))
        for u in heads:
            acc_ref[u] = acc_ref[u] * alpha[u] + pv[u]

    def consume(g, slot):
        js = [g * unroll + t for t in range(unroll)]
        sts = [[s_ref[slot, u, t * BS:(t + 1) * BS, :] for t in range(unroll)] for u in range(n_heads)]
        sels = [[sel_ref[u, pl.ds(j, 1), :] > 0.5 for j in js] for u in range(n_heads)]
        softmax_update(sts, sels, [block_start(j) for j in js])

    slots = s_ref.shape[0]
    assert unroll == 1, "the group after the last past block must be exactly the tile's own block"

    def body(gg, c):
        for t in range(slots):
            produce(slots * gg + t + 1, (t + 1) % slots)
            consume(slots * gg + t, t)
        return c

    n_groups = (i + unroll - 1) // unroll
    produce(0, 0)
    lax.fori_loop(0, (n_groups + slots - 1) // slots, body, 0)
    softmax_update([[jnp.where(krow <= qcol, s_ref[0, u, :BS, :], -jnp.inf)] for u in range(n_heads)],
                   [[None]] * n_heads, [block_start(i)])

    outs = []
    for u in range(n_heads):
        a = acc_ref[u]
        outs.append(a[:HD] / a[HD:HD + 1, :] if u % 2 == 0 else a[PVR - HD:] / a[0:1, :])
    o_ref[0] = jnp.concatenate(outs, axis=0).T.astype(o_ref.dtype)


def moba_attention(proj, B, S):
    nb = S // MOBA_BLOCK
    nbp = -(-nb // 8) * 8
    pairs = 4
    width = pairs * LANES
    groups = A_WIDTH // width
    unroll = 1
    slots = 3
    n_heads = 2 * pairs
    return pl.pallas_call(
        functools.partial(_moba_kernel, nb=nb, nbp=nbp, unroll=unroll, pairs=pairs),
        grid=(B, groups, nb),
        in_specs=[pl.BlockSpec((1, MOBA_BLOCK, width), lambda b, p, i: (b, i, p)),
                  pl.BlockSpec((1, S, width), lambda b, p, i: (b, 0, groups + p), pipeline_mode=pl.Buffered(1)),
                  pl.BlockSpec((1, S, width), lambda b, p, i: (b, 0, 2 * groups + p), pipeline_mode=pl.Buffered(1))],
        out_specs=pl.BlockSpec((1, MOBA_BLOCK, width), lambda b, p, i: (b, i, p)),
        out_shape=jax.ShapeDtypeStruct((B, S, A_WIDTH), BF16),
        scratch_shapes=[pltpu.VMEM((pairs, 2 * nbp, LANES), F32),
                        pltpu.VMEM((width, S), BF16),
                        pltpu.VMEM((n_heads, nbp + 8, MOBA_BLOCK), F32),
                        pltpu.VMEM((n_heads, 1, MOBA_BLOCK), F32),
                        pltpu.VMEM((n_heads, MOBA_HEAD_DIM + BF16_SUBLANES, MOBA_BLOCK), F32),
                        pltpu.VMEM((slots, n_heads, unroll * MOBA_BLOCK, MOBA_BLOCK), BF16)],
        compiler_params=_cparams("parallel", "parallel", "arbitrary"),
    )(proj, proj, proj)


def _pool_kernel(p_ref, halo_ref, w_ref, sc_ref, o_ref, *, blocks_per_seq):
    bm = p_ref.shape[0]
    H = BF16_SUBLANES
    i = pl.program_id(0)
    first = (i % blocks_per_seq) == 0
    t1 = (lax.broadcasted_iota(jnp.int32, (bm, 1), 0) + (i % blocks_per_seq) * bm + 1).astype(F32)
    for g, w in enumerate(POOL_WINDOWS):
        cols = slice(g * POOL_GROUP, (g + 1) * POOL_GROUP)
        cur = p_ref[:, cols].astype(F32)
        halo = jnp.where(first, 0.0, halo_ref[:, cols].astype(F32))
        ext = jnp.concatenate([halo, cur], axis=0)
        acc = ext
        sh = 1
        while sh < w:
            acc = acc + pltpu.roll(acc, sh, axis=0)
            sh *= 2
        win = acc[H:, :]
        pooled = win / jnp.minimum(t1, float(w)) - cur
        y = _dot(pooled.astype(BF16), w_ref[g])
        o_ref[:, cols] = (y * sc_ref[:, cols]).astype(o_ref.dtype)


def multiscale_pool(proj, pool_w, pool_scale, M, S, bm):
    H = BF16_SUBLANES
    pcol = 3 * A_WIDTH // POOL_WIDTH
    return pl.pallas_call(
        functools.partial(_pool_kernel, blocks_per_seq=S // bm),
        grid=(M // bm,),
        in_specs=[pl.BlockSpec((bm, POOL_WIDTH), lambda i: (i, pcol)),
                  pl.BlockSpec((H, POOL_WIDTH), lambda i: (jnp.maximum(i * (bm // H) - 1, 0), pcol)),
                  pl.BlockSpec((len(POOL_WINDOWS), POOL_GROUP, POOL_GROUP), lambda i: (0, 0, 0)),
                  pl.BlockSpec((1, POOL_WIDTH), lambda i: (0, 0))],
        out_specs=pl.BlockSpec((bm, POOL_WIDTH), lambda i: (i, 0)),
        out_shape=jax.ShapeDtypeStruct((M, POOL_WIDTH), BF16),
        compiler_params=_cparams("parallel"),
    )(proj, proj, pool_w, pool_scale.reshape(1, POOL_WIDTH))


def _xattn_absorb_kernel(mem_ref, g_ref, wkv_ref, wq_ref, wo_ref, qk_ref, vo_ref):
    n_mem, D = mem_ref.shape
    hd = D // XATTN_HEADS
    mn = _rmsnorm(mem_ref[...], g_ref[...]).astype(BF16)
    kv = _dot(mn, wkv_ref[...].astype(BF16)).astype(BF16)
    for hh in range(XATTN_HEADS):
        cols = slice(hh * hd, (hh + 1) * hd)
        mcols = slice(hh * n_mem, (hh + 1) * n_mem)
        qk_ref[0, :, mcols] = (_dot_nt(wq_ref[:, cols].astype(BF16), kv[:, cols]) * hd ** -0.5).astype(BF16)
        vo_ref[0, mcols, :] = _dot(kv[:, D + hh * hd:D + (hh + 1) * hd], wo_ref[cols, :].astype(BF16)).astype(BF16)


def xattn_absorb(mem2, g, wkv, wq, wo, layer, B, n_mem):
    D = mem2.shape[1]
    HM = XATTN_HEADS * n_mem
    return pl.pallas_call(
        _xattn_absorb_kernel,
        grid=(B,),
        in_specs=[pl.BlockSpec((n_mem, D), lambda b: (b, 0)),
                  pl.BlockSpec((1, D), lambda b: (0, 0)),
                  _resident(wkv, layer), _resident(wq, layer), _resident(wo, layer)],
        out_specs=[pl.BlockSpec((1, D, HM), lambda b: (b, 0, 0)), pl.BlockSpec((1, HM, D), lambda b: (b, 0, 0))],
        out_shape=[jax.ShapeDtypeStruct((B, D, HM), BF16), jax.ShapeDtypeStruct((B, HM, D), BF16)],
        compiler_params=_cparams("parallel"),
    )(mem2, g.reshape(1, D), wkv, wq, wo)


def _mix_xattn_kernel(h_ref, a_ref, b_ref, wm_ref, g_ref, qk_ref, vo_ref, o_ref, *, n_mem):
    ka = a_ref.shape[1]
    h = h_ref[...] + _dot(a_ref[...], wm_ref[:ka, :]) + _dot(b_ref[...], wm_ref[ka:, :])
    xn = _rmsnorm(h, g_ref[...]).astype(BF16)
    s = _dot(xn, qk_ref[...])
    ps = []
    for hh in range(XATTN_HEADS):
        sh = s[:, hh * n_mem:(hh + 1) * n_mem]
        p = jnp.exp(sh - jnp.max(sh, axis=1, keepdims=True))
        ps.append((p / jnp.sum(p, axis=1, keepdims=True)).astype(BF16))
    o_ref[...] = h + _dot(jnp.concatenate(ps, axis=1), vo_ref[...])


def mix_xattn_residual(h, a, b, w_mix, g, qk, vo, S, n_mem, bm):
    M, D = h.shape
    ka, kb = a.shape[1], b.shape[1]
    bps = S // bm
    HM = qk.shape[2]
    return pl.pallas_call(
        functools.partial(_mix_xattn_kernel, n_mem=n_mem),
        grid=(M // bm,),
        in_specs=[pl.BlockSpec((bm, D), lambda i: (i, 0)),
                  pl.BlockSpec((bm, ka), lambda i: (i, 0)),
                  pl.BlockSpec((bm, kb), lambda i: (i, 0)),
                  _resident(w_mix),
                  pl.BlockSpec((1, D), lambda i: (0, 0)),
                  pl.BlockSpec((None, D, HM), lambda i: (i // bps, 0, 0)),
                  pl.BlockSpec((None, HM, D), lambda i: (i // bps, 0, 0))],
        out_specs=pl.BlockSpec((bm, D), lambda i: (i, 0)),
        out_shape=jax.ShapeDtypeStruct((M, D), F32),
        compiler_params=_cparams("parallel"),
    )(h, a, b, w_mix, g.reshape(1, D), qk, vo)


def _ffn_kernel(h_ref, halo_ref, g_ref, wup_ref, cw_ref, wd_ref, fg_ref, o_ref,
                xn_ref, acc_ref, y_ref, *, blocks_per_seq, final_norm, sub):
    H = BF16_SUBLANES
    nc, cf = wd_ref.shape[0], wd_ref.shape[1]
    n_sub = acc_ref.shape[0] // sub

    first = (pl.program_id(0) % blocks_per_seq) == 0
    xn_ref[:H, :] = jnp.where(first, 0.0, _rmsnorm(halo_ref[...], g_ref[...])).astype(BF16)

    def normalize(r):
        rows = slice(r * sub, (r + 1) * sub)
        xn_ref[H + r * sub:H + (r + 1) * sub, :] = _rmsnorm(h_ref[rows, :], g_ref[...]).astype(BF16)

    chunk_cols = lambda c: pl.ds(pl.multiple_of(c * cf, cf), cf)

    def up(c, r):
        if r == 0:
            xs, dst = xn_ref[:sub + H, :], slice(0, sub + H)
        else:
            xs, dst = xn_ref[H + r * sub:H + (r + 1) * sub, :], slice(H, sub + H)
            y_ref[r % 2, :H, :] = y_ref[(r - 1) % 2, sub:sub + H, :]
        y_ref[r % 2, dst, :cf] = _dot(xs, wup_ref[:, chunk_cols(c)])
        y_ref[r % 2, dst, cf:] = _dot(xs, wup_ref[:, chunk_cols(nc + c)])

    def conv(r, part, cw):
        cols = slice(part * cf, (part + 1) * cf)
        out = y_ref[r % 2, H:, cols] * cw[FFN_CONV - 1:FFN_CONV, :]
        for k in range(1, FFN_CONV):
            out = out + y_ref[r % 2, H - k:H - k + sub, cols] * cw[FFN_CONV - 1 - k:FFN_CONV - k, :]
        return out

    def chunk(c, leading=False, trailing=False):
        cwg, cwu = cw_ref[:, chunk_cols(c)], cw_ref[:, chunk_cols(nc + c)]
        wd = wd_ref[c]
        for r in range(n_sub):
            if r + 1 < n_sub:
                if leading:
                    normalize(r + 1)
                up(c, r + 1)
            elif not trailing:
                up(c + 1, 0)
            down = _dot((_silu(conv(r, 0, cwg)) * conv(r, 1, cwu)).astype(BF16), wd)
            rows = slice(r * sub, (r + 1) * sub)
            if leading:
                acc_ref[rows, :] = down
            elif not trailing:
                acc_ref[rows, :] += down
            else:
                y = h_ref[rows, :] + acc_ref[rows, :] + down
                o_ref[rows, :] = _rmsnorm(y, fg_ref[...]) if final_norm else y

    assert nc >= 3
    normalize(0)
    up(0, 0)
    chunk(0, leading=True)
    lax.fori_loop(1, nc - 1, lambda c, carry: (chunk(c), carry)[1], 0)
    chunk(nc - 1, trailing=True)


def ffn_residual(h, g, w_up, conv_w, w_down, layer, final_g, S, bm, cf, final_norm):
    M, D = h.shape
    d_ff = w_down.shape[1]
    H = BF16_SUBLANES
    nc = d_ff // cf
    sub = min(256, bm)
    assert (bm // sub) % 2 == 0, "the two y_ref slots alternate per sub-block across chunks"
    wd3 = w_down.reshape(-1, nc, cf, D)
    return pl.pallas_call(
        functools.partial(_ffn_kernel, blocks_per_seq=S // bm, final_norm=final_norm, sub=sub),
        grid=(M // bm,),
        in_specs=[pl.BlockSpec((bm, D), lambda i: (i, 0)),
                  pl.BlockSpec((H, D), lambda i: (jnp.maximum(i * (bm // H) - 1, 0), 0)),
                  pl.BlockSpec((1, D), lambda i: (0, 0)),
                  _resident(w_up, layer), _resident(conv_w, layer), _resident(wd3, layer),
                  pl.BlockSpec((1, D), lambda i: (0, 0))],
        out_specs=pl.BlockSpec((bm, D), lambda i: (i, 0)),
        out_shape=jax.ShapeDtypeStruct((M, D), F32),
        scratch_shapes=[pltpu.VMEM((H + bm, D), BF16), pltpu.VMEM((bm, D), F32),
                        pltpu.VMEM((2, H + sub, 2 * cf), F32)],
        compiler_params=_cparams("parallel"),
    )(h, h, g.reshape(1, D), w_up, conv_w, wd3, final_g.reshape(1, D))


def _sgu_kernel(u_ref, v_ref, lg_ref, lb_ref, w_ref, bt_ref, o_ref):
    rows = u_ref.shape[0]
    T = SGU_CHUNK
    v = v_ref[...].astype(F32)
    mu = jnp.mean(v, axis=-1, keepdims=True)
    d = v - mu
    var = jnp.mean(d * d, axis=-1, keepdims=True)
    vn = (d * lax.rsqrt(var + EPS) * lg_ref[...] + lb_ref[...]).astype(BF16)
    causal = (lax.broadcasted_iota(jnp.int32, (T, T), 1) <= lax.broadcasted_iota(jnp.int32, (T, T), 0))
    for g in range(SGU_GROUPS):
        cols = slice(g * SGU_GROUP, (g + 1) * SGU_GROUP)
        wg = jnp.where(causal, w_ref[g], 0.0).astype(BF16)
        bias = bt_ref[:, g:g + 1]
        for c in range(rows // T):
            rs = slice(c * T, (c + 1) * T)
            s = _dot(wg, vn[rs, cols]) + bias
            o_ref[rs, cols] = (u_ref[rs, cols].astype(F32) * s).astype(o_ref.dtype)


def spatial_gating(proj, ln_g, ln_b, w_s, b_s, M, rows):
    return pl.pallas_call(
        _sgu_kernel,
        grid=(M // rows,),
        in_specs=[pl.BlockSpec((rows, SGU_WIDTH), lambda i: (i, 0)),
                  pl.BlockSpec((rows, SGU_WIDTH), lambda i: (i, 1)),
                  pl.BlockSpec((1, SGU_WIDTH), lambda i: (0, 0)),
                  pl.BlockSpec((1, SGU_WIDTH), lambda i: (0, 0)),
                  pl.BlockSpec((SGU_GROUPS, SGU_CHUNK, SGU_CHUNK), lambda i: (0, 0, 0)),
                  pl.BlockSpec((SGU_CHUNK, SGU_GROUPS), lambda i: (0, 0))],
        out_specs=pl.BlockSpec((rows, SGU_WIDTH), lambda i: (i, 0)),
        out_shape=jax.ShapeDtypeStruct((M, SGU_WIDTH), BF16),
        compiler_params=_cparams("parallel"),
    )(proj, proj, ln_g.reshape(1, -1), ln_b.reshape(1, -1), w_s, b_s.T)


def _gdn_intra_kernel(q_ref, k_ref, v_ref, tail_ref, tailt_ref, alog_ref,
                      dtb_ref, u_ref, w_ref, qd_ref, kd_ref, qk_ref, gl_ref):
    rows = q_ref.shape[0]
    C = DN_CHUNK
    HD = DN_HEAD_DIM
    x = jnp.concatenate([q_ref[...], k_ref[...], v_ref[...]], axis=1).astype(F32)

    ii = lax.broadcasted_iota(jnp.int32, (C, C), 0)
    jj = lax.broadcasted_iota(jnp.int32, (C, C), 1)
    lower = jj <= ii
    strict = jj < ii
    su = lax.broadcasted_iota(jnp.int32, (C, LANES), 0)
    ju = lax.broadcasted_iota(jnp.int32, (C, LANES), 1)
    upper_ext = jnp.where(((ju < C) & (su > ju)) | (ju == C), 1.0, 0.0)

    inst = [(c, hh) for hh in range(DN_HEADS) for c in range(rows // C)]
    qs, ks, vs, bs, stacks = [], [], [], [], []
    for hh in range(DN_HEADS):
        qh = x[:, hh * HD:(hh + 1) * HD]
        kh = x[:, DN_WIDTH + hh * HD:DN_WIDTH + (hh + 1) * HD]
        vh = x[:, 2 * DN_WIDTH + hh * HD:2 * DN_WIDTH + (hh + 1) * HD]
        qh = qh * lax.rsqrt(jnp.sum(qh * qh, axis=-1, keepdims=True) + EPS) * HD ** -0.5
        kh = kh * lax.rsqrt(jnp.sum(kh * kh, axis=-1, keepdims=True) + EPS)
        beta = 1.0 / (1.0 + jnp.exp(-tail_ref[:, hh:hh + 1]))
        a_raw = tailt_ref[DN_HEADS + hh:DN_HEADS + hh + 1, :]
        z = a_raw + dtb_ref[0:1, hh:hh + 1]
        softplus = jnp.maximum(z, 0.0) + jnp.log(1.0 + jnp.exp(-jnp.abs(z)))
        g_row = -jnp.exp(alog_ref[0:1, hh:hh + 1]) * softplus
        for c in range(rows // C):
            rs = slice(c * C, (c + 1) * C)
            qs.append(qh[rs]); ks.append(kh[rs]); vs.append(vh[rs]); bs.append(beta[rs])
            gr = jnp.broadcast_to(g_row[:, rs], (C, C))
            stacks += [jnp.where(lower, gr, 0.0), jnp.where(lower, 0.0, gr)]

    stacked = jnp.concatenate(stacks, axis=0)
    s_hi = stacked.astype(BF16)
    s_lo = (stacked - s_hi.astype(F32)).astype(BF16)
    ue = upper_ext.astype(BF16)
    dall = _dot(s_hi, ue) + _dot(s_lo, ue)

    decays, gcs, gc_revs, k16s, kbs = [], [], [], [], []
    for n, (c, hh) in enumerate(inst):
        dext = dall[n * 2 * C:(n + 1) * 2 * C]
        decays.append(jnp.exp(jnp.where(lower, dext[:C, :C], -jnp.inf)))
        gcs.append(dext[:C, C:C + 1])
        gc_revs.append(dext[C:, C:C + 1])
        kbs.append(ks[n] * bs[n])
        k16s.append(ks[n].astype(BF16))
    kq = [_dot_nt(jnp.concatenate([kbs[n], qs[n]], axis=0).astype(BF16), k16s[n]) for n in range(len(inst))]
    kk = [m[:C] for m in kq]
    qk = [m[C:] for m in kq]
    pws = [jnp.where(strict, kk[n] * decays[n], 0.0).astype(BF16) for n in range(len(inst))]
    egc = [jnp.exp(g) for g in gcs]
    rhs = [jnp.concatenate([vs[n] * bs[n], kbs[n] * egc[n]], axis=1) for n in range(len(inst))]
    sols = [rhs[n] - _dot(pws[n], rhs[n].astype(BF16)) for n in range(len(inst))]
    for _ in range(int(math.log2(C)) - 1):
        pws = [_dot(p, p).astype(BF16) for p in pws]
        sols = [s + _dot(p, s.astype(BF16)) for p, s in zip(pws, sols)]
    for n, (c, hh) in enumerate(inst):
        rs = slice(c * C, (c + 1) * C)
        hcols = slice(hh * HD, (hh + 1) * HD)
        u_ref[rs, hcols] = sols[n][:, :HD].astype(u_ref.dtype)
        w_ref[rs, hcols] = sols[n][:, HD:].astype(w_ref.dtype)
        qkd = jnp.where(lower, qk[n] * decays[n], 0.0)
        qk_ref[rs, hcols] = jnp.concatenate([qkd, jnp.zeros_like(qkd)], axis=1).astype(qk_ref.dtype)
        qd_ref[rs, hcols] = (qs[n] * egc[n]).astype(qd_ref.dtype)
        kd_ref[rs, hcols] = (ks[n] * jnp.exp(gc_revs[n])).astype(kd_ref.dtype)
        gl_ref[c * 8:(c + 1) * 8, hcols] = jnp.broadcast_to(egc[n][C - 1:C, :], (8, HD))


def gdn_intra(proj, tail, tail_t, a_log, dt_bias, M, rows):
    c0 = 2 * SGU_WIDTH // DN_WIDTH
    pad = lambda p: jnp.pad(p.reshape(1, -1), ((0, 0), (0, LANES - p.shape[0])))
    seq = lambda dt: jax.ShapeDtypeStruct((M, DN_WIDTH), dt)
    row_spec = pl.BlockSpec((rows, DN_WIDTH), lambda i: (i, 0))
    cur_spec = lambda part: pl.BlockSpec((rows, DN_WIDTH), lambda i: (i, c0 + part))
    return pl.pallas_call(
        _gdn_intra_kernel,
        grid=(M // rows,),
        in_specs=[cur_spec(0), cur_spec(1), cur_spec(2),
                  pl.BlockSpec((rows, LANES), lambda i: (i, 0)),
                  pl.BlockSpec((2 * DN_HEADS, rows), lambda i: (0, i)),
                  pl.BlockSpec((1, LANES), lambda i: (0, 0)),
                  pl.BlockSpec((1, LANES), lambda i: (0, 0))],
        out_specs=[row_spec, row_spec, row_spec, row_spec, row_spec,
                   pl.BlockSpec((rows // DN_CHUNK * 8, DN_WIDTH), lambda i: (i, 0))],
        out_shape=[seq(BF16), seq(BF16), seq(BF16), seq(BF16), seq(BF16),
                   jax.ShapeDtypeStruct((M // DN_CHUNK * 8, DN_WIDTH), F32)],
        compiler_params=_cparams("parallel"),
    )(proj, proj, proj, tail, tail_t, pad(a_log), pad(dt_bias))


def _gdn_scan_kernel(u_ref, w_ref, qd_ref, kd_ref, qk_ref, gl_ref, gate_ref, ng_ref, o_ref, state_ref, *,
                     chunks):
    C = DN_CHUNK
    HD = DN_HEAD_DIM
    B = u_ref.shape[0]

    @pl.when(pl.program_id(0) == 0)
    def _():
        state_ref[...] = jnp.zeros_like(state_ref)

    ng = ng_ref[...]
    inst = [(b, hh) for b in range(B) for hh in range(DN_HEADS)]
    col = lambda hh: slice(hh * HD, (hh + 1) * HD)
    states = [state_ref[b, hh] for b, hh in inst]
    for c in range(chunks):
        rs = slice(c * C, (c + 1) * C)
        kdt = [kd_ref[b, rs, col(hh)].astype(F32).T.astype(BF16) for b, hh in inst]
        st16 = [s.astype(BF16) for s in states]
        wq = [_dot(jnp.concatenate([w_ref[b, rs, col(hh)], qd_ref[b, rs, col(hh)]], axis=0), st16[n])
              for n, (b, hh) in enumerate(inst)]
        vn16 = [(u_ref[b, rs, col(hh)].astype(F32) - wq[n][:C]).astype(BF16) for n, (b, hh) in enumerate(inst)]
        qkv = [_dot(jnp.concatenate([qk_ref[b, rs, col(hh)][:, :C], kdt[n]], axis=0), vn16[n])
               for n, (b, hh) in enumerate(inst)]
        states = [states[n] * gl_ref[b, c * 8:c * 8 + 1, col(hh)] + qkv[n][C:]
                  for n, (b, hh) in enumerate(inst)]
        for n, (b, hh) in enumerate(inst):
            o = wq[n][C:] + qkv[n][:C]
            o = o * lax.rsqrt(jnp.mean(o * o, axis=-1, keepdims=True) + EPS) * ng
            o_ref[b, rs, col(hh)] = (o * _silu(gate_ref[b, rs, col(hh)].astype(F32))).astype(o_ref.dtype)
    for n, (b, hh) in enumerate(inst):
        state_ref[b, hh] = states[n]


def gdn_scan(u, w, qd, kd, qk, gl, proj3, norm_g, B, S, chunks):
    rows = chunks * DN_CHUNK
    r3 = lambda a: a.reshape(B, S, DN_WIDTH)
    gcol = (2 * SGU_WIDTH + 3 * DN_WIDTH) // DN_WIDTH
    seq_spec = pl.BlockSpec((B, rows, DN_WIDTH), lambda n: (0, n, 0))
    return pl.pallas_call(
        functools.partial(_gdn_scan_kernel, chunks=chunks),
        grid=(S // rows,),
        in_specs=[seq_spec, seq_spec, seq_spec, seq_spec, seq_spec,
                  pl.BlockSpec((B, chunks * 8, DN_WIDTH), lambda n: (0, n, 0)),
                  pl.BlockSpec((B, rows, DN_WIDTH), lambda n: (0, n, gcol)),
                  pl.BlockSpec((1, DN_HEAD_DIM), lambda n: (0, 0))],
        out_specs=seq_spec,
        out_shape=jax.ShapeDtypeStruct((B, S, DN_WIDTH), BF16),
        scratch_shapes=[pltpu.VMEM((B, DN_HEADS, DN_HEAD_DIM, DN_HEAD_DIM), F32)],
        compiler_params=_cparams("arbitrary"),
    )(r3(u), r3(w), r3(qd), r3(kd), r3(qk), gl.reshape(B, S // DN_CHUNK * 8, DN_WIDTH), proj3,
      norm_g.reshape(1, DN_HEAD_DIM))


def _tiles(S):
    rows = min(1024, S)
    return dict(rows=rows, sgu_rows=rows, gdn_rows=min(512, S), scan_chunks=min(8, S // DN_CHUNK),
                ffn_chunk=256)


def _forward(x, mem, mem_norm, norm_mix, norm_xattn, norm_ffn, ev_w_in, pool_w, pool_scale, ev_w_out,
             od_w_in, sgu_ln_g, sgu_ln_b, sgu_w, sgu_b, dn_conv, dn_a_log, dn_dt_bias, dn_norm_g,
             od_w_out, xattn_wq, xattn_wkv, xattn_wo, ffn_w_up, ffn_conv, ffn_w_down, final_norm):
    B, S, D = x.shape
    n_mem = mem.shape[1]
    M = B * S
    depth = norm_mix.shape[0]
    bf = lambda a: a.astype(BF16)
    t = _tiles(S)

    h = x.reshape(M, D)
    mem2 = mem.reshape(B * n_mem, D)
    w_up_all, w_down_all = bf(ffn_w_up), bf(ffn_w_down)
    for layer in range(depth):
        i = layer // 2
        if layer % 2 == 0:
            proj, = norm_matmul(h, norm_mix[layer], [bf(ev_w_in[i])], [BF16], t["rows"])
            a_out = moba_attention(proj.reshape(B, S, -1), B, S).reshape(M, A_WIDTH)
            b_out = multiscale_pool(proj, bf(pool_w[i]), pool_scale[i], M, S, t["rows"])
            mix_a, mix_b, w_mix = a_out, b_out, ev_w_out[i]
        else:
            main_w = 2 * SGU_WIDTH + 4 * DN_WIDTH
            w_in = od_w_in[i]
            w_tail = jnp.pad(w_in[:, main_w:], ((0, 0), (0, LANES - 2 * DN_HEADS)))
            proj, tail, tail_t = od_projection(h, norm_mix[layer], bf(w_in), bf(w_tail), dn_conv[i],
                                               S, t["rows"])
            c_out = spatial_gating(proj, sgu_ln_g[i], sgu_ln_b[i], sgu_w[i], sgu_b[i], M, t["sgu_rows"])
            u, w, qd, kd, qk, gl = gdn_intra(proj, tail, tail_t, dn_a_log[i], dn_dt_bias[i], M, t["gdn_rows"])
            d_out = gdn_scan(u, w, qd, kd, qk, gl, proj.reshape(B, S, -1), dn_norm_g[i], B, S,
                             t["scan_chunks"])
            mix_a, mix_b, w_mix = c_out, d_out.reshape(M, DN_WIDTH), od_w_out[i]
        qk, vo = xattn_absorb(mem2, mem_norm, xattn_wkv, xattn_wq, xattn_wo, layer, B, n_mem)
        h = mix_xattn_residual(h, mix_a, mix_b, bf(w_mix), norm_xattn[layer], qk, vo, S, n_mem, t["rows"])
        h = ffn_residual(h, norm_ffn[layer], w_up_all, ffn_conv, w_down_all, layer, final_norm, S, t["rows"],
                         t["ffn_chunk"], final_norm=(layer == depth - 1))
    return h.reshape(B, S, D)


def kernel(x, mem, mem_norm, norm_mix, norm_xattn, norm_ffn, ev_w_in, pool_w, pool_scale, ev_w_out, od_w_in, sgu_ln_g, sgu_ln_b, sgu_w, sgu_b, dn_conv, dn_a_log, dn_dt_bias, dn_norm_g, od_w_out, xattn_wq, xattn_wkv, xattn_wo, ffn_w_up, ffn_conv, ffn_w_down, final_norm):
    return _forward(x, mem, mem_norm, norm_mix, norm_xattn, norm_ffn, ev_w_in, pool_w, pool_scale, ev_w_out,
                    od_w_in, sgu_ln_g, sgu_ln_b, sgu_w, sgu_b, dn_conv, dn_a_log, dn_dt_bias, dn_norm_g,
                    od_w_out, xattn_wq, xattn_wkv, xattn_wo, ffn_w_up, ffn_conv, ffn_w_down, final_norm)
```

```python
import functools
import math

import jax
import jax.numpy as jnp
from jax import lax
from jax.experimental import pallas as pl
from jax.experimental.pallas import tpu as pltpu

F32 = jnp.float32
BF16 = jnp.bfloat16
EPS = 1e-6
NEG_BIG = -1e30

VMEM_LIMIT_BYTES = 48 * 1024 * 1024
BF16_SUBLANES = 16
LANES = 128

MOBA_HEADS, MOBA_HEAD_DIM, MOBA_BLOCK, MOBA_TOPK = 8, 64, 256, 3
A_WIDTH = MOBA_HEADS * MOBA_HEAD_DIM
POOL_WINDOWS = (2, 4, 8, 16)
POOL_GROUP = 128
POOL_WIDTH = POOL_GROUP * len(POOL_WINDOWS)
SGU_GROUPS, SGU_GROUP, SGU_CHUNK = 4, 128, 128
SGU_WIDTH = SGU_GROUPS * SGU_GROUP
DN_HEADS, DN_HEAD_DIM, DN_CONV, DN_CHUNK = 4, 128, 4, 64
DN_WIDTH = DN_HEADS * DN_HEAD_DIM
XATTN_HEADS = 4
FFN_CONV = 3


def _cparams(*sem):
    return pltpu.CompilerParams(dimension_semantics=sem, vmem_limit_bytes=VMEM_LIMIT_BYTES)


def _rmsnorm(x, g):
    return x * lax.rsqrt(jnp.mean(x * x, axis=-1, keepdims=True) + EPS) * g


def _silu(x):
    return x * (0.5 * jnp.tanh(0.5 * x) + 0.5)


def _dot(a, b):
    return jnp.dot(a, b, preferred_element_type=F32)


def _resident(arr, layer=None):
    if layer is None:
        return pl.BlockSpec(arr.shape, lambda i: (0,) * arr.ndim, pipeline_mode=pl.Buffered(1))
    return pl.BlockSpec((None,) + arr.shape[1:], lambda i: (layer,) + (0,) * (arr.ndim - 1),
                        pipeline_mode=pl.Buffered(1))


def _dot_nt(a, b, precision=None):
    return lax.dot_general(a, b, (((1,), (1,)), ((), ())), preferred_element_type=F32,
                           precision=precision)


def _norm_mm_kernel(x_ref, g_ref, *refs, bn):
    n = len(refs) // 2
    xn = _rmsnorm(x_ref[...], g_ref[...]).astype(BF16)
    for w_ref, o_ref in zip(refs[:n], refs[n:]):
        N = w_ref.shape[1]
        for c0 in range(0, N, bn):
            c1 = min(c0 + bn, N)
            o_ref[:, c0:c1] = _dot(xn, w_ref[:, c0:c1]).astype(o_ref.dtype)


def norm_matmul(x, g, ws, out_dtypes, bm, bn=512, layer=None):
    M, D = x.shape
    return pl.pallas_call(
        functools.partial(_norm_mm_kernel, bn=bn),
        grid=(M // bm,),
        in_specs=[pl.BlockSpec((bm, D), lambda i: (i, 0)),
                  pl.BlockSpec((1, D), lambda i: (0, 0))]
                 + [_resident(w, layer) for w in ws],
        out_specs=[pl.BlockSpec((bm, w.shape[-1]), lambda i: (i, 0)) for w in ws],
        out_shape=[jax.ShapeDtypeStruct((M, w.shape[-1]), dt) for w, dt in zip(ws, out_dtypes)],
        compiler_params=_cparams("parallel"),
    )(x, g.reshape(1, D), *ws)


def _gelu_tanh(x):
    return 0.5 * x * (1.0 + jnp.tanh(math.sqrt(2.0 / math.pi) * (x + 0.044715 * (x * x * x))))


def _od_proj_kernel(x_ref, halo_ref, g_ref, w_ref, wt_ref, cw_ref, o_ref, t_ref, tt_ref, xe_ref, *,
                    blocks_per_seq, bn):
    bm = x_ref.shape[0]
    H = BF16_SUBLANES
    z_w, qkv_w = 2 * SGU_WIDTH, 3 * DN_WIDTH
    first = (pl.program_id(0) % blocks_per_seq) == 0
    xe_ref[:H, :] = jnp.where(first, 0.0, _rmsnorm(halo_ref[...], g_ref[...])).astype(BF16)
    xe_ref[H:, :] = _rmsnorm(x_ref[...], g_ref[...]).astype(BF16)
    xn = xe_ref[H:, :]
    tail = _dot(xn, wt_ref[...])
    t_ref[...] = tail
    tt_ref[...] = tail.T[:H, :]
    for c0 in range(0, o_ref.shape[1], bn):
        cols = slice(c0, c0 + bn)
        if c0 < z_w:
            o_ref[:, cols] = _gelu_tanh(_dot(xn, w_ref[:, cols])).astype(o_ref.dtype)
        elif c0 < z_w + qkv_w:
            y = _dot(xe_ref[...], w_ref[:, cols])
            cw = cw_ref[:, c0 - z_w:c0 - z_w + bn]
            out = y[H:, :] * cw[DN_CONV - 1:DN_CONV, :]
            for k in range(1, DN_CONV):
                out = out + pltpu.roll(y, k, axis=0)[H:, :] * cw[DN_CONV - 1 - k:DN_CONV - k, :]
            o_ref[:, cols] = _silu(out).astype(o_ref.dtype)
        else:
            o_ref[:, cols] = _dot(xn, w_ref[:, cols]).astype(o_ref.dtype)


def od_projection(x, g, w, w_tail, conv_w, S, bm, bn=512):
    M, D = x.shape
    N = 2 * SGU_WIDTH + 4 * DN_WIDTH
    H = BF16_SUBLANES
    return pl.pallas_call(
        functools.partial(_od_proj_kernel, blocks_per_seq=S // bm, bn=bn),
        grid=(M // bm,),
        in_specs=[pl.BlockSpec((bm, D), lambda i: (i, 0)),
                  pl.BlockSpec((H, D), lambda i: (jnp.maximum(i * (bm // H) - 1, 0), 0)),
                  pl.BlockSpec((1, D), lambda i: (0, 0)),
                  _resident(w), _resident(w_tail), _resident(conv_w)],
        out_specs=[pl.BlockSpec((bm, N), lambda i: (i, 0)), pl.BlockSpec((bm, LANES), lambda i: (i, 0)),
                   pl.BlockSpec((H, bm), lambda i: (0, i))],
        out_shape=[jax.ShapeDtypeStruct((M, N), BF16), jax.ShapeDtypeStruct((M, LANES), F32),
                   jax.ShapeDtypeStruct((H, M), F32)],
        scratch_shapes=[pltpu.VMEM((H + bm, D), BF16)],
        compiler_params=_cparams("parallel"),
    )(x, x, g.reshape(1, D), w, w_tail, conv_w)


def _moba_kernel(q_ref, k_ref, v_ref, o_ref, kme_ref, vt_ref, sel_ref, m_ref, acc_ref, s_ref, *,
                 nb, nbp, unroll, pairs):
    BS = MOBA_BLOCK
    HD = MOBA_HEAD_DIM
    n_heads = 2 * pairs
    i = pl.program_id(2)
    lane = lax.broadcasted_iota(jnp.int32, (1, LANES), 1)
    head_lanes = (lane < HD, lane >= HD)
    pair_lanes = lambda u: slice((u // 2) * LANES, (u // 2 + 1) * LANES)

    @pl.when(i == 0)
    def _():
        kme_ref[...] = jnp.zeros_like(kme_ref)
        for n in range(nb):
            rows = slice(n * BS, (n + 1) * BS)
            mean = jnp.sum(k_ref[0, rows, :].astype(F32), axis=0, keepdims=True) / BS
            for u in range(n_heads):
                kme_ref[u // 2, (u % 2) * nbp + n:(u % 2) * nbp + n + 1, :] = jnp.where(
                    head_lanes[u % 2], mean[:, pair_lanes(u)], 0.0)
            vt_ref[:, rows] = v_ref[0, rows, :].astype(F32).T.astype(BF16)

    scale = HD ** -0.5 * math.log2(math.e)
    q_t = [q_ref[0, :, p * LANES:(p + 1) * LANES].astype(F32).T for p in range(pairs)]
    pair_row = lax.broadcasted_iota(jnp.int32, (LANES, 1), 0)
    head_rows = (pair_row < HD, pair_row >= HD)
    q_aug = [jnp.where(head_rows[u % 2], q_t[u // 2] * scale, 0.0).astype(BF16) for u in range(n_heads)]

    gates = []
    for p in range(pairs):
        km = kme_ref[p]
        k_hi = km.astype(BF16)
        r1 = km - k_hi.astype(F32)
        k_mid = r1.astype(BF16)
        k_lo = (r1 - k_mid.astype(F32)).astype(BF16)
        q16 = q_t[p].astype(BF16)
        gates.append(_dot(k_hi, q16) + _dot(k_mid, q16) + _dot(k_lo, q16))
    blk = lax.broadcasted_iota(jnp.int32, (nbp, 1), 0).astype(F32)
    valid = blk < i.astype(F32)
    for u in range(n_heads):
        g = jnp.where(valid, gates[u // 2][(u % 2) * nbp:(u % 2 + 1) * nbp], -jnp.inf)
        sel = jnp.zeros(g.shape, jnp.bool_)
        for _ in range(MOBA_TOPK):
            mx = jnp.max(g, axis=0, keepdims=True)
            idx = jnp.min(jnp.where(g == mx, blk, float(1 << 20)), axis=0, keepdims=True)
            pick = blk == idx
            sel = sel | pick
            g = jnp.where(pick, -jnp.inf, g)
        sel_ref[u, :nbp, :] = jnp.where(sel & valid, 1.0, 0.0)
        sel_ref[u, nbp:, :] = jnp.zeros((8, BS), F32)
        m_ref[u] = jnp.full((1, BS), NEG_BIG, F32)
        acc_ref[u] = jnp.zeros(acc_ref.shape[1:], F32)

    krow = lax.broadcasted_iota(jnp.int32, (BS, BS), 0)
    qcol = lax.broadcasted_iota(jnp.int32, (BS, BS), 1)
    PVR = HD + BF16_SUBLANES
    pv_rows = (slice(0, PVR), slice(LANES - PVR, LANES))
    pv_row = lax.broadcasted_iota(jnp.int32, (PVR, 1), 0)
    is_dim = (pv_row < HD, pv_row >= PVR - HD)

    def block_start(j):
        return pl.multiple_of(jnp.minimum(j, i) * BS, BS)

    def produce(g, slot):
        for t in range(unroll):
            rows = pl.ds(block_start(g * unroll + t), BS)
            for u in range(n_heads):
                s_ref[slot, u, t * BS:(t + 1) * BS, :] = _dot(
                    k_ref[0, rows, pair_lanes(u)], q_aug[u]).astype(BF16)

    def softmax_update(sts, sels, starts):
        heads = range(n_heads)
        m_new, alpha = [], []
        for u in heads:
            cand = jnp.full((1, BS), NEG_BIG, F32)
            for st, sel in zip(sts[u], sels[u]):
                mx = jnp.max(st.reshape(BS // BF16_SUBLANES, BF16_SUBLANES, BS), axis=0)
                mx = jnp.max(mx.astype(F32), axis=0, keepdims=True)
                cand = jnp.maximum(cand, mx if sel is None else jnp.where(sel, mx, NEG_BIG))
            m_old = m_ref[u]
            m_new.append(jnp.maximum(m_old, cand))
            alpha.append(jnp.exp2(m_old - m_new[u]))
            m_ref[u] = m_new[u]
        ps = []
        for u in heads:
            pu = []
            for st, sel in zip(sts[u], sels[u]):
                sub = m_new[u] if sel is None else jnp.where(sel, m_new[u], -NEG_BIG)
                pu.append(jnp.exp2(st - sub.astype(BF16)))
            ps.append(pu[0] if len(pu) == 1 else jnp.concatenate(pu, axis=0))
        pv = []
        for u in heads:
            rows = slice((u // 2) * LANES + pv_rows[u % 2].start, (u // 2) * LANES + pv_rows[u % 2].stop)
            vts = [jnp.where(is_dim[u % 2], vt_ref[rows, pl.ds(st0, BS)], jnp.ones((), BF16))
                   for st0 in starts]
            pv.append(_dot(vts[0] if len(vts) == 1 else jnp.concatenate(vts, axis=1), ps[u]))
        for u in heads:
            acc_ref[u] = acc_ref[u] * alpha[u] + pv[u]

    def consume(g, slot):
        js = [g * unroll + t for t in range(unroll)]
        sts = [[s_ref[slot, u, t * BS:(t + 1) * BS, :] for t in range(unroll)] for u in range(n_heads)]
        sels = [[sel_ref[u, pl.ds(j, 1), :] > 0.5 for j in js] for u in range(n_heads)]
        softmax_update(sts, sels, [block_start(j) for j in js])

    slots = s_ref.shape[0]
    assert unroll == 1, "the group after the last past block must be exactly the tile's own block"

    def body(gg, c):
        for t in range(slots):
            produce(slots * gg + t + 1, (t + 1) % slots)
            consume(slots * gg + t, t)
        return c

    n_groups = (i + unroll - 1) // unroll
    produce(0, 0)
    lax.fori_loop(0, (n_groups + slots - 1) // slots, body, 0)
    softmax_update([[jnp.where(krow <= qcol, s_ref[0, u, :BS, :], -jnp.inf)] for u in range(n_heads)],
                   [[None]] * n_heads, [block_start(i)])

    outs = []
    for u in range(n_heads):
        a = acc_ref[u]
        outs.append(a[:HD] / a[HD:HD + 1, :] if u % 2 == 0 else a[PVR - HD:] / a[0:1, :])
    o_ref[0] = jnp.concatenate(outs, axis=0).T.astype(o_ref.dtype)


def moba_attention(proj, B, S):
    nb = S // MOBA_BLOCK
    nbp = -(-nb // 8) * 8
    pairs = 4
    width = pairs * LANES
    groups = A_WIDTH // width
    unroll = 1
    slots = 3
    n_heads = 2 * pairs
    return pl.pallas_call(
        functools.partial(_moba_kernel, nb=nb, nbp=nbp, unroll=unroll, pairs=pairs),
        grid=(B, groups, nb),
        in_specs=[pl.BlockSpec((1, MOBA_BLOCK, width), lambda b, p, i: (b, i, p)),
                  pl.BlockSpec((1, S, width), lambda b, p, i: (b, 0, groups + p), pipeline_mode=pl.Buffered(1)),
                  pl.BlockSpec((1, S, width), lambda b, p, i: (b, 0, 2 * groups + p), pipeline_mode=pl.Buffered(1))],
        out_specs=pl.BlockSpec((1, MOBA_BLOCK, width), lambda b, p, i: (b, i, p)),
        out_shape=jax.ShapeDtypeStruct((B, S, A_WIDTH), BF16),
        scratch_shapes=[pltpu.VMEM((pairs, 2 * nbp, LANES), F32),
                        pltpu.VMEM((width, S), BF16),
                        pltpu.VMEM((n_heads, nbp + 8, MOBA_BLOCK), F32),
                        pltpu.VMEM((n_heads, 1, MOBA_BLOCK), F32),
                        pltpu.VMEM((n_heads, MOBA_HEAD_DIM + BF16_SUBLANES, MOBA_BLOCK), F32),
                        pltpu.VMEM((slots, n_heads, unroll * MOBA_BLOCK, MOBA_BLOCK), BF16)],
        compiler_params=_cparams("parallel", "parallel", "arbitrary"),
    )(proj, proj, proj)


def _pool_kernel(p_ref, halo_ref, w_ref, sc_ref, o_ref, *, blocks_per_seq):
    bm = p_ref.shape[0]
    H = BF16_SUBLANES
    i = pl.program_id(0)
    first = (i % blocks_per_seq) == 0
    t1 = (lax.broadcasted_iota(jnp.int32, (bm, 1), 0) + (i % blocks_per_seq) * bm + 1).astype(F32)
    for g, w in enumerate(POOL_WINDOWS):
        cols = slice(g * POOL_GROUP, (g + 1) * POOL_GROUP)
        cur = p_ref[:, cols].astype(F32)
        halo = jnp.where(first, 0.0, halo_ref[:, cols].astype(F32))
        ext = jnp.concatenate([halo, cur], axis=0)
        acc = ext
        sh = 1
        while sh < w:
            acc = acc + pltpu.roll(acc, sh, axis=0)
            sh *= 2
        win = acc[H:, :]
        pooled = win / jnp.minimum(t1, float(w)) - cur
        y = _dot(pooled.astype(BF16), w_ref[g])
        o_ref[:, cols] = (y * sc_ref[:, cols]).astype(o_ref.dtype)


def multiscale_pool(proj, pool_w, pool_scale, M, S, bm):
    H = BF16_SUBLANES
    pcol = 3 * A_WIDTH // POOL_WIDTH
    return pl.pallas_call(
        functools.partial(_pool_kernel, blocks_per_seq=S // bm),
        grid=(M // bm,),
        in_specs=[pl.BlockSpec((bm, POOL_WIDTH), lambda i: (i, pcol)),
                  pl.BlockSpec((H, POOL_WIDTH), lambda i: (jnp.maximum(i * (bm // H) - 1, 0), pcol)),
                  pl.BlockSpec((len(POOL_WINDOWS), POOL_GROUP, POOL_GROUP), lambda i: (0, 0, 0)),
                  pl.BlockSpec((1, POOL_WIDTH), lambda i: (0, 0))],
        out_specs=pl.BlockSpec((bm, POOL_WIDTH), lambda i: (i, 0)),
        out_shape=jax.ShapeDtypeStruct((M, POOL_WIDTH), BF16),
        compiler_params=_cparams("parallel"),
    )(proj, proj, pool_w, pool_scale.reshape(1, POOL_WIDTH))


def _xattn_absorb_kernel(mem_ref, g_ref, wkv_ref, wq_ref, wo_ref, qk_ref, vo_ref):
    n_mem, D = mem_ref.shape
    hd = D // XATTN_HEADS
    mn = _rmsnorm(mem_ref[...], g_ref[...]).astype(BF16)
    kv = _dot(mn, wkv_ref[...].astype(BF16)).astype(BF16)
    for hh in range(XATTN_HEADS):
        cols = slice(hh * hd, (hh + 1) * hd)
        mcols = slice(hh * n_mem, (hh + 1) * n_mem)
        qk_ref[0, :, mcols] = (_dot_nt(wq_ref[:, cols].astype(BF16), kv[:, cols]) * hd ** -0.5).astype(BF16)
        vo_ref[0, mcols, :] = _dot(kv[:, D + hh * hd:D + (hh + 1) * hd], wo_ref[cols, :].astype(BF16)).astype(BF16)


def xattn_absorb(mem2, g, wkv, wq, wo, B, n_mem):
    D = mem2.shape[1]
    L = wq.shape[0]
    HM = XATTN_HEADS * n_mem
    per_layer = lambda w: pl.BlockSpec((None,) + w.shape[1:], lambda s: (s // B, 0, 0))
    return pl.pallas_call(
        _xattn_absorb_kernel,
        grid=(L * B,),
        in_specs=[pl.BlockSpec((n_mem, D), lambda s: (s % B, 0)),
                  pl.BlockSpec((1, D), lambda s: (0, 0)),
                  per_layer(wkv), per_layer(wq), per_layer(wo)],
        out_specs=[pl.BlockSpec((1, D, HM), lambda s: (s, 0, 0)), pl.BlockSpec((1, HM, D), lambda s: (s, 0, 0))],
        out_shape=[jax.ShapeDtypeStruct((L * B, D, HM), BF16), jax.ShapeDtypeStruct((L * B, HM, D), BF16)],
        compiler_params=_cparams("parallel"),
    )(mem2, g.reshape(1, D), wkv, wq, wo)


def _mix_xattn_kernel(h_ref, a_ref, b_ref, wm_ref, g_ref, qk_ref, vo_ref, o_ref, *, n_mem):
    ka = a_ref.shape[1]
    h = h_ref[...] + _dot(a_ref[...], wm_ref[:ka, :]) + _dot(b_ref[...], wm_ref[ka:, :])
    xn = _rmsnorm(h, g_ref[...]).astype(BF16)
    s = _dot(xn, qk_ref[...])
    ps = []
    for hh in range(XATTN_HEADS):
        sh = s[:, hh * n_mem:(hh + 1) * n_mem]
        p = jnp.exp(sh - jnp.max(sh, axis=1, keepdims=True))
        ps.append((p / jnp.sum(p, axis=1, keepdims=True)).astype(BF16))
    o_ref[...] = h + _dot(jnp.concatenate(ps, axis=1), vo_ref[...])


def mix_xattn_residual(h, a, b, w_mix, g, qk, vo, layer, S, n_mem, bm):
    M, D = h.shape
    ka, kb = a.shape[1], b.shape[1]
    bps = S // bm
    HM = qk.shape[2]
    B = M // S
    return pl.pallas_call(
        functools.partial(_mix_xattn_kernel, n_mem=n_mem),
        grid=(M // bm,),
        in_specs=[pl.BlockSpec((bm, D), lambda i: (i, 0)),
                  pl.BlockSpec((bm, ka), lambda i: (i, 0)),
                  pl.BlockSpec((bm, kb), lambda i: (i, 0)),
                  _resident(w_mix),
                  pl.BlockSpec((1, D), lambda i: (0, 0)),
                  pl.BlockSpec((None, D, HM), lambda i: (layer * B + i // bps, 0, 0)),
                  pl.BlockSpec((None, HM, D), lambda i: (layer * B + i // bps, 0, 0))],
        out_specs=pl.BlockSpec((bm, D), lambda i: (i, 0)),
        out_shape=jax.ShapeDtypeStruct((M, D), F32),
        compiler_params=_cparams("parallel"),
    )(h, a, b, w_mix, g.reshape(1, D), qk, vo)


def _ffn_kernel(h_ref, halo_ref, g_ref, wup_ref, cw_ref, wd_ref, fg_ref, o_ref,
                xn_ref, acc_ref, y_ref, *, blocks_per_seq, final_norm, sub):
    H = BF16_SUBLANES
    nc, cf = wd_ref.shape[0], wd_ref.shape[1]
    n_sub = acc_ref.shape[0] // sub

    first = (pl.program_id(0) % blocks_per_seq) == 0
    xn_ref[:H, :] = jnp.where(first, 0.0, _rmsnorm(halo_ref[...], g_ref[...])).astype(BF16)

    def normalize(r):
        rows = slice(r * sub, (r + 1) * sub)
        xn_ref[H + r * sub:H + (r + 1) * sub, :] = _rmsnorm(h_ref[rows, :], g_ref[...]).astype(BF16)

    chunk_cols = lambda c: pl.ds(pl.multiple_of(c * cf, cf), cf)

    def up(c, r):
        if r == 0:
            xs, dst = xn_ref[:sub + H, :], slice(0, sub + H)
        else:
            xs, dst = xn_ref[H + r * sub:H + (r + 1) * sub, :], slice(H, sub + H)
            y_ref[r % 2, :H, :] = y_ref[(r - 1) % 2, sub:sub + H, :]
        y_ref[r % 2, dst, :cf] = _dot(xs, wup_ref[:, chunk_cols(c)])
        y_ref[r % 2, dst, cf:] = _dot(xs, wup_ref[:, chunk_cols(nc + c)])

    def conv(r, part, cw):
        cols = slice(part * cf, (part + 1) * cf)
        out = y_ref[r % 2, H:, cols] * cw[FFN_CONV - 1:FFN_CONV, :]
        for k in range(1, FFN_CONV):
            out = out + y_ref[r % 2, H - k:H - k + sub, cols] * cw[FFN_CONV - 1 - k:FFN_CONV - k, :]
        return out

    def chunk(c, leading=False, trailing=False):
        cwg, cwu = cw_ref[:, chunk_cols(c)], cw_ref[:, chunk_cols(nc + c)]
        wd = wd_ref[c]
        for r in range(n_sub):
            if r + 1 < n_sub:
                if leading:
                    normalize(r + 1)
                up(c, r + 1)
            elif not trailing:
                up(c + 1, 0)
            down = _dot((_silu(conv(r, 0, cwg)) * conv(r, 1, cwu)).astype(BF16), wd)
            rows = slice(r * sub, (r + 1) * sub)
            if leading:
                acc_ref[rows, :] = down
            elif not trailing:
                acc_ref[rows, :] += down
            else:
                y = h_ref[rows, :] + acc_ref[rows, :] + down
                o_ref[rows, :] = _rmsnorm(y, fg_ref[...]) if final_norm else y

    assert nc >= 3
    normalize(0)
    up(0, 0)
    chunk(0, leading=True)
    lax.fori_loop(1, nc - 1, lambda c, carry: (chunk(c), carry)[1], 0)
    chunk(nc - 1, trailing=True)


def ffn_residual(h, g, w_up, conv_w, w_down, layer, final_g, S, bm, cf, final_norm):
    M, D = h.shape
    d_ff = w_down.shape[1]
    H = BF16_SUBLANES
    nc = d_ff // cf
    sub = min(256, bm)
    assert (bm // sub) % 2 == 0, "the two y_ref slots alternate per sub-block across chunks"
    wd3 = w_down.reshape(-1, nc, cf, D)
    return pl.pallas_call(
        functools.partial(_ffn_kernel, blocks_per_seq=S // bm, final_norm=final_norm, sub=sub),
        grid=(M // bm,),
        in_specs=[pl.BlockSpec((bm, D), lambda i: (i, 0)),
                  pl.BlockSpec((H, D), lambda i: (jnp.maximum(i * (bm // H) - 1, 0), 0)),
                  pl.BlockSpec((1, D), lambda i: (0, 0)),
                  _resident(w_up, layer), _resident(conv_w, layer), _resident(wd3, layer),
                  pl.BlockSpec((1, D), lambda i: (0, 0))],
        out_specs=pl.BlockSpec((bm, D), lambda i: (i, 0)),
        out_shape=jax.ShapeDtypeStruct((M, D), F32),
        scratch_shapes=[pltpu.VMEM((H + bm, D), BF16), pltpu.VMEM((bm, D), F32),
                        pltpu.VMEM((2, H + sub, 2 * cf), F32)],
        compiler_params=_cparams("parallel"),
    )(h, h, g.reshape(1, D), w_up, conv_w, wd3, final_g.reshape(1, D))


def _sgu_kernel(u_ref, v_ref, lg_ref, lb_ref, w_ref, bt_ref, o_ref):
    rows = u_ref.shape[0]
    T = SGU_CHUNK
    v = v_ref[...].astype(F32)
    mu = jnp.mean(v, axis=-1, keepdims=True)
    d = v - mu
    var = jnp.mean(d * d, axis=-1, keepdims=True)
    vn = (d * lax.rsqrt(var + EPS) * lg_ref[...] + lb_ref[...]).astype(BF16)
    causal = (lax.broadcasted_iota(jnp.int32, (T, T), 1) <= lax.broadcasted_iota(jnp.int32, (T, T), 0))
    for g in range(SGU_GROUPS):
        cols = slice(g * SGU_GROUP, (g + 1) * SGU_GROUP)
        wg = jnp.where(causal, w_ref[g], 0.0).astype(BF16)
        bias = bt_ref[:, g:g + 1]
        for c in range(rows // T):
            rs = slice(c * T, (c + 1) * T)
            s = _dot(wg, vn[rs, cols]) + bias
            o_ref[rs, cols] = (u_ref[rs, cols].astype(F32) * s).astype(o_ref.dtype)


def spatial_gating(proj, ln_g, ln_b, w_s, b_s, M, rows):
    return pl.pallas_call(
        _sgu_kernel,
        grid=(M // rows,),
        in_specs=[pl.BlockSpec((rows, SGU_WIDTH), lambda i: (i, 0)),
                  pl.BlockSpec((rows, SGU_WIDTH), lambda i: (i, 1)),
                  pl.BlockSpec((1, SGU_WIDTH), lambda i: (0, 0)),
                  pl.BlockSpec((1, SGU_WIDTH), lambda i: (0, 0)),
                  pl.BlockSpec((SGU_GROUPS, SGU_CHUNK, SGU_CHUNK), lambda i: (0, 0, 0)),
                  pl.BlockSpec((SGU_CHUNK, SGU_GROUPS), lambda i: (0, 0))],
        out_specs=pl.BlockSpec((rows, SGU_WIDTH), lambda i: (i, 0)),
        out_shape=jax.ShapeDtypeStruct((M, SGU_WIDTH), BF16),
        compiler_params=_cparams("parallel"),
    )(proj, proj, ln_g.reshape(1, -1), ln_b.reshape(1, -1), w_s, b_s.T)


def _gdn_intra_kernel(q_ref, k_ref, v_ref, tail_ref, tailt_ref, alog_ref,
                      dtb_ref, u_ref, w_ref, qd_ref, kd_ref, qk_ref, gl_ref):
    rows = q_ref.shape[0]
    C = DN_CHUNK
    HD = DN_HEAD_DIM
    x = jnp.concatenate([q_ref[...], k_ref[...], v_ref[...]], axis=1).astype(F32)

    ii = lax.broadcasted_iota(jnp.int32, (C, C), 0)
    jj = lax.broadcasted_iota(jnp.int32, (C, C), 1)
    lower = jj <= ii
    strict = jj < ii
    su = lax.broadcasted_iota(jnp.int32, (C, LANES), 0)
    ju = lax.broadcasted_iota(jnp.int32, (C, LANES), 1)
    upper_ext = jnp.where(((ju < C) & (su > ju)) | (ju == C), 1.0, 0.0)

    inst = [(c, hh) for hh in range(DN_HEADS) for c in range(rows // C)]
    qs, ks, vs, bs, stacks = [], [], [], [], []
    for hh in range(DN_HEADS):
        qh = x[:, hh * HD:(hh + 1) * HD]
        kh = x[:, DN_WIDTH + hh * HD:DN_WIDTH + (hh + 1) * HD]
        vh = x[:, 2 * DN_WIDTH + hh * HD:2 * DN_WIDTH + (hh + 1) * HD]
        qh = qh * lax.rsqrt(jnp.sum(qh * qh, axis=-1, keepdims=True) + EPS) * HD ** -0.5
        kh = kh * lax.rsqrt(jnp.sum(kh * kh, axis=-1, keepdims=True) + EPS)
        beta = 1.0 / (1.0 + jnp.exp(-tail_ref[:, hh:hh + 1]))
        a_raw = tailt_ref[DN_HEADS + hh:DN_HEADS + hh + 1, :]
        z = a_raw + dtb_ref[0:1, hh:hh + 1]
        softplus = jnp.maximum(z, 0.0) + jnp.log(1.0 + jnp.exp(-jnp.abs(z)))
        g_row = -jnp.exp(alog_ref[0:1, hh:hh + 1]) * softplus
        for c in range(rows // C):
            rs = slice(c * C, (c + 1) * C)
            qs.append(qh[rs]); ks.append(kh[rs]); vs.append(vh[rs]); bs.append(beta[rs])
            gr = jnp.broadcast_to(g_row[:, rs], (C, C))
            stacks += [jnp.where(lower, gr, 0.0), jnp.where(lower, 0.0, gr)]

    stacked = jnp.concatenate(stacks, axis=0)
    s_hi = stacked.astype(BF16)
    s_lo = (stacked - s_hi.astype(F32)).astype(BF16)
    ue = upper_ext.astype(BF16)
    dall = _dot(s_hi, ue) + _dot(s_lo, ue)

    decays, gcs, gc_revs, k16s, kbs = [], [], [], [], []
    for n, (c, hh) in enumerate(inst):
        dext = dall[n * 2 * C:(n + 1) * 2 * C]
        decays.append(jnp.exp(jnp.where(lower, dext[:C, :C], -jnp.inf)))
        gcs.append(dext[:C, C:C + 1])
        gc_revs.append(dext[C:, C:C + 1])
        kbs.append(ks[n] * bs[n])
        k16s.append(ks[n].astype(BF16))
    kq = [_dot_nt(jnp.concatenate([kbs[n], qs[n]], axis=0).astype(BF16), k16s[n]) for n in range(len(inst))]
    kk = [m[:C] for m in kq]
    qk = [m[C:] for m in kq]
    pws = [jnp.where(strict, kk[n] * decays[n], 0.0).astype(BF16) for n in range(len(inst))]
    egc = [jnp.exp(g) for g in gcs]
    rhs = [jnp.concatenate([vs[n] * bs[n], kbs[n] * egc[n]], axis=1) for n in range(len(inst))]
    sols = [rhs[n] - _dot(pws[n], rhs[n].astype(BF16)) for n in range(len(inst))]
    for _ in range(int(math.log2(C)) - 1):
        pws = [_dot(p, p).astype(BF16) for p in pws]
        sols = [s + _dot(p, s.astype(BF16)) for p, s in zip(pws, sols)]
    for n, (c, hh) in enumerate(inst):
        rs = slice(c * C, (c + 1) * C)
        hcols = slice(hh * HD, (hh + 1) * HD)
        u_ref[rs, hcols] = sols[n][:, :HD].astype(u_ref.dtype)
        w_ref[rs, hcols] = sols[n][:, HD:].astype(w_ref.dtype)
        qkd = jnp.where(lower, qk[n] * decays[n], 0.0)
        qk_ref[rs, hcols] = jnp.concatenate([qkd, jnp.zeros_like(qkd)], axis=1).astype(qk_ref.dtype)
        qd_ref[rs, hcols] = (qs[n] * egc[n]).astype(qd_ref.dtype)
        kd_ref[rs, hcols] = (ks[n] * jnp.exp(gc_revs[n])).astype(kd_ref.dtype)
        gl_ref[c * 8:(c + 1) * 8, hcols] = jnp.broadcast_to(egc[n][C - 1:C, :], (8, HD))


def gdn_intra(proj, tail, tail_t, a_log, dt_bias, M, rows):
    c0 = 2 * SGU_WIDTH // DN_WIDTH
    pad = lambda p: jnp.pad(p.reshape(1, -1), ((0, 0), (0, LANES - p.shape[0])))
    seq = lambda dt: jax.ShapeDtypeStruct((M, DN_WIDTH), dt)
    row_spec = pl.BlockSpec((rows, DN_WIDTH), lambda i: (i, 0))
    cur_spec = lambda part: pl.BlockSpec((rows, DN_WIDTH), lambda i: (i, c0 + part))
    return pl.pallas_call(
        _gdn_intra_kernel,
        grid=(M // rows,),
        in_specs=[cur_spec(0), cur_spec(1), cur_spec(2),
                  pl.BlockSpec((rows, LANES), lambda i: (i, 0)),
                  pl.BlockSpec((2 * DN_HEADS, rows), lambda i: (0, i)),
                  pl.BlockSpec((1, LANES), lambda i: (0, 0)),
                  pl.BlockSpec((1, LANES), lambda i: (0, 0))],
        out_specs=[row_spec, row_spec, row_spec, row_spec, row_spec,
                   pl.BlockSpec((rows // DN_CHUNK * 8, DN_WIDTH), lambda i: (i, 0))],
        out_shape=[seq(BF16), seq(BF16), seq(BF16), seq(BF16), seq(BF16),
                   jax.ShapeDtypeStruct((M // DN_CHUNK * 8, DN_WIDTH), F32)],
        compiler_params=_cparams("parallel"),
    )(proj, proj, proj, tail, tail_t, pad(a_log), pad(dt_bias))


def _gdn_scan_kernel(u_ref, w_ref, qd_ref, kd_ref, qk_ref, gl_ref, gate_ref, ng_ref, o_ref, state_ref, *,
                     chunks):
    C = DN_CHUNK
    HD = DN_HEAD_DIM
    B = u_ref.shape[0]

    @pl.when(pl.program_id(0) == 0)
    def _():
        state_ref[...] = jnp.zeros_like(state_ref)

    ng = ng_ref[...]
    inst = [(b, hh) for b in range(B) for hh in range(DN_HEADS)]
    col = lambda hh: slice(hh * HD, (hh + 1) * HD)
    states = [state_ref[b, hh] for b, hh in inst]
    for c in range(chunks):
        rs = slice(c * C, (c + 1) * C)
        kdt = [kd_ref[b, rs, col(hh)].astype(F32).T.astype(BF16) for b, hh in inst]
        st16 = [s.astype(BF16) for s in states]
        wq = [_dot(jnp.concatenate([w_ref[b, rs, col(hh)], qd_ref[b, rs, col(hh)]], axis=0), st16[n])
              for n, (b, hh) in enumerate(inst)]
        vn16 = [(u_ref[b, rs, col(hh)].astype(F32) - wq[n][:C]).astype(BF16) for n, (b, hh) in enumerate(inst)]
        qkv = [_dot(jnp.concatenate([qk_ref[b, rs, col(hh)][:, :C], kdt[n]], axis=0), vn16[n])
               for n, (b, hh) in enumerate(inst)]
        states = [states[n] * gl_ref[b, c * 8:c * 8 + 1, col(hh)] + qkv[n][C:]
                  for n, (b, hh) in enumerate(inst)]
        for n, (b, hh) in enumerate(inst):
            o = wq[n][C:] + qkv[n][:C]
            o = o * lax.rsqrt(jnp.mean(o * o, axis=-1, keepdims=True) + EPS) * ng
            o_ref[b, rs, col(hh)] = (o * _silu(gate_ref[b, rs, col(hh)].astype(F32))).astype(o_ref.dtype)
    for n, (b, hh) in enumerate(inst):
        state_ref[b, hh] = states[n]


def gdn_scan(u, w, qd, kd, qk, gl, proj3, norm_g, B, S, chunks):
    rows = chunks * DN_CHUNK
    r3 = lambda a: a.reshape(B, S, DN_WIDTH)
    gcol = (2 * SGU_WIDTH + 3 * DN_WIDTH) // DN_WIDTH
    seq_spec = pl.BlockSpec((B, rows, DN_WIDTH), lambda n: (0, n, 0))
    return pl.pallas_call(
        functools.partial(_gdn_scan_kernel, chunks=chunks),
        grid=(S // rows,),
        in_specs=[seq_spec, seq_spec, seq_spec, seq_spec, seq_spec,
                  pl.BlockSpec((B, chunks * 8, DN_WIDTH), lambda n: (0, n, 0)),
                  pl.BlockSpec((B, rows, DN_WIDTH), lambda n: (0, n, gcol)),
                  pl.BlockSpec((1, DN_HEAD_DIM), lambda n: (0, 0))],
        out_specs=seq_spec,
        out_shape=jax.ShapeDtypeStruct((B, S, DN_WIDTH), BF16),
        scratch_shapes=[pltpu.VMEM((B, DN_HEADS, DN_HEAD_DIM, DN_HEAD_DIM), F32)],
        compiler_params=_cparams("arbitrary"),
    )(r3(u), r3(w), r3(qd), r3(kd), r3(qk), gl.reshape(B, S // DN_CHUNK * 8, DN_WIDTH), proj3,
      norm_g.reshape(1, DN_HEAD_DIM))


def _tiles(S):
    rows = min(1024, S)
    return dict(rows=rows, sgu_rows=rows, gdn_rows=min(512, S), scan_chunks=min(8, S // DN_CHUNK),
                ffn_chunk=256)


def _forward(x, mem, mem_norm, norm_mix, norm_xattn, norm_ffn, ev_w_in, pool_w, pool_scale, ev_w_out,
             od_w_in, sgu_ln_g, sgu_ln_b, sgu_w, sgu_b, dn_conv, dn_a_log, dn_dt_bias, dn_norm_g,
             od_w_out, xattn_wq, xattn_wkv, xattn_wo, ffn_w_up, ffn_conv, ffn_w_down, final_norm):
    B, S, D = x.shape
    n_mem = mem.shape[1]
    M = B * S
    depth = norm_mix.shape[0]
    bf = lambda a: a.astype(BF16)
    t = _tiles(S)

    h = x.reshape(M, D)
    mem2 = mem.reshape(B * n_mem, D)
    w_up_all, w_down_all = bf(ffn_w_up), bf(ffn_w_down)
    x_qk, x_vo = xattn_absorb(mem2, mem_norm, xattn_wkv, xattn_wq, xattn_wo, B, n_mem)
    for layer in range(depth):
        i = layer // 2
        if layer % 2 == 0:
            proj, = norm_matmul(h, norm_mix[layer], [bf(ev_w_in[i])], [BF16], t["rows"])
            a_out = moba_attention(proj.reshape(B, S, -1), B, S).reshape(M, A_WIDTH)
            b_out = multiscale_pool(proj, bf(pool_w[i]), pool_scale[i], M, S, t["rows"])
            mix_a, mix_b, w_mix = a_out, b_out, ev_w_out[i]
        else:
            main_w = 2 * SGU_WIDTH + 4 * DN_WIDTH
            w_in = od_w_in[i]
            w_tail = jnp.pad(w_in[:, main_w:], ((0, 0), (0, LANES - 2 * DN_HEADS)))
            proj, tail, tail_t = od_projection(h, norm_mix[layer], bf(w_in), bf(w_tail), dn_conv[i],
                                               S, t["rows"])
            c_out = spatial_gating(proj, sgu_ln_g[i], sgu_ln_b[i], sgu_w[i], sgu_b[i], M, t["sgu_rows"])
            u, w, qd, kd, qk, gl = gdn_intra(proj, tail, tail_t, dn_a_log[i], dn_dt_bias[i], M, t["gdn_rows"])
            d_out = gdn_scan(u, w, qd, kd, qk, gl, proj.reshape(B, S, -1), dn_norm_g[i], B, S,
                             t["scan_chunks"])
            mix_a, mix_b, w_mix = c_out, d_out.reshape(M, DN_WIDTH), od_w_out[i]
        h = mix_xattn_residual(h, mix_a, mix_b, bf(w_mix), norm_xattn[layer], x_qk, x_vo, layer, S, n_mem, t["rows"])
        h = ffn_residual(h, norm_ffn[layer], w_up_all, ffn_conv, w_down_all, layer, final_norm, S, t["rows"],
                         t["ffn_chunk"], final_norm=(layer == depth - 1))
    return h.reshape(B, S, D)


def kernel(x, mem, mem_norm, norm_mix, norm_xattn, norm_ffn, ev_w_in, pool_w, pool_scale, ev_w_out, od_w_in, sgu_ln_g, sgu_ln_b, sgu_w, sgu_b, dn_conv, dn_a_log, dn_dt_bias, dn_norm_g, od_w_out, xattn_wq, xattn_wkv, xattn_wo, ffn_w_up, ffn_conv, ffn_w_down, final_norm):
    return _forward(x, mem, mem_norm, norm_mix, norm_xattn, norm_ffn, ev_w_in, pool_w, pool_scale, ev_w_out,
                    od_w_in, sgu_ln_g, sgu_ln_b, sgu_w, sgu_b, dn_conv, dn_a_log, dn_dt_bias, dn_norm_g,
                    od_w_out, xattn_wq, xattn_wkv, xattn_wo, ffn_w_up, ffn_conv, ffn_w_down, final_norm)
```

```python
import functools
import math

import jax
import jax.numpy as jnp
from jax import lax
from jax.experimental import pallas as pl
from jax.experimental.pallas import tpu as pltpu

F32 = jnp.float32
BF16 = jnp.bfloat16
EPS = 1e-6
NEG_BIG = -1e30

VMEM_LIMIT_BYTES = 48 * 1024 * 1024
BF16_SUBLANES = 16
LANES = 128

MOBA_HEADS, MOBA_HEAD_DIM, MOBA_BLOCK, MOBA_TOPK = 8, 64, 256, 3
A_WIDTH = MOBA_HEADS * MOBA_HEAD_DIM
POOL_WINDOWS = (2, 4, 8, 16)
POOL_GROUP = 128
POOL_WIDTH = POOL_GROUP * len(POOL_WINDOWS)
SGU_GROUPS, SGU_GROUP, SGU_CHUNK = 4, 128, 128
SGU_WIDTH = SGU_GROUPS * SGU_GROUP
DN_HEADS, DN_HEAD_DIM, DN_CONV, DN_CHUNK = 4, 128, 4, 64
DN_WIDTH = DN_HEADS * DN_HEAD_DIM
XATTN_HEADS = 4
FFN_CONV = 3


def _cparams(*sem):
    return pltpu.CompilerParams(dimension_semantics=sem, vmem_limit_bytes=VMEM_LIMIT_BYTES)


def _rmsnorm(x, g):
    return x * lax.rsqrt(jnp.mean(x * x, axis=-1, keepdims=True) + EPS) * g


def _silu(x):
    return x * (0.5 * jnp.tanh(0.5 * x) + 0.5)


def _dot(a, b):
    return jnp.dot(a, b, preferred_element_type=F32)


def _resident(arr, layer=None):
    if layer is None:
        return pl.BlockSpec(arr.shape, lambda i: (0,) * arr.ndim, pipeline_mode=pl.Buffered(1))
    return pl.BlockSpec((None,) + arr.shape[1:], lambda i: (layer,) + (0,) * (arr.ndim - 1),
                        pipeline_mode=pl.Buffered(1))


def _dot_nt(a, b, precision=None):
    return lax.dot_general(a, b, (((1,), (1,)), ((), ())), preferred_element_type=F32,
                           precision=precision)


def _norm_mm_kernel(x_ref, g_ref, *refs, bn):
    n = len(refs) // 2
    xn = _rmsnorm(x_ref[...], g_ref[...]).astype(BF16)
    for w_ref, o_ref in zip(refs[:n], refs[n:]):
        N = w_ref.shape[1]
        for c0 in range(0, N, bn):
            c1 = min(c0 + bn, N)
            o_ref[:, c0:c1] = _dot(xn, w_ref[:, c0:c1]).astype(o_ref.dtype)


def norm_matmul(x, g, ws, out_dtypes, bm, bn=512, layer=None):
    M, D = x.shape
    return pl.pallas_call(
        functools.partial(_norm_mm_kernel, bn=bn),
        grid=(M // bm,),
        in_specs=[pl.BlockSpec((bm, D), lambda i: (i, 0)),
                  pl.BlockSpec((1, D), lambda i: (0, 0))]
                 + [_resident(w, layer) for w in ws],
        out_specs=[pl.BlockSpec((bm, w.shape[-1]), lambda i: (i, 0)) for w in ws],
        out_shape=[jax.ShapeDtypeStruct((M, w.shape[-1]), dt) for w, dt in zip(ws, out_dtypes)],
        compiler_params=_cparams("parallel"),
    )(x, g.reshape(1, D), *ws)


def _gelu_tanh(x):
    return 0.5 * x * (1.0 + jnp.tanh(math.sqrt(2.0 / math.pi) * (x + 0.044715 * (x * x * x))))


def _od_proj_kernel(x_ref, halo_ref, g_ref, w_ref, wt_ref, cw_ref, o_ref, t_ref, tt_ref, xe_ref, *,
                    blocks_per_seq, bn):
    bm = x_ref.shape[0]
    H = BF16_SUBLANES
    z_w, qkv_w = 2 * SGU_WIDTH, 3 * DN_WIDTH
    first = (pl.program_id(0) % blocks_per_seq) == 0
    xe_ref[:H, :] = jnp.where(first, 0.0, _rmsnorm(halo_ref[...], g_ref[...])).astype(BF16)
    xe_ref[H:, :] = _rmsnorm(x_ref[...], g_ref[...]).astype(BF16)
    xn = xe_ref[H:, :]
    tail = _dot(xn, wt_ref[...])
    t_ref[...] = tail
    tt_ref[...] = tail.T[:H, :]
    for c0 in range(0, o_ref.shape[1], bn):
        cols = slice(c0, c0 + bn)
        if c0 < z_w:
            o_ref[:, cols] = _gelu_tanh(_dot(xn, w_ref[:, cols])).astype(o_ref.dtype)
        elif c0 < z_w + qkv_w:
            y = _dot(xe_ref[...], w_ref[:, cols])
            cw = cw_ref[:, c0 - z_w:c0 - z_w + bn]
            out = y[H:, :] * cw[DN_CONV - 1:DN_CONV, :]
            for k in range(1, DN_CONV):
                out = out + pltpu.roll(y, k, axis=0)[H:, :] * cw[DN_CONV - 1 - k:DN_CONV - k, :]
            o_ref[:, cols] = _silu(out).astype(o_ref.dtype)
        else:
            o_ref[:, cols] = _dot(xn, w_ref[:, cols]).astype(o_ref.dtype)


def od_projection(x, g, w, w_tail, conv_w, S, bm, bn=512):
    M, D = x.shape
    N = 2 * SGU_WIDTH + 4 * DN_WIDTH
    H = BF16_SUBLANES
    return pl.pallas_call(
        functools.partial(_od_proj_kernel, blocks_per_seq=S // bm, bn=bn),
        grid=(M // bm,),
        in_specs=[pl.BlockSpec((bm, D), lambda i: (i, 0)),
                  pl.BlockSpec((H, D), lambda i: (jnp.maximum(i * (bm // H) - 1, 0), 0)),
                  pl.BlockSpec((1, D), lambda i: (0, 0)),
                  _resident(w), _resident(w_tail), _resident(conv_w)],
        out_specs=[pl.BlockSpec((bm, N), lambda i: (i, 0)), pl.BlockSpec((bm, LANES), lambda i: (i, 0)),
                   pl.BlockSpec((H, bm), lambda i: (0, i))],
        out_shape=[jax.ShapeDtypeStruct((M, N), BF16), jax.ShapeDtypeStruct((M, LANES), F32),
                   jax.ShapeDtypeStruct((H, M), F32)],
        scratch_shapes=[pltpu.VMEM((H + bm, D), BF16)],
        compiler_params=_cparams("parallel"),
    )(x, x, g.reshape(1, D), w, w_tail, conv_w)


def _moba_kernel(q_ref, k_ref, v_ref, o_ref, kme_ref, vt_ref, sel_ref, m_ref, acc_ref, s_ref, *,
                 nb, nbp, unroll, pairs):
    BS = MOBA_BLOCK
    HD = MOBA_HEAD_DIM
    n_heads = 2 * pairs
    i = pl.program_id(2)
    lane = lax.broadcasted_iota(jnp.int32, (1, LANES), 1)
    head_lanes = (lane < HD, lane >= HD)
    pair_lanes = lambda u: slice((u // 2) * LANES, (u // 2 + 1) * LANES)

    @pl.when(i == 0)
    def _():
        kme_ref[...] = jnp.zeros_like(kme_ref)
        for n in range(nb):
            rows = slice(n * BS, (n + 1) * BS)
            mean = jnp.sum(k_ref[0, rows, :].astype(F32), axis=0, keepdims=True) / BS
            for u in range(n_heads):
                kme_ref[u // 2, (u % 2) * nbp + n:(u % 2) * nbp + n + 1, :] = jnp.where(
                    head_lanes[u % 2], mean[:, pair_lanes(u)], 0.0)
            vt_ref[:, rows] = v_ref[0, rows, :].astype(F32).T.astype(BF16)

    scale = HD ** -0.5 * math.log2(math.e)
    q_t = [q_ref[0, :, p * LANES:(p + 1) * LANES].astype(F32).T for p in range(pairs)]
    pair_row = lax.broadcasted_iota(jnp.int32, (LANES, 1), 0)
    head_rows = (pair_row < HD, pair_row >= HD)
    q_aug = [jnp.where(head_rows[u % 2], q_t[u // 2] * scale, 0.0).astype(BF16) for u in range(n_heads)]

    gates = []
    for p in range(pairs):
        km = kme_ref[p]
        k_hi = km.astype(BF16)
        r1 = km - k_hi.astype(F32)
        k_mid = r1.astype(BF16)
        k_lo = (r1 - k_mid.astype(F32)).astype(BF16)
        q16 = q_t[p].astype(BF16)
        gates.append(_dot(k_hi, q16) + _dot(k_mid, q16) + _dot(k_lo, q16))
    blk = lax.broadcasted_iota(jnp.int32, (nbp, 1), 0).astype(F32)
    valid = blk < i.astype(F32)
    for u in range(n_heads):
        g = jnp.where(valid, gates[u // 2][(u % 2) * nbp:(u % 2 + 1) * nbp], -jnp.inf)
        sel = jnp.zeros(g.shape, jnp.bool_)
        for _ in range(MOBA_TOPK):
            mx = jnp.max(g, axis=0, keepdims=True)
            idx = jnp.min(jnp.where(g == mx, blk, float(1 << 20)), axis=0, keepdims=True)
            pick = blk == idx
            sel = sel | pick
            g = jnp.where(pick, -jnp.inf, g)
        sel_ref[u, :nbp, :] = jnp.where(sel & valid, 1.0, 0.0)
        sel_ref[u, nbp:, :] = jnp.zeros((8, BS), F32)
        m_ref[u] = jnp.full((1, BS), NEG_BIG, F32)
        acc_ref[u] = jnp.zeros(acc_ref.shape[1:], F32)

    krow = lax.broadcasted_iota(jnp.int32, (BS, BS), 0)
    qcol = lax.broadcasted_iota(jnp.int32, (BS, BS), 1)
    PVR = HD + BF16_SUBLANES
    pv_rows = (slice(0, PVR), slice(LANES - PVR, LANES))
    pv_row = lax.broadcasted_iota(jnp.int32, (PVR, 1), 0)
    is_dim = (pv_row < HD, pv_row >= PVR - HD)

    def block_start(j):
        return pl.multiple_of(jnp.minimum(j, i) * BS, BS)

    def produce(g, slot):
        for t in range(unroll):
            rows = pl.ds(block_start(g * unroll + t), BS)
            for u in range(n_heads):
                s_ref[slot, u, t * BS:(t + 1) * BS, :] = _dot(
                    k_ref[0, rows, pair_lanes(u)], q_aug[u]).astype(BF16)

    def softmax_update(sts, sels, starts):
        heads = range(n_heads)
        m_new, alpha = [], []
        for u in heads:
            cand = jnp.full((1, BS), NEG_BIG, F32)
            for st, sel in zip(sts[u], sels[u]):
                mx = jnp.max(st.reshape(BS // BF16_SUBLANES, BF16_SUBLANES, BS), axis=0)
                mx = jnp.max(mx.astype(F32), axis=0, keepdims=True)
                cand = jnp.maximum(cand, mx if sel is None else jnp.where(sel, mx, NEG_BIG))
            m_old = m_ref[u]
            m_new.append(jnp.maximum(m_old, cand))
            alpha.append(jnp.exp2(m_old - m_new[u]))
            m_ref[u] = m_new[u]
        ps = []
        for u in heads:
            pu = []
            for st, sel in zip(sts[u], sels[u]):
                sub = m_new[u] if sel is None else jnp.where(sel, m_new[u], -NEG_BIG)
                pu.append(jnp.exp2(st - sub.astype(BF16)))
            ps.append(pu[0] if len(pu) == 1 else jnp.concatenate(pu, axis=0))
        pv = []
        for u in heads:
            rows = slice((u // 2) * LANES + pv_rows[u % 2].start, (u // 2) * LANES + pv_rows[u % 2].stop)
            vts = [jnp.where(is_dim[u % 2], vt_ref[rows, pl.ds(st0, BS)], jnp.ones((), BF16))
                   for st0 in starts]
            pv.append(_dot(vts[0] if len(vts) == 1 else jnp.concatenate(vts, axis=1), ps[u]))
        for u in heads:
            acc_ref[u] = acc_ref[u] * alpha[u] + pv[u]

    def consume(g, slot):
        js = [g * unroll + t for t in range(unroll)]
        sts = [[s_ref[slot, u, t * BS:(t + 1) * BS, :] for t in range(unroll)] for u in range(n_heads)]
        sels = [[sel_ref[u, pl.ds(j, 1), :] > 0.5 for j in js] for u in range(n_heads)]
        softmax_update(sts, sels, [block_start(j) for j in js])

    slots = s_ref.shape[0]
    assert unroll == 1, "the group after the last past block must be exactly the tile's own block"

    def body(gg, c):
        for t in range(slots):
            produce(slots * gg + t + 1, (t + 1) % slots)
            consume(slots * gg + t, t)
        return c

    n_groups = (i + unroll - 1) // unroll
    produce(0, 0)
    lax.fori_loop(0, (n_groups + slots - 1) // slots, body, 0)
    softmax_update([[jnp.where(krow <= qcol, s_ref[0, u, :BS, :], -jnp.inf)] for u in range(n_heads)],
                   [[None]] * n_heads, [block_start(i)])

    outs = []
    for u in range(n_heads):
        a = acc_ref[u]
        outs.append(a[:HD] / a[HD:HD + 1, :] if u % 2 == 0 else a[PVR - HD:] / a[0:1, :])
    o_ref[0] = jnp.concatenate(outs, axis=0).T.astype(o_ref.dtype)


def moba_attention(proj, B, S):
    nb = S // MOBA_BLOCK
    nbp = -(-nb // 8) * 8
    pairs = 4
    width = pairs * LANES
    groups = A_WIDTH // width
    unroll = 1
    slots = 3
    n_heads = 2 * pairs
    return pl.pallas_call(
        functools.partial(_moba_kernel, nb=nb, nbp=nbp, unroll=unroll, pairs=pairs),
        grid=(B, groups, nb),
        in_specs=[pl.BlockSpec((1, MOBA_BLOCK, width), lambda b, p, i: (b, i, p)),
                  pl.BlockSpec((1, S, width), lambda b, p, i: (b, 0, groups + p), pipeline_mode=pl.Buffered(1)),
                  pl.BlockSpec((1, S, width), lambda b, p, i: (b, 0, 2 * groups + p), pipeline_mode=pl.Buffered(1))],
        out_specs=pl.BlockSpec((1, MOBA_BLOCK, width), lambda b, p, i: (b, i, p)),
        out_shape=jax.ShapeDtypeStruct((B, S, A_WIDTH), BF16),
        scratch_shapes=[pltpu.VMEM((pairs, 2 * nbp, LANES), F32),
                        pltpu.VMEM((width, S), BF16),
                        pltpu.VMEM((n_heads, nbp + 8, MOBA_BLOCK), F32),
                        pltpu.VMEM((n_heads, 1, MOBA_BLOCK), F32),
                        pltpu.VMEM((n_heads, MOBA_HEAD_DIM + BF16_SUBLANES, MOBA_BLOCK), F32),
                        pltpu.VMEM((slots, n_heads, unroll * MOBA_BLOCK, MOBA_BLOCK), BF16)],
        compiler_params=_cparams("parallel", "parallel", "arbitrary"),
    )(proj, proj, proj)


def _pool_kernel(p_ref, halo_ref, w_ref, sc_ref, o_ref, *, blocks_per_seq):
    bm = p_ref.shape[0]
    H = BF16_SUBLANES
    i = pl.program_id(0)
    first = (i % blocks_per_seq) == 0
    t1 = (lax.broadcasted_iota(jnp.int32, (bm, 1), 0) + (i % blocks_per_seq) * bm + 1).astype(F32)
    for g, w in enumerate(POOL_WINDOWS):
        cols = slice(g * POOL_GROUP, (g + 1) * POOL_GROUP)
        cur = p_ref[:, cols].astype(F32)
        halo = jnp.where(first, 0.0, halo_ref[:, cols].astype(F32))
        ext = jnp.concatenate([halo, cur], axis=0)
        acc = ext
        sh = 1
        while sh < w:
            acc = acc + pltpu.roll(acc, sh, axis=0)
            sh *= 2
        win = acc[H:, :]
        pooled = win / jnp.minimum(t1, float(w)) - cur
        y = _dot(pooled.astype(BF16), w_ref[g])
        o_ref[:, cols] = (y * sc_ref[:, cols]).astype(o_ref.dtype)


def multiscale_pool(proj, pool_w, pool_scale, M, S, bm):
    H = BF16_SUBLANES
    pcol = 3 * A_WIDTH // POOL_WIDTH
    return pl.pallas_call(
        functools.partial(_pool_kernel, blocks_per_seq=S // bm),
        grid=(M // bm,),
        in_specs=[pl.BlockSpec((bm, POOL_WIDTH), lambda i: (i, pcol)),
                  pl.BlockSpec((H, POOL_WIDTH), lambda i: (jnp.maximum(i * (bm // H) - 1, 0), pcol)),
                  pl.BlockSpec((len(POOL_WINDOWS), POOL_GROUP, POOL_GROUP), lambda i: (0, 0, 0)),
                  pl.BlockSpec((1, POOL_WIDTH), lambda i: (0, 0))],
        out_specs=pl.BlockSpec((bm, POOL_WIDTH), lambda i: (i, 0)),
        out_shape=jax.ShapeDtypeStruct((M, POOL_WIDTH), BF16),
        compiler_params=_cparams("parallel"),
    )(proj, proj, pool_w, pool_scale.reshape(1, POOL_WIDTH))


def _xattn_absorb_kernel(mem_ref, g_ref, wkv_ref, wq_ref, wo_ref, qk_ref, vo_ref):
    n_mem, D = mem_ref.shape
    hd = D // XATTN_HEADS
    mn = _rmsnorm(mem_ref[...], g_ref[...]).astype(BF16)
    kv = _dot(mn, wkv_ref[...].astype(BF16)).astype(BF16)
    for hh in range(XATTN_HEADS):
        cols = slice(hh * hd, (hh + 1) * hd)
        mcols = slice(hh * n_mem, (hh + 1) * n_mem)
        qk_ref[0, :, mcols] = (_dot_nt(wq_ref[:, cols].astype(BF16), kv[:, cols]) * hd ** -0.5).astype(BF16)
        vo_ref[0, mcols, :] = _dot(kv[:, D + hh * hd:D + (hh + 1) * hd], wo_ref[cols, :].astype(BF16)).astype(BF16)


def xattn_absorb(mem2, g, wkv, wq, wo, B, n_mem):
    D = mem2.shape[1]
    L = wq.shape[0]
    HM = XATTN_HEADS * n_mem
    per_layer = lambda w: pl.BlockSpec((None,) + w.shape[1:], lambda s: (s // B, 0, 0))
    return pl.pallas_call(
        _xattn_absorb_kernel,
        grid=(L * B,),
        in_specs=[pl.BlockSpec((n_mem, D), lambda s: (s % B, 0)),
                  pl.BlockSpec((1, D), lambda s: (0, 0)),
                  per_layer(wkv), per_layer(wq), per_layer(wo)],
        out_specs=[pl.BlockSpec((1, D, HM), lambda s: (s, 0, 0)), pl.BlockSpec((1, HM, D), lambda s: (s, 0, 0))],
        out_shape=[jax.ShapeDtypeStruct((L * B, D, HM), BF16), jax.ShapeDtypeStruct((L * B, HM, D), BF16)],
        compiler_params=_cparams("parallel"),
    )(mem2, g.reshape(1, D), wkv, wq, wo)


def _mix_xattn_kernel(h_ref, a_ref, b_ref, wm_ref, g_ref, qk_ref, vo_ref, o_ref, *, n_mem):
    ka = a_ref.shape[1]
    h = h_ref[...] + _dot(a_ref[...], wm_ref[:ka, :]) + _dot(b_ref[...], wm_ref[ka:, :])
    xn = _rmsnorm(h, g_ref[...]).astype(BF16)
    s = _dot(xn, qk_ref[...])
    ps = []
    for hh in range(XATTN_HEADS):
        sh = s[:, hh * n_mem:(hh + 1) * n_mem]
        p = jnp.exp(sh - jnp.max(sh, axis=1, keepdims=True))
        ps.append((p / jnp.sum(p, axis=1, keepdims=True)).astype(BF16))
    o_ref[...] = h + _dot(jnp.concatenate(ps, axis=1), vo_ref[...])


def mix_xattn_residual(h, a, b, w_mix, g, qk, vo, layer, S, n_mem, bm):
    M, D = h.shape
    ka, kb = a.shape[1], b.shape[1]
    bps = S // bm
    HM = qk.shape[2]
    B = M // S
    return pl.pallas_call(
        functools.partial(_mix_xattn_kernel, n_mem=n_mem),
        grid=(M // bm,),
        in_specs=[pl.BlockSpec((bm, D), lambda i: (i, 0)),
                  pl.BlockSpec((bm, ka), lambda i: (i, 0)),
                  pl.BlockSpec((bm, kb), lambda i: (i, 0)),
                  _resident(w_mix),
                  pl.BlockSpec((1, D), lambda i: (0, 0)),
                  pl.BlockSpec((None, D, HM), lambda i: (layer * B + i // bps, 0, 0)),
                  pl.BlockSpec((None, HM, D), lambda i: (layer * B + i // bps, 0, 0))],
        out_specs=pl.BlockSpec((bm, D), lambda i: (i, 0)),
        out_shape=jax.ShapeDtypeStruct((M, D), F32),
        compiler_params=_cparams("parallel"),
    )(h, a, b, w_mix, g.reshape(1, D), qk, vo)


def _ffn_kernel(h_ref, halo_ref, g_ref, wup_ref, cw_ref, wd_ref, fg_ref, o_ref,
                xn_ref, acc_ref, y_ref, *, blocks_per_seq, final_norm, sub):
    H = BF16_SUBLANES
    nc, cf = wd_ref.shape[0], wd_ref.shape[1]
    n_sub = acc_ref.shape[0] // sub

    first = (pl.program_id(0) % blocks_per_seq) == 0
    xn_ref[:H, :] = jnp.where(first, 0.0, _rmsnorm(halo_ref[...], g_ref[...])).astype(BF16)

    def normalize(r):
        rows = slice(r * sub, (r + 1) * sub)
        xn_ref[H + r * sub:H + (r + 1) * sub, :] = _rmsnorm(h_ref[rows, :], g_ref[...]).astype(BF16)

    chunk_cols = lambda c: pl.ds(pl.multiple_of(c * cf, cf), cf)

    def up(c, r):
        if r == 0:
            xs, dst = xn_ref[:sub + H, :], slice(0, sub + H)
        else:
            xs, dst = xn_ref[H + r * sub:H + (r + 1) * sub, :], slice(H, sub + H)
            y_ref[r % 2, :H, :] = y_ref[(r - 1) % 2, sub:sub + H, :]
        y_ref[r % 2, dst, :cf] = _dot(xs, wup_ref[:, chunk_cols(c)])
        y_ref[r % 2, dst, cf:] = _dot(xs, wup_ref[:, chunk_cols(nc + c)])

    def conv(r, part, cw):
        cols = slice(part * cf, (part + 1) * cf)
        out = y_ref[r % 2, H:, cols] * cw[FFN_CONV - 1:FFN_CONV, :]
        for k in range(1, FFN_CONV):
            out = out + y_ref[r % 2, H - k:H - k + sub, cols] * cw[FFN_CONV - 1 - k:FFN_CONV - k, :]
        return out

    def chunk(c, leading=False, trailing=False):
        cwg, cwu = cw_ref[:, chunk_cols(c)], cw_ref[:, chunk_cols(nc + c)]
        wd = wd_ref[c]
        for r in range(n_sub):
            if r + 1 < n_sub:
                if leading:
                    normalize(r + 1)
                up(c, r + 1)
            elif not trailing:
                up(c + 1, 0)
            down = _dot((_silu(conv(r, 0, cwg)) * conv(r, 1, cwu)).astype(BF16), wd)
            rows = slice(r * sub, (r + 1) * sub)
            if leading:
                acc_ref[rows, :] = down
            elif not trailing:
                acc_ref[rows, :] += down
            else:
                y = h_ref[rows, :] + acc_ref[rows, :] + down
                o_ref[rows, :] = _rmsnorm(y, fg_ref[...]) if final_norm else y

    assert nc >= 3
    normalize(0)
    up(0, 0)
    chunk(0, leading=True)
    lax.fori_loop(1, nc - 1, lambda c, carry: (chunk(c), carry)[1], 0)
    chunk(nc - 1, trailing=True)


def ffn_residual(h, g, w_up, conv_w, w_down, layer, final_g, S, bm, cf, final_norm):
    M, D = h.shape
    d_ff = w_down.shape[1]
    H = BF16_SUBLANES
    nc = d_ff // cf
    sub = min(256, bm)
    assert (bm // sub) % 2 == 0, "the two y_ref slots alternate per sub-block across chunks"
    wd3 = w_down.reshape(-1, nc, cf, D)
    return pl.pallas_call(
        functools.partial(_ffn_kernel, blocks_per_seq=S // bm, final_norm=final_norm, sub=sub),
        grid=(M // bm,),
        in_specs=[pl.BlockSpec((bm, D), lambda i: (i, 0)),
                  pl.BlockSpec((H, D), lambda i: (jnp.maximum(i * (bm // H) - 1, 0), 0)),
                  pl.BlockSpec((1, D), lambda i: (0, 0)),
                  _resident(w_up, layer), _resident(conv_w, layer), _resident(wd3, layer),
                  pl.BlockSpec((1, D), lambda i: (0, 0))],
        out_specs=pl.BlockSpec((bm, D), lambda i: (i, 0)),
        out_shape=jax.ShapeDtypeStruct((M, D), F32),
        scratch_shapes=[pltpu.VMEM((H + bm, D), BF16), pltpu.VMEM((bm, D), F32),
                        pltpu.VMEM((2, H + sub, 2 * cf), F32)],
        compiler_params=_cparams("parallel"),
    )(h, h, g.reshape(1, D), w_up, conv_w, wd3, final_g.reshape(1, D))


def _sgu_kernel(u_ref, v_ref, lg_ref, lb_ref, w_ref, bt_ref, o_ref):
    rows = u_ref.shape[0]
    T = SGU_CHUNK
    v = v_ref[...].astype(F32)
    mu = jnp.mean(v, axis=-1, keepdims=True)
    d = v - mu
    var = jnp.mean(d * d, axis=-1, keepdims=True)
    vn = (d * lax.rsqrt(var + EPS) * lg_ref[...] + lb_ref[...]).astype(BF16)
    causal = (lax.broadcasted_iota(jnp.int32, (T, T), 1) <= lax.broadcasted_iota(jnp.int32, (T, T), 0))
    for g in range(SGU_GROUPS):
        cols = slice(g * SGU_GROUP, (g + 1) * SGU_GROUP)
        wg = jnp.where(causal, w_ref[g], 0.0).astype(BF16)
        bias = bt_ref[:, g:g + 1]
        for c in range(rows // T):
            rs = slice(c * T, (c + 1) * T)
            s = _dot(wg, vn[rs, cols]) + bias
            o_ref[rs, cols] = (u_ref[rs, cols].astype(F32) * s).astype(o_ref.dtype)


def spatial_gating(proj, ln_g, ln_b, w_s, b_s, M, rows):
    return pl.pallas_call(
        _sgu_kernel,
        grid=(M // rows,),
        in_specs=[pl.BlockSpec((rows, SGU_WIDTH), lambda i: (i, 0)),
                  pl.BlockSpec((rows, SGU_WIDTH), lambda i: (i, 1)),
                  pl.BlockSpec((1, SGU_WIDTH), lambda i: (0, 0)),
                  pl.BlockSpec((1, SGU_WIDTH), lambda i: (0, 0)),
                  pl.BlockSpec((SGU_GROUPS, SGU_CHUNK, SGU_CHUNK), lambda i: (0, 0, 0)),
                  pl.BlockSpec((SGU_CHUNK, SGU_GROUPS), lambda i: (0, 0))],
        out_specs=pl.BlockSpec((rows, SGU_WIDTH), lambda i: (i, 0)),
        out_shape=jax.ShapeDtypeStruct((M, SGU_WIDTH), BF16),
        compiler_params=_cparams("parallel"),
    )(proj, proj, ln_g.reshape(1, -1), ln_b.reshape(1, -1), w_s, b_s.T)


def _gdn_intra_kernel(q_ref, k_ref, v_ref, tail_ref, tailt_ref, alog_ref,
                      dtb_ref, u_ref, w_ref, qd_ref, kd_ref, qk_ref, gl_ref):
    rows = q_ref.shape[0]
    C = DN_CHUNK
    HD = DN_HEAD_DIM
    x = jnp.concatenate([q_ref[...], k_ref[...], v_ref[...]], axis=1).astype(F32)

    ii = lax.broadcasted_iota(jnp.int32, (C, C), 0)
    jj = lax.broadcasted_iota(jnp.int32, (C, C), 1)
    lower = jj <= ii
    strict = jj < ii
    su = lax.broadcasted_iota(jnp.int32, (C, LANES), 0)
    ju = lax.broadcasted_iota(jnp.int32, (C, LANES), 1)
    upper_ext = jnp.where(((ju < C) & (su > ju)) | (ju == C), 1.0, 0.0)

    inst = [(c, hh) for hh in range(DN_HEADS) for c in range(rows // C)]
    qs, ks, vs, bs, stacks = [], [], [], [], []
    for hh in range(DN_HEADS):
        qh = x[:, hh * HD:(hh + 1) * HD]
        kh = x[:, DN_WIDTH + hh * HD:DN_WIDTH + (hh + 1) * HD]
        vh = x[:, 2 * DN_WIDTH + hh * HD:2 * DN_WIDTH + (hh + 1) * HD]
        qh = qh * lax.rsqrt(jnp.sum(qh * qh, axis=-1, keepdims=True) + EPS) * HD ** -0.5
        kh = kh * lax.rsqrt(jnp.sum(kh * kh, axis=-1, keepdims=True) + EPS)
        beta = 1.0 / (1.0 + jnp.exp(-tail_ref[:, hh:hh + 1]))
        a_raw = tailt_ref[DN_HEADS + hh:DN_HEADS + hh + 1, :]
        z = a_raw + dtb_ref[0:1, hh:hh + 1]
        softplus = jnp.maximum(z, 0.0) + jnp.log(1.0 + jnp.exp(-jnp.abs(z)))
        g_row = -jnp.exp(alog_ref[0:1, hh:hh + 1]) * softplus
        for c in range(rows // C):
            rs = slice(c * C, (c + 1) * C)
            qs.append(qh[rs]); ks.append(kh[rs]); vs.append(vh[rs]); bs.append(beta[rs])
            gr = jnp.broadcast_to(g_row[:, rs], (C, C))
            stacks += [jnp.where(lower, gr, 0.0), jnp.where(lower, 0.0, gr)]

    stacked = jnp.concatenate(stacks, axis=0)
    s_hi = stacked.astype(BF16)
    s_lo = (stacked - s_hi.astype(F32)).astype(BF16)
    ue = upper_ext.astype(BF16)
    dall = _dot(s_hi, ue) + _dot(s_lo, ue)

    decays, gcs, gc_revs, k16s, kbs = [], [], [], [], []
    for n, (c, hh) in enumerate(inst):
        dext = dall[n * 2 * C:(n + 1) * 2 * C]
        decays.append(jnp.exp(jnp.where(lower, dext[:C, :C], -jnp.inf)))
        gcs.append(dext[:C, C:C + 1])
        gc_revs.append(dext[C:, C:C + 1])
        kbs.append(ks[n] * bs[n])
        k16s.append(ks[n].astype(BF16))
    kq = [_dot_nt(jnp.concatenate([kbs[n], qs[n]], axis=0).astype(BF16), k16s[n]) for n in range(len(inst))]
    kk = [m[:C] for m in kq]
    qk = [m[C:] for m in kq]
    pws = [jnp.where(strict, kk[n] * decays[n], 0.0).astype(BF16) for n in range(len(inst))]
    egc = [jnp.exp(g) for g in gcs]
    rhs = [jnp.concatenate([vs[n] * bs[n], kbs[n] * egc[n]], axis=1) for n in range(len(inst))]
    sols = [rhs[n] - _dot(pws[n], rhs[n].astype(BF16)) for n in range(len(inst))]
    for _ in range(int(math.log2(C)) - 1):
        pws = [_dot(p, p).astype(BF16) for p in pws]
        sols = [s + _dot(p, s.astype(BF16)) for p, s in zip(pws, sols)]
    for n, (c, hh) in enumerate(inst):
        rs = slice(c * C, (c + 1) * C)
        hcols = slice(hh * HD, (hh + 1) * HD)
        u_ref[rs, hcols] = sols[n][:, :HD].astype(u_ref.dtype)
        w_ref[rs, hcols] = sols[n][:, HD:].astype(w_ref.dtype)
        qkd = jnp.where(lower, qk[n] * decays[n], 0.0)
        qk_ref[rs, hcols] = jnp.concatenate([qkd, jnp.zeros_like(qkd)], axis=1).astype(qk_ref.dtype)
        qd_ref[rs, hcols] = (qs[n] * egc[n]).astype(qd_ref.dtype)
        kd_ref[rs, hcols] = (ks[n] * jnp.exp(gc_revs[n])).astype(kd_ref.dtype)
        gl_ref[c * 8:(c + 1) * 8, hcols] = jnp.broadcast_to(egc[n][C - 1:C, :], (8, HD))


def gdn_intra(proj, tail, tail_t, a_log, dt_bias, M, rows):
    c0 = 2 * SGU_WIDTH // DN_WIDTH
    pad = lambda p: jnp.pad(p.reshape(1, -1), ((0, 0), (0, LANES - p.shape[0])))
    seq = lambda dt: jax.ShapeDtypeStruct((M, DN_WIDTH), dt)
    row_spec = pl.BlockSpec((rows, DN_WIDTH), lambda i: (i, 0))
    cur_spec = lambda part: pl.BlockSpec((rows, DN_WIDTH), lambda i: (i, c0 + part))
    return pl.pallas_call(
        _gdn_intra_kernel,
        grid=(M // rows,),
        in_specs=[cur_spec(0), cur_spec(1), cur_spec(2),
                  pl.BlockSpec((rows, LANES), lambda i: (i, 0)),
                  pl.BlockSpec((2 * DN_HEADS, rows), lambda i: (0, i)),
                  pl.BlockSpec((1, LANES), lambda i: (0, 0)),
                  pl.BlockSpec((1, LANES), lambda i: (0, 0))],
        out_specs=[row_spec, row_spec, row_spec, row_spec, row_spec,
                   pl.BlockSpec((rows // DN_CHUNK * 8, DN_WIDTH), lambda i: (i, 0))],
        out_shape=[seq(BF16), seq(BF16), seq(BF16), seq(BF16), seq(BF16),
                   jax.ShapeDtypeStruct((M // DN_CHUNK * 8, DN_WIDTH), F32)],
        compiler_params=_cparams("parallel"),
    )(proj, proj, proj, tail, tail_t, pad(a_log), pad(dt_bias))


def _gdn_scan_kernel(u_ref, w_ref, qd_ref, kd_ref, qk_ref, gl_ref, gate_ref, ng_ref, o_ref, state_ref, *,
                     chunks):
    C = DN_CHUNK
    HD = DN_HEAD_DIM
    B = u_ref.shape[0]

    @pl.when(pl.program_id(0) == 0)
    def _():
        state_ref[...] = jnp.zeros_like(state_ref)

    ng = ng_ref[...]
    inst = [(b, hh) for b in range(B) for hh in range(DN_HEADS)]
    col = lambda hh: slice(hh * HD, (hh + 1) * HD)
    states = [state_ref[b, hh] for b, hh in inst]
    for c in range(chunks):
        rs = slice(c * C, (c + 1) * C)
        kdt = [kd_ref[b, rs, col(hh)].astype(F32).T.astype(BF16) for b, hh in inst]
        st16 = [s.astype(BF16) for s in states]
        wq = [_dot(jnp.concatenate([w_ref[b, rs, col(hh)], qd_ref[b, rs, col(hh)]], axis=0), st16[n])
              for n, (b, hh) in enumerate(inst)]
        vn16 = [(u_ref[b, rs, col(hh)].astype(F32) - wq[n][:C]).astype(BF16) for n, (b, hh) in enumerate(inst)]
        qkv = [_dot(jnp.concatenate([qk_ref[b, rs, col(hh)][:, :C], kdt[n]], axis=0), vn16[n])
               for n, (b, hh) in enumerate(inst)]
        states = [states[n] * gl_ref[b, c * 8:c * 8 + 1, col(hh)] + qkv[n][C:]
                  for n, (b, hh) in enumerate(inst)]
        for n, (b, hh) in enumerate(inst):
            o = wq[n][C:] + qkv[n][:C]
            o = o * lax.rsqrt(jnp.mean(o * o, axis=-1, keepdims=True) + EPS) * ng
            o_ref[b, rs, col(hh)] = (o * _silu(gate_ref[b, rs, col(hh)].astype(F32))).astype(o_ref.dtype)
    for n, (b, hh) in enumerate(inst):
        state_ref[b, hh] = states[n]


def gdn_scan(u, w, qd, kd, qk, gl, proj3, norm_g, B, S, chunks):
    rows = chunks * DN_CHUNK
    r3 = lambda a: a.reshape(B, S, DN_WIDTH)
    gcol = (2 * SGU_WIDTH + 3 * DN_WIDTH) // DN_WIDTH
    seq_spec = pl.BlockSpec((B, rows, DN_WIDTH), lambda n: (0, n, 0))
    return pl.pallas_call(
        functools.partial(_gdn_scan_kernel, chunks=chunks),
        grid=(S // rows,),
        in_specs=[seq_spec, seq_spec, seq_spec, seq_spec, seq_spec,
                  pl.BlockSpec((B, chunks * 8, DN_WIDTH), lambda n: (0, n, 0)),
                  pl.BlockSpec((B, rows, DN_WIDTH), lambda n: (0, n, gcol)),
                  pl.BlockSpec((1, DN_HEAD_DIM), lambda n: (0, 0))],
        out_specs=seq_spec,
        out_shape=jax.ShapeDtypeStruct((B, S, DN_WIDTH), BF16),
        scratch_shapes=[pltpu.VMEM((B, DN_HEADS, DN_HEAD_DIM, DN_HEAD_DIM), F32)],
        compiler_params=_cparams("arbitrary"),
    )(r3(u), r3(w), r3(qd), r3(kd), r3(qk), gl.reshape(B, S // DN_CHUNK * 8, DN_WIDTH), proj3,
      norm_g.reshape(1, DN_HEAD_DIM))


def _tiles(S):
    rows = min(1024, S)
    return dict(rows=rows, sgu_rows=rows, gdn_rows=min(512, S), scan_chunks=min(16, S // DN_CHUNK),
                ffn_chunk=256)


def _forward(x, mem, mem_norm, norm_mix, norm_xattn, norm_ffn, ev_w_in, pool_w, pool_scale, ev_w_out,
             od_w_in, sgu_ln_g, sgu_ln_b, sgu_w, sgu_b, dn_conv, dn_a_log, dn_dt_bias, dn_norm_g,
             od_w_out, xattn_wq, xattn_wkv, xattn_wo, ffn_w_up, ffn_conv, ffn_w_down, final_norm):
    B, S, D = x.shape
    n_mem = mem.shape[1]
    M = B * S
    depth = norm_mix.shape[0]
    bf = lambda a: a.astype(BF16)
    t = _tiles(S)

    h = x.reshape(M, D)
    mem2 = mem.reshape(B * n_mem, D)
    w_up_all, w_down_all = bf(ffn_w_up), bf(ffn_w_down)
    x_qk, x_vo = xattn_absorb(mem2, mem_norm, xattn_wkv, xattn_wq, xattn_wo, B, n_mem)
    for layer in range(depth):
        i = layer // 2
        if layer % 2 == 0:
            proj, = norm_matmul(h, norm_mix[layer], [bf(ev_w_in[i])], [BF16], t["rows"])
            a_out = moba_attention(proj.reshape(B, S, -1), B, S).reshape(M, A_WIDTH)
            b_out = multiscale_pool(proj, bf(pool_w[i]), pool_scale[i], M, S, t["rows"])
            mix_a, mix_b, w_mix = a_out, b_out, ev_w_out[i]
        else:
            main_w = 2 * SGU_WIDTH + 4 * DN_WIDTH
            w_in = od_w_in[i]
            w_tail = jnp.pad(w_in[:, main_w:], ((0, 0), (0, LANES - 2 * DN_HEADS)))
            proj, tail, tail_t = od_projection(h, norm_mix[layer], bf(w_in), bf(w_tail), dn_conv[i],
                                               S, t["rows"])
            c_out = spatial_gating(proj, sgu_ln_g[i], sgu_ln_b[i], sgu_w[i], sgu_b[i], M, t["sgu_rows"])
            u, w, qd, kd, qk, gl = gdn_intra(proj, tail, tail_t, dn_a_log[i], dn_dt_bias[i], M, t["gdn_rows"])
            d_out = gdn_scan(u, w, qd, kd, qk, gl, proj.reshape(B, S, -1), dn_norm_g[i], B, S,
                             t["scan_chunks"])
            mix_a, mix_b, w_mix = c_out, d_out.reshape(M, DN_WIDTH), od_w_out[i]
        h = mix_xattn_residual(h, mix_a, mix_b, bf(w_mix), norm_xattn[layer], x_qk, x_vo, layer, S, n_mem, t["rows"])
        h = ffn_residual(h, norm_ffn[layer], w_up_all, ffn_conv, w_down_all, layer, final_norm, S, t["rows"],
                         t["ffn_chunk"], final_norm=(layer == depth - 1))
    return h.reshape(B, S, D)


def kernel(x, mem, mem_norm, norm_mix, norm_xattn, norm_ffn, ev_w_in, pool_w, pool_scale, ev_w_out, od_w_in, sgu_ln_g, sgu_ln_b, sgu_w, sgu_b, dn_conv, dn_a_log, dn_dt_bias, dn_norm_g, od_w_out, xattn_wq, xattn_wkv, xattn_wo, ffn_w_up, ffn_conv, ffn_w_down, final_norm):
    return _forward(x, mem, mem_norm, norm_mix, norm_xattn, norm_ffn, ev_w_in, pool_w, pool_scale, ev_w_out,
                    od_w_in, sgu_ln_g, sgu_ln_b, sgu_w, sgu_b, dn_conv, dn_a_log, dn_dt_bias, dn_norm_g,
                    od_w_out, xattn_wq, xattn_wkv, xattn_wo, ffn_w_up, ffn_conv, ffn_w_down, final_norm)
```
